```python
import math
import jax, jax.numpy as jnp
from jax import lax
import numpy as np

D_MODEL = 1024
BATCH = 8
SEQ = 2048
DEPTH = 2

CTX_LEN = 256
GRID_W = 64
EPS = 1e-6
HY_WIDTH = 256
HY_ORDER = 2
HY_EMB = 33
HY_BANDS = (HY_EMB - 1) // 2
HY_FILTER_HIDDEN = 64
HY_MIN_DECAY = math.log(1e-2) / 1.5
HY_MAX_DECAY = math.log(1e-2) / 0.3
FN_GROUPS = 4
FN_GROUP_DIM = 64
FN_WIDTH = FN_GROUPS * FN_GROUP_DIM
DA_HEADS = 4
DA_QK_DIM = 64
DA_V_DIM = 2 * DA_QK_DIM
DA_WIDTH = DA_HEADS * DA_V_DIM
ROPE_BASE = 10000.0
Q_BLOCK = 128
N_BRANCH = 3
P_HY = 3 * HY_WIDTH
P_FN = FN_WIDTH
P_Q = DA_HEADS * 2 * DA_QK_DIM
P_K = DA_HEADS * 2 * DA_QK_DIM
P_V = DA_WIDTH
P_GATE = N_BRANCH * D_MODEL
P_IN = P_HY + P_FN + P_Q + P_K + P_V + P_GATE
SPLIT_IDX = (P_HY, P_HY + P_FN, P_HY + P_FN + P_Q, P_HY + P_FN + P_Q + P_K, P_HY + P_FN + P_Q + P_K + P_V)
PEER_HEADS = 8
PEER_NKEYS = 128
PEER_EXPERTS = PEER_NKEYS * PEER_NKEYS
PEER_DQ = 256
PEER_TOPK = 16
PEER_CHUNK = 128

kernel_name = 'hybrid_hyena_fnet_diffattn_peer_dit'


def rms_norm(x, g):
    x32 = x.astype(jnp.float32)
    y = x32 * lax.rsqrt(jnp.mean(x32 * x32, axis=-1, keepdims=True) + EPS)
    return (y * g.astype(jnp.float32)).astype(x.dtype)


def modulate(x, g, shift, scale):
    return rms_norm(x, g) * (1 + scale) + shift


def short_conv(z, w, b):
    zp = jnp.pad(z, ((0, 0), (1, 1), (0, 0)))
    return zp[:, :-2] * w[0] + zp[:, 1:-1] * w[1] + zp[:, 2:] * w[2] + b


def hyena_kernels(L, w1, b1, freq, w2, b2, w3):
    f32 = lambda a: a.astype(jnp.float32)
    pos = jnp.arange(L, dtype=jnp.float32)
    t = pos / max(L - 1, 1)
    w = 2.0 * math.pi * pos / L
    f = jnp.linspace(1e-4, HY_BANDS - 1, HY_BANDS, dtype=jnp.float32)
    feats = jnp.concatenate([t[:, None], jnp.cos(w[:, None] * f), -jnp.sin(w[:, None] * f)], axis=-1)
    fr = f32(freq)
    h = jnp.sin(fr * (feats @ f32(w1) + f32(b1)))
    h = jnp.sin(fr * (h @ f32(w2) + f32(b2)))
    h = (h @ f32(w3)).reshape(L, 2, HY_ORDER, HY_WIDTH)
    deltas = jnp.abs(jnp.linspace(HY_MIN_DECAY, HY_MAX_DECAY, HY_WIDTH, dtype=jnp.float32))
    h = h * jnp.exp(-t[:, None, None, None] * deltas)
    k = jnp.concatenate([h[:, 0], jnp.zeros((1, HY_ORDER, HY_WIDTH), jnp.float32), h[:0:-1, 1]], axis=0)
    k = k / jnp.sum(jnp.abs(k), axis=0, keepdims=True)
    return jnp.fft.rfft(k, axis=0)


def hyena_mix(u, kf, bias):
    L = u.shape[1]
    v, x1, x2 = jnp.split(u.astype(jnp.float32), 3, axis=-1)
    z = v
    for o, gate in enumerate((x1, x2)):
        zf = jnp.fft.rfft(z, n=2 * L, axis=1)
        conv = jnp.fft.irfft(zf * kf[:, o], n=2 * L, axis=1)[:, :L]
        z = gate * (conv + bias[o].astype(jnp.float32) * z)
    return z.astype(u.dtype)


def fourier_mix(z):
    B, L, _ = z.shape
    zg = z.astype(jnp.float32).reshape(B, L, FN_GROUPS, FN_GROUP_DIM)
    y = jnp.fft.fftn(zg, axes=(1, 3), norm='ortho').real
    return y.reshape(B, L, FN_WIDTH).astype(z.dtype)


def axial_rope(L):
    rows = L // GRID_W
    row = jnp.repeat(jnp.arange(rows), GRID_W).astype(jnp.float32)
    col = jnp.tile(jnp.arange(GRID_W), rows).astype(jnp.float32)
    half = DA_QK_DIM // 2
    inv = ROPE_BASE ** (-jnp.arange(0, half, 2, dtype=jnp.float32) / half)
    ang = jnp.stack([row[:, None] * inv, col[:, None] * inv], axis=1)
    return jnp.cos(ang), jnp.sin(ang)


def apply_rope(x, cos, sin):
    shp = x.shape
    xr = x.astype(jnp.float32).reshape(shp[:-1] + (2, 2, DA_QK_DIM // 4))
    x1, x2 = xr[..., 0, :], xr[..., 1, :]
    cc = cos[None, :, None, None]
    ss = sin[None, :, None, None]
    out = jnp.stack([x1 * cc - x2 * ss, x2 * cc + x1 * ss], axis=-2)
    return out.reshape(shp).astype(x.dtype)


def split_in(h):
    B, L, _ = h.shape
    hy, fn, q, k, v, gates = jnp.split(h, SPLIT_IDX, axis=-1)
    q = q.reshape(B, L, DA_HEADS, 2, DA_QK_DIM)
    k = k.reshape(B, L, DA_HEADS, 2, DA_QK_DIM)
    v = v.reshape(B, L, DA_HEADS, DA_V_DIM)
    return hy, fn, q, k, v, gates


def heads_first(a):
    return jnp.moveaxis(a, 2, 1)


def diff_attend(q, k, v, lam_val):
    s = jnp.einsum('bhqcd,bhkcd->bhcqk', q, k).astype(jnp.float32) * (DA_QK_DIM ** -0.5)
    p = jax.nn.softmax(s, axis=-1)
    a = p[:, :, 0] - lam_val * p[:, :, 1]
    return jnp.einsum('bhqk,bhkd->bhqd', a.astype(v.dtype), v)


def diff_out(o, g_sub, lam_init):
    B, H, L, _ = o.shape
    o = rms_norm(o, g_sub) * (1.0 - lam_init)
    return jnp.moveaxis(o, 1, 2).reshape(B, L, DA_WIDTH)


def merge(hy_o, fn_o, at_o, gates, w_hy, w_fn, w_at, w_out):
    g = jax.nn.sigmoid(gates.astype(jnp.float32)).astype(gates.dtype)
    g1, g2, g3 = jnp.split(g, N_BRANCH, axis=-1)
    y = g1 * (hy_o @ w_hy) + g2 * (fn_o @ w_fn) + g3 * (at_o @ w_at)
    return y @ w_out


def peer(t, wq, keys, u, v):
    T = t.shape[0]
    q = (t @ wq).reshape(T, PEER_HEADS, 2, PEER_DQ // 2)
    s = jnp.einsum('thpd,hpnd->thpn', q, keys).astype(jnp.float32)
    s1, i1 = lax.top_k(s[:, :, 0], PEER_TOPK)
    s2, i2 = lax.top_k(s[:, :, 1], PEER_TOPK)
    cand = (s1[..., :, None] + s2[..., None, :]).reshape(T, PEER_HEADS, PEER_TOPK * PEER_TOPK)
    sc, ci = lax.top_k(cand, PEER_TOPK)
    e1 = jnp.take_along_axis(i1, ci // PEER_TOPK, axis=-1)
    e2 = jnp.take_along_axis(i2, ci % PEER_TOPK, axis=-1)
    idx = (e1 * PEER_NKEYS + e2).reshape(T, PEER_HEADS * PEER_TOPK)
    g = jax.nn.softmax(sc, axis=-1).reshape(T, PEER_HEADS * PEER_TOPK).astype(t.dtype)
    nchunk = T // PEER_CHUNK

    def chunk(args):
        tc, ic, gc = args
        hid = jnp.einsum('td,tkd->tk', tc, u[ic])
        return jnp.einsum('tk,tkd->td', gc * jax.nn.gelu(hid, approximate=False), v[ic])

    y = lax.map(chunk, (t.reshape(nchunk, PEER_CHUNK, -1), idx.reshape(nchunk, PEER_CHUNK, -1),
                        g.reshape(nchunk, PEER_CHUNK, -1)))
    return y.reshape(T, -1)


def setup_inputs(seed: int = 0) -> dict:
    key = jax.random.key(seed)
    ks = jax.random.split(key, 32)
    D = D_MODEL
    L = DEPTH

    def nrm(i, shape, scale):
        return jax.random.normal(ks[i], shape, jnp.float32) * scale

    return {
        'x': nrm(0, (BATCH, SEQ, D), 1.0),
        'c': nrm(1, (BATCH, D), 1.0),
        'ctx': nrm(2, (BATCH, CTX_LEN, D), 1.0),
        'c_ctx': nrm(3, (D,), 1.0),
        'w_ada': nrm(4, (L, D, 6 * D), 0.5 * D ** -0.5),
        'b_ada': nrm(5, (L, 6 * D), 0.02),
        'g_mix': 1.0 + nrm(6, (L, D), 0.02),
        'g_ffn': 1.0 + nrm(7, (L, D), 0.02),
        'w_in': nrm(8, (L, D, P_IN), D ** -0.5),
        'hy_conv_w': nrm(9, (L, 3, P_HY), 0.5),
        'hy_conv_b': nrm(10, (L, P_HY), 0.02),
        'hy_w1': nrm(11, (L, HY_EMB, HY_FILTER_HIDDEN), HY_EMB ** -0.5),
        'hy_b1': nrm(12, (L, HY_FILTER_HIDDEN), 0.02),
        'hy_freq': 1.0 + nrm(13, (L, HY_FILTER_HIDDEN), 0.1),
        'hy_w2': nrm(14, (L, HY_FILTER_HIDDEN, HY_FILTER_HIDDEN), HY_FILTER_HIDDEN ** -0.5),
        'hy_b2': nrm(15, (L, HY_FILTER_HIDDEN), 0.02),
        'hy_w3': nrm(16, (L, HY_FILTER_HIDDEN, 2 * HY_ORDER * HY_WIDTH), HY_FILTER_HIDDEN ** -0.5),
        'hy_bias': nrm(17, (L, HY_ORDER, HY_WIDTH), 0.5),
        'g_q': 1.0 + nrm(18, (L, 2, DA_QK_DIM), 0.02),
        'g_k': 1.0 + nrm(19, (L, 2, DA_QK_DIM), 0.02),
        'lam': nrm(20, (L, 4, DA_QK_DIM), 0.1),
        'g_sub': 1.0 + nrm(21, (L, DA_V_DIM), 0.02),
        'w_hy': nrm(22, (L, HY_WIDTH, D), HY_WIDTH ** -0.5),
        'w_fn': nrm(23, (L, FN_WIDTH, D), FN_WIDTH ** -0.5),
        'w_at': nrm(24, (L, DA_WIDTH, D), DA_WIDTH ** -0.5),
        'w_out': nrm(25, (L, D, D), D ** -0.5),
        'peer_wq': nrm(26, (L, D, PEER_HEADS * PEER_DQ), D ** -0.5),
        'peer_keys': nrm(27, (L, PEER_HEADS, 2, PEER_NKEYS, PEER_DQ // 2), (PEER_DQ // 2) ** -0.5),
        'peer_u': nrm(28, (L, PEER_EXPERTS, D), D ** -0.5),
        'peer_v': nrm(29, (L, PEER_EXPERTS, D), 0.25),
    }


def reference(x, c, ctx, c_ctx, w_ada, b_ada, g_mix, g_ffn, w_in, hy_conv_w, hy_conv_b,
              hy_w1, hy_b1, hy_freq, hy_w2, hy_b2, hy_w3, hy_bias, g_q, g_k, lam, g_sub,
              w_hy, w_fn, w_at, w_out, peer_wq, peer_keys, peer_u, peer_v):
    B, S, D = x.shape
    C = ctx.shape[1]
    cos, sin = axial_rope(S)
    xl, xc = x, ctx
    for l in range(DEPTH):
        last = l == DEPTH - 1
        mod_l = jnp.split((jax.nn.silu(c) @ w_ada[l] + b_ada[l])[:, None, :], 6, axis=-1)
        mod_c = jnp.split((jax.nn.silu(c_ctx) @ w_ada[l] + b_ada[l])[None, None, :], 6, axis=-1)
        lam_init = 0.8 - 0.6 * math.exp(-0.3 * l)
        lf = lam[l].astype(jnp.float32)
        lam_val = jnp.exp(jnp.sum(lf[0] * lf[1])) - jnp.exp(jnp.sum(lf[2] * lf[3])) + lam_init
        hy_args = (hy_w1[l], hy_b1[l], hy_freq[l], hy_w2[l], hy_b2[l], hy_w3[l])

        hy_l, fn_l, q_l, k_l, v_l, gate_l = split_in(modulate(xl, g_mix[l], mod_l[0], mod_l[1]) @ w_in[l])
        hy_c, fn_c, q_c, k_c, v_c, gate_c = split_in(modulate(xc, g_mix[l], mod_c[0], mod_c[1]) @ w_in[l])
        q_l = apply_rope(rms_norm(q_l, g_q[l]), cos, sin)
        k_l = apply_rope(rms_norm(k_l, g_k[l]), cos, sin)
        k_c = heads_first(rms_norm(k_c, g_k[l]))
        v_c = heads_first(v_c)
        k_all = jnp.concatenate([heads_first(k_l), k_c], axis=2)
        v_all = jnp.concatenate([heads_first(v_l), v_c], axis=2)
        qb = heads_first(q_l).reshape(B, DA_HEADS, S // Q_BLOCK, Q_BLOCK, 2, DA_QK_DIM)
        o_l = lax.map(lambda qq: diff_attend(qq, k_all, v_all, lam_val), jnp.moveaxis(qb, 2, 0))
        o_l = jnp.moveaxis(o_l, 0, 2).reshape(B, DA_HEADS, S, DA_V_DIM)
        att_l = diff_out(o_l, g_sub[l], lam_init)
        hyo_l = hyena_mix(short_conv(hy_l, hy_conv_w[l], hy_conv_b[l]), hyena_kernels(S, *hy_args), hy_bias[l])
        mix_l = merge(hyo_l, fourier_mix(fn_l), att_l, gate_l, w_hy[l], w_fn[l], w_at[l], w_out[l])
        if not last:
            q_c = heads_first(rms_norm(q_c, g_q[l]))
            att_c = diff_out(diff_attend(q_c, k_c, v_c, lam_val), g_sub[l], lam_init)
            hyo_c = hyena_mix(short_conv(hy_c, hy_conv_w[l], hy_conv_b[l]), hyena_kernels(C, *hy_args), hy_bias[l])
            mix_c = merge(hyo_c, fourier_mix(fn_c), att_c, gate_c, w_hy[l], w_fn[l], w_at[l], w_out[l])
            xc = xc + mod_c[2] * mix_c
        xl = xl + mod_l[2] * mix_l

        n_l = modulate(xl, g_ffn[l], mod_l[3], mod_l[4]).reshape(B * S, D)
        if last:
            y_l = peer(n_l, peer_wq[l], peer_keys[l], peer_u[l], peer_v[l])
        else:
            n_c = modulate(xc, g_ffn[l], mod_c[3], mod_c[4]).reshape(B * C, D)
            y = peer(jnp.concatenate([n_c, n_l], axis=0), peer_wq[l], peer_keys[l], peer_u[l], peer_v[l])
            y_l = y[B * C:]
            xc = xc + mod_c[5] * y[:B * C].reshape(B, C, D)
        xl = xl + mod_l[5] * y_l.reshape(B, S, D)
    return xl
```

```python
import functools
import math

import jax
import jax.numpy as jnp
import numpy as np
from jax import lax
from jax.experimental import pallas as pl
from jax.experimental.pallas import tpu as pltpu

F32 = jnp.float32
BF16 = jnp.bfloat16

EPS = 1e-6
GRID_W = 64
ROPE_BASE = 10000.0
N_HEADS = 4
QK_DIM = 64
HEAD_W = 2 * QK_DIM
HY_W = 256
HY_ORDER = 2
HY_EMB = 33
HY_BANDS = (HY_EMB - 1) // 2
HY_MIN_DECAY = math.log(1e-2) / 1.5
HY_MAX_DECAY = math.log(1e-2) / 0.3
FN_GROUP = 64
FN_W = 256
PEER_HEADS = 8
PEER_NKEYS = 128
PEER_TOPK = 16
N_MOD = 6
MOD_ROWS = 16
NEG = -3.0e38

VMEM_LIMIT = 56 * 1024 * 1024

COL_GATE = 0
COL_HY = 3072
COL_FN = 3840
COL_Q = 4096
COL_K = 4608
COL_V = 5120
P_IN = 5632


def _cp(*sem):
    return pltpu.CompilerParams(dimension_semantics=sem, vmem_limit_bytes=VMEM_LIMIT)


def _dot(a, b):
    return jnp.dot(a, b, preferred_element_type=F32)


def _dot_nt(a, b):
    return lax.dot_general(a, b, (((1,), (1,)), ((), ())), preferred_element_type=F32)


def _split(a):
    hi = a.astype(BF16)
    lo = (a - hi.astype(F32)).astype(BF16)
    return hi, lo


def _modulated_norm(x, g, shift, scale):
    ms = jnp.mean(x * x, axis=-1, keepdims=True)
    y = x * lax.rsqrt(ms + EPS) * g
    return y * (1.0 + scale) + shift


def _mod_spec(layer, chunk, row_fn):
    def imap(*idx):
        return (layer, row_fn(*idx), chunk, 0, 0)
    return imap


def _ada_kernel(c_ref, w_ref, b_ref, o_ref):
    c = c_ref[...]
    a = c / (1.0 + jnp.exp(-c))
    ah, al = _split(a)
    wh, wl = _split(w_ref[...])
    o_ref[...] = _dot(ah, wh) + _dot(ah, wl) + _dot(al, wh) + b_ref[...]


def _ada_mods(cc, w_ada, b_ada):
    depth, d, n = w_ada.shape
    tn = 512
    out = pl.pallas_call(
        _ada_kernel,
        grid=(depth, n // tn),
        in_specs=[pl.BlockSpec((MOD_ROWS, d), lambda l, j: (0, 0)),
                  pl.BlockSpec((None, d, tn), lambda l, j: (l, 0, j)),
                  pl.BlockSpec((None, 1, tn), lambda l, j: (l, 0, j))],
        out_specs=pl.BlockSpec((None, MOD_ROWS, tn), lambda l, j: (l, 0, j)),
        out_shape=jax.ShapeDtypeStruct((depth, MOD_ROWS, n), F32),
        compiler_params=_cp("parallel", "parallel"),
        name="ada_mods",
    )(cc, w_ada, b_ada.reshape(depth, 1, n))
    return out.reshape(depth, MOD_ROWS, N_MOD, 1, d)


def _inproj_kernel(x_ref, g_ref, sh_ref, sc_ref, w_ref, o_ref, xn_ref):
    @pl.when(pl.program_id(2) == 0)
    def _():
        xn_ref[...] = _modulated_norm(x_ref[...], g_ref[...], sh_ref[...], sc_ref[...]).astype(BF16)
    o_ref[...] = _dot(xn_ref[...], w_ref[...]).astype(o_ref.dtype)


def _in_projection(x, g, mods, layer, row_fn, w, col_lo, col_hi):
    b, l, d = x.shape
    tm = min(l, 1024)
    tn = 512
    j0 = col_lo // tn
    nj = (col_hi - col_lo) // tn
    mrow = lambda bi, i, j: row_fn(bi)
    return pl.pallas_call(
        _inproj_kernel,
        grid=(b, l // tm, nj),
        in_specs=[pl.BlockSpec((None, tm, d), lambda bi, i, j: (bi, i, 0)),
                  pl.BlockSpec((1, d), lambda bi, i, j: (0, 0)),
                  pl.BlockSpec((None, None, None, 1, d), _mod_spec(layer, 0, mrow)),
                  pl.BlockSpec((None, None, None, 1, d), _mod_spec(layer, 1, mrow)),
                  pl.BlockSpec((d, tn), lambda bi, i, j: (0, j + j0))],
        out_specs=pl.BlockSpec((None, tm, tn), lambda bi, i, j: (bi, i, j + j0)),
        out_shape=jax.ShapeDtypeStruct((b, l, w.shape[1]), BF16),
        scratch_shapes=[pltpu.VMEM((tm, d), BF16)],
        compiler_params=_cp("parallel", "parallel", "arbitrary"),
        name="in_projection",
    )(x, g, mods, mods, w)


def _qkprep_kernel(*refs, rope):
    if rope:
        q_ref, k_ref, gq_ref, gk_ref, bd_ref, cos_ref, sin_ref, qo_ref, ko_ref = refs
    else:
        q_ref, k_ref, gq_ref, gk_ref, bd_ref, qo_ref, ko_ref = refs
    for src, g_ref, dst in ((q_ref, gq_ref, qo_ref), (k_ref, gk_ref, ko_ref)):
        x = src[...].astype(F32)
        hi, lo = _split(x * x)
        ms = _dot(hi, bd_ref[...]) + _dot(lo, bd_ref[...])
        y = x * lax.rsqrt(ms + EPS) * g_ref[...]
        if rope:
            w = y.shape[1]
            lane = lax.broadcasted_iota(jnp.int32, y.shape, 1)
            first = (lane % (QK_DIM // 2)) < (QK_DIM // 4)
            partner = jnp.where(first, pltpu.roll(y, w - QK_DIM // 4, 1), pltpu.roll(y, QK_DIM // 4, 1))
            y = y * cos_ref[...] + partner * sin_ref[...]
        dst[...] = y.astype(BF16)


def _qk_prep(h, gq, gk, bd, rope_tabs):
    b, l, _ = h.shape
    w = N_HEADS * HEAD_W
    tm = min(l, 512)
    rope = rope_tabs is not None
    in_specs = [pl.BlockSpec((None, tm, w), lambda bi, i: (bi, i, COL_Q // w)),
                pl.BlockSpec((None, tm, w), lambda bi, i: (bi, i, COL_K // w)),
                pl.BlockSpec((1, w), lambda bi, i: (0, 0)),
                pl.BlockSpec((1, w), lambda bi, i: (0, 0)),
                pl.BlockSpec((w, w), lambda bi, i: (0, 0))]
    args = [h, h, gq, gk, bd]
    if rope:
        in_specs += [pl.BlockSpec((tm, w), lambda bi, i: (i, 0))] * 2
        args += list(rope_tabs)
    return pl.pallas_call(
        functools.partial(_qkprep_kernel, rope=rope),
        grid=(b, l // tm),
        in_specs=in_specs,
        out_specs=[pl.BlockSpec((None, tm, w), lambda bi, i: (bi, i, 0))] * 2,
        out_shape=[jax.ShapeDtypeStruct((b, l, w), BF16)] * 2,
        compiler_params=_cp("parallel", "parallel"),
        name="qk_prep",
    )(*args)


def _attn_kernel(*refs, n_src, lam_init):
    q_ref = refs[0]
    kv = refs[1:1 + 2 * n_src]
    lam_ref, gsub_ref, o_ref = refs[1 + 2 * n_src:]
    q = q_ref[...]
    tq = q.shape[0]
    lane = lax.broadcasted_iota(jnp.int32, q.shape, 1)
    zero = jnp.zeros_like(q)
    qq = jnp.concatenate([jnp.where(lane < QK_DIM, q, zero), jnp.where(lane >= QK_DIM, q, zero)], axis=0)
    scores = [_dot_nt(qq, kv[2 * i][...]) for i in range(n_src)]
    m = jnp.max(scores[0], axis=-1, keepdims=True)
    for s in scores[1:]:
        m = jnp.maximum(m, jnp.max(s, axis=-1, keepdims=True))
    z = jnp.zeros_like(m)
    acc = jnp.zeros((2 * tq, HEAD_W), F32)
    for i, s in enumerate(scores):
        e = jnp.exp(s - m)
        z = z + jnp.sum(e, axis=-1, keepdims=True)
        acc = acc + _dot(e.astype(BF16), kv[2 * i + 1][...])
    o2 = acc / z
    lf = lam_ref[...]
    lam_val = (jnp.exp(jnp.sum(lf[0:1] * lf[1:2], axis=-1, keepdims=True))
               - jnp.exp(jnp.sum(lf[2:3] * lf[3:4], axis=-1, keepdims=True)) + lam_init)
    o = o2[:tq] - lam_val * o2[tq:]
    ms = jnp.mean(o * o, axis=-1, keepdims=True)
    o = o * lax.rsqrt(ms + EPS) * gsub_ref[...] * (1.0 - lam_init)
    o_ref[...] = o.astype(BF16)


def _diff_attention(q, sources, lam_l, gsub, lam_init):
    b, lq, w = q.shape
    tq = 256
    in_specs = [pl.BlockSpec((None, tq, HEAD_W), lambda bi, hi, i: (bi, i, hi))]
    args = [q]
    for k, varr, vblk in sources:
        lk = k.shape[1]
        in_specs.append(pl.BlockSpec((None, lk, HEAD_W), lambda bi, hi, i: (bi, 0, hi)))
        in_specs.append(pl.BlockSpec((None, lk, HEAD_W), lambda bi, hi, i, vblk=vblk: (bi, 0, vblk + hi)))
        args += [k, varr]
    in_specs += [pl.BlockSpec(lam_l.shape, lambda bi, hi, i: (0, 0)),
                 pl.BlockSpec((1, HEAD_W), lambda bi, hi, i: (0, 0))]
    args += [lam_l, gsub]
    return pl.pallas_call(
        functools.partial(_attn_kernel, n_src=len(sources), lam_init=lam_init),
        grid=(b, N_HEADS, lq // tq),
        in_specs=in_specs,
        out_specs=pl.BlockSpec((None, tq, HEAD_W), lambda bi, hi, i: (bi, i, hi)),
        out_shape=jax.ShapeDtypeStruct((b, lq, w), BF16),
        compiler_params=_cp("parallel", "parallel", "arbitrary"),
        name="diff_attention",
    )(*args)


def _sconv_kernel(h_ref, w_ref, b_ref, o_ref):
    x = h_ref[...].astype(F32)
    n = x.shape[0]
    row = lax.broadcasted_iota(jnp.int32, x.shape, 0)
    prev = jnp.where(row == 0, 0.0, pltpu.roll(x, 1, 0))
    nxt = jnp.where(row == n - 1, 0.0, pltpu.roll(x, n - 1, 0))
    o_ref[...] = prev * w_ref[0:1, :] + x * w_ref[1:2, :] + nxt * w_ref[2:3, :] + b_ref[...]


def _short_conv(h, w, bias):
    b, l, _ = h.shape
    return pl.pallas_call(
        _sconv_kernel,
        grid=(b, 3),
        in_specs=[pl.BlockSpec((None, l, HY_W), lambda bi, j: (bi, 0, COL_HY // HY_W + j)),
                  pl.BlockSpec((3, HY_W), lambda bi, j: (0, j)),
                  pl.BlockSpec((1, HY_W), lambda bi, j: (0, j))],
        out_specs=pl.BlockSpec((None, None, l, HY_W), lambda bi, j: (bi, j, 0, 0)),
        out_shape=jax.ShapeDtypeStruct((b, 3, l, HY_W), F32),
        compiler_params=_cp("parallel", "parallel"),
        name="short_conv",
    )(h, w, bias)


def _filter_kernel(f_ref, w1_ref, b1_ref, fr_ref, w2_ref, b2_ref, w3_ref, dec_ref, o_ref):
    def mm(a, w_ref_):
        ah, al = _split(a)
        wh, wl = _split(w_ref_[...])
        return _dot(ah, wh) + _dot(ah, wl) + _dot(al, wh)
    fr = fr_ref[...]
    h = jnp.sin(fr * (mm(f_ref[...], w1_ref) + b1_ref[...]))
    h = jnp.sin(fr * (mm(h, w2_ref) + b2_ref[...]))
    h = mm(h, w3_ref)
    half = h.shape[1] // 2
    dec = dec_ref[...]
    hf = h[:, :half] * dec
    hb = h[:, half:] * dec
    row = lax.broadcasted_iota(jnp.int32, hb.shape, 0)
    hb = jnp.where(row == 0, 0.0, hb)
    norm = jnp.sum(jnp.abs(hf) + jnp.abs(hb), axis=0, keepdims=True)
    o_ref[:, :half] = (hf + hb) / norm
    o_ref[:, half:] = (hf - hb) / norm


def _hyena_filter_taps(feats, w1, b1, freq, w2, b2, w3, dec):
    l = feats.shape[0]
    n = w3.shape[1]
    full = lambda a: pl.BlockSpec(a.shape, lambda i: (0,) * a.ndim)
    args = (feats, w1, b1, freq, w2, b2, w3, dec)
    return pl.pallas_call(
        _filter_kernel,
        grid=(1,),
        in_specs=[full(a) for a in args],
        out_specs=pl.BlockSpec((l, n), lambda i: (0, 0)),
        out_shape=jax.ShapeDtypeStruct((l, n), F32),
        compiler_params=_cp("arbitrary"),
        name="hyena_filter_taps",
    )(*args)


def _table_mm_kernel(t_ref, x_ref, o_ref):
    o_ref[...] = _dot(t_ref[...], x_ref[...].astype(BF16)).astype(o_ref.dtype)


def _table_matmul(table, x, out_dtype):
    m, k = table.shape
    n = x.shape[1]
    tm = min(m, 512)
    return pl.pallas_call(
        _table_mm_kernel,
        grid=(m // tm,),
        in_specs=[pl.BlockSpec((tm, k), lambda i: (i, 0)),
                  pl.BlockSpec((k, n), lambda i: (0, 0))],
        out_specs=pl.BlockSpec((tm, n), lambda i: (i, 0)),
        out_shape=jax.ShapeDtypeStruct((m, n), out_dtype),
        compiler_params=_cp("parallel"),
        name="table_matmul",
    )(table, x)


def _dftmul_kernel(fc_ref, fs_ref, z_ref, k_ref, p_ref):
    z = z_ref[...].astype(BF16)
    zc = _dot(fc_ref[...], z)
    zs = _dot(fs_ref[...], z)
    p_ref[0] = (zc * k_ref[0] - zs * k_ref[1]).astype(BF16)
    p_ref[1] = (zc * k_ref[2] + zs * k_ref[3]).astype(BF16)


def _dft_multiply(ffwd, z, z_spec, kf):
    l = ffwd.shape[1]
    b = z.shape[0]
    tf = min(l, 512)
    nf = l // tf
    out = pl.pallas_call(
        _dftmul_kernel,
        grid=(nf, b),
        in_specs=[pl.BlockSpec((tf, l), lambda i, bi: (i, 0)),
                  pl.BlockSpec((tf, l), lambda i, bi: (i + nf, 0)),
                  z_spec,
                  pl.BlockSpec((4, tf, HY_W), lambda i, bi: (0, i, 0))],
        out_specs=pl.BlockSpec((None, 2, tf, HY_W), lambda i, bi: (bi, 0, i, 0)),
        out_shape=jax.ShapeDtypeStruct((b, 2, l, HY_W), BF16),
        compiler_params=_cp("parallel", "arbitrary"),
        name="dft_multiply",
    )(ffwd, ffwd, z, kf)
    return out.reshape(b, 2 * l, HY_W)


def _idft_gate_kernel(fi_ref, p_ref, g_ref, z_ref, b_ref, o_ref):
    conv = _dot(fi_ref[...], p_ref[...])
    o_ref[...] = (g_ref[...] * (conv + b_ref[...] * z_ref[...])).astype(o_ref.dtype)


def _idft_gate(finv, p, gate, gate_spec, z, z_spec, bias, out_dtype):
    l = finv.shape[0]
    b = p.shape[0]
    tt = min(l, 512)
    return pl.pallas_call(
        _idft_gate_kernel,
        grid=(l // tt, b),
        in_specs=[pl.BlockSpec((tt, 2 * l), lambda i, bi: (i, 0)),
                  pl.BlockSpec((None, 2 * l, HY_W), lambda i, bi: (bi, 0, 0)),
                  gate_spec, z_spec,
                  pl.BlockSpec((1, HY_W), lambda i, bi: (0, 0))],
        out_specs=pl.BlockSpec((None, tt, HY_W), lambda i, bi: (bi, i, 0)),
        out_shape=jax.ShapeDtypeStruct((b, l, HY_W), out_dtype),
        compiler_params=_cp("parallel", "arbitrary"),
        name="idft_gate",
    )(finv, p, gate, z, bias)


def _hyena_mix(h, conv_w, conv_b, kf, hy_bias, ffwd, finv):
    b, l, _ = h.shape
    tt = min(l, 512)
    u = _short_conv(h, conv_w, conv_b)
    part = lambda j, rows: pl.BlockSpec((None, None, rows, HY_W),
                                        lambda i, bi, j=j: (bi, j, i if rows != l else 0, 0))
    p = _dft_multiply(ffwd, u, part(0, l), kf[0])
    z1 = _idft_gate(finv, p, u, part(1, tt), u, part(0, tt), hy_bias[0:1], F32)
    p = _dft_multiply(ffwd, z1, pl.BlockSpec((None, l, HY_W), lambda i, bi: (bi, 0, 0)), kf[1])
    return _idft_gate(finv, p, u, part(2, tt), z1,
                      pl.BlockSpec((None, tt, HY_W), lambda i, bi: (bi, i, 0)), hy_bias[1:2], BF16)


def _fn1_kernel(z_ref, m_ref, o_ref):
    r = _dot(z_ref[...], m_ref[...])
    half = r.shape[1] // 2
    o_ref[0] = r[:, :half].astype(BF16)
    o_ref[1] = r[:, half:].astype(BF16)


def _fourier_mix(h, m1, t2):
    b, l, _ = h.shape
    tm = min(l, 512)
    zz = pl.pallas_call(
        _fn1_kernel,
        grid=(b, l // tm),
        in_specs=[pl.BlockSpec((None, tm, FN_W), lambda bi, i: (bi, i, COL_FN // FN_W)),
                  pl.BlockSpec((FN_W, 2 * FN_W), lambda bi, i: (0, 0))],
        out_specs=pl.BlockSpec((None, 2, tm, FN_W), lambda bi, i: (bi, 0, i, 0)),
        out_shape=jax.ShapeDtypeStruct((b, 2, l, FN_W), BF16),
        compiler_params=_cp("parallel", "parallel"),
        name="fnet_channels",
    )(h, m1).reshape(b, 2 * l, FN_W)
    return pl.pallas_call(
        _table_mm_kernel,
        grid=(l // tm, b),
        in_specs=[pl.BlockSpec((tm, 2 * l), lambda i, bi: (i, 0)),
                  pl.BlockSpec((None, 2 * l, FN_W), lambda i, bi: (bi, 0, 0))],
        out_specs=pl.BlockSpec((None, tm, FN_W), lambda i, bi: (bi, i, 0)),
        out_shape=jax.ShapeDtypeStruct((b, l, FN_W), BF16),
        compiler_params=_cp("parallel", "arbitrary"),
        name="fnet_positions",
    )(t2, zz)


def _merge_kernel(hy_ref, fn_ref, at_ref, g_ref, x_ref, gm_ref, why_ref, wfn_ref, wat_ref, wout_ref, o_ref):
    d = x_ref.shape[-1]
    g = 1.0 / (1.0 + jnp.exp(-g_ref[...].astype(F32)))
    y = (g[:, :d] * _dot(hy_ref[...], why_ref[...])
         + g[:, d:2 * d] * _dot(fn_ref[...], wfn_ref[...])
         + g[:, 2 * d:] * _dot(at_ref[...], wat_ref[...]))
    mix = _dot(y.astype(BF16), wout_ref[...])
    o_ref[...] = x_ref[...] + gm_ref[...] * mix


def _merge(hyo, fno, att, h, x, mods, layer, row_fn, w_hy, w_fn, w_at, w_out):
    b, l, d = x.shape
    tm = min(l, 512)
    full = lambda a: pl.BlockSpec(a.shape, lambda bi, i: (0,) * a.ndim)
    tok = lambda wdt, blk=0: pl.BlockSpec((None, tm, wdt), lambda bi, i: (bi, i, blk))
    return pl.pallas_call(
        _merge_kernel,
        grid=(b, l // tm),
        in_specs=[tok(HY_W), tok(FN_W), tok(N_HEADS * HEAD_W), tok(3 * d, COL_GATE), tok(d),
                  pl.BlockSpec((None, None, None, 1, d), _mod_spec(layer, 2, lambda bi, i: row_fn(bi))),
                  full(w_hy), full(w_fn), full(w_at), full(w_out)],
        out_specs=tok(d),
        out_shape=jax.ShapeDtypeStruct((b, l, d), F32),
        compiler_params=_cp("parallel", "parallel"),
        name="merge_residual",
    )(hyo, fno, att, h, x, mods, w_hy, w_fn, w_at, w_out)


def _peer_q_kernel(x_ref, g_ref, sh_ref, sc_ref, wqh_ref, wql_ref, kh_ref, kl_ref,
                   s_ref, nt_ref, nh_ref, nl_ref):
    @pl.when(pl.program_id(2) == 0)
    def _():
        n = _modulated_norm(x_ref[...], g_ref[...], sh_ref[...], sc_ref[...])
        hi, lo = _split(n)
        nh_ref[...] = hi
        nl_ref[...] = lo
        nt_ref[...] = n.T.astype(BF16)
    nh = nh_ref[...]
    q_t = _dot_nt(wqh_ref[...], nh) + _dot_nt(wqh_ref[...], nl_ref[...]) + _dot_nt(wql_ref[...], nh)
    qh, ql = _split(q_t)
    s_ref[...] = _dot(kh_ref[...], qh) + _dot(kh_ref[...], ql) + _dot(kl_ref[...], qh)


def _peer_scores(x, g, mods, layer, row_fn, wq_t, keys):
    b, l, d = x.shape
    tt = 256
    nt = l // tt
    nhp = 2 * PEER_HEADS
    dq = wq_t[0].shape[0] // nhp
    mrow = lambda bi, i, hp: row_fn(bi)
    return pl.pallas_call(
        _peer_q_kernel,
        grid=(b, nt, nhp),
        in_specs=[pl.BlockSpec((None, tt, d), lambda bi, i, hp: (bi, i, 0)),
                  pl.BlockSpec((1, d), lambda bi, i, hp: (0, 0)),
                  pl.BlockSpec((None, None, None, 1, d), _mod_spec(layer, 3, mrow)),
                  pl.BlockSpec((None, None, None, 1, d), _mod_spec(layer, 4, mrow)),
                  pl.BlockSpec((dq, d), lambda bi, i, hp: (hp, 0)),
                  pl.BlockSpec((dq, d), lambda bi, i, hp: (hp, 0)),
                  pl.BlockSpec((None, PEER_NKEYS, dq), lambda bi, i, hp: (hp, 0, 0)),
                  pl.BlockSpec((None, PEER_NKEYS, dq), lambda bi, i, hp: (hp, 0, 0))],
        out_specs=[pl.BlockSpec((None, PEER_NKEYS, tt), lambda bi, i, hp: (hp, 0, bi * nt + i)),
                   pl.BlockSpec((d, tt), lambda bi, i, hp: (0, bi * nt + i))],
        out_shape=[jax.ShapeDtypeStruct((nhp, PEER_NKEYS, b * l), F32),
                   jax.ShapeDtypeStruct((d, b * l), BF16)],
        scratch_shapes=[pltpu.VMEM((tt, d), BF16), pltpu.VMEM((tt, d), BF16)],
        compiler_params=_cp("parallel", "parallel", "arbitrary"),
        name="peer_scores",
    )(x, g, mods, mods, wq_t[0], wq_t[1], keys[0], keys[1])


def _top_rows(s, k):
    vals = []
    for r in range(k):
        m = jnp.max(s, axis=0, keepdims=True)
        vals.append(m)
        if r + 1 < k:
            s = jnp.where(s == m, NEG, s)
    return vals


def _peer_expert_kernel(s_ref, nt_ref, u_ref, vt_ref, x_ref, gm_ref, o_ref,
                        tau_ref, ea_ref, eb_ref, p_ref, acc_ref):
    c = pl.program_id(1)
    k = PEER_TOPK

    @pl.when(c == 0)
    def _():
        acc_ref[...] = jnp.zeros_like(acc_ref)
        for h in range(PEER_HEADS):
            s1 = s_ref[2 * h]
            s2 = s_ref[2 * h + 1]
            a = _top_rows(s1, k)
            b = _top_rows(s2, k)
            cand = [a[i] + b[j] for i in range(k) for j in range(k // (i + 1))]
            pad = (-len(cand)) % 8
            cand = jnp.concatenate(cand + [jnp.full_like(a[0], NEG)] * pad, axis=0)
            top = _top_rows(cand, k)
            z = jnp.zeros_like(top[0])
            for t in top:
                z = z + jnp.exp(t - top[0])
            tau_ref[h] = top[k - 1]
            ea_ref[h] = jnp.exp(s1 - a[0])
            eb_ref[h] = jnp.exp(s2 - b[0]) / z

    hid = _dot(u_ref[...], nt_ref[...])
    act = 0.5 * hid * (1.0 + lax.erf(hid * (2.0 ** -0.5)))
    rows_per = PEER_NKEYS
    n_i = u_ref.shape[0] // rows_per
    for ii in range(n_i):
        i = c * n_i + ii
        gate = jnp.zeros((rows_per, hid.shape[1]), F32)
        for h in range(PEER_HEADS):
            sums = s_ref[2 * h, pl.ds(i, 1), :] + s_ref[2 * h + 1]
            w = ea_ref[h, pl.ds(i, 1), :] * eb_ref[h]
            gate = gate + jnp.where(sums >= tau_ref[h], w, 0.0)
        p_ref[ii * rows_per:(ii + 1) * rows_per, :] = (
            gate * act[ii * rows_per:(ii + 1) * rows_per, :]).astype(BF16)
    acc_ref[...] += _dot(vt_ref[...], p_ref[...])

    @pl.when(c == pl.num_programs(1) - 1)
    def _():
        o_ref[...] = x_ref[...] + gm_ref[...] * acc_ref[...].T


def _peer_experts(scores, n_t, u, v_t, x, mods, layer, row_fn):
    b, l, d = x.shape
    tt = 256
    nt = l // tt
    ec = 1024
    nc = u.shape[0] // ec
    nhp = scores.shape[0]
    return pl.pallas_call(
        _peer_expert_kernel,
        grid=(b * nt, nc),
        in_specs=[pl.BlockSpec((nhp, PEER_NKEYS, tt), lambda t, c: (0, 0, t)),
                  pl.BlockSpec((d, tt), lambda t, c: (0, t)),
                  pl.BlockSpec((ec, d), lambda t, c: (c, 0)),
                  pl.BlockSpec((d, ec), lambda t, c: (0, c)),
                  pl.BlockSpec((None, tt, d), lambda t, c: (t // nt, t % nt, 0)),
                  pl.BlockSpec((None, None, None, 1, d), _mod_spec(layer, 5, lambda t, c: row_fn(t // nt)))],
        out_specs=pl.BlockSpec((None, tt, d), lambda t, c: (t // nt, t % nt, 0)),
        out_shape=jax.ShapeDtypeStruct((b, l, d), F32),
        scratch_shapes=[pltpu.VMEM((PEER_HEADS, 1, tt), F32),
                        pltpu.VMEM((PEER_HEADS, PEER_NKEYS, tt), F32),
                        pltpu.VMEM((PEER_HEADS, PEER_NKEYS, tt), F32),
                        pltpu.VMEM((ec, tt), BF16),
                        pltpu.VMEM((d, tt), F32)],
        compiler_params=_cp("parallel", "arbitrary"),
        name="peer_experts",
    )(scores, n_t, u, v_t, x, mods)


def _peer(x, g, mods, layer, row_fn, wq_t, keys, u, v_t):
    scores, n_t = _peer_scores(x, g, mods, layer, row_fn, wq_t, keys)
    return _peer_experts(scores, n_t, u, v_t, x, mods, layer, row_fn)


def _dft_tables(l):
    n = 2 * l
    r = jnp.arange(n, dtype=jnp.int32)[:, None]
    t = jnp.arange(l, dtype=jnp.int32)[None, :]
    f = jnp.where(r < l, r, r - l)
    ang = (2.0 * math.pi) * (((f * t) % n).astype(F32) / n)
    tab = jnp.where(r < l, jnp.cos(ang), jnp.sin(ang))
    tab = jnp.where(r == l, jnp.where(t % 2 == 0, 1.0, -1.0), tab)
    ffwd = tab.astype(BF16)
    return ffwd, ffwd.T


def _fnet_tables(l):
    f = jnp.arange(l, dtype=jnp.int32)[:, None]
    t = jnp.arange(l, dtype=jnp.int32)[None, :]
    ang = (2.0 * math.pi) * (((f * t) % l).astype(F32) / l)
    t2 = jnp.concatenate([jnp.cos(ang), jnp.sin(ang)], axis=1).astype(BF16)
    k = np.arange(FN_GROUP)
    ang64 = 2.0 * np.pi * ((k[:, None] * k[None, :]) % FN_GROUP) / FN_GROUP
    eye = np.eye(FN_W // FN_GROUP)
    scale = 1.0 / math.sqrt(FN_GROUP * l)
    m1 = np.concatenate([np.kron(eye, np.cos(ang64)), -np.kron(eye, np.sin(ang64))], axis=1) * scale
    return jnp.asarray(m1, F32).astype(BF16), t2


def _rope_tables(l):
    rows = l // GRID_W
    row = jnp.repeat(jnp.arange(rows), GRID_W).astype(F32)
    col = jnp.tile(jnp.arange(GRID_W), rows).astype(F32)
    half = QK_DIM // 2
    inv = ROPE_BASE ** (-jnp.arange(0, half, 2, dtype=F32) / half)
    ang = jnp.stack([row[:, None] * inv, col[:, None] * inv], axis=1)
    cos = jnp.repeat(jnp.cos(ang)[:, :, None, :], 2, axis=2)
    sin = jnp.sin(ang)
    sin = jnp.stack([-sin, sin], axis=2)
    rep = lambda a: jnp.tile(a.reshape(l, QK_DIM), (1, 2 * N_HEADS))
    return rep(cos), rep(sin)


def _filter_features(l):
    pos = jnp.arange(l, dtype=F32)
    t = pos / max(l - 1, 1)
    w = 2.0 * math.pi * pos / l
    f = jnp.linspace(1e-4, HY_BANDS - 1, HY_BANDS, dtype=F32)
    feats = jnp.concatenate([t[:, None], jnp.cos(w[:, None] * f), -jnp.sin(w[:, None] * f)], axis=-1)
    feats = jnp.pad(feats, ((0, 0), (0, 64 - HY_EMB)))
    deltas = jnp.abs(jnp.linspace(HY_MIN_DECAY, HY_MAX_DECAY, HY_W, dtype=F32))
    dec = jnp.exp(-t[:, None] * deltas)
    return feats, jnp.tile(dec, (1, HY_ORDER))


def _hyena_filters(l, tabs, ffwd, w1, b1, freq, w2, b2, w3):
    feats, dec = tabs
    taps = _hyena_filter_taps(feats, jnp.pad(w1, ((0, 64 - HY_EMB), (0, 0))), b1[None], freq[None],
                              w2, b2[None], w3, dec)
    kf = _table_matmul(ffwd, taps, F32)
    half = HY_ORDER * HY_W
    kc = kf[:l, :half]
    nyq = kf[l, :half]
    ks = kf[l:, half:]
    n = 2.0 * l
    first = (jnp.arange(l) == 0)[:, None]
    wc = jnp.where(first, 1.0 / n, 2.0 / n)
    ka = kc * wc
    kb = jnp.where(first, 0.0, ks * (2.0 / n))
    kd = jnp.where(first, nyq[None, :] / n, kc * (2.0 / n))
    stack = jnp.stack([ka, kb, kb, kd], axis=0)
    return jnp.moveaxis(stack.reshape(4, l, HY_ORDER, HY_W), 2, 0)


def kernel(x, c, ctx, c_ctx, w_ada, b_ada, g_mix, g_ffn, w_in, hy_conv_w, hy_conv_b, hy_w1, hy_b1, hy_freq, hy_w2, hy_b2, hy_w3, hy_bias, g_q, g_k, lam, g_sub, w_hy, w_fn, w_at, w_out, peer_wq, peer_keys, peer_u, peer_v):
    bsz, seq, d = x.shape
    clen = ctx.shape[1]
    depth = w_ada.shape[0]

    cc = jnp.concatenate([c, c_ctx[None], jnp.zeros((MOD_ROWS - bsz - 1, d), F32)], axis=0)
    mods = _ada_mods(cc, w_ada, b_ada)
    lat_row = lambda bi: bi
    ctx_row = lambda bi: bsz

    rope = _rope_tables(seq)
    tabs = {n: dict(dft=_dft_tables(n), fnet=_fnet_tables(n), feat=_filter_features(n)) for n in (seq, clen)}
    w = N_HEADS * HEAD_W
    lane = np.arange(w)
    bd = jnp.asarray((lane[:, None] // QK_DIM == lane[None, :] // QK_DIM) / QK_DIM, F32).astype(BF16)

    xl, xc = x, ctx
    for l in range(depth):
        last = l == depth - 1
        lam_init = 0.8 - 0.6 * math.exp(-0.3 * l)
        wl = w_in[l]
        w_perm = jnp.concatenate([wl[:, 2560:], wl[:, :2560]], axis=1).astype(BF16)
        gq = jnp.tile(g_q[l].reshape(1, HEAD_W), (1, N_HEADS)) * (QK_DIM ** -0.5)
        gk = jnp.tile(g_k[l].reshape(1, HEAD_W), (1, N_HEADS))
        gsub = g_sub[l][None]
        wts = [a[l].astype(BF16) for a in (w_hy, w_fn, w_at, w_out)]
        filt_args = (hy_w1[l], hy_b1[l], hy_freq[l], hy_w2[l], hy_b2[l], hy_w3[l])

        h_l = _in_projection(xl, g_mix[l][None], mods, l, lat_row, w_perm, 0, P_IN)
        h_c = _in_projection(xc, g_mix[l][None], mods, l, ctx_row, w_perm, COL_K if last else 0, P_IN)
        q_l, k_l = _qk_prep(h_l, gq, gk, bd, rope)
        q_c, k_c = _qk_prep(h_c, gq, gk, bd, None)
        vblk = COL_V // HEAD_W
        att_l = _diff_attention(q_l, [(k_l, h_l, vblk), (k_c, h_c, vblk)], lam[l], gsub, lam_init)
        ffwd, finv = tabs[seq]["dft"]
        kf = _hyena_filters(seq, tabs[seq]["feat"], ffwd, *filt_args)
        hyo_l = _hyena_mix(h_l, hy_conv_w[l], hy_conv_b[l][None], kf, hy_bias[l], ffwd, finv)
        fno_l = _fourier_mix(h_l, *tabs[seq]["fnet"])
        if not last:
            att_c = _diff_attention(q_c, [(k_c, h_c, vblk)], lam[l], gsub, lam_init)
            ffwd_c, finv_c = tabs[clen]["dft"]
            kf_c = _hyena_filters(clen, tabs[clen]["feat"], ffwd_c, *filt_args)
            hyo_c = _hyena_mix(h_c, hy_conv_w[l], hy_conv_b[l][None], kf_c, hy_bias[l], ffwd_c, finv_c)
            fno_c = _fourier_mix(h_c, *tabs[clen]["fnet"])
            xc = _merge(hyo_c, fno_c, att_c, h_c, xc, mods, l, ctx_row, *wts)
        xl = _merge(hyo_l, fno_l, att_l, h_l, xl, mods, l, lat_row, *wts)

        wq_t = _split(peer_wq[l].T)
        keys = _split(peer_keys[l].reshape(2 * PEER_HEADS, PEER_NKEYS, -1))
        u = peer_u[l].astype(BF16)
        v_t = peer_v[l].T.astype(BF16)
        if not last:
            xc = _peer(xc, g_ffn[l][None], mods, l, ctx_row, wq_t, keys, u, v_t)
        xl = _peer(xl, g_ffn[l][None], mods, l, lat_row, wq_t, keys, u, v_t)
    return xl
```

```python
import functools
import math

import jax
import jax.numpy as jnp
import numpy as np
from jax import lax
from jax.experimental import pallas as pl
from jax.experimental.pallas import tpu as pltpu

F32 = jnp.float32
BF16 = jnp.bfloat16

EPS = 1e-6
GRID_W = 64
ROPE_BASE = 10000.0
N_HEADS = 4
QK_DIM = 64
HEAD_W = 2 * QK_DIM
HY_W = 256
HY_ORDER = 2
HY_EMB = 33
HY_BANDS = (HY_EMB - 1) // 2
HY_MIN_DECAY = math.log(1e-2) / 1.5
HY_MAX_DECAY = math.log(1e-2) / 0.3
FN_GROUP = 64
FN_W = 256
PEER_HEADS = 8
PEER_NKEYS = 128
PEER_TOPK = 16
N_MOD = 6
MOD_ROWS = 16
NEG = -3.0e38

VMEM_LIMIT = 56 * 1024 * 1024

COL_GATE = 0
COL_HY = 3072
COL_FN = 3840
COL_Q = 4096
COL_K = 4608
COL_V = 5120
P_IN = 5632


def _cp(*sem):
    return pltpu.CompilerParams(dimension_semantics=sem, vmem_limit_bytes=VMEM_LIMIT)


def _dot(a, b):
    return jnp.dot(a, b, preferred_element_type=F32)


def _dot_nt(a, b):
    return lax.dot_general(a, b, (((1,), (1,)), ((), ())), preferred_element_type=F32)


def _split(a):
    hi = a.astype(BF16)
    lo = (a - hi.astype(F32)).astype(BF16)
    return hi, lo


def _modulated_norm(x, g, shift, scale):
    ms = jnp.mean(x * x, axis=-1, keepdims=True)
    y = x * lax.rsqrt(ms + EPS) * g
    return y * (1.0 + scale) + shift


def _mod_spec(layer, chunk, row_fn):
    def imap(*idx):
        return (layer, row_fn(*idx), chunk, 0, 0)
    return imap


def _ada_kernel(c_ref, w_ref, b_ref, o_ref):
    c = c_ref[...]
    a = c / (1.0 + jnp.exp(-c))
    ah, al = _split(a)
    wh, wl = _split(w_ref[...])
    o_ref[...] = _dot(ah, wh) + _dot(ah, wl) + _dot(al, wh) + b_ref[...]


def _ada_mods(cc, w_ada, b_ada):
    depth, d, n = w_ada.shape
    tn = 512
    out = pl.pallas_call(
        _ada_kernel,
        grid=(depth, n // tn),
        in_specs=[pl.BlockSpec((MOD_ROWS, d), lambda l, j: (0, 0)),
                  pl.BlockSpec((None, d, tn), lambda l, j: (l, 0, j)),
                  pl.BlockSpec((None, 1, tn), lambda l, j: (l, 0, j))],
        out_specs=pl.BlockSpec((None, MOD_ROWS, tn), lambda l, j: (l, 0, j)),
        out_shape=jax.ShapeDtypeStruct((depth, MOD_ROWS, n), F32),
        compiler_params=_cp("parallel", "parallel"),
        name="ada_mods",
    )(cc, w_ada, b_ada.reshape(depth, 1, n))
    return out.reshape(depth, MOD_ROWS, N_MOD, 1, d)


def _inproj_kernel(x_ref, g_ref, sh_ref, sc_ref, w_ref, o_ref, xn_ref):
    @pl.when(pl.program_id(2) == 0)
    def _():
        xn_ref[...] = _modulated_norm(x_ref[...], g_ref[...], sh_ref[...], sc_ref[...]).astype(BF16)
    o_ref[...] = _dot(xn_ref[...], w_ref[...]).astype(o_ref.dtype)


def _in_projection(x, g, mods, layer, row_fn, w, col_lo, col_hi):
    b, l, d = x.shape
    tm = min(l, 1024)
    tn = 512
    j0 = col_lo // tn
    nj = (col_hi - col_lo) // tn
    mrow = lambda bi, i, j: row_fn(bi)
    return pl.pallas_call(
        _inproj_kernel,
        grid=(b, l // tm, nj),
        in_specs=[pl.BlockSpec((None, tm, d), lambda bi, i, j: (bi, i, 0)),
                  pl.BlockSpec((1, d), lambda bi, i, j: (0, 0)),
                  pl.BlockSpec((None, None, None, 1, d), _mod_spec(layer, 0, mrow)),
                  pl.BlockSpec((None, None, None, 1, d), _mod_spec(layer, 1, mrow)),
                  pl.BlockSpec((d, tn), lambda bi, i, j: (0, j + j0))],
        out_specs=pl.BlockSpec((None, tm, tn), lambda bi, i, j: (bi, i, j + j0)),
        out_shape=jax.ShapeDtypeStruct((b, l, w.shape[1]), BF16),
        scratch_shapes=[pltpu.VMEM((tm, d), BF16)],
        compiler_params=_cp("parallel", "parallel", "arbitrary"),
        name="in_projection",
    )(x, g, mods, mods, w)


def _qkprep_kernel(*refs, rope):
    if rope:
        q_ref, k_ref, gq_ref, gk_ref, bd_ref, cos_ref, sin_ref, qo_ref, ko_ref = refs
    else:
        q_ref, k_ref, gq_ref, gk_ref, bd_ref, qo_ref, ko_ref = refs
    for src, g_ref, dst in ((q_ref, gq_ref, qo_ref), (k_ref, gk_ref, ko_ref)):
        x = src[...].astype(F32)
        hi, lo = _split(x * x)
        ms = _dot(hi, bd_ref[...]) + _dot(lo, bd_ref[...])
        y = x * lax.rsqrt(ms + EPS) * g_ref[...]
        if rope:
            w = y.shape[1]
            lane = lax.broadcasted_iota(jnp.int32, y.shape, 1)
            first = (lane % (QK_DIM // 2)) < (QK_DIM // 4)
            partner = jnp.where(first, pltpu.roll(y, w - QK_DIM // 4, 1), pltpu.roll(y, QK_DIM // 4, 1))
            y = y * cos_ref[...] + partner * sin_ref[...]
        dst[...] = y.astype(BF16)


def _qk_prep(h, gq, gk, bd, rope_tabs):
    b, l, _ = h.shape
    w = N_HEADS * HEAD_W
    tm = min(l, 512)
    rope = rope_tabs is not None
    in_specs = [pl.BlockSpec((None, tm, w), lambda bi, i: (bi, i, COL_Q // w)),
                pl.BlockSpec((None, tm, w), lambda bi, i: (bi, i, COL_K // w)),
                pl.BlockSpec((1, w), lambda bi, i: (0, 0)),
                pl.BlockSpec((1, w), lambda bi, i: (0, 0)),
                pl.BlockSpec((w, w), lambda bi, i: (0, 0))]
    args = [h, h, gq, gk, bd]
    if rope:
        in_specs += [pl.BlockSpec((tm, w), lambda bi, i: (i, 0))] * 2
        args += list(rope_tabs)
    return pl.pallas_call(
        functools.partial(_qkprep_kernel, rope=rope),
        grid=(b, l // tm),
        in_specs=in_specs,
        out_specs=[pl.BlockSpec((None, tm, w), lambda bi, i: (bi, i, 0))] * 2,
        out_shape=[jax.ShapeDtypeStruct((b, l, w), BF16)] * 2,
        compiler_params=_cp("parallel", "parallel"),
        name="qk_prep",
    )(*args)


def _attn_kernel(*refs, n_src, lam_init):
    q_ref = refs[0]
    kv = refs[1:1 + 2 * n_src]
    lam_ref, gsub_ref, o_ref = refs[1 + 2 * n_src:]
    q = q_ref[...]
    tq = q.shape[0]
    lane = lax.broadcasted_iota(jnp.int32, q.shape, 1)
    zero = jnp.zeros_like(q)
    qq = jnp.concatenate([jnp.where(lane < QK_DIM, q, zero), jnp.where(lane >= QK_DIM, q, zero)], axis=0)
    scores = [_dot_nt(qq, kv[2 * i][...]) for i in range(n_src)]
    m = jnp.max(scores[0], axis=-1, keepdims=True)
    for s in scores[1:]:
        m = jnp.maximum(m, jnp.max(s, axis=-1, keepdims=True))
    z = jnp.zeros_like(m)
    acc = jnp.zeros((2 * tq, HEAD_W), F32)
    for i, s in enumerate(scores):
        e = jnp.exp(s - m)
        z = z + jnp.sum(e, axis=-1, keepdims=True)
        acc = acc + _dot(e.astype(BF16), kv[2 * i + 1][...])
    o2 = acc / z
    lf = lam_ref[...]
    lam_val = (jnp.exp(jnp.sum(lf[0:1] * lf[1:2], axis=-1, keepdims=True))
               - jnp.exp(jnp.sum(lf[2:3] * lf[3:4], axis=-1, keepdims=True)) + lam_init)
    o = o2[:tq] - lam_val * o2[tq:]
    ms = jnp.mean(o * o, axis=-1, keepdims=True)
    o = o * lax.rsqrt(ms + EPS) * gsub_ref[...] * (1.0 - lam_init)
    o_ref[...] = o.astype(BF16)


def _diff_attention(q, sources, lam_l, gsub, lam_init):
    b, lq, w = q.shape
    tq = 256
    in_specs = [pl.BlockSpec((None, tq, HEAD_W), lambda bi, hi, i: (bi, i, hi))]
    args = [q]
    for k, varr, vblk in sources:
        lk = k.shape[1]
        in_specs.append(pl.BlockSpec((None, lk, HEAD_W), lambda bi, hi, i: (bi, 0, hi)))
        in_specs.append(pl.BlockSpec((None, lk, HEAD_W), lambda bi, hi, i, vblk=vblk: (bi, 0, vblk + hi)))
        args += [k, varr]
    in_specs += [pl.BlockSpec(lam_l.shape, lambda bi, hi, i: (0, 0)),
                 pl.BlockSpec((1, HEAD_W), lambda bi, hi, i: (0, 0))]
    args += [lam_l, gsub]
    return pl.pallas_call(
        functools.partial(_attn_kernel, n_src=len(sources), lam_init=lam_init),
        grid=(b, N_HEADS, lq // tq),
        in_specs=in_specs,
        out_specs=pl.BlockSpec((None, tq, HEAD_W), lambda bi, hi, i: (bi, i, hi)),
        out_shape=jax.ShapeDtypeStruct((b, lq, w), BF16),
        compiler_params=_cp("parallel", "parallel", "arbitrary"),
        name="diff_attention",
    )(*args)


def _sconv_kernel(h_ref, w_ref, b_ref, o_ref):
    x = h_ref[...].astype(F32)
    n = x.shape[0]
    row = lax.broadcasted_iota(jnp.int32, x.shape, 0)
    prev = jnp.where(row == 0, 0.0, pltpu.roll(x, 1, 0))
    nxt = jnp.where(row == n - 1, 0.0, pltpu.roll(x, n - 1, 0))
    o_ref[...] = prev * w_ref[0:1, :] + x * w_ref[1:2, :] + nxt * w_ref[2:3, :] + b_ref[...]


def _short_conv(h, w, bias):
    b, l, _ = h.shape
    return pl.pallas_call(
        _sconv_kernel,
        grid=(b, 3),
        in_specs=[pl.BlockSpec((None, l, HY_W), lambda bi, j: (bi, 0, COL_HY // HY_W + j)),
                  pl.BlockSpec((3, HY_W), lambda bi, j: (0, j)),
                  pl.BlockSpec((1, HY_W), lambda bi, j: (0, j))],
        out_specs=pl.BlockSpec((None, None, l, HY_W), lambda bi, j: (bi, j, 0, 0)),
        out_shape=jax.ShapeDtypeStruct((b, 3, l, HY_W), F32),
        compiler_params=_cp("parallel", "parallel"),
        name="short_conv",
    )(h, w, bias)


def _filter_kernel(f_ref, w1_ref, b1_ref, fr_ref, w2_ref, b2_ref, w3_ref, dec_ref, o_ref):
    def mm(a, w_ref_):
        ah, al = _split(a)
        wh, wl = _split(w_ref_[...])
        return _dot(ah, wh) + _dot(ah, wl) + _dot(al, wh)
    fr = fr_ref[...]
    h = jnp.sin(fr * (mm(f_ref[...], w1_ref) + b1_ref[...]))
    h = jnp.sin(fr * (mm(h, w2_ref) + b2_ref[...]))
    h = mm(h, w3_ref)
    half = h.shape[1] // 2
    dec = dec_ref[...]
    hf = h[:, :half] * dec
    hb = h[:, half:] * dec
    row = lax.broadcasted_iota(jnp.int32, hb.shape, 0)
    hb = jnp.where(row == 0, 0.0, hb)
    norm = jnp.sum(jnp.abs(hf) + jnp.abs(hb), axis=0, keepdims=True)
    o_ref[:, :half] = (hf + hb) / norm
    o_ref[:, half:] = (hf - hb) / norm


def _hyena_filter_taps(feats, w1, b1, freq, w2, b2, w3, dec):
    l = feats.shape[0]
    n = w3.shape[1]
    full = lambda a: pl.BlockSpec(a.shape, lambda i: (0,) * a.ndim)
    args = (feats, w1, b1, freq, w2, b2, w3, dec)
    return pl.pallas_call(
        _filter_kernel,
        grid=(1,),
        in_specs=[full(a) for a in args],
        out_specs=pl.BlockSpec((l, n), lambda i: (0, 0)),
        out_shape=jax.ShapeDtypeStruct((l, n), F32),
        compiler_params=_cp("arbitrary"),
        name="hyena_filter_taps",
    )(*args)


def _table_mm_kernel(t_ref, x_ref, o_ref):
    o_ref[...] = _dot(t_ref[...], x_ref[...].astype(BF16)).astype(o_ref.dtype)


def _table_matmul(table, x, out_dtype):
    m, k = table.shape
    n = x.shape[1]
    tm = min(m, 512)
    return pl.pallas_call(
        _table_mm_kernel,
        grid=(m // tm,),
        in_specs=[pl.BlockSpec((tm, k), lambda i: (i, 0)),
                  pl.BlockSpec((k, n), lambda i: (0, 0))],
        out_specs=pl.BlockSpec((tm, n), lambda i: (i, 0)),
        out_shape=jax.ShapeDtypeStruct((m, n), out_dtype),
        compiler_params=_cp("parallel"),
        name="table_matmul",
    )(table, x)


def _dftmul_kernel(fc_ref, fs_ref, z_ref, k_ref, p_ref):
    z = z_ref[...].astype(BF16)
    zc = _dot(fc_ref[...], z)
    zs = _dot(fs_ref[...], z)
    p_ref[0] = (zc * k_ref[0] - zs * k_ref[1]).astype(BF16)
    p_ref[1] = (zc * k_ref[2] + zs * k_ref[3]).astype(BF16)


def _dft_multiply(ffwd, z, z_spec, kf):
    l = ffwd.shape[1]
    b = z.shape[0]
    tf = min(l, 512)
    nf = l // tf
    out = pl.pallas_call(
        _dftmul_kernel,
        grid=(nf, b),
        in_specs=[pl.BlockSpec((tf, l), lambda i, bi: (i, 0)),
                  pl.BlockSpec((tf, l), lambda i, bi: (i + nf, 0)),
                  z_spec,
                  pl.BlockSpec((4, tf, HY_W), lambda i, bi: (0, i, 0))],
        out_specs=pl.BlockSpec((None, 2, tf, HY_W), lambda i, bi: (bi, 0, i, 0)),
        out_shape=jax.ShapeDtypeStruct((b, 2, l, HY_W), BF16),
        compiler_params=_cp("parallel", "arbitrary"),
        name="dft_multiply",
    )(ffwd, ffwd, z, kf)
    return out.reshape(b, 2 * l, HY_W)


def _idft_gate_kernel(fi_ref, p_ref, g_ref, z_ref, b_ref, o_ref):
    conv = _dot(fi_ref[...], p_ref[...])
    o_ref[...] = (g_ref[...] * (conv + b_ref[...] * z_ref[...])).astype(o_ref.dtype)


def _idft_gate(finv, p, gate, gate_spec, z, z_spec, bias, out_dtype):
    l = finv.shape[0]
    b = p.shape[0]
    tt = min(l, 512)
    return pl.pallas_call(
        _idft_gate_kernel,
        grid=(l // tt, b),
        in_specs=[pl.BlockSpec((tt, 2 * l), lambda i, bi: (i, 0)),
                  pl.BlockSpec((None, 2 * l, HY_W), lambda i, bi: (bi, 0, 0)),
                  gate_spec, z_spec,
                  pl.BlockSpec((1, HY_W), lambda i, bi: (0, 0))],
        out_specs=pl.BlockSpec((None, tt, HY_W), lambda i, bi: (bi, i, 0)),
        out_shape=jax.ShapeDtypeStruct((b, l, HY_W), out_dtype),
        compiler_params=_cp("parallel", "arbitrary"),
        name="idft_gate",
    )(finv, p, gate, z, bias)


def _hyena_mix(h, conv_w, conv_b, kf, hy_bias, ffwd, finv):
    b, l, _ = h.shape
    tt = min(l, 512)
    u = _short_conv(h, conv_w, conv_b)
    part = lambda j, rows: pl.BlockSpec((None, None, rows, HY_W),
                                        lambda i, bi, j=j: (bi, j, i if rows != l else 0, 0))
    p = _dft_multiply(ffwd, u, part(0, l), kf[0])
    z1 = _idft_gate(finv, p, u, part(1, tt), u, part(0, tt), hy_bias[0:1], F32)
    p = _dft_multiply(ffwd, z1, pl.BlockSpec((None, l, HY_W), lambda i, bi: (bi, 0, 0)), kf[1])
    return _idft_gate(finv, p, u, part(2, tt), z1,
                      pl.BlockSpec((None, tt, HY_W), lambda i, bi: (bi, i, 0)), hy_bias[1:2], BF16)


def _fn1_kernel(z_ref, m_ref, o_ref):
    r = _dot(z_ref[...], m_ref[...])
    half = r.shape[1] // 2
    o_ref[0] = r[:, :half].astype(BF16)
    o_ref[1] = r[:, half:].astype(BF16)


def _fourier_mix(h, m1, t2):
    b, l, _ = h.shape
    tm = min(l, 512)
    zz = pl.pallas_call(
        _fn1_kernel,
        grid=(b, l // tm),
        in_specs=[pl.BlockSpec((None, tm, FN_W), lambda bi, i: (bi, i, COL_FN // FN_W)),
                  pl.BlockSpec((FN_W, 2 * FN_W), lambda bi, i: (0, 0))],
        out_specs=pl.BlockSpec((None, 2, tm, FN_W), lambda bi, i: (bi, 0, i, 0)),
        out_shape=jax.ShapeDtypeStruct((b, 2, l, FN_W), BF16),
        compiler_params=_cp("parallel", "parallel"),
        name="fnet_channels",
    )(h, m1).reshape(b, 2 * l, FN_W)
    return pl.pallas_call(
        _table_mm_kernel,
        grid=(l // tm, b),
        in_specs=[pl.BlockSpec((tm, 2 * l), lambda i, bi: (i, 0)),
                  pl.BlockSpec((None, 2 * l, FN_W), lambda i, bi: (bi, 0, 0))],
        out_specs=pl.BlockSpec((None, tm, FN_W), lambda i, bi: (bi, i, 0)),
        out_shape=jax.ShapeDtypeStruct((b, l, FN_W), BF16),
        compiler_params=_cp("parallel", "arbitrary"),
        name="fnet_positions",
    )(t2, zz)


def _merge_kernel(hy_ref, fn_ref, at_ref, g_ref, x_ref, gm_ref, why_ref, wfn_ref, wat_ref, wout_ref, o_ref):
    d = x_ref.shape[-1]
    g = 1.0 / (1.0 + jnp.exp(-g_ref[...].astype(F32)))
    y = (g[:, :d] * _dot(hy_ref[...], why_ref[...])
         + g[:, d:2 * d] * _dot(fn_ref[...], wfn_ref[...])
         + g[:, 2 * d:] * _dot(at_ref[...], wat_ref[...]))
    mix = _dot(y.astype(BF16), wout_ref[...])
    o_ref[...] = x_ref[...] + gm_ref[...] * mix


def _merge(hyo, fno, att, h, x, mods, layer, row_fn, w_hy, w_fn, w_at, w_out):
    b, l, d = x.shape
    tm = min(l, 512)
    full = lambda a: pl.BlockSpec(a.shape, lambda bi, i: (0,) * a.ndim)
    tok = lambda wdt, blk=0: pl.BlockSpec((None, tm, wdt), lambda bi, i: (bi, i, blk))
    return pl.pallas_call(
        _merge_kernel,
        grid=(b, l // tm),
        in_specs=[tok(HY_W), tok(FN_W), tok(N_HEADS * HEAD_W), tok(3 * d, COL_GATE), tok(d),
                  pl.BlockSpec((None, None, None, 1, d), _mod_spec(layer, 2, lambda bi, i: row_fn(bi))),
                  full(w_hy), full(w_fn), full(w_at), full(w_out)],
        out_specs=tok(d),
        out_shape=jax.ShapeDtypeStruct((b, l, d), F32),
        compiler_params=_cp("parallel", "parallel"),
        name="merge_residual",
    )(hyo, fno, att, h, x, mods, w_hy, w_fn, w_at, w_out)


def _peer_q_kernel(x_ref, g_ref, sh_ref, sc_ref, wq_ref, kh_ref, kl_ref, s_ref, nt_ref):
    n = _modulated_norm(x_ref[...], g_ref[...], sh_ref[...], sc_ref[...])
    nt_ref[...] = n.T.astype(BF16)
    q = _dot(n.astype(BF16), wq_ref[...])
    dq = kh_ref.shape[1]
    for hp in range(kh_ref.shape[0]):
        qh, ql = _split(q[:, hp * dq:(hp + 1) * dq])
        kh = kh_ref[hp]
        s = _dot(qh, kh) + _dot(ql, kh) + _dot(qh, kl_ref[hp])
        s_ref[hp] = s.T


def _peer_scores(x, g, mods, layer, row_fn, wq, keys_t):
    b, l, d = x.shape
    tt = min(l, 512)
    nt = l // tt
    nhp = 2 * PEER_HEADS
    mrow = lambda bi, i: row_fn(bi)
    full = lambda a: pl.BlockSpec(a.shape, lambda bi, i: (0,) * a.ndim)
    return pl.pallas_call(
        _peer_q_kernel,
        grid=(b, nt),
        in_specs=[pl.BlockSpec((None, tt, d), lambda bi, i: (bi, i, 0)),
                  pl.BlockSpec((1, d), lambda bi, i: (0, 0)),
                  pl.BlockSpec((None, None, None, 1, d), _mod_spec(layer, 3, mrow)),
                  pl.BlockSpec((None, None, None, 1, d), _mod_spec(layer, 4, mrow)),
                  full(wq), full(keys_t[0]), full(keys_t[1])],
        out_specs=[pl.BlockSpec((nhp, PEER_NKEYS, tt), lambda bi, i: (0, 0, bi * nt + i)),
                   pl.BlockSpec((d, tt), lambda bi, i: (0, bi * nt + i))],
        out_shape=[jax.ShapeDtypeStruct((nhp, PEER_NKEYS, b * l), F32),
                   jax.ShapeDtypeStruct((d, b * l), BF16)],
        compiler_params=_cp("parallel", "parallel"),
        name="peer_scores",
    )(x, g, mods, mods, wq, keys_t[0], keys_t[1])


def _top_rows(s, k):
    vals = []
    for r in range(k):
        m = jnp.max(s, axis=0, keepdims=True)
        vals.append(m)
        if r + 1 < k:
            s = jnp.where(s == m, NEG, s)
    return vals


def _gelu(x):
    ax = jnp.abs(x) * (2.0 ** -0.5)
    t = 1.0 / (1.0 + 0.3275911 * ax)
    half_poly = t * (0.127414796 + t * (-0.142248368 + t * (0.7107068705 + t * (-0.7265760135 + t * 0.5307027145))))
    q = half_poly * jnp.exp(-(ax * ax))
    return x * jnp.where(x < 0.0, q, 1.0 - q)


def _peer_expert_kernel(s_ref, nt_ref, u_ref, vt_ref, x_ref, gm_ref, o_ref,
                        tau_ref, ea_ref, eb_ref, hid_ref, p_ref, acc_ref):
    c = pl.program_id(1)
    k = PEER_TOPK
    tt = nt_ref.shape[1]
    lanes = 128

    @pl.when(c == 0)
    def _():
        acc_ref[...] = jnp.zeros_like(acc_ref)
        for h in range(PEER_HEADS):
            for lt in range(tt // lanes):
                cols = slice(lt * lanes, (lt + 1) * lanes)
                s1 = s_ref[2 * h, :, cols]
                s2 = s_ref[2 * h + 1, :, cols]
                a = _top_rows(s1, k)
                b = _top_rows(s2, k)
                cand = [a[i] + b[j] for i in range(k) for j in range(k // (i + 1))]
                pad = (-len(cand)) % 8
                cand = jnp.concatenate(cand + [jnp.full_like(a[0], NEG)] * pad, axis=0)
                top = _top_rows(cand, k)
                z = jnp.zeros_like(top[0])
                for t in top:
                    z = z + jnp.exp(t - top[0])
                tau_ref[h, :, cols] = top[k - 1]
                ea_ref[h, :, cols] = jnp.exp(s1 - a[0])
                eb_ref[h, :, cols] = jnp.exp(s2 - b[0]) / z

    hid_ref[...] = _dot(u_ref[...], nt_ref[...])
    n_i = u_ref.shape[0] // PEER_NKEYS
    piece = 32
    for ii in range(n_i):
        i = c * n_i + ii
        s1_row = [s_ref[2 * h, pl.ds(i, 1), :] for h in range(PEER_HEADS)]
        ea_row = [ea_ref[h, pl.ds(i, 1), :] for h in range(PEER_HEADS)]
        for jp in range(PEER_NKEYS // piece):
            js = slice(jp * piece, (jp + 1) * piece)
            gate = jnp.zeros((piece, tt), F32)
            for h in range(PEER_HEADS):
                sums = s1_row[h] + s_ref[2 * h + 1, js, :]
                w = ea_row[h] * eb_ref[h, js, :]
                gate = gate + jnp.where(sums >= tau_ref[h], w, 0.0)
            rows = slice(ii * PEER_NKEYS + jp * piece, ii * PEER_NKEYS + (jp + 1) * piece)
            p_ref[rows, :] = (gate * _gelu(hid_ref[rows, :])).astype(BF16)
    acc_ref[...] += _dot(vt_ref[...], p_ref[...])

    @pl.when(c == pl.num_programs(1) - 1)
    def _():
        o_ref[...] = x_ref[...] + gm_ref[...] * acc_ref[...].T


def _peer_experts(scores, n_t, u, v_t, x, mods, layer, row_fn):
    b, l, d = x.shape
    tt = 256
    nt = l // tt
    ec = 1024
    nc = u.shape[0] // ec
    nhp = scores.shape[0]
    return pl.pallas_call(
        _peer_expert_kernel,
        grid=(b * nt, nc),
        in_specs=[pl.BlockSpec((nhp, PEER_NKEYS, tt), lambda t, c: (0, 0, t)),
                  pl.BlockSpec((d, tt), lambda t, c: (0, t)),
                  pl.BlockSpec((ec, d), lambda t, c: (c, 0)),
                  pl.BlockSpec((d, ec), lambda t, c: (0, c)),
                  pl.BlockSpec((None, tt, d), lambda t, c: (t // nt, t % nt, 0)),
                  pl.BlockSpec((None, None, None, 1, d), _mod_spec(layer, 5, lambda t, c: row_fn(t // nt)))],
        out_specs=pl.BlockSpec((None, tt, d), lambda t, c: (t // nt, t % nt, 0)),
        out_shape=jax.ShapeDtypeStruct((b, l, d), F32),
        scratch_shapes=[pltpu.VMEM((PEER_HEADS, 1, tt), F32),
                        pltpu.VMEM((PEER_HEADS, PEER_NKEYS, tt), F32),
                        pltpu.VMEM((PEER_HEADS, PEER_NKEYS, tt), F32),
                        pltpu.VMEM((ec, tt), F32),
                        pltpu.VMEM((ec, tt), BF16),
                        pltpu.VMEM((d, tt), F32)],
        compiler_params=_cp("parallel", "arbitrary"),
        name="peer_experts",
    )(scores, n_t, u, v_t, x, mods)


def _peer(x, g, mods, layer, row_fn, wq_t, keys, u, v_t):
    scores, n_t = _peer_scores(x, g, mods, layer, row_fn, wq_t, keys)
    return _peer_experts(scores, n_t, u, v_t, x, mods, layer, row_fn)


def _dft_tables(l):
    n = 2 * l
    r = jnp.arange(n, dtype=jnp.int32)[:, None]
    t = jnp.arange(l, dtype=jnp.int32)[None, :]
    f = jnp.where(r < l, r, r - l)
    ang = (2.0 * math.pi) * (((f * t) % n).astype(F32) / n)
    tab = jnp.where(r < l, jnp.cos(ang), jnp.sin(ang))
    tab = jnp.where(r == l, jnp.where(t % 2 == 0, 1.0, -1.0), tab)
    ffwd = tab.astype(BF16)
    return ffwd, ffwd.T


def _fnet_tables(l):
    f = jnp.arange(l, dtype=jnp.int32)[:, None]
    t = jnp.arange(l, dtype=jnp.int32)[None, :]
    ang = (2.0 * math.pi) * (((f * t) % l).astype(F32) / l)
    t2 = jnp.concatenate([jnp.cos(ang), jnp.sin(ang)], axis=1).astype(BF16)
    k = np.arange(FN_GROUP)
    ang64 = 2.0 * np.pi * ((k[:, None] * k[None, :]) % FN_GROUP) / FN_GROUP
    eye = np.eye(FN_W // FN_GROUP)
    scale = 1.0 / math.sqrt(FN_GROUP * l)
    m1 = np.concatenate([np.kron(eye, np.cos(ang64)), -np.kron(eye, np.sin(ang64))], axis=1) * scale
    return jnp.asarray(m1, F32).astype(BF16), t2


def _rope_tables(l):
    rows = l // GRID_W
    row = jnp.repeat(jnp.arange(rows), GRID_W).astype(F32)
    col = jnp.tile(jnp.arange(GRID_W), rows).astype(F32)
    half = QK_DIM // 2
    inv = ROPE_BASE ** (-jnp.arange(0, half, 2, dtype=F32) / half)
    ang = jnp.stack([row[:, None] * inv, col[:, None] * inv], axis=1)
    cos = jnp.repeat(jnp.cos(ang)[:, :, None, :], 2, axis=2)
    sin = jnp.sin(ang)
    sin = jnp.stack([-sin, sin], axis=2)
    rep = lambda a: jnp.tile(a.reshape(l, QK_DIM), (1, 2 * N_HEADS))
    return rep(cos), rep(sin)


def _filter_features(l):
    pos = jnp.arange(l, dtype=F32)
    t = pos / max(l - 1, 1)
    w = 2.0 * math.pi * pos / l
    f = jnp.linspace(1e-4, HY_BANDS - 1, HY_BANDS, dtype=F32)
    feats = jnp.concatenate([t[:, None], jnp.cos(w[:, None] * f), -jnp.sin(w[:, None] * f)], axis=-1)
    feats = jnp.pad(feats, ((0, 0), (0, 64 - HY_EMB)))
    deltas = jnp.abs(jnp.linspace(HY_MIN_DECAY, HY_MAX_DECAY, HY_W, dtype=F32))
    dec = jnp.exp(-t[:, None] * deltas)
    return feats, jnp.tile(dec, (1, HY_ORDER))


def _hyena_filters(l, tabs, ffwd, w1, b1, freq, w2, b2, w3):
    feats, dec = tabs
    taps = _hyena_filter_taps(feats, jnp.pad(w1, ((0, 64 - HY_EMB), (0, 0))), b1[None], freq[None],
                              w2, b2[None], w3, dec)
    kf = _table_matmul(ffwd, taps, F32)
    half = HY_ORDER * HY_W
    kc = kf[:l, :half]
    nyq = kf[l, :half]
    ks = kf[l:, half:]
    n = 2.0 * l
    first = (jnp.arange(l) == 0)[:, None]
    wc = jnp.where(first, 1.0 / n, 2.0 / n)
    ka = kc * wc
    kb = jnp.where(first, 0.0, ks * (2.0 / n))
    kd = jnp.where(first, nyq[None, :] / n, kc * (2.0 / n))
    stack = jnp.stack([ka, kb, kb, kd], axis=0)
    return jnp.moveaxis(stack.reshape(4, l, HY_ORDER, HY_W), 2, 0)


def kernel(x, c, ctx, c_ctx, w_ada, b_ada, g_mix, g_ffn, w_in, hy_conv_w, hy_conv_b, hy_w1, hy_b1, hy_freq, hy_w2, hy_b2, hy_w3, hy_bias, g_q, g_k, lam, g_sub, w_hy, w_fn, w_at, w_out, peer_wq, peer_keys, peer_u, peer_v):
    bsz, seq, d = x.shape
    clen = ctx.shape[1]
    depth = w_ada.shape[0]

    cc = jnp.concatenate([c, c_ctx[None], jnp.zeros((MOD_ROWS - bsz - 1, d), F32)], axis=0)
    mods = _ada_mods(cc, w_ada, b_ada)
    lat_row = lambda bi: bi
    ctx_row = lambda bi: bsz

    rope = _rope_tables(seq)
    tabs = {n: dict(dft=_dft_tables(n), fnet=_fnet_tables(n), feat=_filter_features(n)) for n in (seq, clen)}
    w = N_HEADS * HEAD_W
    lane = np.arange(w)
    bd = jnp.asarray((lane[:, None] // QK_DIM == lane[None, :] // QK_DIM) / QK_DIM, F32).astype(BF16)

    xl, xc = x, ctx
    for l in range(depth):
        last = l == depth - 1
        lam_init = 0.8 - 0.6 * math.exp(-0.3 * l)
        wl = w_in[l]
        w_perm = jnp.concatenate([wl[:, 2560:], wl[:, :2560]], axis=1).astype(BF16)
        gq = jnp.tile(g_q[l].reshape(1, HEAD_W), (1, N_HEADS)) * (QK_DIM ** -0.5)
        gk = jnp.tile(g_k[l].reshape(1, HEAD_W), (1, N_HEADS))
        gsub = g_sub[l][None]
        wts = [a[l].astype(BF16) for a in (w_hy, w_fn, w_at, w_out)]
        filt_args = (hy_w1[l], hy_b1[l], hy_freq[l], hy_w2[l], hy_b2[l], hy_w3[l])

        h_l = _in_projection(xl, g_mix[l][None], mods, l, lat_row, w_perm, 0, P_IN)
        h_c = _in_projection(xc, g_mix[l][None], mods, l, ctx_row, w_perm, COL_K if last else 0, P_IN)
        q_l, k_l = _qk_prep(h_l, gq, gk, bd, rope)
        q_c, k_c = _qk_prep(h_c, gq, gk, bd, None)
        vblk = COL_V // HEAD_W
        att_l = _diff_attention(q_l, [(k_l, h_l, vblk), (k_c, h_c, vblk)], lam[l], gsub, lam_init)
        ffwd, finv = tabs[seq]["dft"]
        kf = _hyena_filters(seq, tabs[seq]["feat"], ffwd, *filt_args)
        hyo_l = _hyena_mix(h_l, hy_conv_w[l], hy_conv_b[l][None], kf, hy_bias[l], ffwd, finv)
        fno_l = _fourier_mix(h_l, *tabs[seq]["fnet"])
        if not last:
            att_c = _diff_attention(q_c, [(k_c, h_c, vblk)], lam[l], gsub, lam_init)
            ffwd_c, finv_c = tabs[clen]["dft"]
            kf_c = _hyena_filters(clen, tabs[clen]["feat"], ffwd_c, *filt_args)
            hyo_c = _hyena_mix(h_c, hy_conv_w[l], hy_conv_b[l][None], kf_c, hy_bias[l], ffwd_c, finv_c)
            fno_c = _fourier_mix(h_c, *tabs[clen]["fnet"])
            xc = _merge(hyo_c, fno_c, att_c, h_c, xc, mods, l, ctx_row, *wts)
        xl = _merge(hyo_l, fno_l, att_l, h_l, xl, mods, l, lat_row, *wts)

        wq_t = peer_wq[l].astype(BF16)
        keys = _split(jnp.swapaxes(peer_keys[l].reshape(2 * PEER_HEADS, PEER_NKEYS, -1), 1, 2))
        u = peer_u[l].astype(BF16)
        v_t = peer_v[l].T.astype(BF16)
        if not last:
            xc = _peer(xc, g_ffn[l][None], mods, l, ctx_row, wq_t, keys, u, v_t)
        xl = _peer(xl, g_ffn[l][None], mods, l, lat_row, wq_t, keys, u, v_t)
    return xl
```

```python
import functools
import math

import jax
import jax.numpy as jnp
import numpy as np
from jax import lax
from jax.experimental import pallas as pl
from jax.experimental.pallas import tpu as pltpu

F32 = jnp.float32
BF16 = jnp.bfloat16

EPS = 1e-6
GRID_W = 64
ROPE_BASE = 10000.0
N_HEADS = 4
QK_DIM = 64
HEAD_W = 2 * QK_DIM
HY_W = 256
HY_ORDER = 2
HY_EMB = 33
HY_BANDS = (HY_EMB - 1) // 2
HY_MIN_DECAY = math.log(1e-2) / 1.5
HY_MAX_DECAY = math.log(1e-2) / 0.3
FN_GROUP = 64
FN_W = 256
PEER_HEADS = 8
PEER_NKEYS = 128
PEER_TOPK = 16
N_MOD = 6
MOD_ROWS = 16
NEG = -3.0e38

VMEM_LIMIT = 56 * 1024 * 1024

COL_GATE = 0
COL_HY = 3072
COL_FN = 3840
COL_Q = 4096
COL_K = 4608
COL_V = 5120
P_IN = 5632


def _cp(*sem):
    return pltpu.CompilerParams(dimension_semantics=sem, vmem_limit_bytes=VMEM_LIMIT)


def _dot(a, b):
    return jnp.dot(a, b, preferred_element_type=F32)


def _dot_nt(a, b):
    return lax.dot_general(a, b, (((1,), (1,)), ((), ())), preferred_element_type=F32)


def _split(a):
    hi = a.astype(BF16)
    lo = (a - hi.astype(F32)).astype(BF16)
    return hi, lo


def _modulated_norm(x, g, shift, scale):
    ms = jnp.mean(x * x, axis=-1, keepdims=True)
    y = x * lax.rsqrt(ms + EPS) * g
    return y * (1.0 + scale) + shift


def _mod_spec(layer, chunk, row_fn):
    def imap(*idx):
        return (layer, row_fn(*idx), chunk, 0, 0)
    return imap


def _ada_kernel(c_ref, w_ref, b_ref, o_ref):
    c = c_ref[...]
    a = c / (1.0 + jnp.exp(-c))
    ah, al = _split(a)
    wh, wl = _split(w_ref[...])
    o_ref[...] = _dot(ah, wh) + _dot(ah, wl) + _dot(al, wh) + b_ref[...]


def _ada_mods(cc, w_ada, b_ada):
    depth, d, n = w_ada.shape
    tn = 512
    out = pl.pallas_call(
        _ada_kernel,
        grid=(depth, n // tn),
        in_specs=[pl.BlockSpec((MOD_ROWS, d), lambda l, j: (0, 0)),
                  pl.BlockSpec((None, d, tn), lambda l, j: (l, 0, j)),
                  pl.BlockSpec((None, 1, tn), lambda l, j: (l, 0, j))],
        out_specs=pl.BlockSpec((None, MOD_ROWS, tn), lambda l, j: (l, 0, j)),
        out_shape=jax.ShapeDtypeStruct((depth, MOD_ROWS, n), F32),
        compiler_params=_cp("parallel", "parallel"),
        name="ada_mods",
    )(cc, w_ada, b_ada.reshape(depth, 1, n))
    return out.reshape(depth, MOD_ROWS, N_MOD, 1, d)


def _inproj_kernel(x_ref, g_ref, sh_ref, sc_ref, w_ref, o_ref, xn_ref):
    @pl.when(pl.program_id(2) == 0)
    def _():
        xn_ref[...] = _modulated_norm(x_ref[...], g_ref[...], sh_ref[...], sc_ref[...]).astype(BF16)
    o_ref[...] = _dot(xn_ref[...], w_ref[...]).astype(o_ref.dtype)


def _in_projection(x, g, mods, layer, row_fn, w, col_lo, col_hi):
    b, l, d = x.shape
    tm = min(l, 1024)
    tn = 512
    j0 = col_lo // tn
    nj = (col_hi - col_lo) // tn
    mrow = lambda bi, i, j: row_fn(bi)
    return pl.pallas_call(
        _inproj_kernel,
        grid=(b, l // tm, nj),
        in_specs=[pl.BlockSpec((None, tm, d), lambda bi, i, j: (bi, i, 0)),
                  pl.BlockSpec((1, d), lambda bi, i, j: (0, 0)),
                  pl.BlockSpec((None, None, None, 1, d), _mod_spec(layer, 0, mrow)),
                  pl.BlockSpec((None, None, None, 1, d), _mod_spec(layer, 1, mrow)),
                  pl.BlockSpec((d, tn), lambda bi, i, j: (0, j + j0))],
        out_specs=pl.BlockSpec((None, tm, tn), lambda bi, i, j: (bi, i, j + j0)),
        out_shape=jax.ShapeDtypeStruct((b, l, w.shape[1]), BF16),
        scratch_shapes=[pltpu.VMEM((tm, d), BF16)],
        compiler_params=_cp("parallel", "parallel", "arbitrary"),
        name="in_projection",
    )(x, g, mods, mods, w)


def _qkprep_kernel(*refs, rope):
    if rope:
        q_ref, k_ref, gq_ref, gk_ref, bd_ref, cos_ref, sin_ref, qo_ref, ko_ref = refs
    else:
        q_ref, k_ref, gq_ref, gk_ref, bd_ref, qo_ref, ko_ref = refs
    for src, g_ref, dst in ((q_ref, gq_ref, qo_ref), (k_ref, gk_ref, ko_ref)):
        x = src[...].astype(F32)
        hi, lo = _split(x * x)
        ms = _dot(hi, bd_ref[...]) + _dot(lo, bd_ref[...])
        y = x * lax.rsqrt(ms + EPS) * g_ref[...]
        if rope:
            w = y.shape[1]
            lane = lax.broadcasted_iota(jnp.int32, y.shape, 1)
            first = (lane % (QK_DIM // 2)) < (QK_DIM // 4)
            partner = jnp.where(first, pltpu.roll(y, w - QK_DIM // 4, 1), pltpu.roll(y, QK_DIM // 4, 1))
            y = y * cos_ref[...] + partner * sin_ref[...]
        dst[...] = y.astype(BF16)


def _qk_prep(h, gq, gk, bd, rope_tabs):
    b, l, _ = h.shape
    w = N_HEADS * HEAD_W
    tm = min(l, 512)
    rope = rope_tabs is not None
    in_specs = [pl.BlockSpec((None, tm, w), lambda bi, i: (bi, i, COL_Q // w)),
                pl.BlockSpec((None, tm, w), lambda bi, i: (bi, i, COL_K // w)),
                pl.BlockSpec((1, w), lambda bi, i: (0, 0)),
                pl.BlockSpec((1, w), lambda bi, i: (0, 0)),
                pl.BlockSpec((w, w), lambda bi, i: (0, 0))]
    args = [h, h, gq, gk, bd]
    if rope:
        in_specs += [pl.BlockSpec((tm, w), lambda bi, i: (i, 0))] * 2
        args += list(rope_tabs)
    return pl.pallas_call(
        functools.partial(_qkprep_kernel, rope=rope),
        grid=(b, l // tm),
        in_specs=in_specs,
        out_specs=[pl.BlockSpec((None, tm, w), lambda bi, i: (bi, i, 0))] * 2,
        out_shape=[jax.ShapeDtypeStruct((b, l, w), BF16)] * 2,
        compiler_params=_cp("parallel", "parallel"),
        name="qk_prep",
    )(*args)


def _attn_kernel(*refs, n_src, lam_init):
    q_ref = refs[0]
    kv = refs[1:1 + 2 * n_src]
    lam_ref, gsub_ref, o_ref = refs[1 + 2 * n_src:]
    q = q_ref[...]
    tq = q.shape[0]
    lane = lax.broadcasted_iota(jnp.int32, q.shape, 1)
    zero = jnp.zeros_like(q)
    qq = jnp.concatenate([jnp.where(lane < QK_DIM, q, zero), jnp.where(lane >= QK_DIM, q, zero)], axis=0)
    scores = [_dot_nt(qq, kv[2 * i][...]) for i in range(n_src)]
    m = jnp.max(scores[0], axis=-1, keepdims=True)
    for s in scores[1:]:
        m = jnp.maximum(m, jnp.max(s, axis=-1, keepdims=True))
    z = jnp.zeros_like(m)
    acc = jnp.zeros((2 * tq, HEAD_W), F32)
    for i, s in enumerate(scores):
        e = jnp.exp(s - m)
        z = z + jnp.sum(e, axis=-1, keepdims=True)
        acc = acc + _dot(e.astype(BF16), kv[2 * i + 1][...])
    o2 = acc / z
    lf = lam_ref[...]
    lam_val = (jnp.exp(jnp.sum(lf[0:1] * lf[1:2], axis=-1, keepdims=True))
               - jnp.exp(jnp.sum(lf[2:3] * lf[3:4], axis=-1, keepdims=True)) + lam_init)
    o = o2[:tq] - lam_val * o2[tq:]
    ms = jnp.mean(o * o, axis=-1, keepdims=True)
    o = o * lax.rsqrt(ms + EPS) * gsub_ref[...] * (1.0 - lam_init)
    o_ref[...] = o.astype(BF16)


def _diff_attention(q, sources, lam_l, gsub, lam_init):
    b, lq, w = q.shape
    tq = 256
    in_specs = [pl.BlockSpec((None, tq, HEAD_W), lambda bi, hi, i: (bi, i, hi))]
    args = [q]
    for k, varr, vblk in sources:
        lk = k.shape[1]
        in_specs.append(pl.BlockSpec((None, lk, HEAD_W), lambda bi, hi, i: (bi, 0, hi)))
        in_specs.append(pl.BlockSpec((None, lk, HEAD_W), lambda bi, hi, i, vblk=vblk: (bi, 0, vblk + hi)))
        args += [k, varr]
    in_specs += [pl.BlockSpec(lam_l.shape, lambda bi, hi, i: (0, 0)),
                 pl.BlockSpec((1, HEAD_W), lambda bi, hi, i: (0, 0))]
    args += [lam_l, gsub]
    return pl.pallas_call(
        functools.partial(_attn_kernel, n_src=len(sources), lam_init=lam_init),
        grid=(b, N_HEADS, lq // tq),
        in_specs=in_specs,
        out_specs=pl.BlockSpec((None, tq, HEAD_W), lambda bi, hi, i: (bi, i, hi)),
        out_shape=jax.ShapeDtypeStruct((b, lq, w), BF16),
        compiler_params=_cp("parallel", "parallel", "arbitrary"),
        name="diff_attention",
    )(*args)


def _sconv_kernel(h_ref, w_ref, b_ref, o_ref):
    x = h_ref[...].astype(F32)
    n = x.shape[0]
    row = lax.broadcasted_iota(jnp.int32, x.shape, 0)
    prev = jnp.where(row == 0, 0.0, pltpu.roll(x, 1, 0))
    nxt = jnp.where(row == n - 1, 0.0, pltpu.roll(x, n - 1, 0))
    o_ref[...] = prev * w_ref[0:1, :] + x * w_ref[1:2, :] + nxt * w_ref[2:3, :] + b_ref[...]


def _short_conv(h, w, bias):
    b, l, _ = h.shape
    return pl.pallas_call(
        _sconv_kernel,
        grid=(b, 3),
        in_specs=[pl.BlockSpec((None, l, HY_W), lambda bi, j: (bi, 0, COL_HY // HY_W + j)),
                  pl.BlockSpec((3, HY_W), lambda bi, j: (0, j)),
                  pl.BlockSpec((1, HY_W), lambda bi, j: (0, j))],
        out_specs=pl.BlockSpec((None, None, l, HY_W), lambda bi, j: (bi, j, 0, 0)),
        out_shape=jax.ShapeDtypeStruct((b, 3, l, HY_W), F32),
        compiler_params=_cp("parallel", "parallel"),
        name="short_conv",
    )(h, w, bias)


def _filter_kernel(f_ref, w1_ref, b1_ref, fr_ref, w2_ref, b2_ref, w3_ref, dec_ref, o_ref):
    def mm(a, w_ref_):
        ah, al = _split(a)
        wh, wl = _split(w_ref_[...])
        return _dot(ah, wh) + _dot(ah, wl) + _dot(al, wh)
    fr = fr_ref[...]
    h = jnp.sin(fr * (mm(f_ref[...], w1_ref) + b1_ref[...]))
    h = jnp.sin(fr * (mm(h, w2_ref) + b2_ref[...]))
    h = mm(h, w3_ref)
    half = h.shape[1] // 2
    dec = dec_ref[...]
    hf = h[:, :half] * dec
    hb = h[:, half:] * dec
    row = lax.broadcasted_iota(jnp.int32, hb.shape, 0)
    hb = jnp.where(row == 0, 0.0, hb)
    norm = jnp.sum(jnp.abs(hf) + jnp.abs(hb), axis=0, keepdims=True)
    o_ref[:, :half] = (hf + hb) / norm
    o_ref[:, half:] = (hf - hb) / norm


def _hyena_filter_taps(feats, w1, b1, freq, w2, b2, w3, dec):
    l = feats.shape[0]
    n = w3.shape[1]
    full = lambda a: pl.BlockSpec(a.shape, lambda i: (0,) * a.ndim)
    args = (feats, w1, b1, freq, w2, b2, w3, dec)
    return pl.pallas_call(
        _filter_kernel,
        grid=(1,),
        in_specs=[full(a) for a in args],
        out_specs=pl.BlockSpec((l, n), lambda i: (0, 0)),
        out_shape=jax.ShapeDtypeStruct((l, n), F32),
        compiler_params=_cp("arbitrary"),
        name="hyena_filter_taps",
    )(*args)


def _table_mm_kernel(t_ref, x_ref, o_ref):
    o_ref[...] = _dot(t_ref[...], x_ref[...].astype(BF16)).astype(o_ref.dtype)


def _table_matmul(table, x, out_dtype):
    m, k = table.shape
    n = x.shape[1]
    tm = min(m, 512)
    return pl.pallas_call(
        _table_mm_kernel,
        grid=(m // tm,),
        in_specs=[pl.BlockSpec((tm, k), lambda i: (i, 0)),
                  pl.BlockSpec((k, n), lambda i: (0, 0))],
        out_specs=pl.BlockSpec((tm, n), lambda i: (i, 0)),
        out_shape=jax.ShapeDtypeStruct((m, n), out_dtype),
        compiler_params=_cp("parallel"),
        name="table_matmul",
    )(table, x)


def _dftmul_kernel(fc_ref, fs_ref, z_ref, k_ref, p_ref):
    z = z_ref[...].astype(BF16)
    zc = _dot(fc_ref[...], z)
    zs = _dot(fs_ref[...], z)
    p_ref[0] = (zc * k_ref[0] - zs * k_ref[1]).astype(BF16)
    p_ref[1] = (zc * k_ref[2] + zs * k_ref[3]).astype(BF16)


def _dft_multiply(ffwd, z, z_spec, kf):
    l = ffwd.shape[1]
    b = z.shape[0]
    tf = min(l, 512)
    nf = l // tf
    out = pl.pallas_call(
        _dftmul_kernel,
        grid=(nf, b),
        in_specs=[pl.BlockSpec((tf, l), lambda i, bi: (i, 0)),
                  pl.BlockSpec((tf, l), lambda i, bi: (i + nf, 0)),
                  z_spec,
                  pl.BlockSpec((4, tf, HY_W), lambda i, bi: (0, i, 0))],
        out_specs=pl.BlockSpec((None, 2, tf, HY_W), lambda i, bi: (bi, 0, i, 0)),
        out_shape=jax.ShapeDtypeStruct((b, 2, l, HY_W), BF16),
        compiler_params=_cp("parallel", "arbitrary"),
        name="dft_multiply",
    )(ffwd, ffwd, z, kf)
    return out.reshape(b, 2 * l, HY_W)


def _idft_gate_kernel(fi_ref, p_ref, g_ref, z_ref, b_ref, o_ref):
    conv = _dot(fi_ref[...], p_ref[...])
    o_ref[...] = (g_ref[...] * (conv + b_ref[...] * z_ref[...])).astype(o_ref.dtype)


def _idft_gate(finv, p, gate, gate_spec, z, z_spec, bias, out_dtype):
    l = finv.shape[0]
    b = p.shape[0]
    tt = min(l, 512)
    return pl.pallas_call(
        _idft_gate_kernel,
        grid=(l // tt, b),
        in_specs=[pl.BlockSpec((tt, 2 * l), lambda i, bi: (i, 0)),
                  pl.BlockSpec((None, 2 * l, HY_W), lambda i, bi: (bi, 0, 0)),
                  gate_spec, z_spec,
                  pl.BlockSpec((1, HY_W), lambda i, bi: (0, 0))],
        out_specs=pl.BlockSpec((None, tt, HY_W), lambda i, bi: (bi, i, 0)),
        out_shape=jax.ShapeDtypeStruct((b, l, HY_W), out_dtype),
        compiler_params=_cp("parallel", "arbitrary"),
        name="idft_gate",
    )(finv, p, gate, z, bias)


def _hyena_mix(h, conv_w, conv_b, kf, hy_bias, ffwd, finv):
    b, l, _ = h.shape
    tt = min(l, 512)
    u = _short_conv(h, conv_w, conv_b)
    part = lambda j, rows: pl.BlockSpec((None, None, rows, HY_W),
                                        lambda i, bi, j=j: (bi, j, i if rows != l else 0, 0))
    p = _dft_multiply(ffwd, u, part(0, l), kf[0])
    z1 = _idft_gate(finv, p, u, part(1, tt), u, part(0, tt), hy_bias[0:1], F32)
    p = _dft_multiply(ffwd, z1, pl.BlockSpec((None, l, HY_W), lambda i, bi: (bi, 0, 0)), kf[1])
    return _idft_gate(finv, p, u, part(2, tt), z1,
                      pl.BlockSpec((None, tt, HY_W), lambda i, bi: (bi, i, 0)), hy_bias[1:2], BF16)


def _fn1_kernel(z_ref, m_ref, o_ref):
    r = _dot(z_ref[...], m_ref[...])
    half = r.shape[1] // 2
    o_ref[0] = r[:, :half].astype(BF16)
    o_ref[1] = r[:, half:].astype(BF16)


def _fourier_mix(h, m1, t2):
    b, l, _ = h.shape
    tm = min(l, 512)
    zz = pl.pallas_call(
        _fn1_kernel,
        grid=(b, l // tm),
        in_specs=[pl.BlockSpec((None, tm, FN_W), lambda bi, i: (bi, i, COL_FN // FN_W)),
                  pl.BlockSpec((FN_W, 2 * FN_W), lambda bi, i: (0, 0))],
        out_specs=pl.BlockSpec((None, 2, tm, FN_W), lambda bi, i: (bi, 0, i, 0)),
        out_shape=jax.ShapeDtypeStruct((b, 2, l, FN_W), BF16),
        compiler_params=_cp("parallel", "parallel"),
        name="fnet_channels",
    )(h, m1).reshape(b, 2 * l, FN_W)
    return pl.pallas_call(
        _table_mm_kernel,
        grid=(l // tm, b),
        in_specs=[pl.BlockSpec((tm, 2 * l), lambda i, bi: (i, 0)),
                  pl.BlockSpec((None, 2 * l, FN_W), lambda i, bi: (bi, 0, 0))],
        out_specs=pl.BlockSpec((None, tm, FN_W), lambda i, bi: (bi, i, 0)),
        out_shape=jax.ShapeDtypeStruct((b, l, FN_W), BF16),
        compiler_params=_cp("parallel", "arbitrary"),
        name="fnet_positions",
    )(t2, zz)


def _merge_kernel(hy_ref, fn_ref, at_ref, g_ref, x_ref, gm_ref, why_ref, wfn_ref, wat_ref, wout_ref, o_ref):
    d = x_ref.shape[-1]
    g = 1.0 / (1.0 + jnp.exp(-g_ref[...].astype(F32)))
    y = (g[:, :d] * _dot(hy_ref[...], why_ref[...])
         + g[:, d:2 * d] * _dot(fn_ref[...], wfn_ref[...])
         + g[:, 2 * d:] * _dot(at_ref[...], wat_ref[...]))
    mix = _dot(y.astype(BF16), wout_ref[...])
    o_ref[...] = x_ref[...] + gm_ref[...] * mix


def _merge(hyo, fno, att, h, x, mods, layer, row_fn, w_hy, w_fn, w_at, w_out):
    b, l, d = x.shape
    tm = min(l, 512)
    full = lambda a: pl.BlockSpec(a.shape, lambda bi, i: (0,) * a.ndim)
    tok = lambda wdt, blk=0: pl.BlockSpec((None, tm, wdt), lambda bi, i: (bi, i, blk))
    return pl.pallas_call(
        _merge_kernel,
        grid=(b, l // tm),
        in_specs=[tok(HY_W), tok(FN_W), tok(N_HEADS * HEAD_W), tok(3 * d, COL_GATE), tok(d),
                  pl.BlockSpec((None, None, None, 1, d), _mod_spec(layer, 2, lambda bi, i: row_fn(bi))),
                  full(w_hy), full(w_fn), full(w_at), full(w_out)],
        out_specs=tok(d),
        out_shape=jax.ShapeDtypeStruct((b, l, d), F32),
        compiler_params=_cp("parallel", "parallel"),
        name="merge_residual",
    )(hyo, fno, att, h, x, mods, w_hy, w_fn, w_at, w_out)


def _peer_q_kernel(x_ref, g_ref, sh_ref, sc_ref, wq_ref, kh_ref, kl_ref, s_ref, nt_ref):
    n = _modulated_norm(x_ref[...], g_ref[...], sh_ref[...], sc_ref[...])
    nt_ref[...] = n.T.astype(BF16)
    q = _dot(n.astype(BF16), wq_ref[...])
    dq = kh_ref.shape[1]
    for hp in range(kh_ref.shape[0]):
        qh, ql = _split(q[:, hp * dq:(hp + 1) * dq])
        kh = kh_ref[hp]
        s = _dot(qh, kh) + _dot(ql, kh) + _dot(qh, kl_ref[hp])
        s_ref[hp] = s.T


def _peer_scores(x, g, mods, layer, row_fn, wq, keys_t):
    b, l, d = x.shape
    tt = min(l, 512)
    nt = l // tt
    nhp = 2 * PEER_HEADS
    mrow = lambda bi, i: row_fn(bi)
    full = lambda a: pl.BlockSpec(a.shape, lambda bi, i: (0,) * a.ndim)
    return pl.pallas_call(
        _peer_q_kernel,
        grid=(b, nt),
        in_specs=[pl.BlockSpec((None, tt, d), lambda bi, i: (bi, i, 0)),
                  pl.BlockSpec((1, d), lambda bi, i: (0, 0)),
                  pl.BlockSpec((None, None, None, 1, d), _mod_spec(layer, 3, mrow)),
                  pl.BlockSpec((None, None, None, 1, d), _mod_spec(layer, 4, mrow)),
                  full(wq), full(keys_t[0]), full(keys_t[1])],
        out_specs=[pl.BlockSpec((nhp, PEER_NKEYS, tt), lambda bi, i: (0, 0, bi * nt + i)),
                   pl.BlockSpec((d, tt), lambda bi, i: (0, bi * nt + i))],
        out_shape=[jax.ShapeDtypeStruct((nhp, PEER_NKEYS, b * l), F32),
                   jax.ShapeDtypeStruct((d, b * l), BF16)],
        compiler_params=_cp("parallel", "parallel"),
        name="peer_scores",
    )(x, g, mods, mods, wq, keys_t[0], keys_t[1])


def _top_rows(s, k):
    vals = []
    for r in range(k):
        m = jnp.max(s, axis=0, keepdims=True)
        vals.append(m)
        if r + 1 < k:
            s = jnp.where(s == m, NEG, s)
    return vals


def _peer_expert_kernel(s_ref, nt_ref, u_ref, vt_ref, x_ref, gm_ref, o_ref,
                        tau_ref, ea_ref, eb_ref, hid0_ref, hid1_ref, p0_ref, p1_ref, acc_ref):
    s = pl.program_id(1)
    nc = pl.num_programs(1) - 2
    k = PEER_TOPK
    tt = nt_ref.shape[1]
    n_i = u_ref.shape[0] // PEER_NKEYS
    piece = 32

    def hidden(hid_ref):
        hid_ref[...] = _dot(u_ref[...], nt_ref[...])

    def stages(chunk, hid_w, hid_r, p_w, p_r):
        for ii in range(n_i):
            slab = slice(ii * PEER_NKEYS, (ii + 1) * PEER_NKEYS)
            hid_w[slab, :] = _dot(u_ref[slab, :], nt_ref[...])
            i = chunk * n_i + ii
            s1_row = [s_ref[2 * h, pl.ds(i, 1), :] for h in range(PEER_HEADS)]
            ea_row = [ea_ref[h, pl.ds(i, 1), :] for h in range(PEER_HEADS)]
            for jp in range(PEER_NKEYS // piece):
                js = slice(jp * piece, (jp + 1) * piece)
                gate = jnp.zeros((piece, tt), F32)
                for h in range(PEER_HEADS):
                    sums = s1_row[h] + s_ref[2 * h + 1, js, :]
                    w = ea_row[h] * eb_ref[h, js, :]
                    gate = gate + jnp.where(sums >= tau_ref[h], w, 0.0)
                rows = slice(ii * PEER_NKEYS + jp * piece, ii * PEER_NKEYS + (jp + 1) * piece)
                hid = hid_r[rows, :]
                act = 0.5 * hid * (1.0 + lax.erf(hid * (2.0 ** -0.5)))
                p_w[rows, :] = (gate * act).astype(BF16)
            acc_ref[slab, :] += _dot(vt_ref[slab, :], p_r[...])

    @pl.when(s == 0)
    def _():
        acc_ref[...] = jnp.zeros_like(acc_ref)
        p1_ref[...] = jnp.zeros_like(p1_ref)
        hidden(hid0_ref)
        for h in range(PEER_HEADS):
            s1 = s_ref[2 * h]
            s2 = s_ref[2 * h + 1]
            a = _top_rows(s1, k)
            b = _top_rows(s2, k)
            cand = [a[i] + b[j] for i in range(k) for j in range(k // (i + 1))]
            pad = (-len(cand)) % 8
            cand = jnp.concatenate(cand + [jnp.full_like(a[0], NEG)] * pad, axis=0)
            top = _top_rows(cand, k)
            z = jnp.zeros_like(top[0])
            for t in top:
                z = z + jnp.exp(t - top[0])
            tau_ref[h] = top[k - 1]
            ea_ref[h] = jnp.exp(s1 - a[0])
            eb_ref[h] = jnp.exp(s2 - b[0]) / z

    @pl.when((s >= 1) & (s <= nc) & (s % 2 == 1))
    def _():
        stages(s - 1, hid1_ref, hid0_ref, p0_ref, p1_ref)

    @pl.when((s >= 1) & (s <= nc) & (s % 2 == 0))
    def _():
        stages(s - 1, hid0_ref, hid1_ref, p1_ref, p0_ref)

    @pl.when(s == nc + 1)
    def _():
        last = p0_ref if (nc - 1) % 2 == 0 else p1_ref
        out_t = acc_ref[...] + _dot(vt_ref[...], last[...])
        o_ref[...] = x_ref[...] + gm_ref[...] * out_t.T


def _peer_experts(scores, n_t, u, v_t, x, mods, layer, row_fn):
    b, l, d = x.shape
    tt = 256
    nt = l // tt
    ec = d
    nc = u.shape[0] // ec
    nhp = scores.shape[0]
    return pl.pallas_call(
        _peer_expert_kernel,
        grid=(b * nt, nc + 2),
        in_specs=[pl.BlockSpec((nhp, PEER_NKEYS, tt), lambda t, s: (0, 0, t)),
                  pl.BlockSpec((d, tt), lambda t, s: (0, t)),
                  pl.BlockSpec((ec, d), lambda t, s: (jnp.minimum(s, nc - 1), 0)),
                  pl.BlockSpec((d, ec), lambda t, s: (0, jnp.clip(s - 2, 0, nc - 1))),
                  pl.BlockSpec((None, tt, d), lambda t, s: (t // nt, t % nt, 0)),
                  pl.BlockSpec((None, None, None, 1, d), _mod_spec(layer, 5, lambda t, s: row_fn(t // nt)))],
        out_specs=pl.BlockSpec((None, tt, d), lambda t, s: (t // nt, t % nt, 0)),
        out_shape=jax.ShapeDtypeStruct((b, l, d), F32),
        scratch_shapes=[pltpu.VMEM((PEER_HEADS, 1, tt), F32),
                        pltpu.VMEM((PEER_HEADS, PEER_NKEYS, tt), F32),
                        pltpu.VMEM((PEER_HEADS, PEER_NKEYS, tt), F32),
                        pltpu.VMEM((ec, tt), F32),
                        pltpu.VMEM((ec, tt), F32),
                        pltpu.VMEM((ec, tt), BF16),
                        pltpu.VMEM((ec, tt), BF16),
                        pltpu.VMEM((d, tt), F32)],
        compiler_params=_cp("parallel", "arbitrary"),
        name="peer_experts",
    )(scores, n_t, u, v_t, x, mods)


def _peer(x, g, mods, layer, row_fn, wq_t, keys, u, v_t):
    scores, n_t = _peer_scores(x, g, mods, layer, row_fn, wq_t, keys)
    return _peer_experts(scores, n_t, u, v_t, x, mods, layer, row_fn)


def _dft_tables(l):
    n = 2 * l
    r = jnp.arange(n, dtype=jnp.int32)[:, None]
    t = jnp.arange(l, dtype=jnp.int32)[None, :]
    f = jnp.where(r < l, r, r - l)
    ang = (2.0 * math.pi) * (((f * t) % n).astype(F32) / n)
    tab = jnp.where(r < l, jnp.cos(ang), jnp.sin(ang))
    tab = jnp.where(r == l, jnp.where(t % 2 == 0, 1.0, -1.0), tab)
    ffwd = tab.astype(BF16)
    return ffwd, ffwd.T


def _fnet_tables(l):
    f = jnp.arange(l, dtype=jnp.int32)[:, None]
    t = jnp.arange(l, dtype=jnp.int32)[None, :]
    ang = (2.0 * math.pi) * (((f * t) % l).astype(F32) / l)
    t2 = jnp.concatenate([jnp.cos(ang), jnp.sin(ang)], axis=1).astype(BF16)
    k = np.arange(FN_GROUP)
    ang64 = 2.0 * np.pi * ((k[:, None] * k[None, :]) % FN_GROUP) / FN_GROUP
    eye = np.eye(FN_W // FN_GROUP)
    scale = 1.0 / math.sqrt(FN_GROUP * l)
    m1 = np.concatenate([np.kron(eye, np.cos(ang64)), -np.kron(eye, np.sin(ang64))], axis=1) * scale
    return jnp.asarray(m1, F32).astype(BF16), t2


def _rope_tables(l):
    rows = l // GRID_W
    row = jnp.repeat(jnp.arange(rows), GRID_W).astype(F32)
    col = jnp.tile(jnp.arange(GRID_W), rows).astype(F32)
    half = QK_DIM // 2
    inv = ROPE_BASE ** (-jnp.arange(0, half, 2, dtype=F32) / half)
    ang = jnp.stack([row[:, None] * inv, col[:, None] * inv], axis=1)
    cos = jnp.repeat(jnp.cos(ang)[:, :, None, :], 2, axis=2)
    sin = jnp.sin(ang)
    sin = jnp.stack([-sin, sin], axis=2)
    rep = lambda a: jnp.tile(a.reshape(l, QK_DIM), (1, 2 * N_HEADS))
    return rep(cos), rep(sin)


def _filter_features(l):
    pos = jnp.arange(l, dtype=F32)
    t = pos / max(l - 1, 1)
    w = 2.0 * math.pi * pos / l
    f = jnp.linspace(1e-4, HY_BANDS - 1, HY_BANDS, dtype=F32)
    feats = jnp.concatenate([t[:, None], jnp.cos(w[:, None] * f), -jnp.sin(w[:, None] * f)], axis=-1)
    feats = jnp.pad(feats, ((0, 0), (0, 64 - HY_EMB)))
    deltas = jnp.abs(jnp.linspace(HY_MIN_DECAY, HY_MAX_DECAY, HY_W, dtype=F32))
    dec = jnp.exp(-t[:, None] * deltas)
    return feats, jnp.tile(dec, (1, HY_ORDER))


def _hyena_filters(l, tabs, ffwd, w1, b1, freq, w2, b2, w3):
    feats, dec = tabs
    taps = _hyena_filter_taps(feats, jnp.pad(w1, ((0, 64 - HY_EMB), (0, 0))), b1[None], freq[None],
                              w2, b2[None], w3, dec)
    kf = _table_matmul(ffwd, taps, F32)
    half = HY_ORDER * HY_W
    kc = kf[:l, :half]
    nyq = kf[l, :half]
    ks = kf[l:, half:]
    n = 2.0 * l
    first = (jnp.arange(l) == 0)[:, None]
    wc = jnp.where(first, 1.0 / n, 2.0 / n)
    ka = kc * wc
    kb = jnp.where(first, 0.0, ks * (2.0 / n))
    kd = jnp.where(first, nyq[None, :] / n, kc * (2.0 / n))
    stack = jnp.stack([ka, kb, kb, kd], axis=0)
    return jnp.moveaxis(stack.reshape(4, l, HY_ORDER, HY_W), 2, 0)


def kernel(x, c, ctx, c_ctx, w_ada, b_ada, g_mix, g_ffn, w_in, hy_conv_w, hy_conv_b, hy_w1, hy_b1, hy_freq, hy_w2, hy_b2, hy_w3, hy_bias, g_q, g_k, lam, g_sub, w_hy, w_fn, w_at, w_out, peer_wq, peer_keys, peer_u, peer_v):
    bsz, seq, d = x.shape
    clen = ctx.shape[1]
    depth = w_ada.shape[0]

    cc = jnp.concatenate([c, c_ctx[None], jnp.zeros((MOD_ROWS - bsz - 1, d), F32)], axis=0)
    mods = _ada_mods(cc, w_ada, b_ada)
    lat_row = lambda bi: bi
    ctx_row = lambda bi: bsz

    rope = _rope_tables(seq)
    tabs = {n: dict(dft=_dft_tables(n), fnet=_fnet_tables(n), feat=_filter_features(n)) for n in (seq, clen)}
    w = N_HEADS * HEAD_W
    lane = np.arange(w)
    bd = jnp.asarray((lane[:, None] // QK_DIM == lane[None, :] // QK_DIM) / QK_DIM, F32).astype(BF16)

    xl, xc = x, ctx
    for l in range(depth):
        last = l == depth - 1
        lam_init = 0.8 - 0.6 * math.exp(-0.3 * l)
        wl = w_in[l]
        w_perm = jnp.concatenate([wl[:, 2560:], wl[:, :2560]], axis=1).astype(BF16)
        gq = jnp.tile(g_q[l].reshape(1, HEAD_W), (1, N_HEADS)) * (QK_DIM ** -0.5)
        gk = jnp.tile(g_k[l].reshape(1, HEAD_W), (1, N_HEADS))
        gsub = g_sub[l][None]
        wts = [a[l].astype(BF16) for a in (w_hy, w_fn, w_at, w_out)]
        filt_args = (hy_w1[l], hy_b1[l], hy_freq[l], hy_w2[l], hy_b2[l], hy_w3[l])

        h_l = _in_projection(xl, g_mix[l][None], mods, l, lat_row, w_perm, 0, P_IN)
        h_c = _in_projection(xc, g_mix[l][None], mods, l, ctx_row, w_perm, COL_K if last else 0, P_IN)
        q_l, k_l = _qk_prep(h_l, gq, gk, bd, rope)
        q_c, k_c = _qk_prep(h_c, gq, gk, bd, None)
        vblk = COL_V // HEAD_W
        att_l = _diff_attention(q_l, [(k_l, h_l, vblk), (k_c, h_c, vblk)], lam[l], gsub, lam_init)
        ffwd, finv = tabs[seq]["dft"]
        kf = _hyena_filters(seq, tabs[seq]["feat"], ffwd, *filt_args)
        hyo_l = _hyena_mix(h_l, hy_conv_w[l], hy_conv_b[l][None], kf, hy_bias[l], ffwd, finv)
        fno_l = _fourier_mix(h_l, *tabs[seq]["fnet"])
        if not last:
            att_c = _diff_attention(q_c, [(k_c, h_c, vblk)], lam[l], gsub, lam_init)
            ffwd_c, finv_c = tabs[clen]["dft"]
            kf_c = _hyena_filters(clen, tabs[clen]["feat"], ffwd_c, *filt_args)
            hyo_c = _hyena_mix(h_c, hy_conv_w[l], hy_conv_b[l][None], kf_c, hy_bias[l], ffwd_c, finv_c)
            fno_c = _fourier_mix(h_c, *tabs[clen]["fnet"])
            xc = _merge(hyo_c, fno_c, att_c, h_c, xc, mods, l, ctx_row, *wts)
        xl = _merge(hyo_l, fno_l, att_l, h_l, xl, mods, l, lat_row, *wts)

        wq_t = peer_wq[l].astype(BF16)
        keys = _split(jnp.swapaxes(peer_keys[l].reshape(2 * PEER_HEADS, PEER_NKEYS, -1), 1, 2))
        u = peer_u[l].astype(BF16)
        v_t = peer_v[l].T.astype(BF16)
        if not last:
            xc = _peer(xc, g_ffn[l][None], mods, l, ctx_row, wq_t, keys, u, v_t)
        xl = _peer(xl, g_ffn[l][None], mods, l, lat_row, wq_t, keys, u, v_t)
    return xl
```

```python
import functools
import math

import jax
import jax.numpy as jnp
import numpy as np
from jax import lax
from jax.experimental import pallas as pl
from jax.experimental.pallas import tpu as pltpu

F32 = jnp.float32
BF16 = jnp.bfloat16

EPS = 1e-6
GRID_W = 64
ROPE_BASE = 10000.0
N_HEADS = 4
QK_DIM = 64
HEAD_W = 2 * QK_DIM
HY_W = 256
HY_ORDER = 2
HY_EMB = 33
HY_BANDS = (HY_EMB - 1) // 2
HY_MIN_DECAY = math.log(1e-2) / 1.5
HY_MAX_DECAY = math.log(1e-2) / 0.3
FN_GROUP = 64
FN_W = 256
PEER_HEADS = 8
PEER_NKEYS = 128
PEER_TOPK = 16
N_MOD = 6
MOD_ROWS = 16
NEG = -3.0e38

VMEM_LIMIT = 56 * 1024 * 1024

COL_GATE = 0
COL_HY = 3072
COL_FN = 3840
COL_Q = 4096
COL_K = 4608
COL_V = 5120
P_IN = 5632


def _cp(*sem):
    return pltpu.CompilerParams(dimension_semantics=sem, vmem_limit_bytes=VMEM_LIMIT)


def _dot(a, b):
    return jnp.dot(a, b, preferred_element_type=F32)


def _dot_nt(a, b):
    return lax.dot_general(a, b, (((1,), (1,)), ((), ())), preferred_element_type=F32)


def _split(a):
    hi = a.astype(BF16)
    lo = (a - hi.astype(F32)).astype(BF16)
    return hi, lo


def _modulated_norm(x, g, shift, scale):
    ms = jnp.mean(x * x, axis=-1, keepdims=True)
    y = x * lax.rsqrt(ms + EPS) * g
    return y * (1.0 + scale) + shift


def _mod_spec(layer, chunk, row_fn):
    def imap(*idx):
        return (layer, row_fn(*idx), chunk, 0, 0)
    return imap


def _ada_kernel(c_ref, w_ref, b_ref, o_ref):
    c = c_ref[...]
    a = c / (1.0 + jnp.exp(-c))
    ah, al = _split(a)
    wh, wl = _split(w_ref[...])
    o_ref[...] = _dot(ah, wh) + _dot(ah, wl) + _dot(al, wh) + b_ref[...]


def _ada_mods(cc, w_ada, b_ada):
    depth, d, n = w_ada.shape
    tn = 512
    out = pl.pallas_call(
        _ada_kernel,
        grid=(depth, n // tn),
        in_specs=[pl.BlockSpec((MOD_ROWS, d), lambda l, j: (0, 0)),
                  pl.BlockSpec((None, d, tn), lambda l, j: (l, 0, j)),
                  pl.BlockSpec((None, 1, tn), lambda l, j: (l, 0, j))],
        out_specs=pl.BlockSpec((None, MOD_ROWS, tn), lambda l, j: (l, 0, j)),
        out_shape=jax.ShapeDtypeStruct((depth, MOD_ROWS, n), F32),
        compiler_params=_cp("parallel", "parallel"),
        name="ada_mods",
    )(cc, w_ada, b_ada.reshape(depth, 1, n))
    return out.reshape(depth, MOD_ROWS, N_MOD, 1, d)


def _inproj_kernel(x_ref, g_ref, sh_ref, sc_ref, w_ref, o_ref, xn_ref):
    @pl.when(pl.program_id(2) == 0)
    def _():
        xn_ref[...] = _modulated_norm(x_ref[...], g_ref[...], sh_ref[...], sc_ref[...]).astype(BF16)
    o_ref[...] = _dot(xn_ref[...], w_ref[...]).astype(o_ref.dtype)


def _in_projection(x, g, mods, layer, row_fn, w, col_lo, col_hi):
    b, l, d = x.shape
    tm = min(l, 1024)
    tn = 512
    j0 = col_lo // tn
    nj = (col_hi - col_lo) // tn
    mrow = lambda bi, i, j: row_fn(bi)
    return pl.pallas_call(
        _inproj_kernel,
        grid=(b, l // tm, nj),
        in_specs=[pl.BlockSpec((None, tm, d), lambda bi, i, j: (bi, i, 0)),
                  pl.BlockSpec((1, d), lambda bi, i, j: (0, 0)),
                  pl.BlockSpec((None, None, None, 1, d), _mod_spec(layer, 0, mrow)),
                  pl.BlockSpec((None, None, None, 1, d), _mod_spec(layer, 1, mrow)),
                  pl.BlockSpec((d, tn), lambda bi, i, j: (0, j + j0))],
        out_specs=pl.BlockSpec((None, tm, tn), lambda bi, i, j: (bi, i, j + j0)),
        out_shape=jax.ShapeDtypeStruct((b, l, w.shape[1]), BF16),
        scratch_shapes=[pltpu.VMEM((tm, d), BF16)],
        compiler_params=_cp("parallel", "parallel", "arbitrary"),
        name="in_projection",
    )(x, g, mods, mods, w)


def _qkprep_kernel(*refs, rope):
    if rope:
        q_ref, k_ref, gq_ref, gk_ref, bd_ref, cos_ref, sin_ref, qo_ref, ko_ref = refs
    else:
        q_ref, k_ref, gq_ref, gk_ref, bd_ref, qo_ref, ko_ref = refs
    for src, g_ref, dst in ((q_ref, gq_ref, qo_ref), (k_ref, gk_ref, ko_ref)):
        x = src[...].astype(F32)
        hi, lo = _split(x * x)
        ms = _dot(hi, bd_ref[...]) + _dot(lo, bd_ref[...])
        y = x * lax.rsqrt(ms + EPS) * g_ref[...]
        if rope:
            w = y.shape[1]
            lane = lax.broadcasted_iota(jnp.int32, y.shape, 1)
            first = (lane % (QK_DIM // 2)) < (QK_DIM // 4)
            partner = jnp.where(first, pltpu.roll(y, w - QK_DIM // 4, 1), pltpu.roll(y, QK_DIM // 4, 1))
            y = y * cos_ref[...] + partner * sin_ref[...]
        dst[...] = y.astype(BF16)


def _qk_prep(h, gq, gk, bd, rope_tabs):
    b, l, _ = h.shape
    w = N_HEADS * HEAD_W
    tm = min(l, 512)
    rope = rope_tabs is not None
    in_specs = [pl.BlockSpec((None, tm, w), lambda bi, i: (bi, i, COL_Q // w)),
                pl.BlockSpec((None, tm, w), lambda bi, i: (bi, i, COL_K // w)),
                pl.BlockSpec((1, w), lambda bi, i: (0, 0)),
                pl.BlockSpec((1, w), lambda bi, i: (0, 0)),
                pl.BlockSpec((w, w), lambda bi, i: (0, 0))]
    args = [h, h, gq, gk, bd]
    if rope:
        in_specs += [pl.BlockSpec((tm, w), lambda bi, i: (i, 0))] * 2
        args += list(rope_tabs)
    return pl.pallas_call(
        functools.partial(_qkprep_kernel, rope=rope),
        grid=(b, l // tm),
        in_specs=in_specs,
        out_specs=[pl.BlockSpec((None, tm, w), lambda bi, i: (bi, i, 0))] * 2,
        out_shape=[jax.ShapeDtypeStruct((b, l, w), BF16)] * 2,
        compiler_params=_cp("parallel", "parallel"),
        name="qk_prep",
    )(*args)


def _attn_kernel(*refs, n_src, lam_init):
    q_ref = refs[0]
    kv = refs[1:1 + 2 * n_src]
    lam_ref, gsub_ref, o_ref = refs[1 + 2 * n_src:]
    q = q_ref[...]
    tq = q.shape[0]
    lane = lax.broadcasted_iota(jnp.int32, q.shape, 1)
    zero = jnp.zeros_like(q)
    qq = jnp.concatenate([jnp.where(lane < QK_DIM, q, zero), jnp.where(lane >= QK_DIM, q, zero)], axis=0)
    scores = [_dot_nt(qq, kv[2 * i][...]) for i in range(n_src)]
    m = jnp.max(scores[0], axis=-1, keepdims=True)
    for s in scores[1:]:
        m = jnp.maximum(m, jnp.max(s, axis=-1, keepdims=True))
    z = jnp.zeros_like(m)
    acc = jnp.zeros((2 * tq, HEAD_W), F32)
    for i, s in enumerate(scores):
        e = jnp.exp(s - m)
        z = z + jnp.sum(e, axis=-1, keepdims=True)
        acc = acc + _dot(e.astype(BF16), kv[2 * i + 1][...])
    o2 = acc / z
    lf = lam_ref[...]
    lam_val = (jnp.exp(jnp.sum(lf[0:1] * lf[1:2], axis=-1, keepdims=True))
               - jnp.exp(jnp.sum(lf[2:3] * lf[3:4], axis=-1, keepdims=True)) + lam_init)
    o = o2[:tq] - lam_val * o2[tq:]
    ms = jnp.mean(o * o, axis=-1, keepdims=True)
    o = o * lax.rsqrt(ms + EPS) * gsub_ref[...] * (1.0 - lam_init)
    o_ref[...] = o.astype(BF16)


def _diff_attention(q, sources, lam_l, gsub, lam_init):
    b, lq, w = q.shape
    tq = 256
    in_specs = [pl.BlockSpec((None, tq, HEAD_W), lambda bi, hi, i: (bi, i, hi))]
    args = [q]
    for k, varr, vblk in sources:
        lk = k.shape[1]
        in_specs.append(pl.BlockSpec((None, lk, HEAD_W), lambda bi, hi, i: (bi, 0, hi)))
        in_specs.append(pl.BlockSpec((None, lk, HEAD_W), lambda bi, hi, i, vblk=vblk: (bi, 0, vblk + hi)))
        args += [k, varr]
    in_specs += [pl.BlockSpec(lam_l.shape, lambda bi, hi, i: (0, 0)),
                 pl.BlockSpec((1, HEAD_W), lambda bi, hi, i: (0, 0))]
    args += [lam_l, gsub]
    return pl.pallas_call(
        functools.partial(_attn_kernel, n_src=len(sources), lam_init=lam_init),
        grid=(b, N_HEADS, lq // tq),
        in_specs=in_specs,
        out_specs=pl.BlockSpec((None, tq, HEAD_W), lambda bi, hi, i: (bi, i, hi)),
        out_shape=jax.ShapeDtypeStruct((b, lq, w), BF16),
        compiler_params=_cp("parallel", "parallel", "arbitrary"),
        name="diff_attention",
    )(*args)


def _sconv_kernel(h_ref, w_ref, b_ref, o_ref):
    x = h_ref[...].astype(F32)
    n = x.shape[0]
    row = lax.broadcasted_iota(jnp.int32, x.shape, 0)
    prev = jnp.where(row == 0, 0.0, pltpu.roll(x, 1, 0))
    nxt = jnp.where(row == n - 1, 0.0, pltpu.roll(x, n - 1, 0))
    o_ref[...] = prev * w_ref[0:1, :] + x * w_ref[1:2, :] + nxt * w_ref[2:3, :] + b_ref[...]


def _short_conv(h, w, bias):
    b, l, _ = h.shape
    return pl.pallas_call(
        _sconv_kernel,
        grid=(b, 3),
        in_specs=[pl.BlockSpec((None, l, HY_W), lambda bi, j: (bi, 0, COL_HY // HY_W + j)),
                  pl.BlockSpec((3, HY_W), lambda bi, j: (0, j)),
                  pl.BlockSpec((1, HY_W), lambda bi, j: (0, j))],
        out_specs=pl.BlockSpec((None, None, l, HY_W), lambda bi, j: (bi, j, 0, 0)),
        out_shape=jax.ShapeDtypeStruct((b, 3, l, HY_W), F32),
        compiler_params=_cp("parallel", "parallel"),
        name="short_conv",
    )(h, w, bias)


def _filter_kernel(f_ref, w1_ref, b1_ref, fr_ref, w2_ref, b2_ref, w3_ref, dec_ref, o_ref):
    def mm(a, w_ref_):
        ah, al = _split(a)
        wh, wl = _split(w_ref_[...])
        return _dot(ah, wh) + _dot(ah, wl) + _dot(al, wh)
    fr = fr_ref[...]
    h = jnp.sin(fr * (mm(f_ref[...], w1_ref) + b1_ref[...]))
    h = jnp.sin(fr * (mm(h, w2_ref) + b2_ref[...]))
    h = mm(h, w3_ref)
    half = h.shape[1] // 2
    dec = dec_ref[...]
    hf = h[:, :half] * dec
    hb = h[:, half:] * dec
    row = lax.broadcasted_iota(jnp.int32, hb.shape, 0)
    hb = jnp.where(row == 0, 0.0, hb)
    norm = jnp.sum(jnp.abs(hf) + jnp.abs(hb), axis=0, keepdims=True)
    o_ref[:, :half] = (hf + hb) / norm
    o_ref[:, half:] = (hf - hb) / norm


def _hyena_filter_taps(feats, w1, b1, freq, w2, b2, w3, dec):
    l = feats.shape[0]
    n = w3.shape[1]
    full = lambda a: pl.BlockSpec(a.shape, lambda i: (0,) * a.ndim)
    args = (feats, w1, b1, freq, w2, b2, w3, dec)
    return pl.pallas_call(
        _filter_kernel,
        grid=(1,),
        in_specs=[full(a) for a in args],
        out_specs=pl.BlockSpec((l, n), lambda i: (0, 0)),
        out_shape=jax.ShapeDtypeStruct((l, n), F32),
        compiler_params=_cp("arbitrary"),
        name="hyena_filter_taps",
    )(*args)


def _table_mm_kernel(t_ref, x_ref, o_ref):
    o_ref[...] = _dot(t_ref[...], x_ref[...].astype(BF16)).astype(o_ref.dtype)


def _table_matmul(table, x, out_dtype):
    m, k = table.shape
    n = x.shape[1]
    tm = min(m, 512)
    return pl.pallas_call(
        _table_mm_kernel,
        grid=(m // tm,),
        in_specs=[pl.BlockSpec((tm, k), lambda i: (i, 0)),
                  pl.BlockSpec((k, n), lambda i: (0, 0))],
        out_specs=pl.BlockSpec((tm, n), lambda i: (i, 0)),
        out_shape=jax.ShapeDtypeStruct((m, n), out_dtype),
        compiler_params=_cp("parallel"),
        name="table_matmul",
    )(table, x)


def _dftmul_kernel(fc_ref, fs_ref, z_ref, k_ref, p_ref):
    z = z_ref[...].astype(BF16)
    zc = _dot(fc_ref[...], z)
    zs = _dot(fs_ref[...], z)
    p_ref[0] = (zc * k_ref[0] - zs * k_ref[1]).astype(BF16)
    p_ref[1] = (zc * k_ref[2] + zs * k_ref[3]).astype(BF16)


def _dft_multiply(ffwd, z, z_spec, kf):
    l = ffwd.shape[1]
    b = z.shape[0]
    tf = min(l, 512)
    nf = l // tf
    out = pl.pallas_call(
        _dftmul_kernel,
        grid=(nf, b),
        in_specs=[pl.BlockSpec((tf, l), lambda i, bi: (i, 0)),
                  pl.BlockSpec((tf, l), lambda i, bi: (i + nf, 0)),
                  z_spec,
                  pl.BlockSpec((4, tf, HY_W), lambda i, bi: (0, i, 0))],
        out_specs=pl.BlockSpec((None, 2, tf, HY_W), lambda i, bi: (bi, 0, i, 0)),
        out_shape=jax.ShapeDtypeStruct((b, 2, l, HY_W), BF16),
        compiler_params=_cp("parallel", "arbitrary"),
        name="dft_multiply",
    )(ffwd, ffwd, z, kf)
    return out.reshape(b, 2 * l, HY_W)


def _idft_gate_kernel(fi_ref, p_ref, g_ref, z_ref, b_ref, o_ref):
    conv = _dot(fi_ref[...], p_ref[...])
    o_ref[...] = (g_ref[...] * (conv + b_ref[...] * z_ref[...])).astype(o_ref.dtype)


def _idft_gate(finv, p, gate, gate_spec, z, z_spec, bias, out_dtype):
    l = finv.shape[0]
    b = p.shape[0]
    tt = min(l, 512)
    return pl.pallas_call(
        _idft_gate_kernel,
        grid=(l // tt, b),
        in_specs=[pl.BlockSpec((tt, 2 * l), lambda i, bi: (i, 0)),
                  pl.BlockSpec((None, 2 * l, HY_W), lambda i, bi: (bi, 0, 0)),
                  gate_spec, z_spec,
                  pl.BlockSpec((1, HY_W), lambda i, bi: (0, 0))],
        out_specs=pl.BlockSpec((None, tt, HY_W), lambda i, bi: (bi, i, 0)),
        out_shape=jax.ShapeDtypeStruct((b, l, HY_W), out_dtype),
        compiler_params=_cp("parallel", "arbitrary"),
        name="idft_gate",
    )(finv, p, gate, z, bias)


def _hyena_mix(h, conv_w, conv_b, kf, hy_bias, ffwd, finv):
    b, l, _ = h.shape
    tt = min(l, 512)
    u = _short_conv(h, conv_w, conv_b)
    part = lambda j, rows: pl.BlockSpec((None, None, rows, HY_W),
                                        lambda i, bi, j=j: (bi, j, i if rows != l else 0, 0))
    p = _dft_multiply(ffwd, u, part(0, l), kf[0])
    z1 = _idft_gate(finv, p, u, part(1, tt), u, part(0, tt), hy_bias[0:1], F32)
    p = _dft_multiply(ffwd, z1, pl.BlockSpec((None, l, HY_W), lambda i, bi: (bi, 0, 0)), kf[1])
    return _idft_gate(finv, p, u, part(2, tt), z1,
                      pl.BlockSpec((None, tt, HY_W), lambda i, bi: (bi, i, 0)), hy_bias[1:2], BF16)


def _fn1_kernel(z_ref, m_ref, o_ref):
    r = _dot(z_ref[...], m_ref[...])
    half = r.shape[1] // 2
    o_ref[0] = r[:, :half].astype(BF16)
    o_ref[1] = r[:, half:].astype(BF16)


def _fourier_mix(h, m1, t2):
    b, l, _ = h.shape
    tm = min(l, 512)
    zz = pl.pallas_call(
        _fn1_kernel,
        grid=(b, l // tm),
        in_specs=[pl.BlockSpec((None, tm, FN_W), lambda bi, i: (bi, i, COL_FN // FN_W)),
                  pl.BlockSpec((FN_W, 2 * FN_W), lambda bi, i: (0, 0))],
        out_specs=pl.BlockSpec((None, 2, tm, FN_W), lambda bi, i: (bi, 0, i, 0)),
        out_shape=jax.ShapeDtypeStruct((b, 2, l, FN_W), BF16),
        compiler_params=_cp("parallel", "parallel"),
        name="fnet_channels",
    )(h, m1).reshape(b, 2 * l, FN_W)
    return pl.pallas_call(
        _table_mm_kernel,
        grid=(l // tm, b),
        in_specs=[pl.BlockSpec((tm, 2 * l), lambda i, bi: (i, 0)),
                  pl.BlockSpec((None, 2 * l, FN_W), lambda i, bi: (bi, 0, 0))],
        out_specs=pl.BlockSpec((None, tm, FN_W), lambda i, bi: (bi, i, 0)),
        out_shape=jax.ShapeDtypeStruct((b, l, FN_W), BF16),
        compiler_params=_cp("parallel", "arbitrary"),
        name="fnet_positions",
    )(t2, zz)


def _merge_kernel(hy_ref, fn_ref, at_ref, g_ref, x_ref, gm_ref, why_ref, wfn_ref, wat_ref, wout_ref, o_ref):
    d = x_ref.shape[-1]
    g = 1.0 / (1.0 + jnp.exp(-g_ref[...].astype(F32)))
    y = (g[:, :d] * _dot(hy_ref[...], why_ref[...])
         + g[:, d:2 * d] * _dot(fn_ref[...], wfn_ref[...])
         + g[:, 2 * d:] * _dot(at_ref[...], wat_ref[...]))
    mix = _dot(y.astype(BF16), wout_ref[...])
    o_ref[...] = x_ref[...] + gm_ref[...] * mix


def _merge(hyo, fno, att, h, x, mods, layer, row_fn, w_hy, w_fn, w_at, w_out):
    b, l, d = x.shape
    tm = min(l, 512)
    full = lambda a: pl.BlockSpec(a.shape, lambda bi, i: (0,) * a.ndim)
    tok = lambda wdt, blk=0: pl.BlockSpec((None, tm, wdt), lambda bi, i: (bi, i, blk))
    return pl.pallas_call(
        _merge_kernel,
        grid=(b, l // tm),
        in_specs=[tok(HY_W), tok(FN_W), tok(N_HEADS * HEAD_W), tok(3 * d, COL_GATE), tok(d),
                  pl.BlockSpec((None, None, None, 1, d), _mod_spec(layer, 2, lambda bi, i: row_fn(bi))),
                  full(w_hy), full(w_fn), full(w_at), full(w_out)],
        out_specs=tok(d),
        out_shape=jax.ShapeDtypeStruct((b, l, d), F32),
        compiler_params=_cp("parallel", "parallel"),
        name="merge_residual",
    )(hyo, fno, att, h, x, mods, w_hy, w_fn, w_at, w_out)


def _peer_q_kernel(x_ref, g_ref, sh_ref, sc_ref, wq_ref, kh_ref, kl_ref, s_ref, nt_ref):
    n = _modulated_norm(x_ref[...], g_ref[...], sh_ref[...], sc_ref[...])
    nt_ref[...] = n.T.astype(BF16)
    q = _dot(n.astype(BF16), wq_ref[...])
    dq = kh_ref.shape[1]
    for hp in range(kh_ref.shape[0]):
        qh, ql = _split(q[:, hp * dq:(hp + 1) * dq])
        kh = kh_ref[hp]
        s = _dot(qh, kh) + _dot(ql, kh) + _dot(qh, kl_ref[hp])
        s_ref[hp] = s.T


def _peer_scores(x, g, mods, layer, row_fn, wq, keys_t):
    b, l, d = x.shape
    tt = min(l, 512)
    nt = l // tt
    nhp = 2 * PEER_HEADS
    mrow = lambda bi, i: row_fn(bi)
    full = lambda a: pl.BlockSpec(a.shape, lambda bi, i: (0,) * a.ndim)
    return pl.pallas_call(
        _peer_q_kernel,
        grid=(b, nt),
        in_specs=[pl.BlockSpec((None, tt, d), lambda bi, i: (bi, i, 0)),
                  pl.BlockSpec((1, d), lambda bi, i: (0, 0)),
                  pl.BlockSpec((None, None, None, 1, d), _mod_spec(layer, 3, mrow)),
                  pl.BlockSpec((None, None, None, 1, d), _mod_spec(layer, 4, mrow)),
                  full(wq), full(keys_t[0]), full(keys_t[1])],
        out_specs=[pl.BlockSpec((nhp, PEER_NKEYS, tt), lambda bi, i: (0, 0, bi * nt + i)),
                   pl.BlockSpec((d, tt), lambda bi, i: (0, bi * nt + i))],
        out_shape=[jax.ShapeDtypeStruct((nhp, PEER_NKEYS, b * l), F32),
                   jax.ShapeDtypeStruct((d, b * l), BF16)],
        compiler_params=_cp("parallel", "parallel"),
        name="peer_scores",
    )(x, g, mods, mods, wq, keys_t[0], keys_t[1])


def _top_rows(s, k):
    vals = []
    for r in range(k):
        m = jnp.max(s, axis=0, keepdims=True)
        vals.append(m)
        if r + 1 < k:
            s = jnp.where(s == m, NEG, s)
    return vals


def _peer_expert_kernel(s_ref, nt_ref, u_ref, vt_ref, x_ref, gm_ref, o_ref,
                        tau_ref, ea_ref, eb_ref, hid0_ref, hid1_ref, p0_ref, p1_ref, acc_ref):
    s = pl.program_id(1)
    nc = pl.num_programs(1) - 2
    k = PEER_TOPK
    tt = nt_ref.shape[1]
    n_i = u_ref.shape[0] // PEER_NKEYS
    piece = 32

    def hidden(hid_ref):
        hid_ref[...] = _dot(u_ref[...], nt_ref[...])

    def stages(chunk, hid_w, hid_r, p_w, p_r):
        for ii in range(n_i):
            slab = slice(ii * PEER_NKEYS, (ii + 1) * PEER_NKEYS)
            hid_w[slab, :] = _dot(u_ref[slab, :], nt_ref[...])
            i = chunk * n_i + ii
            s1_row = [s_ref[2 * h, pl.ds(i, 1), :] for h in range(PEER_HEADS)]
            ea_row = [ea_ref[h, pl.ds(i, 1), :] for h in range(PEER_HEADS)]
            for jp in range(PEER_NKEYS // piece):
                js = slice(jp * piece, (jp + 1) * piece)
                gate = jnp.zeros((piece, tt), F32)
                for h in range(PEER_HEADS):
                    sums = s1_row[h] + s_ref[2 * h + 1, js, :]
                    w = ea_row[h] * eb_ref[h, js, :]
                    gate = gate + jnp.where(sums >= tau_ref[h], w, 0.0)
                rows = slice(ii * PEER_NKEYS + jp * piece, ii * PEER_NKEYS + (jp + 1) * piece)
                hid = hid_r[rows, :]
                act = 0.5 * hid * (1.0 + lax.erf(hid * (2.0 ** -0.5)))
                p_w[rows, :] = (gate * act).astype(BF16)
            acc_ref[slab, :] += _dot(vt_ref[slab, :], p_r[...])

    @pl.when(s == 0)
    def _():
        acc_ref[...] = jnp.zeros_like(acc_ref)
        p1_ref[...] = jnp.zeros_like(p1_ref)
        hidden(hid0_ref)
        width = min(tt, 256)
        parts = tt // width

        def select(it, carry):
            h = it // parts
            cols = pl.ds(pl.multiple_of((it % parts) * width, width), width)
            s1 = s_ref[2 * h, :, cols]
            s2 = s_ref[2 * h + 1, :, cols]
            a = _top_rows(s1, k)
            b = _top_rows(s2, k)
            cand = [a[i] + b[j] for i in range(k) for j in range(k // (i + 1))]
            pad = (-len(cand)) % 8
            cand = jnp.concatenate(cand + [jnp.full_like(a[0], NEG)] * pad, axis=0)
            top = _top_rows(cand, k)
            z = jnp.zeros_like(top[0])
            for t in top:
                z = z + jnp.exp(t - top[0])
            tau_ref[h, :, cols] = top[k - 1]
            ea_ref[h, :, cols] = jnp.exp(s1 - a[0])
            eb_ref[h, :, cols] = jnp.exp(s2 - b[0]) / z
            return carry

        lax.fori_loop(0, PEER_HEADS * parts, select, 0)

    @pl.when((s >= 1) & (s <= nc) & (s % 2 == 1))
    def _():
        stages(s - 1, hid1_ref, hid0_ref, p0_ref, p1_ref)

    @pl.when((s >= 1) & (s <= nc) & (s % 2 == 0))
    def _():
        stages(s - 1, hid0_ref, hid1_ref, p1_ref, p0_ref)

    @pl.when(s == nc + 1)
    def _():
        last = p0_ref if (nc - 1) % 2 == 0 else p1_ref
        out_t = acc_ref[...] + _dot(vt_ref[...], last[...])
        o_ref[...] = x_ref[...] + gm_ref[...] * out_t.T


def _peer_experts(scores, n_t, u, v_t, x, mods, layer, row_fn):
    b, l, d = x.shape
    tt = min(l, 512)
    nt = l // tt
    nc, _, ec = v_t.shape
    assert ec == d
    nhp = scores.shape[0]
    return pl.pallas_call(
        _peer_expert_kernel,
        grid=(b * nt, nc + 2),
        in_specs=[pl.BlockSpec((nhp, PEER_NKEYS, tt), lambda t, s: (0, 0, t)),
                  pl.BlockSpec((d, tt), lambda t, s: (0, t)),
                  pl.BlockSpec((ec, d), lambda t, s: (jnp.minimum(s, nc - 1), 0)),
                  pl.BlockSpec((None, d, ec), lambda t, s: (jnp.clip(s - 2, 0, nc - 1), 0, 0)),
                  pl.BlockSpec((None, tt, d), lambda t, s: (t // nt, t % nt, 0)),
                  pl.BlockSpec((None, None, None, 1, d), _mod_spec(layer, 5, lambda t, s: row_fn(t // nt)))],
        out_specs=pl.BlockSpec((None, tt, d), lambda t, s: (t // nt, t % nt, 0)),
        out_shape=jax.ShapeDtypeStruct((b, l, d), F32),
        scratch_shapes=[pltpu.VMEM((PEER_HEADS, 1, tt), F32),
                        pltpu.VMEM((PEER_HEADS, PEER_NKEYS, tt), F32),
                        pltpu.VMEM((PEER_HEADS, PEER_NKEYS, tt), F32),
                        pltpu.VMEM((ec, tt), F32),
                        pltpu.VMEM((ec, tt), F32),
                        pltpu.VMEM((ec, tt), BF16),
                        pltpu.VMEM((ec, tt), BF16),
                        pltpu.VMEM((d, tt), F32)],
        compiler_params=_cp("parallel", "arbitrary"),
        name="peer_experts",
    )(scores, n_t, u, v_t, x, mods)


def _peer(x, g, mods, layer, row_fn, wq_t, keys, u, v_t):
    scores, n_t = _peer_scores(x, g, mods, layer, row_fn, wq_t, keys)
    return _peer_experts(scores, n_t, u, v_t, x, mods, layer, row_fn)


def _dft_tables(l):
    n = 2 * l
    r = jnp.arange(n, dtype=jnp.int32)[:, None]
    t = jnp.arange(l, dtype=jnp.int32)[None, :]
    f = jnp.where(r < l, r, r - l)
    ang = (2.0 * math.pi) * (((f * t) % n).astype(F32) / n)
    tab = jnp.where(r < l, jnp.cos(ang), jnp.sin(ang))
    tab = jnp.where(r == l, jnp.where(t % 2 == 0, 1.0, -1.0), tab)
    ffwd = tab.astype(BF16)
    return ffwd, ffwd.T


def _fnet_tables(l):
    f = jnp.arange(l, dtype=jnp.int32)[:, None]
    t = jnp.arange(l, dtype=jnp.int32)[None, :]
    ang = (2.0 * math.pi) * (((f * t) % l).astype(F32) / l)
    t2 = jnp.concatenate([jnp.cos(ang), jnp.sin(ang)], axis=1).astype(BF16)
    k = np.arange(FN_GROUP)
    ang64 = 2.0 * np.pi * ((k[:, None] * k[None, :]) % FN_GROUP) / FN_GROUP
    eye = np.eye(FN_W // FN_GROUP)
    scale = 1.0 / math.sqrt(FN_GROUP * l)
    m1 = np.concatenate([np.kron(eye, np.cos(ang64)), -np.kron(eye, np.sin(ang64))], axis=1) * scale
    return jnp.asarray(m1, F32).astype(BF16), t2


def _rope_tables(l):
    rows = l // GRID_W
    row = jnp.repeat(jnp.arange(rows), GRID_W).astype(F32)
    col = jnp.tile(jnp.arange(GRID_W), rows).astype(F32)
    half = QK_DIM // 2
    inv = ROPE_BASE ** (-jnp.arange(0, half, 2, dtype=F32) / half)
    ang = jnp.stack([row[:, None] * inv, col[:, None] * inv], axis=1)
    cos = jnp.repeat(jnp.cos(ang)[:, :, None, :], 2, axis=2)
    sin = jnp.sin(ang)
    sin = jnp.stack([-sin, sin], axis=2)
    rep = lambda a: jnp.tile(a.reshape(l, QK_DIM), (1, 2 * N_HEADS))
    return rep(cos), rep(sin)


def _filter_features(l):
    pos = jnp.arange(l, dtype=F32)
    t = pos / max(l - 1, 1)
    w = 2.0 * math.pi * pos / l
    f = jnp.linspace(1e-4, HY_BANDS - 1, HY_BANDS, dtype=F32)
    feats = jnp.concatenate([t[:, None], jnp.cos(w[:, None] * f), -jnp.sin(w[:, None] * f)], axis=-1)
    feats = jnp.pad(feats, ((0, 0), (0, 64 - HY_EMB)))
    deltas = jnp.abs(jnp.linspace(HY_MIN_DECAY, HY_MAX_DECAY, HY_W, dtype=F32))
    dec = jnp.exp(-t[:, None] * deltas)
    return feats, jnp.tile(dec, (1, HY_ORDER))


def _hyena_filters(l, tabs, ffwd, w1, b1, freq, w2, b2, w3):
    feats, dec = tabs
    taps = _hyena_filter_taps(feats, jnp.pad(w1, ((0, 64 - HY_EMB), (0, 0))), b1[None], freq[None],
                              w2, b2[None], w3, dec)
    kf = _table_matmul(ffwd, taps, F32)
    half = HY_ORDER * HY_W
    kc = kf[:l, :half]
    nyq = kf[l, :half]
    ks = kf[l:, half:]
    n = 2.0 * l
    first = (jnp.arange(l) == 0)[:, None]
    wc = jnp.where(first, 1.0 / n, 2.0 / n)
    ka = kc * wc
    kb = jnp.where(first, 0.0, ks * (2.0 / n))
    kd = jnp.where(first, nyq[None, :] / n, kc * (2.0 / n))
    stack = jnp.stack([ka, kb, kb, kd], axis=0)
    return jnp.moveaxis(stack.reshape(4, l, HY_ORDER, HY_W), 2, 0)


def kernel(x, c, ctx, c_ctx, w_ada, b_ada, g_mix, g_ffn, w_in, hy_conv_w, hy_conv_b, hy_w1, hy_b1, hy_freq, hy_w2, hy_b2, hy_w3, hy_bias, g_q, g_k, lam, g_sub, w_hy, w_fn, w_at, w_out, peer_wq, peer_keys, peer_u, peer_v):
    bsz, seq, d = x.shape
    clen = ctx.shape[1]
    depth = w_ada.shape[0]

    cc = jnp.concatenate([c, c_ctx[None], jnp.zeros((MOD_ROWS - bsz - 1, d), F32)], axis=0)
    mods = _ada_mods(cc, w_ada, b_ada)
    lat_row = lambda bi: bi
    ctx_row = lambda bi: bsz

    rope = _rope_tables(seq)
    tabs = {n: dict(dft=_dft_tables(n), fnet=_fnet_tables(n), feat=_filter_features(n)) for n in (seq, clen)}
    w = N_HEADS * HEAD_W
    lane = np.arange(w)
    bd = jnp.asarray((lane[:, None] // QK_DIM == lane[None, :] // QK_DIM) / QK_DIM, F32).astype(BF16)

    xl, xc = x, ctx
    for l in range(depth):
        last = l == depth - 1
        lam_init = 0.8 - 0.6 * math.exp(-0.3 * l)
        wl = w_in[l]
        w_perm = jnp.concatenate([wl[:, 2560:], wl[:, :2560]], axis=1).astype(BF16)
        gq = jnp.tile(g_q[l].reshape(1, HEAD_W), (1, N_HEADS)) * (QK_DIM ** -0.5)
        gk = jnp.tile(g_k[l].reshape(1, HEAD_W), (1, N_HEADS))
        gsub = g_sub[l][None]
        wts = [a[l].astype(BF16) for a in (w_hy, w_fn, w_at, w_out)]
        filt_args = (hy_w1[l], hy_b1[l], hy_freq[l], hy_w2[l], hy_b2[l], hy_w3[l])

        h_l = _in_projection(xl, g_mix[l][None], mods, l, lat_row, w_perm, 0, P_IN)
        h_c = _in_projection(xc, g_mix[l][None], mods, l, ctx_row, w_perm, COL_K if last else 0, P_IN)
        q_l, k_l = _qk_prep(h_l, gq, gk, bd, rope)
        q_c, k_c = _qk_prep(h_c, gq, gk, bd, None)
        vblk = COL_V // HEAD_W
        att_l = _diff_attention(q_l, [(k_l, h_l, vblk), (k_c, h_c, vblk)], lam[l], gsub, lam_init)
        ffwd, finv = tabs[seq]["dft"]
        kf = _hyena_filters(seq, tabs[seq]["feat"], ffwd, *filt_args)
        hyo_l = _hyena_mix(h_l, hy_conv_w[l], hy_conv_b[l][None], kf, hy_bias[l], ffwd, finv)
        fno_l = _fourier_mix(h_l, *tabs[seq]["fnet"])
        if not last:
            att_c = _diff_attention(q_c, [(k_c, h_c, vblk)], lam[l], gsub, lam_init)
            ffwd_c, finv_c = tabs[clen]["dft"]
            kf_c = _hyena_filters(clen, tabs[clen]["feat"], ffwd_c, *filt_args)
            hyo_c = _hyena_mix(h_c, hy_conv_w[l], hy_conv_b[l][None], kf_c, hy_bias[l], ffwd_c, finv_c)
            fno_c = _fourier_mix(h_c, *tabs[clen]["fnet"])
            xc = _merge(hyo_c, fno_c, att_c, h_c, xc, mods, l, ctx_row, *wts)
        xl = _merge(hyo_l, fno_l, att_l, h_l, xl, mods, l, lat_row, *wts)

        wq_t = peer_wq[l].astype(BF16)
        keys = _split(jnp.swapaxes(peer_keys[l].reshape(2 * PEER_HEADS, PEER_NKEYS, -1), 1, 2))
        u = peer_u[l].astype(BF16)
        v_t = jnp.swapaxes(peer_v[l].reshape(-1, d, d), 1, 2).astype(BF16)
        if not last:
            xc = _peer(xc, g_ffn[l][None], mods, l, ctx_row, wq_t, keys, u, v_t)
        xl = _peer(xl, g_ffn[l][None], mods, l, lat_row, wq_t, keys, u, v_t)
    return xl
```

```python
import functools
import math

import jax
import jax.numpy as jnp
import numpy as np
from jax import lax
from jax.experimental import pallas as pl
from jax.experimental.pallas import tpu as pltpu

F32 = jnp.float32
BF16 = jnp.bfloat16

EPS = 1e-6
GRID_W = 64
ROPE_BASE = 10000.0
N_HEADS = 4
QK_DIM = 64
HEAD_W = 2 * QK_DIM
HY_W = 256
HY_ORDER = 2
HY_EMB = 33
HY_BANDS = (HY_EMB - 1) // 2
HY_MIN_DECAY = math.log(1e-2) / 1.5
HY_MAX_DECAY = math.log(1e-2) / 0.3
FN_GROUP = 64
FN_W = 256
PEER_HEADS = 8
PEER_NKEYS = 128
PEER_TOPK = 16
N_MOD = 6
MOD_ROWS = 16
NEG = -3.0e38
MXU_SLAB = 512
MXU_COLS = 256

VMEM_LIMIT = 56 * 1024 * 1024

COL_GATE = 0
COL_HY = 3072
COL_FN = 3840
COL_Q = 4096
COL_K = 4608
COL_V = 5120
P_IN = 5632


def _cp(*sem):
    return pltpu.CompilerParams(dimension_semantics=sem, vmem_limit_bytes=VMEM_LIMIT)


def _dot(a, b):
    return jnp.dot(a, b, preferred_element_type=F32)


def _dot_nt(a, b):
    return lax.dot_general(a, b, (((1,), (1,)), ((), ())), preferred_element_type=F32)


def _split(a):
    hi = a.astype(BF16)
    lo = (a - hi.astype(F32)).astype(BF16)
    return hi, lo


def _modulated_norm(x, g, shift, scale):
    ms = jnp.mean(x * x, axis=-1, keepdims=True)
    y = x * lax.rsqrt(ms + EPS) * g
    return y * (1.0 + scale) + shift


def _mod_spec(layer, chunk, row_fn):
    def imap(*idx):
        return (layer, row_fn(*idx), chunk, 0, 0)
    return imap


def _ada_kernel(c_ref, w_ref, b_ref, o_ref):
    c = c_ref[...]
    a = c / (1.0 + jnp.exp(-c))
    ah, al = _split(a)
    wh, wl = _split(w_ref[...])
    o_ref[...] = _dot(ah, wh) + _dot(ah, wl) + _dot(al, wh) + b_ref[...]


def _ada_mods(cc, w_ada, b_ada):
    depth, d, n = w_ada.shape
    tn = 512
    out = pl.pallas_call(
        _ada_kernel,
        grid=(depth, n // tn),
        in_specs=[pl.BlockSpec((MOD_ROWS, d), lambda l, j: (0, 0)),
                  pl.BlockSpec((None, d, tn), lambda l, j: (l, 0, j)),
                  pl.BlockSpec((None, 1, tn), lambda l, j: (l, 0, j))],
        out_specs=pl.BlockSpec((None, MOD_ROWS, tn), lambda l, j: (l, 0, j)),
        out_shape=jax.ShapeDtypeStruct((depth, MOD_ROWS, n), F32),
        compiler_params=_cp("parallel", "parallel"),
        name="ada_mods",
    )(cc, w_ada, b_ada.reshape(depth, 1, n))
    return out.reshape(depth, MOD_ROWS, N_MOD, 1, d)


def _inproj_kernel(x_ref, g_ref, sh_ref, sc_ref, w_ref, o_ref, xn_ref):
    @pl.when(pl.program_id(2) == 0)
    def _():
        xn_ref[...] = _modulated_norm(x_ref[...], g_ref[...], sh_ref[...], sc_ref[...]).astype(BF16)
    o_ref[...] = _dot(xn_ref[...], w_ref[...]).astype(o_ref.dtype)


def _in_projection(x, g, mods, layer, row_fn, w, col_lo, col_hi):
    b, l, d = x.shape
    tm = min(l, 1024)
    tn = 512
    j0 = col_lo // tn
    nj = (col_hi - col_lo) // tn
    mrow = lambda bi, i, j: row_fn(bi)
    return pl.pallas_call(
        _inproj_kernel,
        grid=(b, l // tm, nj),
        in_specs=[pl.BlockSpec((None, tm, d), lambda bi, i, j: (bi, i, 0)),
                  pl.BlockSpec((1, d), lambda bi, i, j: (0, 0)),
                  pl.BlockSpec((None, None, None, 1, d), _mod_spec(layer, 0, mrow)),
                  pl.BlockSpec((None, None, None, 1, d), _mod_spec(layer, 1, mrow)),
                  pl.BlockSpec((d, tn), lambda bi, i, j: (0, j + j0))],
        out_specs=pl.BlockSpec((None, tm, tn), lambda bi, i, j: (bi, i, j + j0)),
        out_shape=jax.ShapeDtypeStruct((b, l, w.shape[1]), BF16),
        scratch_shapes=[pltpu.VMEM((tm, d), BF16)],
        compiler_params=_cp("parallel", "parallel", "arbitrary"),
        name="in_projection",
    )(x, g, mods, mods, w)


def _qkprep_kernel(*refs, rope):
    if rope:
        q_ref, k_ref, gq_ref, gk_ref, bd_ref, cos_ref, sin_ref, qo_ref, ko_ref = refs
    else:
        q_ref, k_ref, gq_ref, gk_ref, bd_ref, qo_ref, ko_ref = refs
    for src, g_ref, dst in ((q_ref, gq_ref, qo_ref), (k_ref, gk_ref, ko_ref)):
        x = src[...].astype(F32)
        hi, lo = _split(x * x)
        ms = _dot(hi, bd_ref[...]) + _dot(lo, bd_ref[...])
        y = x * lax.rsqrt(ms + EPS) * g_ref[...]
        if rope:
            w = y.shape[1]
            lane = lax.broadcasted_iota(jnp.int32, y.shape, 1)
            first = (lane % (QK_DIM // 2)) < (QK_DIM // 4)
            partner = jnp.where(first, pltpu.roll(y, w - QK_DIM // 4, 1), pltpu.roll(y, QK_DIM // 4, 1))
            y = y * cos_ref[...] + partner * sin_ref[...]
        dst[...] = y.astype(BF16)


def _qk_prep(h, gq, gk, bd, rope_tabs):
    b, l, _ = h.shape
    w = N_HEADS * HEAD_W
    tm = min(l, 512)
    rope = rope_tabs is not None
    in_specs = [pl.BlockSpec((None, tm, w), lambda bi, i: (bi, i, COL_Q // w)),
                pl.BlockSpec((None, tm, w), lambda bi, i: (bi, i, COL_K // w)),
                pl.BlockSpec((1, w), lambda bi, i: (0, 0)),
                pl.BlockSpec((1, w), lambda bi, i: (0, 0)),
                pl.BlockSpec((w, w), lambda bi, i: (0, 0))]
    args = [h, h, gq, gk, bd]
    if rope:
        in_specs += [pl.BlockSpec((tm, w), lambda bi, i: (i, 0))] * 2
        args += list(rope_tabs)
    return pl.pallas_call(
        functools.partial(_qkprep_kernel, rope=rope),
        grid=(b, l // tm),
        in_specs=in_specs,
        out_specs=[pl.BlockSpec((None, tm, w), lambda bi, i: (bi, i, 0))] * 2,
        out_shape=[jax.ShapeDtypeStruct((b, l, w), BF16)] * 2,
        compiler_params=_cp("parallel", "parallel"),
        name="qk_prep",
    )(*args)


def _attn_kernel(*refs, n_src, lam_init):
    q_ref = refs[0]
    kv = refs[1:1 + 2 * n_src]
    lam_ref, gsub_ref, o_ref = refs[1 + 2 * n_src:]
    q = q_ref[...]
    tq = q.shape[0]
    lane = lax.broadcasted_iota(jnp.int32, q.shape, 1)
    zero = jnp.zeros_like(q)
    qq = jnp.concatenate([jnp.where(lane < QK_DIM, q, zero), jnp.where(lane >= QK_DIM, q, zero)], axis=0)
    scores = [_dot_nt(qq, kv[2 * i][...]) for i in range(n_src)]
    m = jnp.max(scores[0], axis=-1, keepdims=True)
    for s in scores[1:]:
        m = jnp.maximum(m, jnp.max(s, axis=-1, keepdims=True))
    z = jnp.zeros_like(m)
    acc = jnp.zeros((2 * tq, HEAD_W), F32)
    for i, s in enumerate(scores):
        e = jnp.exp(s - m)
        z = z + jnp.sum(e, axis=-1, keepdims=True)
        acc = acc + _dot(e.astype(BF16), kv[2 * i + 1][...])
    o2 = acc / z
    lf = lam_ref[...]
    lam_val = (jnp.exp(jnp.sum(lf[0:1] * lf[1:2], axis=-1, keepdims=True))
               - jnp.exp(jnp.sum(lf[2:3] * lf[3:4], axis=-1, keepdims=True)) + lam_init)
    o = o2[:tq] - lam_val * o2[tq:]
    ms = jnp.mean(o * o, axis=-1, keepdims=True)
    o = o * lax.rsqrt(ms + EPS) * gsub_ref[...] * (1.0 - lam_init)
    o_ref[...] = o.astype(BF16)


def _diff_attention(q, sources, lam_l, gsub, lam_init):
    b, lq, w = q.shape
    tq = 256
    in_specs = [pl.BlockSpec((None, tq, HEAD_W), lambda bi, hi, i: (bi, i, hi))]
    args = [q]
    for k, varr, vblk in sources:
        lk = k.shape[1]
        in_specs.append(pl.BlockSpec((None, lk, HEAD_W), lambda bi, hi, i: (bi, 0, hi)))
        in_specs.append(pl.BlockSpec((None, lk, HEAD_W), lambda bi, hi, i, vblk=vblk: (bi, 0, vblk + hi)))
        args += [k, varr]
    in_specs += [pl.BlockSpec(lam_l.shape, lambda bi, hi, i: (0, 0)),
                 pl.BlockSpec((1, HEAD_W), lambda bi, hi, i: (0, 0))]
    args += [lam_l, gsub]
    return pl.pallas_call(
        functools.partial(_attn_kernel, n_src=len(sources), lam_init=lam_init),
        grid=(b, N_HEADS, lq // tq),
        in_specs=in_specs,
        out_specs=pl.BlockSpec((None, tq, HEAD_W), lambda bi, hi, i: (bi, i, hi)),
        out_shape=jax.ShapeDtypeStruct((b, lq, w), BF16),
        compiler_params=_cp("parallel", "parallel", "arbitrary"),
        name="diff_attention",
    )(*args)


def _sconv_kernel(h_ref, w_ref, b_ref, o_ref):
    x = h_ref[...].astype(F32)
    n = x.shape[0]
    row = lax.broadcasted_iota(jnp.int32, x.shape, 0)
    prev = jnp.where(row == 0, 0.0, pltpu.roll(x, 1, 0))
    nxt = jnp.where(row == n - 1, 0.0, pltpu.roll(x, n - 1, 0))
    o_ref[...] = prev * w_ref[0:1, :] + x * w_ref[1:2, :] + nxt * w_ref[2:3, :] + b_ref[...]


def _short_conv(h, w, bias):
    b, l, _ = h.shape
    return pl.pallas_call(
        _sconv_kernel,
        grid=(b, 3),
        in_specs=[pl.BlockSpec((None, l, HY_W), lambda bi, j: (bi, 0, COL_HY // HY_W + j)),
                  pl.BlockSpec((3, HY_W), lambda bi, j: (0, j)),
                  pl.BlockSpec((1, HY_W), lambda bi, j: (0, j))],
        out_specs=pl.BlockSpec((None, None, l, HY_W), lambda bi, j: (bi, j, 0, 0)),
        out_shape=jax.ShapeDtypeStruct((b, 3, l, HY_W), F32),
        compiler_params=_cp("parallel", "parallel"),
        name="short_conv",
    )(h, w, bias)


def _filter_kernel(f_ref, w1_ref, b1_ref, fr_ref, w2_ref, b2_ref, w3_ref, dec_ref, o_ref):
    def mm(a, w_ref_):
        ah, al = _split(a)
        wh, wl = _split(w_ref_[...])
        return _dot(ah, wh) + _dot(ah, wl) + _dot(al, wh)
    fr = fr_ref[...]
    h = jnp.sin(fr * (mm(f_ref[...], w1_ref) + b1_ref[...]))
    h = jnp.sin(fr * (mm(h, w2_ref) + b2_ref[...]))
    h = mm(h, w3_ref)
    half = h.shape[1] // 2
    dec = dec_ref[...]
    hf = h[:, :half] * dec
    hb = h[:, half:] * dec
    row = lax.broadcasted_iota(jnp.int32, hb.shape, 0)
    hb = jnp.where(row == 0, 0.0, hb)
    norm = jnp.sum(jnp.abs(hf) + jnp.abs(hb), axis=0, keepdims=True)
    o_ref[:, :half] = (hf + hb) / norm
    o_ref[:, half:] = (hf - hb) / norm


def _hyena_filter_taps(feats, w1, b1, freq, w2, b2, w3, dec):
    l = feats.shape[0]
    n = w3.shape[1]
    full = lambda a: pl.BlockSpec(a.shape, lambda i: (0,) * a.ndim)
    args = (feats, w1, b1, freq, w2, b2, w3, dec)
    return pl.pallas_call(
        _filter_kernel,
        grid=(1,),
        in_specs=[full(a) for a in args],
        out_specs=pl.BlockSpec((l, n), lambda i: (0, 0)),
        out_shape=jax.ShapeDtypeStruct((l, n), F32),
        compiler_params=_cp("arbitrary"),
        name="hyena_filter_taps",
    )(*args)


def _table_mm_kernel(t_ref, x_ref, o_ref):
    o_ref[...] = _dot(t_ref[...], x_ref[...].astype(BF16)).astype(o_ref.dtype)


def _table_matmul(table, x, out_dtype):
    m, k = table.shape
    n = x.shape[1]
    tm = min(m, 512)
    return pl.pallas_call(
        _table_mm_kernel,
        grid=(m // tm,),
        in_specs=[pl.BlockSpec((tm, k), lambda i: (i, 0)),
                  pl.BlockSpec((k, n), lambda i: (0, 0))],
        out_specs=pl.BlockSpec((tm, n), lambda i: (i, 0)),
        out_shape=jax.ShapeDtypeStruct((m, n), out_dtype),
        compiler_params=_cp("parallel"),
        name="table_matmul",
    )(table, x)


def _dftmul_kernel(fc_ref, fs_ref, z_ref, k_ref, p_ref):
    z = z_ref[...].astype(BF16)
    zc = _dot(fc_ref[...], z)
    zs = _dot(fs_ref[...], z)
    p_ref[0] = (zc * k_ref[0] - zs * k_ref[1]).astype(BF16)
    p_ref[1] = (zc * k_ref[2] + zs * k_ref[3]).astype(BF16)


def _dft_multiply(ffwd, z, z_spec, kf):
    l = ffwd.shape[1]
    b = z.shape[0]
    tf = min(l, 512)
    nf = l // tf
    out = pl.pallas_call(
        _dftmul_kernel,
        grid=(nf, b),
        in_specs=[pl.BlockSpec((tf, l), lambda i, bi: (i, 0)),
                  pl.BlockSpec((tf, l), lambda i, bi: (i + nf, 0)),
                  z_spec,
                  pl.BlockSpec((4, tf, HY_W), lambda i, bi: (0, i, 0))],
        out_specs=pl.BlockSpec((None, 2, tf, HY_W), lambda i, bi: (bi, 0, i, 0)),
        out_shape=jax.ShapeDtypeStruct((b, 2, l, HY_W), BF16),
        compiler_params=_cp("parallel", "arbitrary"),
        name="dft_multiply",
    )(ffwd, ffwd, z, kf)
    return out.reshape(b, 2 * l, HY_W)


def _idft_gate_kernel(fi_ref, p_ref, g_ref, z_ref, b_ref, o_ref):
    conv = _dot(fi_ref[...], p_ref[...])
    o_ref[...] = (g_ref[...] * (conv + b_ref[...] * z_ref[...])).astype(o_ref.dtype)


def _idft_gate(finv, p, gate, gate_spec, z, z_spec, bias, out_dtype):
    l = finv.shape[0]
    b = p.shape[0]
    tt = min(l, 512)
    return pl.pallas_call(
        _idft_gate_kernel,
        grid=(l // tt, b),
        in_specs=[pl.BlockSpec((tt, 2 * l), lambda i, bi: (i, 0)),
                  pl.BlockSpec((None, 2 * l, HY_W), lambda i, bi: (bi, 0, 0)),
                  gate_spec, z_spec,
                  pl.BlockSpec((1, HY_W), lambda i, bi: (0, 0))],
        out_specs=pl.BlockSpec((None, tt, HY_W), lambda i, bi: (bi, i, 0)),
        out_shape=jax.ShapeDtypeStruct((b, l, HY_W), out_dtype),
        compiler_params=_cp("parallel", "arbitrary"),
        name="idft_gate",
    )(finv, p, gate, z, bias)


def _hyena_mix(h, conv_w, conv_b, kf, hy_bias, ffwd, finv):
    b, l, _ = h.shape
    tt = min(l, 512)
    u = _short_conv(h, conv_w, conv_b)
    part = lambda j, rows: pl.BlockSpec((None, None, rows, HY_W),
                                        lambda i, bi, j=j: (bi, j, i if rows != l else 0, 0))
    p = _dft_multiply(ffwd, u, part(0, l), kf[0])
    z1 = _idft_gate(finv, p, u, part(1, tt), u, part(0, tt), hy_bias[0:1], F32)
    p = _dft_multiply(ffwd, z1, pl.BlockSpec((None, l, HY_W), lambda i, bi: (bi, 0, 0)), kf[1])
    return _idft_gate(finv, p, u, part(2, tt), z1,
                      pl.BlockSpec((None, tt, HY_W), lambda i, bi: (bi, i, 0)), hy_bias[1:2], BF16)


def _fn1_kernel(z_ref, m_ref, o_ref):
    r = _dot(z_ref[...], m_ref[...])
    half = r.shape[1] // 2
    o_ref[0] = r[:, :half].astype(BF16)
    o_ref[1] = r[:, half:].astype(BF16)


def _fourier_mix(h, m1, t2):
    b, l, _ = h.shape
    tm = min(l, 512)
    zz = pl.pallas_call(
        _fn1_kernel,
        grid=(b, l // tm),
        in_specs=[pl.BlockSpec((None, tm, FN_W), lambda bi, i: (bi, i, COL_FN // FN_W)),
                  pl.BlockSpec((FN_W, 2 * FN_W), lambda bi, i: (0, 0))],
        out_specs=pl.BlockSpec((None, 2, tm, FN_W), lambda bi, i: (bi, 0, i, 0)),
        out_shape=jax.ShapeDtypeStruct((b, 2, l, FN_W), BF16),
        compiler_params=_cp("parallel", "parallel"),
        name="fnet_channels",
    )(h, m1).reshape(b, 2 * l, FN_W)
    return pl.pallas_call(
        _table_mm_kernel,
        grid=(l // tm, b),
        in_specs=[pl.BlockSpec((tm, 2 * l), lambda i, bi: (i, 0)),
                  pl.BlockSpec((None, 2 * l, FN_W), lambda i, bi: (bi, 0, 0))],
        out_specs=pl.BlockSpec((None, tm, FN_W), lambda i, bi: (bi, i, 0)),
        out_shape=jax.ShapeDtypeStruct((b, l, FN_W), BF16),
        compiler_params=_cp("parallel", "arbitrary"),
        name="fnet_positions",
    )(t2, zz)


def _merge_kernel(hy_ref, fn_ref, at_ref, g_ref, x_ref, gm_ref, why_ref, wfn_ref, wat_ref, wout_ref, o_ref):
    d = x_ref.shape[-1]
    g = 1.0 / (1.0 + jnp.exp(-g_ref[...].astype(F32)))
    y = (g[:, :d] * _dot(hy_ref[...], why_ref[...])
         + g[:, d:2 * d] * _dot(fn_ref[...], wfn_ref[...])
         + g[:, 2 * d:] * _dot(at_ref[...], wat_ref[...]))
    mix = _dot(y.astype(BF16), wout_ref[...])
    o_ref[...] = x_ref[...] + gm_ref[...] * mix


def _merge(hyo, fno, att, h, x, mods, layer, row_fn, w_hy, w_fn, w_at, w_out):
    b, l, d = x.shape
    tm = min(l, 512)
    full = lambda a: pl.BlockSpec(a.shape, lambda bi, i: (0,) * a.ndim)
    tok = lambda wdt, blk=0: pl.BlockSpec((None, tm, wdt), lambda bi, i: (bi, i, blk))
    return pl.pallas_call(
        _merge_kernel,
        grid=(b, l // tm),
        in_specs=[tok(HY_W), tok(FN_W), tok(N_HEADS * HEAD_W), tok(3 * d, COL_GATE), tok(d),
                  pl.BlockSpec((None, None, None, 1, d), _mod_spec(layer, 2, lambda bi, i: row_fn(bi))),
                  full(w_hy), full(w_fn), full(w_at), full(w_out)],
        out_specs=tok(d),
        out_shape=jax.ShapeDtypeStruct((b, l, d), F32),
        compiler_params=_cp("parallel", "parallel"),
        name="merge_residual",
    )(hyo, fno, att, h, x, mods, w_hy, w_fn, w_at, w_out)


def _peer_q_kernel(x_ref, g_ref, sh_ref, sc_ref, wq_ref, kh_ref, kl_ref, s_ref, nt_ref):
    n = _modulated_norm(x_ref[...], g_ref[...], sh_ref[...], sc_ref[...])
    nt_ref[...] = n.T.astype(BF16)
    q = _dot(n.astype(BF16), wq_ref[...])
    dq = kh_ref.shape[1]
    for hp in range(kh_ref.shape[0]):
        qh, ql = _split(q[:, hp * dq:(hp + 1) * dq])
        kh = kh_ref[hp]
        s = _dot(qh, kh) + _dot(ql, kh) + _dot(qh, kl_ref[hp])
        s_ref[hp] = s.T


def _peer_scores(x, g, mods, layer, row_fn, wq, keys_t):
    b, l, d = x.shape
    tt = min(l, 512)
    nt = l // tt
    nhp = 2 * PEER_HEADS
    mrow = lambda bi, i: row_fn(bi)
    full = lambda a: pl.BlockSpec(a.shape, lambda bi, i: (0,) * a.ndim)
    return pl.pallas_call(
        _peer_q_kernel,
        grid=(b, nt),
        in_specs=[pl.BlockSpec((None, tt, d), lambda bi, i: (bi, i, 0)),
                  pl.BlockSpec((1, d), lambda bi, i: (0, 0)),
                  pl.BlockSpec((None, None, None, 1, d), _mod_spec(layer, 3, mrow)),
                  pl.BlockSpec((None, None, None, 1, d), _mod_spec(layer, 4, mrow)),
                  full(wq), full(keys_t[0]), full(keys_t[1])],
        out_specs=[pl.BlockSpec((nhp, PEER_NKEYS, tt), lambda bi, i: (0, 0, bi * nt + i)),
                   pl.BlockSpec((d, tt), lambda bi, i: (0, bi * nt + i))],
        out_shape=[jax.ShapeDtypeStruct((nhp, PEER_NKEYS, b * l), F32),
                   jax.ShapeDtypeStruct((d, b * l), BF16)],
        compiler_params=_cp("parallel", "parallel"),
        name="peer_scores",
    )(x, g, mods, mods, wq, keys_t[0], keys_t[1])


def _top_rows(s, k):
    vals = []
    for r in range(k):
        m = jnp.max(s, axis=0, keepdims=True)
        vals.append(m)
        if r + 1 < k:
            s = jnp.where(s == m, NEG, s)
    return vals


def _peer_expert_kernel(s_ref, nt_ref, u_ref, vt_ref, x_ref, gm_ref, o_ref,
                        tau_ref, ea_ref, eb_ref, hid0_ref, hid1_ref, p0_ref, p1_ref, acc_ref):
    s = pl.program_id(1)
    nc = pl.num_programs(1) - 2
    k = PEER_TOPK
    tt = nt_ref.shape[1]
    n_i = u_ref.shape[0] // PEER_NKEYS
    piece = 32

    def hidden(hid_ref):
        hid_ref[...] = _dot(u_ref[...], nt_ref[...])

    def stages(chunk, hid_w, hid_r, p_w, p_r):
        def hidden_task(slab, cols):
            def run():
                hid_w[slab, cols] = _dot(u_ref[slab, :], nt_ref[:, cols])
            return run

        def project_task(slab, cols):
            def run():
                acc_ref[slab, cols] += _dot(vt_ref[slab, :], p_r[:, cols])
            return run

        tasks = []
        for r0 in range(0, u_ref.shape[0], MXU_SLAB):
            for c0 in range(0, tt, MXU_COLS):
                slab, cols = slice(r0, r0 + MXU_SLAB), slice(c0, min(c0 + MXU_COLS, tt))
                tasks += [hidden_task(slab, cols), project_task(slab, cols)]
        per = -(-len(tasks) // n_i)
        for ii in range(n_i):
            for task in tasks[ii * per:(ii + 1) * per]:
                task()
            i = chunk * n_i + ii
            s1_row = [s_ref[2 * h, pl.ds(i, 1), :] for h in range(PEER_HEADS)]
            ea_row = [ea_ref[h, pl.ds(i, 1), :] for h in range(PEER_HEADS)]
            for jp in range(PEER_NKEYS // piece):
                js = slice(jp * piece, (jp + 1) * piece)
                gate = jnp.zeros((piece, tt), F32)
                for h in range(PEER_HEADS):
                    sums = s1_row[h] + s_ref[2 * h + 1, js, :]
                    w = ea_row[h] * eb_ref[h, js, :]
                    gate = gate + jnp.where(sums >= tau_ref[h], w, 0.0)
                rows = slice(ii * PEER_NKEYS + jp * piece, ii * PEER_NKEYS + (jp + 1) * piece)
                hid = hid_r[rows, :]
                act = 0.5 * hid * (1.0 + lax.erf(hid * (2.0 ** -0.5)))
                p_w[rows, :] = (gate * act).astype(BF16)

    @pl.when(s == 0)
    def _():
        acc_ref[...] = jnp.zeros_like(acc_ref)
        p1_ref[...] = jnp.zeros_like(p1_ref)
        hidden(hid0_ref)
        width = min(tt, 256)
        parts = tt // width

        def select(it, carry):
            h = it // parts
            cols = pl.ds(pl.multiple_of((it % parts) * width, width), width)
            s1 = s_ref[2 * h, :, cols]
            s2 = s_ref[2 * h + 1, :, cols]
            a = _top_rows(s1, k)
            b = _top_rows(s2, k)
            cand = [a[i] + b[j] for i in range(k) for j in range(k // (i + 1))]
            pad = (-len(cand)) % 8
            cand = jnp.concatenate(cand + [jnp.full_like(a[0], NEG)] * pad, axis=0)
            top = _top_rows(cand, k)
            z = jnp.zeros_like(top[0])
            for t in top:
                z = z + jnp.exp(t - top[0])
            tau_ref[h, :, cols] = top[k - 1]
            ea_ref[h, :, cols] = jnp.exp(s1 - a[0])
            eb_ref[h, :, cols] = jnp.exp(s2 - b[0]) / z
            return carry

        lax.fori_loop(0, PEER_HEADS * parts, select, 0)

    @pl.when((s >= 1) & (s <= nc) & (s % 2 == 1))
    def _():
        stages(s - 1, hid1_ref, hid0_ref, p0_ref, p1_ref)

    @pl.when((s >= 1) & (s <= nc) & (s % 2 == 0))
    def _():
        stages(s - 1, hid0_ref, hid1_ref, p1_ref, p0_ref)

    @pl.when(s == nc + 1)
    def _():
        last = p0_ref if (nc - 1) % 2 == 0 else p1_ref
        out_t = acc_ref[...] + _dot(vt_ref[...], last[...])
        o_ref[...] = x_ref[...] + gm_ref[...] * out_t.T


def _peer_experts(scores, n_t, u, v_t, x, mods, layer, row_fn):
    b, l, d = x.shape
    tt = min(l, 512)
    nt = l // tt
    nc, _, ec = v_t.shape
    assert ec == d
    nhp = scores.shape[0]
    return pl.pallas_call(
        _peer_expert_kernel,
        grid=(b * nt, nc + 2),
        in_specs=[pl.BlockSpec((nhp, PEER_NKEYS, tt), lambda t, s: (0, 0, t)),
                  pl.BlockSpec((d, tt), lambda t, s: (0, t)),
                  pl.BlockSpec((ec, d), lambda t, s: (jnp.minimum(s, nc - 1), 0)),
                  pl.BlockSpec((None, d, ec), lambda t, s: (jnp.clip(s - 2, 0, nc - 1), 0, 0)),
                  pl.BlockSpec((None, tt, d), lambda t, s: (t // nt, t % nt, 0)),
                  pl.BlockSpec((None, None, None, 1, d), _mod_spec(layer, 5, lambda t, s: row_fn(t // nt)))],
        out_specs=pl.BlockSpec((None, tt, d), lambda t, s: (t // nt, t % nt, 0)),
        out_shape=jax.ShapeDtypeStruct((b, l, d), F32),
        scratch_shapes=[pltpu.VMEM((PEER_HEADS, 1, tt), F32),
                        pltpu.VMEM((PEER_HEADS, PEER_NKEYS, tt), F32),
                        pltpu.VMEM((PEER_HEADS, PEER_NKEYS, tt), F32),
                        pltpu.VMEM((ec, tt), F32),
                        pltpu.VMEM((ec, tt), F32),
                        pltpu.VMEM((ec, tt), BF16),
                        pltpu.VMEM((ec, tt), BF16),
                        pltpu.VMEM((d, tt), F32)],
        compiler_params=_cp("parallel", "arbitrary"),
        name="peer_experts",
    )(scores, n_t, u, v_t, x, mods)


def _peer(x, g, mods, layer, row_fn, wq_t, keys, u, v_t):
    scores, n_t = _peer_scores(x, g, mods, layer, row_fn, wq_t, keys)
    return _peer_experts(scores, n_t, u, v_t, x, mods, layer, row_fn)


def _dft_tables(l):
    n = 2 * l
    r = jnp.arange(n, dtype=jnp.int32)[:, None]
    t = jnp.arange(l, dtype=jnp.int32)[None, :]
    f = jnp.where(r < l, r, r - l)
    ang = (2.0 * math.pi) * (((f * t) % n).astype(F32) / n)
    tab = jnp.where(r < l, jnp.cos(ang), jnp.sin(ang))
    tab = jnp.where(r == l, jnp.where(t % 2 == 0, 1.0, -1.0), tab)
    ffwd = tab.astype(BF16)
    return ffwd, ffwd.T


def _fnet_tables(l):
    f = jnp.arange(l, dtype=jnp.int32)[:, None]
    t = jnp.arange(l, dtype=jnp.int32)[None, :]
    ang = (2.0 * math.pi) * (((f * t) % l).astype(F32) / l)
    t2 = jnp.concatenate([jnp.cos(ang), jnp.sin(ang)], axis=1).astype(BF16)
    k = np.arange(FN_GROUP)
    ang64 = 2.0 * np.pi * ((k[:, None] * k[None, :]) % FN_GROUP) / FN_GROUP
    eye = np.eye(FN_W // FN_GROUP)
    scale = 1.0 / math.sqrt(FN_GROUP * l)
    m1 = np.concatenate([np.kron(eye, np.cos(ang64)), -np.kron(eye, np.sin(ang64))], axis=1) * scale
    return jnp.asarray(m1, F32).astype(BF16), t2


def _rope_tables(l):
    rows = l // GRID_W
    row = jnp.repeat(jnp.arange(rows), GRID_W).astype(F32)
    col = jnp.tile(jnp.arange(GRID_W), rows).astype(F32)
    half = QK_DIM // 2
    inv = ROPE_BASE ** (-jnp.arange(0, half, 2, dtype=F32) / half)
    ang = jnp.stack([row[:, None] * inv, col[:, None] * inv], axis=1)
    cos = jnp.repeat(jnp.cos(ang)[:, :, None, :], 2, axis=2)
    sin = jnp.sin(ang)
    sin = jnp.stack([-sin, sin], axis=2)
    rep = lambda a: jnp.tile(a.reshape(l, QK_DIM), (1, 2 * N_HEADS))
    return rep(cos), rep(sin)


def _filter_features(l):
    pos = jnp.arange(l, dtype=F32)
    t = pos / max(l - 1, 1)
    w = 2.0 * math.pi * pos / l
    f = jnp.linspace(1e-4, HY_BANDS - 1, HY_BANDS, dtype=F32)
    feats = jnp.concatenate([t[:, None], jnp.cos(w[:, None] * f), -jnp.sin(w[:, None] * f)], axis=-1)
    feats = jnp.pad(feats, ((0, 0), (0, 64 - HY_EMB)))
    deltas = jnp.abs(jnp.linspace(HY_MIN_DECAY, HY_MAX_DECAY, HY_W, dtype=F32))
    dec = jnp.exp(-t[:, None] * deltas)
    return feats, jnp.tile(dec, (1, HY_ORDER))


def _hyena_filters(l, tabs, ffwd, w1, b1, freq, w2, b2, w3):
    feats, dec = tabs
    taps = _hyena_filter_taps(feats, jnp.pad(w1, ((0, 64 - HY_EMB), (0, 0))), b1[None], freq[None],
                              w2, b2[None], w3, dec)
    kf = _table_matmul(ffwd, taps, F32)
    half = HY_ORDER * HY_W
    kc = kf[:l, :half]
    nyq = kf[l, :half]
    ks = kf[l:, half:]
    n = 2.0 * l
    first = (jnp.arange(l) == 0)[:, None]
    wc = jnp.where(first, 1.0 / n, 2.0 / n)
    ka = kc * wc
    kb = jnp.where(first, 0.0, ks * (2.0 / n))
    kd = jnp.where(first, nyq[None, :] / n, kc * (2.0 / n))
    stack = jnp.stack([ka, kb, kb, kd], axis=0)
    return jnp.moveaxis(stack.reshape(4, l, HY_ORDER, HY_W), 2, 0)


def kernel(x, c, ctx, c_ctx, w_ada, b_ada, g_mix, g_ffn, w_in, hy_conv_w, hy_conv_b, hy_w1, hy_b1, hy_freq, hy_w2, hy_b2, hy_w3, hy_bias, g_q, g_k, lam, g_sub, w_hy, w_fn, w_at, w_out, peer_wq, peer_keys, peer_u, peer_v):
    bsz, seq, d = x.shape
    clen = ctx.shape[1]
    depth = w_ada.shape[0]

    cc = jnp.concatenate([c, c_ctx[None], jnp.zeros((MOD_ROWS - bsz - 1, d), F32)], axis=0)
    mods = _ada_mods(cc, w_ada, b_ada)
    lat_row = lambda bi: bi
    ctx_row = lambda bi: bsz

    rope = _rope_tables(seq)
    tabs = {n: dict(dft=_dft_tables(n), fnet=_fnet_tables(n), feat=_filter_features(n)) for n in (seq, clen)}
    w = N_HEADS * HEAD_W
    lane = np.arange(w)
    bd = jnp.asarray((lane[:, None] // QK_DIM == lane[None, :] // QK_DIM) / QK_DIM, F32).astype(BF16)

    xl, xc = x, ctx
    for l in range(depth):
        last = l == depth - 1
        lam_init = 0.8 - 0.6 * math.exp(-0.3 * l)
        wl = w_in[l]
        w_perm = jnp.concatenate([wl[:, 2560:], wl[:, :2560]], axis=1).astype(BF16)
        gq = jnp.tile(g_q[l].reshape(1, HEAD_W), (1, N_HEADS)) * (QK_DIM ** -0.5)
        gk = jnp.tile(g_k[l].reshape(1, HEAD_W), (1, N_HEADS))
        gsub = g_sub[l][None]
        wts = [a[l].astype(BF16) for a in (w_hy, w_fn, w_at, w_out)]
        filt_args = (hy_w1[l], hy_b1[l], hy_freq[l], hy_w2[l], hy_b2[l], hy_w3[l])

        h_l = _in_projection(xl, g_mix[l][None], mods, l, lat_row, w_perm, 0, P_IN)
        h_c = _in_projection(xc, g_mix[l][None], mods, l, ctx_row, w_perm, COL_K if last else 0, P_IN)
        q_l, k_l = _qk_prep(h_l, gq, gk, bd, rope)
        q_c, k_c = _qk_prep(h_c, gq, gk, bd, None)
        vblk = COL_V // HEAD_W
        att_l = _diff_attention(q_l, [(k_l, h_l, vblk), (k_c, h_c, vblk)], lam[l], gsub, lam_init)
        ffwd, finv = tabs[seq]["dft"]
        kf = _hyena_filters(seq, tabs[seq]["feat"], ffwd, *filt_args)
        hyo_l = _hyena_mix(h_l, hy_conv_w[l], hy_conv_b[l][None], kf, hy_bias[l], ffwd, finv)
        fno_l = _fourier_mix(h_l, *tabs[seq]["fnet"])
        if not last:
            att_c = _diff_attention(q_c, [(k_c, h_c, vblk)], lam[l], gsub, lam_init)
            ffwd_c, finv_c = tabs[clen]["dft"]
            kf_c = _hyena_filters(clen, tabs[clen]["feat"], ffwd_c, *filt_args)
            hyo_c = _hyena_mix(h_c, hy_conv_w[l], hy_conv_b[l][None], kf_c, hy_bias[l], ffwd_c, finv_c)
            fno_c = _fourier_mix(h_c, *tabs[clen]["fnet"])
            xc = _merge(hyo_c, fno_c, att_c, h_c, xc, mods, l, ctx_row, *wts)
        xl = _merge(hyo_l, fno_l, att_l, h_l, xl, mods, l, lat_row, *wts)

        wq_t = peer_wq[l].astype(BF16)
        keys = _split(jnp.swapaxes(peer_keys[l].reshape(2 * PEER_HEADS, PEER_NKEYS, -1), 1, 2))
        u = peer_u[l].astype(BF16)
        v_t = jnp.swapaxes(peer_v[l].reshape(-1, d, d), 1, 2).astype(BF16)
        if not last:
            xc = _peer(xc, g_ffn[l][None], mods, l, ctx_row, wq_t, keys, u, v_t)
        xl = _peer(xl, g_ffn[l][None], mods, l, lat_row, wq_t, keys, u, v_t)
    return xl
```

```python
import functools
import math

import jax
import jax.numpy as jnp
import numpy as np
from jax import lax
from jax.experimental import pallas as pl
from jax.experimental.pallas import tpu as pltpu

F32 = jnp.float32
BF16 = jnp.bfloat16

EPS = 1e-6
GRID_W = 64
ROPE_BASE = 10000.0
N_HEADS = 4
QK_DIM = 64
HEAD_W = 2 * QK_DIM
HY_W = 256
HY_ORDER = 2
HY_EMB = 33
HY_BANDS = (HY_EMB - 1) // 2
HY_MIN_DECAY = math.log(1e-2) / 1.5
HY_MAX_DECAY = math.log(1e-2) / 0.3
FN_GROUP = 64
FN_W = 256
PEER_HEADS = 8
PEER_NKEYS = 128
PEER_TOPK = 16
N_MOD = 6
MOD_ROWS = 16
NEG = -3.0e38
MXU_SLAB = 512
MXU_COLS = 256
GATE_VREGS = 8
PEER_TOKENS = 256

VMEM_LIMIT = 56 * 1024 * 1024

COL_GATE = 0
COL_HY = 3072
COL_FN = 3840
COL_Q = 4096
COL_K = 4608
COL_V = 5120
P_IN = 5632


def _cp(*sem):
    return pltpu.CompilerParams(dimension_semantics=sem, vmem_limit_bytes=VMEM_LIMIT)


def _dot(a, b):
    return jnp.dot(a, b, preferred_element_type=F32)


def _dot_nt(a, b):
    return lax.dot_general(a, b, (((1,), (1,)), ((), ())), preferred_element_type=F32)


def _split(a):
    hi = a.astype(BF16)
    lo = (a - hi.astype(F32)).astype(BF16)
    return hi, lo


def _modulated_norm(x, g, shift, scale):
    ms = jnp.mean(x * x, axis=-1, keepdims=True)
    y = x * lax.rsqrt(ms + EPS) * g
    return y * (1.0 + scale) + shift


def _mod_spec(layer, chunk, row_fn):
    def imap(*idx):
        return (layer, row_fn(*idx), chunk, 0, 0)
    return imap


def _ada_kernel(c_ref, w_ref, b_ref, o_ref):
    c = c_ref[...]
    a = c / (1.0 + jnp.exp(-c))
    ah, al = _split(a)
    wh, wl = _split(w_ref[...])
    o_ref[...] = _dot(ah, wh) + _dot(ah, wl) + _dot(al, wh) + b_ref[...]


def _ada_mods(cc, w_ada, b_ada):
    depth, d, n = w_ada.shape
    tn = 512
    out = pl.pallas_call(
        _ada_kernel,
        grid=(depth, n // tn),
        in_specs=[pl.BlockSpec((MOD_ROWS, d), lambda l, j: (0, 0)),
                  pl.BlockSpec((None, d, tn), lambda l, j: (l, 0, j)),
                  pl.BlockSpec((None, 1, tn), lambda l, j: (l, 0, j))],
        out_specs=pl.BlockSpec((None, MOD_ROWS, tn), lambda l, j: (l, 0, j)),
        out_shape=jax.ShapeDtypeStruct((depth, MOD_ROWS, n), F32),
        compiler_params=_cp("parallel", "parallel"),
        name="ada_mods",
    )(cc, w_ada, b_ada.reshape(depth, 1, n))
    return out.reshape(depth, MOD_ROWS, N_MOD, 1, d)


def _inproj_kernel(x_ref, g_ref, sh_ref, sc_ref, w_ref, o_ref, xn_ref):
    @pl.when(pl.program_id(2) == 0)
    def _():
        xn_ref[...] = _modulated_norm(x_ref[...], g_ref[...], sh_ref[...], sc_ref[...]).astype(BF16)
    o_ref[...] = _dot(xn_ref[...], w_ref[...]).astype(o_ref.dtype)


def _in_projection(x, g, mods, layer, row_fn, w, col_lo, col_hi):
    b, l, d = x.shape
    tm = min(l, 1024)
    tn = 512
    j0 = col_lo // tn
    nj = (col_hi - col_lo) // tn
    mrow = lambda bi, i, j: row_fn(bi)
    return pl.pallas_call(
        _inproj_kernel,
        grid=(b, l // tm, nj),
        in_specs=[pl.BlockSpec((None, tm, d), lambda bi, i, j: (bi, i, 0)),
                  pl.BlockSpec((1, d), lambda bi, i, j: (0, 0)),
                  pl.BlockSpec((None, None, None, 1, d), _mod_spec(layer, 0, mrow)),
                  pl.BlockSpec((None, None, None, 1, d), _mod_spec(layer, 1, mrow)),
                  pl.BlockSpec((d, tn), lambda bi, i, j: (0, j + j0))],
        out_specs=pl.BlockSpec((None, tm, tn), lambda bi, i, j: (bi, i, j + j0)),
        out_shape=jax.ShapeDtypeStruct((b, l, w.shape[1]), BF16),
        scratch_shapes=[pltpu.VMEM((tm, d), BF16)],
        compiler_params=_cp("parallel", "parallel", "arbitrary"),
        name="in_projection",
    )(x, g, mods, mods, w)


def _qkprep_kernel(*refs, rope):
    if rope:
        q_ref, k_ref, gq_ref, gk_ref, bd_ref, cos_ref, sin_ref, qo_ref, ko_ref = refs
    else:
        q_ref, k_ref, gq_ref, gk_ref, bd_ref, qo_ref, ko_ref = refs
    for src, g_ref, dst in ((q_ref, gq_ref, qo_ref), (k_ref, gk_ref, ko_ref)):
        x = src[...].astype(F32)
        hi, lo = _split(x * x)
        ms = _dot(hi, bd_ref[...]) + _dot(lo, bd_ref[...])
        y = x * lax.rsqrt(ms + EPS) * g_ref[...]
        if rope:
            w = y.shape[1]
            lane = lax.broadcasted_iota(jnp.int32, y.shape, 1)
            first = (lane % (QK_DIM // 2)) < (QK_DIM // 4)
            partner = jnp.where(first, pltpu.roll(y, w - QK_DIM // 4, 1), pltpu.roll(y, QK_DIM // 4, 1))
            y = y * cos_ref[...] + partner * sin_ref[...]
        dst[...] = y.astype(BF16)


def _qk_prep(h, gq, gk, bd, rope_tabs):
    b, l, _ = h.shape
    w = N_HEADS * HEAD_W
    tm = min(l, 512)
    rope = rope_tabs is not None
    in_specs = [pl.BlockSpec((None, tm, w), lambda bi, i: (bi, i, COL_Q // w)),
                pl.BlockSpec((None, tm, w), lambda bi, i: (bi, i, COL_K // w)),
                pl.BlockSpec((1, w), lambda bi, i: (0, 0)),
                pl.BlockSpec((1, w), lambda bi, i: (0, 0)),
                pl.BlockSpec((w, w), lambda bi, i: (0, 0))]
    args = [h, h, gq, gk, bd]
    if rope:
        in_specs += [pl.BlockSpec((tm, w), lambda bi, i: (i, 0))] * 2
        args += list(rope_tabs)
    return pl.pallas_call(
        functools.partial(_qkprep_kernel, rope=rope),
        grid=(b, l // tm),
        in_specs=in_specs,
        out_specs=[pl.BlockSpec((None, tm, w), lambda bi, i: (bi, i, 0))] * 2,
        out_shape=[jax.ShapeDtypeStruct((b, l, w), BF16)] * 2,
        compiler_params=_cp("parallel", "parallel"),
        name="qk_prep",
    )(*args)


def _attn_kernel(*refs, n_src, lam_init):
    q_ref = refs[0]
    kv = refs[1:1 + 2 * n_src]
    lam_ref, gsub_ref, o_ref = refs[1 + 2 * n_src:]
    q = q_ref[...]
    tq = q.shape[0]
    lane = lax.broadcasted_iota(jnp.int32, q.shape, 1)
    zero = jnp.zeros_like(q)
    qq = jnp.concatenate([jnp.where(lane < QK_DIM, q, zero), jnp.where(lane >= QK_DIM, q, zero)], axis=0)
    scores = [_dot_nt(qq, kv[2 * i][...]) for i in range(n_src)]
    m = jnp.max(scores[0], axis=-1, keepdims=True)
    for s in scores[1:]:
        m = jnp.maximum(m, jnp.max(s, axis=-1, keepdims=True))
    z = jnp.zeros_like(m)
    acc = jnp.zeros((2 * tq, HEAD_W), F32)
    for i, s in enumerate(scores):
        e = jnp.exp(s - m)
        z = z + jnp.sum(e, axis=-1, keepdims=True)
        acc = acc + _dot(e.astype(BF16), kv[2 * i + 1][...])
    o2 = acc / z
    lf = lam_ref[...]
    lam_val = (jnp.exp(jnp.sum(lf[0:1] * lf[1:2], axis=-1, keepdims=True))
               - jnp.exp(jnp.sum(lf[2:3] * lf[3:4], axis=-1, keepdims=True)) + lam_init)
    o = o2[:tq] - lam_val * o2[tq:]
    ms = jnp.mean(o * o, axis=-1, keepdims=True)
    o = o * lax.rsqrt(ms + EPS) * gsub_ref[...] * (1.0 - lam_init)
    o_ref[...] = o.astype(BF16)


def _diff_attention(q, sources, lam_l, gsub, lam_init):
    b, lq, w = q.shape
    tq = 256
    in_specs = [pl.BlockSpec((None, tq, HEAD_W), lambda bi, hi, i: (bi, i, hi))]
    args = [q]
    for k, varr, vblk in sources:
        lk = k.shape[1]
        in_specs.append(pl.BlockSpec((None, lk, HEAD_W), lambda bi, hi, i: (bi, 0, hi)))
        in_specs.append(pl.BlockSpec((None, lk, HEAD_W), lambda bi, hi, i, vblk=vblk: (bi, 0, vblk + hi)))
        args += [k, varr]
    in_specs += [pl.BlockSpec(lam_l.shape, lambda bi, hi, i: (0, 0)),
                 pl.BlockSpec((1, HEAD_W), lambda bi, hi, i: (0, 0))]
    args += [lam_l, gsub]
    return pl.pallas_call(
        functools.partial(_attn_kernel, n_src=len(sources), lam_init=lam_init),
        grid=(b, N_HEADS, lq // tq),
        in_specs=in_specs,
        out_specs=pl.BlockSpec((None, tq, HEAD_W), lambda bi, hi, i: (bi, i, hi)),
        out_shape=jax.ShapeDtypeStruct((b, lq, w), BF16),
        compiler_params=_cp("parallel", "parallel", "arbitrary"),
        name="diff_attention",
    )(*args)


def _sconv_kernel(h_ref, w_ref, b_ref, o_ref):
    x = h_ref[...].astype(F32)
    n = x.shape[0]
    row = lax.broadcasted_iota(jnp.int32, x.shape, 0)
    prev = jnp.where(row == 0, 0.0, pltpu.roll(x, 1, 0))
    nxt = jnp.where(row == n - 1, 0.0, pltpu.roll(x, n - 1, 0))
    o_ref[...] = prev * w_ref[0:1, :] + x * w_ref[1:2, :] + nxt * w_ref[2:3, :] + b_ref[...]


def _short_conv(h, w, bias):
    b, l, _ = h.shape
    return pl.pallas_call(
        _sconv_kernel,
        grid=(b, 3),
        in_specs=[pl.BlockSpec((None, l, HY_W), lambda bi, j: (bi, 0, COL_HY // HY_W + j)),
                  pl.BlockSpec((3, HY_W), lambda bi, j: (0, j)),
                  pl.BlockSpec((1, HY_W), lambda bi, j: (0, j))],
        out_specs=pl.BlockSpec((None, None, l, HY_W), lambda bi, j: (bi, j, 0, 0)),
        out_shape=jax.ShapeDtypeStruct((b, 3, l, HY_W), F32),
        compiler_params=_cp("parallel", "parallel"),
        name="short_conv",
    )(h, w, bias)


def _filter_kernel(f_ref, w1_ref, b1_ref, fr_ref, w2_ref, b2_ref, w3_ref, dec_ref, o_ref):
    def mm(a, w_ref_):
        ah, al = _split(a)
        wh, wl = _split(w_ref_[...])
        return _dot(ah, wh) + _dot(ah, wl) + _dot(al, wh)
    fr = fr_ref[...]
    h = jnp.sin(fr * (mm(f_ref[...], w1_ref) + b1_ref[...]))
    h = jnp.sin(fr * (mm(h, w2_ref) + b2_ref[...]))
    h = mm(h, w3_ref)
    half = h.shape[1] // 2
    dec = dec_ref[...]
    hf = h[:, :half] * dec
    hb = h[:, half:] * dec
    row = lax.broadcasted_iota(jnp.int32, hb.shape, 0)
    hb = jnp.where(row == 0, 0.0, hb)
    norm = jnp.sum(jnp.abs(hf) + jnp.abs(hb), axis=0, keepdims=True)
    o_ref[:, :half] = (hf + hb) / norm
    o_ref[:, half:] = (hf - hb) / norm


def _hyena_filter_taps(feats, w1, b1, freq, w2, b2, w3, dec):
    l = feats.shape[0]
    n = w3.shape[1]
    full = lambda a: pl.BlockSpec(a.shape, lambda i: (0,) * a.ndim)
    args = (feats, w1, b1, freq, w2, b2, w3, dec)
    return pl.pallas_call(
        _filter_kernel,
        grid=(1,),
        in_specs=[full(a) for a in args],
        out_specs=pl.BlockSpec((l, n), lambda i: (0, 0)),
        out_shape=jax.ShapeDtypeStruct((l, n), F32),
        compiler_params=_cp("arbitrary"),
        name="hyena_filter_taps",
    )(*args)


def _table_mm_kernel(t_ref, x_ref, o_ref):
    o_ref[...] = _dot(t_ref[...], x_ref[...].astype(BF16)).astype(o_ref.dtype)


def _table_matmul(table, x, out_dtype):
    m, k = table.shape
    n = x.shape[1]
    tm = min(m, 512)
    return pl.pallas_call(
        _table_mm_kernel,
        grid=(m // tm,),
        in_specs=[pl.BlockSpec((tm, k), lambda i: (i, 0)),
                  pl.BlockSpec((k, n), lambda i: (0, 0))],
        out_specs=pl.BlockSpec((tm, n), lambda i: (i, 0)),
        out_shape=jax.ShapeDtypeStruct((m, n), out_dtype),
        compiler_params=_cp("parallel"),
        name="table_matmul",
    )(table, x)


def _dftmul_kernel(fc_ref, fs_ref, z_ref, k_ref, p_ref):
    z = z_ref[...].astype(BF16)
    zc = _dot(fc_ref[...], z)
    zs = _dot(fs_ref[...], z)
    p_ref[0] = (zc * k_ref[0] - zs * k_ref[1]).astype(BF16)
    p_ref[1] = (zc * k_ref[2] + zs * k_ref[3]).astype(BF16)


def _dft_multiply(ffwd, z, z_spec, kf):
    l = ffwd.shape[1]
    b = z.shape[0]
    tf = min(l, 512)
    nf = l // tf
    out = pl.pallas_call(
        _dftmul_kernel,
        grid=(nf, b),
        in_specs=[pl.BlockSpec((tf, l), lambda i, bi: (i, 0)),
                  pl.BlockSpec((tf, l), lambda i, bi: (i + nf, 0)),
                  z_spec,
                  pl.BlockSpec((4, tf, HY_W), lambda i, bi: (0, i, 0))],
        out_specs=pl.BlockSpec((None, 2, tf, HY_W), lambda i, bi: (bi, 0, i, 0)),
        out_shape=jax.ShapeDtypeStruct((b, 2, l, HY_W), BF16),
        compiler_params=_cp("parallel", "arbitrary"),
        name="dft_multiply",
    )(ffwd, ffwd, z, kf)
    return out.reshape(b, 2 * l, HY_W)


def _idft_gate_kernel(fi_ref, p_ref, g_ref, z_ref, b_ref, o_ref):
    conv = _dot(fi_ref[...], p_ref[...])
    o_ref[...] = (g_ref[...] * (conv + b_ref[...] * z_ref[...])).astype(o_ref.dtype)


def _idft_gate(finv, p, gate, gate_spec, z, z_spec, bias, out_dtype):
    l = finv.shape[0]
    b = p.shape[0]
    tt = min(l, 512)
    return pl.pallas_call(
        _idft_gate_kernel,
        grid=(l // tt, b),
        in_specs=[pl.BlockSpec((tt, 2 * l), lambda i, bi: (i, 0)),
                  pl.BlockSpec((None, 2 * l, HY_W), lambda i, bi: (bi, 0, 0)),
                  gate_spec, z_spec,
                  pl.BlockSpec((1, HY_W), lambda i, bi: (0, 0))],
        out_specs=pl.BlockSpec((None, tt, HY_W), lambda i, bi: (bi, i, 0)),
        out_shape=jax.ShapeDtypeStruct((b, l, HY_W), out_dtype),
        compiler_params=_cp("parallel", "arbitrary"),
        name="idft_gate",
    )(finv, p, gate, z, bias)


def _hyena_mix(h, conv_w, conv_b, kf, hy_bias, ffwd, finv):
    b, l, _ = h.shape
    tt = min(l, 512)
    u = _short_conv(h, conv_w, conv_b)
    part = lambda j, rows: pl.BlockSpec((None, None, rows, HY_W),
                                        lambda i, bi, j=j: (bi, j, i if rows != l else 0, 0))
    p = _dft_multiply(ffwd, u, part(0, l), kf[0])
    z1 = _idft_gate(finv, p, u, part(1, tt), u, part(0, tt), hy_bias[0:1], F32)
    p = _dft_multiply(ffwd, z1, pl.BlockSpec((None, l, HY_W), lambda i, bi: (bi, 0, 0)), kf[1])
    return _idft_gate(finv, p, u, part(2, tt), z1,
                      pl.BlockSpec((None, tt, HY_W), lambda i, bi: (bi, i, 0)), hy_bias[1:2], BF16)


def _fn1_kernel(z_ref, m_ref, o_ref):
    r = _dot(z_ref[...], m_ref[...])
    half = r.shape[1] // 2
    o_ref[0] = r[:, :half].astype(BF16)
    o_ref[1] = r[:, half:].astype(BF16)


def _fourier_mix(h, m1, t2):
    b, l, _ = h.shape
    tm = min(l, 512)
    zz = pl.pallas_call(
        _fn1_kernel,
        grid=(b, l // tm),
        in_specs=[pl.BlockSpec((None, tm, FN_W), lambda bi, i: (bi, i, COL_FN // FN_W)),
                  pl.BlockSpec((FN_W, 2 * FN_W), lambda bi, i: (0, 0))],
        out_specs=pl.BlockSpec((None, 2, tm, FN_W), lambda bi, i: (bi, 0, i, 0)),
        out_shape=jax.ShapeDtypeStruct((b, 2, l, FN_W), BF16),
        compiler_params=_cp("parallel", "parallel"),
        name="fnet_channels",
    )(h, m1).reshape(b, 2 * l, FN_W)
    return pl.pallas_call(
        _table_mm_kernel,
        grid=(l // tm, b),
        in_specs=[pl.BlockSpec((tm, 2 * l), lambda i, bi: (i, 0)),
                  pl.BlockSpec((None, 2 * l, FN_W), lambda i, bi: (bi, 0, 0))],
        out_specs=pl.BlockSpec((None, tm, FN_W), lambda i, bi: (bi, i, 0)),
        out_shape=jax.ShapeDtypeStruct((b, l, FN_W), BF16),
        compiler_params=_cp("parallel", "arbitrary"),
        name="fnet_positions",
    )(t2, zz)


def _merge_kernel(hy_ref, fn_ref, at_ref, g_ref, x_ref, gm_ref, why_ref, wfn_ref, wat_ref, wout_ref, o_ref):
    d = x_ref.shape[-1]
    g = 1.0 / (1.0 + jnp.exp(-g_ref[...].astype(F32)))
    y = (g[:, :d] * _dot(hy_ref[...], why_ref[...])
         + g[:, d:2 * d] * _dot(fn_ref[...], wfn_ref[...])
         + g[:, 2 * d:] * _dot(at_ref[...], wat_ref[...]))
    mix = _dot(y.astype(BF16), wout_ref[...])
    o_ref[...] = x_ref[...] + gm_ref[...] * mix


def _merge(hyo, fno, att, h, x, mods, layer, row_fn, w_hy, w_fn, w_at, w_out):
    b, l, d = x.shape
    tm = min(l, 512)
    full = lambda a: pl.BlockSpec(a.shape, lambda bi, i: (0,) * a.ndim)
    tok = lambda wdt, blk=0: pl.BlockSpec((None, tm, wdt), lambda bi, i: (bi, i, blk))
    return pl.pallas_call(
        _merge_kernel,
        grid=(b, l // tm),
        in_specs=[tok(HY_W), tok(FN_W), tok(N_HEADS * HEAD_W), tok(3 * d, COL_GATE), tok(d),
                  pl.BlockSpec((None, None, None, 1, d), _mod_spec(layer, 2, lambda bi, i: row_fn(bi))),
                  full(w_hy), full(w_fn), full(w_at), full(w_out)],
        out_specs=tok(d),
        out_shape=jax.ShapeDtypeStruct((b, l, d), F32),
        compiler_params=_cp("parallel", "parallel"),
        name="merge_residual",
    )(hyo, fno, att, h, x, mods, w_hy, w_fn, w_at, w_out)


def _peer_q_kernel(x_ref, g_ref, sh_ref, sc_ref, wq_ref, kh_ref, kl_ref, s_ref, nt_ref):
    n = _modulated_norm(x_ref[...], g_ref[...], sh_ref[...], sc_ref[...])
    nt_ref[...] = n.T.astype(BF16)
    q = _dot(n.astype(BF16), wq_ref[...])
    dq = kh_ref.shape[1]
    for hp in range(kh_ref.shape[0]):
        qh, ql = _split(q[:, hp * dq:(hp + 1) * dq])
        kh = kh_ref[hp]
        s = _dot(qh, kh) + _dot(ql, kh) + _dot(qh, kl_ref[hp])
        s_ref[hp] = s.T


def _peer_scores(x, g, mods, layer, row_fn, wq, keys_t):
    b, l, d = x.shape
    tt = min(l, 512)
    nt = l // tt
    nhp = 2 * PEER_HEADS
    mrow = lambda bi, i: row_fn(bi)
    full = lambda a: pl.BlockSpec(a.shape, lambda bi, i: (0,) * a.ndim)
    return pl.pallas_call(
        _peer_q_kernel,
        grid=(b, nt),
        in_specs=[pl.BlockSpec((None, tt, d), lambda bi, i: (bi, i, 0)),
                  pl.BlockSpec((1, d), lambda bi, i: (0, 0)),
                  pl.BlockSpec((None, None, None, 1, d), _mod_spec(layer, 3, mrow)),
                  pl.BlockSpec((None, None, None, 1, d), _mod_spec(layer, 4, mrow)),
                  full(wq), full(keys_t[0]), full(keys_t[1])],
        out_specs=[pl.BlockSpec((nhp, PEER_NKEYS, tt), lambda bi, i: (0, 0, bi * nt + i)),
                   pl.BlockSpec((d, tt), lambda bi, i: (0, bi * nt + i))],
        out_shape=[jax.ShapeDtypeStruct((nhp, PEER_NKEYS, b * l), F32),
                   jax.ShapeDtypeStruct((d, b * l), BF16)],
        compiler_params=_cp("parallel", "parallel"),
        name="peer_scores",
    )(x, g, mods, mods, wq, keys_t[0], keys_t[1])


def _top_rows(s, k):
    vals = []
    for r in range(k):
        m = jnp.max(s, axis=0, keepdims=True)
        vals.append(m)
        if r + 1 < k:
            s = jnp.where(s == m, NEG, s)
    return vals


def _peer_expert_kernel(s_ref, nt_ref, u_ref, vt_ref, x_ref, gm_ref, o_ref,
                        tau_ref, l1_ref, l2_ref, hid0_ref, hid1_ref, p0_ref, p1_ref, acc_ref):
    s = pl.program_id(1)
    nc = pl.num_programs(1) - 2
    k = PEER_TOPK
    tt = nt_ref.shape[1]
    n_i = u_ref.shape[0] // PEER_NKEYS
    piece = GATE_VREGS * 8 * 128 // tt

    def hidden(hid_ref):
        hid_ref[...] = _dot(u_ref[...], nt_ref[...])

    def stages(chunk, hid_w, hid_r, p_w, p_r):
        def hidden_task(slab, cols):
            def run():
                hid_w[slab, cols] = _dot(u_ref[slab, :], nt_ref[:, cols])
            return run

        def project_task(slab, cols):
            def run():
                acc_ref[slab, cols] += _dot(vt_ref[slab, :], p_r[:, cols])
            return run

        tasks = []
        for r0 in range(0, u_ref.shape[0], MXU_SLAB):
            for c0 in range(0, tt, MXU_COLS):
                slab, cols = slice(r0, r0 + MXU_SLAB), slice(c0, min(c0 + MXU_COLS, tt))
                tasks += [hidden_task(slab, cols), project_task(slab, cols)]
        n_jp = PEER_NKEYS // piece
        slots = [[] for _ in range(n_i * n_jp)]
        for t, task in enumerate(tasks):
            slots[t * len(slots) // len(tasks)].append(task)
        for ii in range(n_i):
            i = chunk * n_i + ii
            l1_row = [jnp.broadcast_to(l1_ref[h, pl.ds(i, 1), :], (8, tt))[None] for h in range(PEER_HEADS)]
            for jp in range(n_jp):
                for task in slots[ii * n_jp + jp]:
                    task()
                js = slice(jp * piece, (jp + 1) * piece)
                gate = jnp.zeros((piece // 8, 8, tt), F32)
                for h in range(PEER_HEADS):
                    logw = l1_row[h] + l2_ref[h, js, :].reshape(piece // 8, 8, tt)
                    gate = gate + jnp.where(logw >= tau_ref[h][None], jnp.exp2(logw), 0.0)
                rows = slice(ii * PEER_NKEYS + jp * piece, ii * PEER_NKEYS + (jp + 1) * piece)
                hid = hid_r[rows, :]
                act = 0.5 * hid * (1.0 + lax.erf(hid * (2.0 ** -0.5)))
                p_w[rows, :] = (gate.reshape(piece, tt) * act).astype(BF16)

    @pl.when(s == 0)
    def _():
        acc_ref[...] = jnp.zeros_like(acc_ref)
        p1_ref[...] = jnp.zeros_like(p1_ref)
        hidden(hid0_ref)
        width = min(tt, 256)
        parts = tt // width

        def select(it, carry):
            h = it // parts
            cols = pl.ds(pl.multiple_of((it % parts) * width, width), width)
            s1 = s_ref[2 * h, :, cols]
            s2 = s_ref[2 * h + 1, :, cols]
            a = _top_rows(s1, k + 1)
            b = _top_rows(s2, k + 1)
            cand = [a[i] + b[j] for i in range(k + 1) for j in range((k + 1) // (i + 1))]
            pad = (-len(cand)) % 8
            top = _top_rows(jnp.concatenate(cand + [jnp.full_like(a[0], NEG)] * pad, axis=0), k + 1)
            z = jnp.zeros_like(top[0])
            for t in top[:k]:
                z = z + jnp.exp(t - top[0])
            shift = top[0] + jnp.log(z)
            log2e = 1.0 / math.log(2.0)
            l1_ref[h, :, cols] = s1 * log2e
            l2_ref[h, :, cols] = (s2 - shift) * log2e
            tau = (0.5 * (top[k - 1] + top[k]) - shift) * log2e
            tau_ref[h, :, cols] = jnp.broadcast_to(tau, (8, width))
            return carry

        lax.fori_loop(0, PEER_HEADS * parts, select, 0)

    @pl.when((s >= 1) & (s <= nc) & (s % 2 == 1))
    def _():
        stages(s - 1, hid1_ref, hid0_ref, p0_ref, p1_ref)

    @pl.when((s >= 1) & (s <= nc) & (s % 2 == 0))
    def _():
        stages(s - 1, hid0_ref, hid1_ref, p1_ref, p0_ref)

    @pl.when(s == nc + 1)
    def _():
        last = p0_ref if (nc - 1) % 2 == 0 else p1_ref
        out_t = acc_ref[...] + _dot(vt_ref[...], last[...])
        o_ref[...] = x_ref[...] + gm_ref[...] * out_t.T


def _peer_experts(scores, n_t, u, v_t, x, mods, layer, row_fn):
    b, l, d = x.shape
    tt = min(l, PEER_TOKENS)
    nt = l // tt
    nc, _, ec = v_t.shape
    assert ec == d
    nhp = scores.shape[0]
    return pl.pallas_call(
        _peer_expert_kernel,
        grid=(b * nt, nc + 2),
        in_specs=[pl.BlockSpec((nhp, PEER_NKEYS, tt), lambda t, s: (0, 0, t)),
                  pl.BlockSpec((d, tt), lambda t, s: (0, t)),
                  pl.BlockSpec((ec, d), lambda t, s: (jnp.minimum(s, nc - 1), 0)),
                  pl.BlockSpec((None, d, ec), lambda t, s: (jnp.clip(s - 2, 0, nc - 1), 0, 0)),
                  pl.BlockSpec((None, tt, d), lambda t, s: (t // nt, t % nt, 0)),
                  pl.BlockSpec((None, None, None, 1, d), _mod_spec(layer, 5, lambda t, s: row_fn(t // nt)))],
        out_specs=pl.BlockSpec((None, tt, d), lambda t, s: (t // nt, t % nt, 0)),
        out_shape=jax.ShapeDtypeStruct((b, l, d), F32),
        scratch_shapes=[pltpu.VMEM((PEER_HEADS, 8, tt), F32),
                        pltpu.VMEM((PEER_HEADS, PEER_NKEYS, tt), F32),
                        pltpu.VMEM((PEER_HEADS, PEER_NKEYS, tt), F32),
                        pltpu.VMEM((ec, tt), F32),
                        pltpu.VMEM((ec, tt), F32),
                        pltpu.VMEM((ec, tt), BF16),
                        pltpu.VMEM((ec, tt), BF16),
                        pltpu.VMEM((d, tt), F32)],
        compiler_params=_cp("parallel", "arbitrary"),
        name="peer_experts",
    )(scores, n_t, u, v_t, x, mods)


def _peer(x, g, mods, layer, row_fn, wq_t, keys, u, v_t):
    scores, n_t = _peer_scores(x, g, mods, layer, row_fn, wq_t, keys)
    return _peer_experts(scores, n_t, u, v_t, x, mods, layer, row_fn)


def _dft_tables(l):
    cos, sin = _cos_sin_table(l, l, 2 * l)
    t = jnp.arange(l, dtype=jnp.int32)[None, :]
    first = (jnp.arange(l) == 0)[:, None]
    sin = jnp.where(first, jnp.where(t % 2 == 0, 1.0, -1.0), sin)
    ffwd = jnp.concatenate([cos, sin], axis=0).astype(BF16)
    return ffwd, ffwd.T


def _cos_sin_table(nf, nt, n):
    step = 64
    f = jnp.arange(nf, dtype=jnp.int32)[:, None]
    angle = lambda prod: (2.0 * math.pi) * ((prod % n).astype(F32) / n)
    a = angle(f * (step * jnp.arange(nt // step, dtype=jnp.int32)[None, :]))
    b = angle(f * jnp.arange(step, dtype=jnp.int32)[None, :])
    ca, sa = jnp.cos(a)[:, :, None], jnp.sin(a)[:, :, None]
    cb, sb = jnp.cos(b)[:, None, :], jnp.sin(b)[:, None, :]
    return (ca * cb - sa * sb).reshape(nf, nt), (sa * cb + ca * sb).reshape(nf, nt)


def _fnet_tables(l):
    t2 = jnp.concatenate(_cos_sin_table(l, l, l), axis=1).astype(BF16)
    k = np.arange(FN_GROUP)
    ang64 = 2.0 * np.pi * ((k[:, None] * k[None, :]) % FN_GROUP) / FN_GROUP
    eye = np.eye(FN_W // FN_GROUP)
    scale = 1.0 / math.sqrt(FN_GROUP * l)
    m1 = np.concatenate([np.kron(eye, np.cos(ang64)), -np.kron(eye, np.sin(ang64))], axis=1) * scale
    return jnp.asarray(m1, F32).astype(BF16), t2


def _rope_tables(l):
    rows = l // GRID_W
    row = jnp.repeat(jnp.arange(rows), GRID_W).astype(F32)
    col = jnp.tile(jnp.arange(GRID_W), rows).astype(F32)
    half = QK_DIM // 2
    inv = ROPE_BASE ** (-jnp.arange(0, half, 2, dtype=F32) / half)
    ang = jnp.stack([row[:, None] * inv, col[:, None] * inv], axis=1)
    cos = jnp.repeat(jnp.cos(ang)[:, :, None, :], 2, axis=2)
    sin = jnp.sin(ang)
    sin = jnp.stack([-sin, sin], axis=2)
    rep = lambda a: jnp.tile(a.reshape(l, QK_DIM), (1, 2 * N_HEADS))
    return rep(cos), rep(sin)


def _filter_features(l):
    pos = jnp.arange(l, dtype=F32)
    t = pos / max(l - 1, 1)
    w = 2.0 * math.pi * pos / l
    f = jnp.linspace(1e-4, HY_BANDS - 1, HY_BANDS, dtype=F32)
    feats = jnp.concatenate([t[:, None], jnp.cos(w[:, None] * f), -jnp.sin(w[:, None] * f)], axis=-1)
    feats = jnp.pad(feats, ((0, 0), (0, 64 - HY_EMB)))
    deltas = jnp.abs(jnp.linspace(HY_MIN_DECAY, HY_MAX_DECAY, HY_W, dtype=F32))
    dec = jnp.exp(-t[:, None] * deltas)
    return feats, jnp.tile(dec, (1, HY_ORDER))


def _hyena_filters(l, tabs, ffwd, w1, b1, freq, w2, b2, w3):
    feats, dec = tabs
    taps = _hyena_filter_taps(feats, jnp.pad(w1, ((0, 64 - HY_EMB), (0, 0))), b1[None], freq[None],
                              w2, b2[None], w3, dec)
    kf = _table_matmul(ffwd, taps, F32)
    half = HY_ORDER * HY_W
    kc = kf[:l, :half]
    nyq = kf[l, :half]
    ks = kf[l:, half:]
    n = 2.0 * l
    first = (jnp.arange(l) == 0)[:, None]
    wc = jnp.where(first, 1.0 / n, 2.0 / n)
    ka = kc * wc
    kb = jnp.where(first, 0.0, ks * (2.0 / n))
    kd = jnp.where(first, nyq[None, :] / n, kc * (2.0 / n))
    stack = jnp.stack([ka, kb, kb, kd], axis=0)
    return jnp.moveaxis(stack.reshape(4, l, HY_ORDER, HY_W), 2, 0)


def kernel(x, c, ctx, c_ctx, w_ada, b_ada, g_mix, g_ffn, w_in, hy_conv_w, hy_conv_b, hy_w1, hy_b1, hy_freq, hy_w2, hy_b2, hy_w3, hy_bias, g_q, g_k, lam, g_sub, w_hy, w_fn, w_at, w_out, peer_wq, peer_keys, peer_u, peer_v):
    bsz, seq, d = x.shape
    clen = ctx.shape[1]
    depth = w_ada.shape[0]

    cc = jnp.concatenate([c, c_ctx[None], jnp.zeros((MOD_ROWS - bsz - 1, d), F32)], axis=0)
    mods = _ada_mods(cc, w_ada, b_ada)
    lat_row = lambda bi: bi
    ctx_row = lambda bi: bsz

    rope = _rope_tables(seq)
    tabs = {n: dict(dft=_dft_tables(n), fnet=_fnet_tables(n), feat=_filter_features(n)) for n in (seq, clen)}
    w = N_HEADS * HEAD_W
    lane = np.arange(w)
    bd = jnp.asarray((lane[:, None] // QK_DIM == lane[None, :] // QK_DIM) / QK_DIM, F32).astype(BF16)

    xl, xc = x, ctx
    for l in range(depth):
        last = l == depth - 1
        lam_init = 0.8 - 0.6 * math.exp(-0.3 * l)
        wl = w_in[l]
        w_perm = jnp.concatenate([wl[:, 2560:], wl[:, :2560]], axis=1).astype(BF16)
        gq = jnp.tile(g_q[l].reshape(1, HEAD_W), (1, N_HEADS)) * (QK_DIM ** -0.5)
        gk = jnp.tile(g_k[l].reshape(1, HEAD_W), (1, N_HEADS))
        gsub = g_sub[l][None]
        wts = [a[l].astype(BF16) for a in (w_hy, w_fn, w_at, w_out)]
        filt_args = (hy_w1[l], hy_b1[l], hy_freq[l], hy_w2[l], hy_b2[l], hy_w3[l])

        h_l = _in_projection(xl, g_mix[l][None], mods, l, lat_row, w_perm, 0, P_IN)
        h_c = _in_projection(xc, g_mix[l][None], mods, l, ctx_row, w_perm, COL_K if last else 0, P_IN)
        q_l, k_l = _qk_prep(h_l, gq, gk, bd, rope)
        q_c, k_c = _qk_prep(h_c, gq, gk, bd, None)
        vblk = COL_V // HEAD_W
        att_l = _diff_attention(q_l, [(k_l, h_l, vblk), (k_c, h_c, vblk)], lam[l], gsub, lam_init)
        ffwd, finv = tabs[seq]["dft"]
        kf = _hyena_filters(seq, tabs[seq]["feat"], ffwd, *filt_args)
        hyo_l = _hyena_mix(h_l, hy_conv_w[l], hy_conv_b[l][None], kf, hy_bias[l], ffwd, finv)
        fno_l = _fourier_mix(h_l, *tabs[seq]["fnet"])
        if not last:
            att_c = _diff_attention(q_c, [(k_c, h_c, vblk)], lam[l], gsub, lam_init)
            ffwd_c, finv_c = tabs[clen]["dft"]
            kf_c = _hyena_filters(clen, tabs[clen]["feat"], ffwd_c, *filt_args)
            hyo_c = _hyena_mix(h_c, hy_conv_w[l], hy_conv_b[l][None], kf_c, hy_bias[l], ffwd_c, finv_c)
            fno_c = _fourier_mix(h_c, *tabs[clen]["fnet"])
            xc = _merge(hyo_c, fno_c, att_c, h_c, xc, mods, l, ctx_row, *wts)
        xl = _merge(hyo_l, fno_l, att_l, h_l, xl, mods, l, lat_row, *wts)

        wq_t = peer_wq[l].astype(BF16)
        keys = _split(jnp.swapaxes(peer_keys[l].reshape(2 * PEER_HEADS, PEER_NKEYS, -1), 1, 2))
        u = peer_u[l].astype(BF16)
        v_t = jnp.swapaxes(peer_v[l].reshape(-1, d, d), 1, 2).astype(BF16)
        if not last:
            xc = _peer(xc, g_ffn[l][None], mods, l, ctx_row, wq_t, keys, u, v_t)
        xl = _peer(xl, g_ffn[l][None], mods, l, lat_row, wq_t, keys, u, v_t)
    return xl
```

```python
import functools
import math

import jax
import jax.numpy as jnp
import numpy as np
from jax import lax
from jax.experimental import pallas as pl
from jax.experimental.pallas import tpu as pltpu

F32 = jnp.float32
BF16 = jnp.bfloat16

EPS = 1e-6
GRID_W = 64
ROPE_BASE = 10000.0
N_HEADS = 4
QK_DIM = 64
HEAD_W = 2 * QK_DIM
HY_W = 256
HY_ORDER = 2
HY_EMB = 33
HY_BANDS = (HY_EMB - 1) // 2
HY_MIN_DECAY = math.log(1e-2) / 1.5
HY_MAX_DECAY = math.log(1e-2) / 0.3
FN_GROUP = 64
FN_W = 256
PEER_HEADS = 8
PEER_NKEYS = 128
PEER_TOPK = 16
N_MOD = 6
MOD_ROWS = 16
NEG = -3.0e38
MXU_SLAB = 512
MXU_COLS = 256
GATE_VREGS = 8
PEER_TOKENS = 256

VMEM_LIMIT = 56 * 1024 * 1024

COL_GATE = 0
COL_HY = 3072
COL_FN = 3840
COL_Q = 4096
COL_K = 4608
COL_V = 5120
P_IN = 5632


def _cp(*sem):
    return pltpu.CompilerParams(dimension_semantics=sem, vmem_limit_bytes=VMEM_LIMIT)


def _dot(a, b):
    return jnp.dot(a, b, preferred_element_type=F32)


def _dot_nt(a, b):
    return lax.dot_general(a, b, (((1,), (1,)), ((), ())), preferred_element_type=F32)


def _split(a):
    hi = a.astype(BF16)
    lo = (a - hi.astype(F32)).astype(BF16)
    return hi, lo


def _modulated_norm(x, g, shift, scale):
    ms = jnp.mean(x * x, axis=-1, keepdims=True)
    y = x * lax.rsqrt(ms + EPS) * g
    return y * (1.0 + scale) + shift


def _mod_spec(layer, chunk, row_fn):
    def imap(*idx):
        return (layer, row_fn(*idx), chunk, 0, 0)
    return imap


def _ada_kernel(c_ref, w_ref, b_ref, o_ref):
    c = c_ref[...]
    a = c / (1.0 + jnp.exp(-c))
    ah, al = _split(a)
    wh, wl = _split(w_ref[...])
    o_ref[...] = _dot(ah, wh) + _dot(ah, wl) + _dot(al, wh) + b_ref[...]


def _ada_mods(cc, w_ada, b_ada):
    depth, d, n = w_ada.shape
    tn = 512
    out = pl.pallas_call(
        _ada_kernel,
        grid=(depth, n // tn),
        in_specs=[pl.BlockSpec((MOD_ROWS, d), lambda l, j: (0, 0)),
                  pl.BlockSpec((None, d, tn), lambda l, j: (l, 0, j)),
                  pl.BlockSpec((None, 1, tn), lambda l, j: (l, 0, j))],
        out_specs=pl.BlockSpec((None, MOD_ROWS, tn), lambda l, j: (l, 0, j)),
        out_shape=jax.ShapeDtypeStruct((depth, MOD_ROWS, n), F32),
        compiler_params=_cp("parallel", "parallel"),
        name="ada_mods",
    )(cc, w_ada, b_ada.reshape(depth, 1, n))
    return out.reshape(depth, MOD_ROWS, N_MOD, 1, d)


def _inproj_kernel(x_ref, g_ref, sh_ref, sc_ref, w_ref, o_ref, xn_ref):
    @pl.when(pl.program_id(2) == 0)
    def _():
        xn_ref[...] = _modulated_norm(x_ref[...], g_ref[...], sh_ref[...], sc_ref[...]).astype(BF16)
    o_ref[...] = _dot(xn_ref[...], w_ref[...]).astype(o_ref.dtype)


def _in_projection(x, g, mods, layer, row_fn, w, col_lo, col_hi):
    b, l, d = x.shape
    tm = min(l, 1024)
    tn = 512
    j0 = col_lo // tn
    nj = (col_hi - col_lo) // tn
    mrow = lambda bi, i, j: row_fn(bi)
    return pl.pallas_call(
        _inproj_kernel,
        grid=(b, l // tm, nj),
        in_specs=[pl.BlockSpec((None, tm, d), lambda bi, i, j: (bi, i, 0)),
                  pl.BlockSpec((1, d), lambda bi, i, j: (0, 0)),
                  pl.BlockSpec((None, None, None, 1, d), _mod_spec(layer, 0, mrow)),
                  pl.BlockSpec((None, None, None, 1, d), _mod_spec(layer, 1, mrow)),
                  pl.BlockSpec((d, tn), lambda bi, i, j: (0, j + j0))],
        out_specs=pl.BlockSpec((None, tm, tn), lambda bi, i, j: (bi, i, j + j0)),
        out_shape=jax.ShapeDtypeStruct((b, l, w.shape[1]), BF16),
        scratch_shapes=[pltpu.VMEM((tm, d), BF16)],
        compiler_params=_cp("parallel", "parallel", "arbitrary"),
        name="in_projection",
    )(x, g, mods, mods, w)


def _qkprep_kernel(*refs, rope):
    if rope:
        q_ref, k_ref, gq_ref, gk_ref, bd_ref, cos_ref, sin_ref, qo_ref, ko_ref = refs
    else:
        q_ref, k_ref, gq_ref, gk_ref, bd_ref, qo_ref, ko_ref = refs
    for src, g_ref, dst in ((q_ref, gq_ref, qo_ref), (k_ref, gk_ref, ko_ref)):
        x = src[...].astype(F32)
        hi, lo = _split(x * x)
        ms = _dot(hi, bd_ref[...]) + _dot(lo, bd_ref[...])
        y = x * lax.rsqrt(ms + EPS) * g_ref[...]
        if rope:
            w = y.shape[1]
            lane = lax.broadcasted_iota(jnp.int32, y.shape, 1)
            first = (lane % (QK_DIM // 2)) < (QK_DIM // 4)
            partner = jnp.where(first, pltpu.roll(y, w - QK_DIM // 4, 1), pltpu.roll(y, QK_DIM // 4, 1))
            y = y * cos_ref[...] + partner * sin_ref[...]
        dst[...] = y.astype(BF16)


def _qk_prep(h, gq, gk, bd, rope_tabs):
    b, l, _ = h.shape
    w = N_HEADS * HEAD_W
    tm = min(l, 512)
    rope = rope_tabs is not None
    in_specs = [pl.BlockSpec((None, tm, w), lambda bi, i: (bi, i, COL_Q // w)),
                pl.BlockSpec((None, tm, w), lambda bi, i: (bi, i, COL_K // w)),
                pl.BlockSpec((1, w), lambda bi, i: (0, 0)),
                pl.BlockSpec((1, w), lambda bi, i: (0, 0)),
                pl.BlockSpec((w, w), lambda bi, i: (0, 0))]
    args = [h, h, gq, gk, bd]
    if rope:
        in_specs += [pl.BlockSpec((tm, w), lambda bi, i: (i, 0))] * 2
        args += list(rope_tabs)
    return pl.pallas_call(
        functools.partial(_qkprep_kernel, rope=rope),
        grid=(b, l // tm),
        in_specs=in_specs,
        out_specs=[pl.BlockSpec((None, tm, w), lambda bi, i: (bi, i, 0))] * 2,
        out_shape=[jax.ShapeDtypeStruct((b, l, w), BF16)] * 2,
        compiler_params=_cp("parallel", "parallel"),
        name="qk_prep",
    )(*args)


def _attn_kernel(*refs, n_src, lam_init):
    q_ref = refs[0]
    kv = refs[1:1 + 2 * n_src]
    lam_ref, gsub_ref, o_ref = refs[1 + 2 * n_src:]
    q = q_ref[...]
    tq = q.shape[0]
    lane = lax.broadcasted_iota(jnp.int32, q.shape, 1)
    zero = jnp.zeros_like(q)
    qq = jnp.concatenate([jnp.where(lane < QK_DIM, q, zero), jnp.where(lane >= QK_DIM, q, zero)], axis=0)
    scores = [_dot_nt(qq, kv[2 * i][...]) for i in range(n_src)]
    m = jnp.max(scores[0], axis=-1, keepdims=True)
    for s in scores[1:]:
        m = jnp.maximum(m, jnp.max(s, axis=-1, keepdims=True))
    z = jnp.zeros_like(m)
    acc = jnp.zeros((2 * tq, HEAD_W), F32)
    for i, s in enumerate(scores):
        e = jnp.exp(s - m)
        z = z + jnp.sum(e, axis=-1, keepdims=True)
        acc = acc + _dot(e.astype(BF16), kv[2 * i + 1][...])
    o2 = acc / z
    lf = lam_ref[...]
    lam_val = (jnp.exp(jnp.sum(lf[0:1] * lf[1:2], axis=-1, keepdims=True))
               - jnp.exp(jnp.sum(lf[2:3] * lf[3:4], axis=-1, keepdims=True)) + lam_init)
    o = o2[:tq] - lam_val * o2[tq:]
    ms = jnp.mean(o * o, axis=-1, keepdims=True)
    o = o * lax.rsqrt(ms + EPS) * gsub_ref[...] * (1.0 - lam_init)
    o_ref[...] = o.astype(BF16)


def _diff_attention(q, sources, lam_l, gsub, lam_init):
    b, lq, w = q.shape
    tq = 256
    in_specs = [pl.BlockSpec((None, tq, HEAD_W), lambda bi, hi, i: (bi, i, hi))]
    args = [q]
    for k, varr, vblk in sources:
        lk = k.shape[1]
        in_specs.append(pl.BlockSpec((None, lk, HEAD_W), lambda bi, hi, i: (bi, 0, hi)))
        in_specs.append(pl.BlockSpec((None, lk, HEAD_W), lambda bi, hi, i, vblk=vblk: (bi, 0, vblk + hi)))
        args += [k, varr]
    in_specs += [pl.BlockSpec(lam_l.shape, lambda bi, hi, i: (0, 0)),
                 pl.BlockSpec((1, HEAD_W), lambda bi, hi, i: (0, 0))]
    args += [lam_l, gsub]
    return pl.pallas_call(
        functools.partial(_attn_kernel, n_src=len(sources), lam_init=lam_init),
        grid=(b, N_HEADS, lq // tq),
        in_specs=in_specs,
        out_specs=pl.BlockSpec((None, tq, HEAD_W), lambda bi, hi, i: (bi, i, hi)),
        out_shape=jax.ShapeDtypeStruct((b, lq, w), BF16),
        compiler_params=_cp("parallel", "parallel", "arbitrary"),
        name="diff_attention",
    )(*args)


def _sconv_kernel(h_ref, w_ref, b_ref, o_ref):
    x = h_ref[...].astype(F32)
    n = x.shape[0]
    row = lax.broadcasted_iota(jnp.int32, x.shape, 0)
    prev = jnp.where(row == 0, 0.0, pltpu.roll(x, 1, 0))
    nxt = jnp.where(row == n - 1, 0.0, pltpu.roll(x, n - 1, 0))
    o_ref[...] = prev * w_ref[0:1, :] + x * w_ref[1:2, :] + nxt * w_ref[2:3, :] + b_ref[...]


def _short_conv(h, w, bias):
    b, l, _ = h.shape
    return pl.pallas_call(
        _sconv_kernel,
        grid=(b, 3),
        in_specs=[pl.BlockSpec((None, l, HY_W), lambda bi, j: (bi, 0, COL_HY // HY_W + j)),
                  pl.BlockSpec((3, HY_W), lambda bi, j: (0, j)),
                  pl.BlockSpec((1, HY_W), lambda bi, j: (0, j))],
        out_specs=pl.BlockSpec((None, None, l, HY_W), lambda bi, j: (bi, j, 0, 0)),
        out_shape=jax.ShapeDtypeStruct((b, 3, l, HY_W), F32),
        compiler_params=_cp("parallel", "parallel"),
        name="short_conv",
    )(h, w, bias)


def _filter_kernel(f_ref, w1_ref, b1_ref, fr_ref, w2_ref, b2_ref, w3_ref, dec_ref, o_ref):
    def mm(a, w_ref_):
        ah, al = _split(a)
        wh, wl = _split(w_ref_[...])
        return _dot(ah, wh) + _dot(ah, wl) + _dot(al, wh)
    fr = fr_ref[...]
    h = jnp.sin(fr * (mm(f_ref[...], w1_ref) + b1_ref[...]))
    h = jnp.sin(fr * (mm(h, w2_ref) + b2_ref[...]))
    h = mm(h, w3_ref)
    half = h.shape[1] // 2
    dec = dec_ref[...]
    hf = h[:, :half] * dec
    hb = h[:, half:] * dec
    row = lax.broadcasted_iota(jnp.int32, hb.shape, 0)
    hb = jnp.where(row == 0, 0.0, hb)
    norm = jnp.sum(jnp.abs(hf) + jnp.abs(hb), axis=0, keepdims=True)
    o_ref[:, :half] = (hf + hb) / norm
    o_ref[:, half:] = (hf - hb) / norm


def _hyena_filter_taps(feats, w1, b1, freq, w2, b2, w3, dec):
    l = feats.shape[0]
    n = w3.shape[1]
    full = lambda a: pl.BlockSpec(a.shape, lambda i: (0,) * a.ndim)
    args = (feats, w1, b1, freq, w2, b2, w3, dec)
    return pl.pallas_call(
        _filter_kernel,
        grid=(1,),
        in_specs=[full(a) for a in args],
        out_specs=pl.BlockSpec((l, n), lambda i: (0, 0)),
        out_shape=jax.ShapeDtypeStruct((l, n), F32),
        compiler_params=_cp("arbitrary"),
        name="hyena_filter_taps",
    )(*args)


def _table_mm_kernel(t_ref, x_ref, o_ref):
    o_ref[...] = _dot(t_ref[...], x_ref[...].astype(BF16)).astype(o_ref.dtype)


def _table_matmul(table, x, out_dtype):
    m, k = table.shape
    n = x.shape[1]
    tm = min(m, 512)
    return pl.pallas_call(
        _table_mm_kernel,
        grid=(m // tm,),
        in_specs=[pl.BlockSpec((tm, k), lambda i: (i, 0)),
                  pl.BlockSpec((k, n), lambda i: (0, 0))],
        out_specs=pl.BlockSpec((tm, n), lambda i: (i, 0)),
        out_shape=jax.ShapeDtypeStruct((m, n), out_dtype),
        compiler_params=_cp("parallel"),
        name="table_matmul",
    )(table, x)


def _dftmul_kernel(fc_ref, fs_ref, z_ref, k_ref, p_ref):
    z = z_ref[...].astype(BF16)
    zc = _dot(fc_ref[...], z)
    zs = _dot(fs_ref[...], z)
    p_ref[0] = (zc * k_ref[0] - zs * k_ref[1]).astype(BF16)
    p_ref[1] = (zc * k_ref[2] + zs * k_ref[3]).astype(BF16)


def _dft_multiply(ffwd, z, z_spec, kf):
    l = ffwd.shape[1]
    b = z.shape[0]
    tf = min(l, 512)
    nf = l // tf
    out = pl.pallas_call(
        _dftmul_kernel,
        grid=(nf, b),
        in_specs=[pl.BlockSpec((tf, l), lambda i, bi: (i, 0)),
                  pl.BlockSpec((tf, l), lambda i, bi: (i + nf, 0)),
                  z_spec,
                  pl.BlockSpec((4, tf, HY_W), lambda i, bi: (0, i, 0))],
        out_specs=pl.BlockSpec((None, 2, tf, HY_W), lambda i, bi: (bi, 0, i, 0)),
        out_shape=jax.ShapeDtypeStruct((b, 2, l, HY_W), BF16),
        compiler_params=_cp("parallel", "arbitrary"),
        name="dft_multiply",
    )(ffwd, ffwd, z, kf)
    return out.reshape(b, 2 * l, HY_W)


def _idft_gate_kernel(fi_ref, p_ref, g_ref, z_ref, b_ref, o_ref):
    conv = _dot(fi_ref[...], p_ref[...])
    o_ref[...] = (g_ref[...] * (conv + b_ref[...] * z_ref[...])).astype(o_ref.dtype)


def _idft_gate(finv, p, gate, gate_spec, z, z_spec, bias, out_dtype):
    l = finv.shape[0]
    b = p.shape[0]
    tt = min(l, 512)
    return pl.pallas_call(
        _idft_gate_kernel,
        grid=(l // tt, b),
        in_specs=[pl.BlockSpec((tt, 2 * l), lambda i, bi: (i, 0)),
                  pl.BlockSpec((None, 2 * l, HY_W), lambda i, bi: (bi, 0, 0)),
                  gate_spec, z_spec,
                  pl.BlockSpec((1, HY_W), lambda i, bi: (0, 0))],
        out_specs=pl.BlockSpec((None, tt, HY_W), lambda i, bi: (bi, i, 0)),
        out_shape=jax.ShapeDtypeStruct((b, l, HY_W), out_dtype),
        compiler_params=_cp("parallel", "arbitrary"),
        name="idft_gate",
    )(finv, p, gate, z, bias)


def _hyena_mix(h, conv_w, conv_b, kf, hy_bias, ffwd, finv):
    b, l, _ = h.shape
    tt = min(l, 512)
    u = _short_conv(h, conv_w, conv_b)
    part = lambda j, rows: pl.BlockSpec((None, None, rows, HY_W),
                                        lambda i, bi, j=j: (bi, j, i if rows != l else 0, 0))
    p = _dft_multiply(ffwd, u, part(0, l), kf[0])
    z1 = _idft_gate(finv, p, u, part(1, tt), u, part(0, tt), hy_bias[0:1], F32)
    p = _dft_multiply(ffwd, z1, pl.BlockSpec((None, l, HY_W), lambda i, bi: (bi, 0, 0)), kf[1])
    return _idft_gate(finv, p, u, part(2, tt), z1,
                      pl.BlockSpec((None, tt, HY_W), lambda i, bi: (bi, i, 0)), hy_bias[1:2], BF16)


def _fn1_kernel(z_ref, m_ref, o_ref):
    r = _dot(z_ref[...], m_ref[...])
    half = r.shape[1] // 2
    o_ref[0] = r[:, :half].astype(BF16)
    o_ref[1] = r[:, half:].astype(BF16)


def _fourier_mix(h, m1, t2):
    b, l, _ = h.shape
    tm = min(l, 512)
    zz = pl.pallas_call(
        _fn1_kernel,
        grid=(b, l // tm),
        in_specs=[pl.BlockSpec((None, tm, FN_W), lambda bi, i: (bi, i, COL_FN // FN_W)),
                  pl.BlockSpec((FN_W, 2 * FN_W), lambda bi, i: (0, 0))],
        out_specs=pl.BlockSpec((None, 2, tm, FN_W), lambda bi, i: (bi, 0, i, 0)),
        out_shape=jax.ShapeDtypeStruct((b, 2, l, FN_W), BF16),
        compiler_params=_cp("parallel", "parallel"),
        name="fnet_channels",
    )(h, m1).reshape(b, 2 * l, FN_W)
    return pl.pallas_call(
        _table_mm_kernel,
        grid=(l // tm, b),
        in_specs=[pl.BlockSpec((tm, 2 * l), lambda i, bi: (i, 0)),
                  pl.BlockSpec((None, 2 * l, FN_W), lambda i, bi: (bi, 0, 0))],
        out_specs=pl.BlockSpec((None, tm, FN_W), lambda i, bi: (bi, i, 0)),
        out_shape=jax.ShapeDtypeStruct((b, l, FN_W), BF16),
        compiler_params=_cp("parallel", "arbitrary"),
        name="fnet_positions",
    )(t2, zz)


def _merge_kernel(hy_ref, fn_ref, at_ref, g_ref, x_ref, gm_ref, why_ref, wfn_ref, wat_ref, wout_ref, o_ref):
    d = x_ref.shape[-1]
    g = 1.0 / (1.0 + jnp.exp(-g_ref[...].astype(F32)))
    y = (g[:, :d] * _dot(hy_ref[...], why_ref[...])
         + g[:, d:2 * d] * _dot(fn_ref[...], wfn_ref[...])
         + g[:, 2 * d:] * _dot(at_ref[...], wat_ref[...]))
    mix = _dot(y.astype(BF16), wout_ref[...])
    o_ref[...] = x_ref[...] + gm_ref[...] * mix


def _merge(hyo, fno, att, h, x, mods, layer, row_fn, w_hy, w_fn, w_at, w_out):
    b, l, d = x.shape
    tm = min(l, 512)
    full = lambda a: pl.BlockSpec(a.shape, lambda bi, i: (0,) * a.ndim)
    tok = lambda wdt, blk=0: pl.BlockSpec((None, tm, wdt), lambda bi, i: (bi, i, blk))
    return pl.pallas_call(
        _merge_kernel,
        grid=(b, l // tm),
        in_specs=[tok(HY_W), tok(FN_W), tok(N_HEADS * HEAD_W), tok(3 * d, COL_GATE), tok(d),
                  pl.BlockSpec((None, None, None, 1, d), _mod_spec(layer, 2, lambda bi, i: row_fn(bi))),
                  full(w_hy), full(w_fn), full(w_at), full(w_out)],
        out_specs=tok(d),
        out_shape=jax.ShapeDtypeStruct((b, l, d), F32),
        compiler_params=_cp("parallel", "parallel"),
        name="merge_residual",
    )(hyo, fno, att, h, x, mods, w_hy, w_fn, w_at, w_out)


def _peer_q_kernel(x_ref, g_ref, sh_ref, sc_ref, wq_ref, kh_ref, kl_ref, s_ref, nt_ref):
    n = _modulated_norm(x_ref[...], g_ref[...], sh_ref[...], sc_ref[...])
    nt_ref[...] = n.T.astype(BF16)
    q = _dot(n.astype(BF16), wq_ref[...])
    dq = kh_ref.shape[1]
    for hp in range(kh_ref.shape[0]):
        qh, ql = _split(q[:, hp * dq:(hp + 1) * dq])
        kh = kh_ref[hp]
        s = _dot(qh, kh) + _dot(ql, kh) + _dot(qh, kl_ref[hp])
        s_ref[hp] = s.T


def _peer_scores(x, g, mods, layer, row_fn, wq, keys_t):
    b, l, d = x.shape
    tt = min(l, 512)
    nt = l // tt
    nhp = 2 * PEER_HEADS
    mrow = lambda bi, i: row_fn(bi)
    full = lambda a: pl.BlockSpec(a.shape, lambda bi, i: (0,) * a.ndim)
    return pl.pallas_call(
        _peer_q_kernel,
        grid=(b, nt),
        in_specs=[pl.BlockSpec((None, tt, d), lambda bi, i: (bi, i, 0)),
                  pl.BlockSpec((1, d), lambda bi, i: (0, 0)),
                  pl.BlockSpec((None, None, None, 1, d), _mod_spec(layer, 3, mrow)),
                  pl.BlockSpec((None, None, None, 1, d), _mod_spec(layer, 4, mrow)),
                  full(wq), full(keys_t[0]), full(keys_t[1])],
        out_specs=[pl.BlockSpec((nhp, PEER_NKEYS, tt), lambda bi, i: (0, 0, bi * nt + i)),
                   pl.BlockSpec((d, tt), lambda bi, i: (0, bi * nt + i))],
        out_shape=[jax.ShapeDtypeStruct((nhp, PEER_NKEYS, b * l), F32),
                   jax.ShapeDtypeStruct((d, b * l), BF16)],
        compiler_params=_cp("parallel", "parallel"),
        name="peer_scores",
    )(x, g, mods, mods, wq, keys_t[0], keys_t[1])


def _top_rows(s, k):
    vals = []
    for r in range(k):
        m = jnp.max(s, axis=0, keepdims=True)
        vals.append(m)
        if r + 1 < k:
            s = jnp.where(s == m, NEG, s)
    return vals


def _merge_sort_pairs(n):
    pairs = []
    p = 1
    while p < n:
        k = p
        while k >= 1:
            for j in range(k % p, n - k, 2 * k):
                for i in range(min(k, n - j - k)):
                    if (i + j) // (2 * p) == (i + j + k) // (2 * p):
                        pairs.append((i + j, i + j + k))
            k //= 2
        p *= 2
    return pairs


def _top_rows_grouped(s, k):
    groups = s.shape[0] // 8
    v = [s[g * 8:(g + 1) * 8, :] for g in range(groups)]
    for i, j in _merge_sort_pairs(groups):
        v[i], v[j] = jnp.maximum(v[i], v[j]), jnp.minimum(v[i], v[j])
    v.append(jnp.full_like(v[0], NEG))
    vals = []
    for r in range(k):
        m = jnp.max(v[0], axis=0, keepdims=True)
        vals.append(m)
        taken = v[0] == m
        for d in range(min(groups, k - 1 - r)):
            v[d] = jnp.where(taken, v[d + 1], v[d])
    return vals


def _peer_expert_kernel(s_ref, nt_ref, u_ref, vt_ref, x_ref, gm_ref, o_ref,
                        tau_ref, l1_ref, l2_ref, l1c_ref, hid0_ref, hid1_ref, p0_ref, p1_ref, acc_ref):
    s = pl.program_id(1)
    nc = pl.num_programs(1) - 2
    k = PEER_TOPK
    tt = nt_ref.shape[1]
    n_i = u_ref.shape[0] // PEER_NKEYS
    piece = GATE_VREGS * 8 * 128 // tt

    def hidden(hid_ref):
        hid_ref[...] = _dot(u_ref[...], nt_ref[...])

    def stages(chunk, hid_w, hid_r, p_w, p_r):
        def hidden_task(slab, cols):
            def run():
                hid_w[slab, cols] = _dot(u_ref[slab, :], nt_ref[:, cols])
            return run

        def project_task(slab, cols):
            def run():
                acc_ref[slab, cols] += _dot(vt_ref[slab, :], p_r[:, cols])
            return run

        tasks = []
        for r0 in range(0, u_ref.shape[0], MXU_SLAB):
            for c0 in range(0, tt, MXU_COLS):
                slab, cols = slice(r0, r0 + MXU_SLAB), slice(c0, min(c0 + MXU_COLS, tt))
                tasks += [hidden_task(slab, cols), project_task(slab, cols)]
        n_jp = PEER_NKEYS // piece
        slots = [[] for _ in range(n_i * n_jp)]
        for t, task in enumerate(tasks):
            slots[t * len(slots) // len(tasks)].append(task)
        first = pl.multiple_of(chunk * n_i, n_i)
        for h in range(PEER_HEADS):
            l1c_ref[h] = l1_ref[h, pl.ds(first, n_i), :]
        for ii in range(n_i):
            l1_row = [jnp.broadcast_to(l1c_ref[h, ii:ii + 1, :], (8, tt))[None] for h in range(PEER_HEADS)]
            for jp in range(n_jp):
                for task in slots[ii * n_jp + jp]:
                    task()
                js = slice(jp * piece, (jp + 1) * piece)
                gate = jnp.zeros((piece // 8, 8, tt), F32)
                for h in range(PEER_HEADS):
                    logw = l1_row[h] + l2_ref[h, js, :].reshape(piece // 8, 8, tt)
                    gate = gate + jnp.where(logw >= tau_ref[h][None], jnp.exp2(logw), 0.0)
                rows = slice(ii * PEER_NKEYS + jp * piece, ii * PEER_NKEYS + (jp + 1) * piece)
                hid = hid_r[rows, :]
                act = hid * (1.0 + lax.erf(hid * (2.0 ** -0.5)))
                p_w[rows, :] = (gate.reshape(piece, tt) * act).astype(BF16)

    @pl.when(s == 0)
    def _():
        acc_ref[...] = jnp.zeros_like(acc_ref)
        p1_ref[...] = jnp.zeros_like(p1_ref)
        hidden(hid0_ref)
        width = min(tt, 256)
        parts = tt // width

        def select(it, carry):
            h = it // parts
            cols = pl.ds(pl.multiple_of((it % parts) * width, width), width)
            s1 = s_ref[2 * h, :, cols]
            s2 = s_ref[2 * h + 1, :, cols]
            a = _top_rows_grouped(s1, k + 1)
            b = _top_rows_grouped(s2, k + 1)
            cand = [a[i] + b[j] for i in range(k + 1) for j in range((k + 1) // (i + 1))]
            pad = (-len(cand)) % 64
            top = _top_rows_grouped(jnp.concatenate(cand + [jnp.full_like(a[0], NEG)] * pad, axis=0), k + 1)
            z = jnp.zeros_like(top[0])
            for t in top[:k]:
                z = z + jnp.exp(t - top[0])
            shift = top[0] + jnp.log(z)
            log2e = 1.0 / math.log(2.0)
            l1_ref[h, :, cols] = s1 * log2e
            l2_ref[h, :, cols] = (s2 - shift) * log2e - 1.0
            tau = (0.5 * (top[k - 1] + top[k]) - shift) * log2e - 1.0
            tau_ref[h, :, cols] = jnp.broadcast_to(tau, (8, width))
            return carry

        lax.fori_loop(0, PEER_HEADS * parts, select, 0)

    @pl.when((s >= 1) & (s <= nc) & (s % 2 == 1))
    def _():
        stages(s - 1, hid1_ref, hid0_ref, p0_ref, p1_ref)

    @pl.when((s >= 1) & (s <= nc) & (s % 2 == 0))
    def _():
        stages(s - 1, hid0_ref, hid1_ref, p1_ref, p0_ref)

    @pl.when(s == nc + 1)
    def _():
        last = p0_ref if (nc - 1) % 2 == 0 else p1_ref
        out_t = acc_ref[...] + _dot(vt_ref[...], last[...])
        o_ref[...] = x_ref[...] + gm_ref[...] * out_t.T


def _peer_experts(scores, n_t, u, v_t, x, mods, layer, row_fn):
    b, l, d = x.shape
    tt = min(l, PEER_TOKENS)
    nt = l // tt
    nc, _, ec = v_t.shape
    assert ec == d
    nhp = scores.shape[0]
    return pl.pallas_call(
        _peer_expert_kernel,
        grid=(b * nt, nc + 2),
        in_specs=[pl.BlockSpec((nhp, PEER_NKEYS, tt), lambda t, s: (0, 0, t)),
                  pl.BlockSpec((d, tt), lambda t, s: (0, t)),
                  pl.BlockSpec((ec, d), lambda t, s: (jnp.minimum(s, nc - 1), 0)),
                  pl.BlockSpec((None, d, ec), lambda t, s: (jnp.clip(s - 2, 0, nc - 1), 0, 0)),
                  pl.BlockSpec((None, tt, d), lambda t, s: (t // nt, t % nt, 0)),
                  pl.BlockSpec((None, None, None, 1, d), _mod_spec(layer, 5, lambda t, s: row_fn(t // nt)))],
        out_specs=pl.BlockSpec((None, tt, d), lambda t, s: (t // nt, t % nt, 0)),
        out_shape=jax.ShapeDtypeStruct((b, l, d), F32),
        scratch_shapes=[pltpu.VMEM((PEER_HEADS, 8, tt), F32),
                        pltpu.VMEM((PEER_HEADS, PEER_NKEYS, tt), F32),
                        pltpu.VMEM((PEER_HEADS, PEER_NKEYS, tt), F32),
                        pltpu.VMEM((PEER_HEADS, d // PEER_NKEYS, tt), F32),
                        pltpu.VMEM((ec, tt), F32),
                        pltpu.VMEM((ec, tt), F32),
                        pltpu.VMEM((ec, tt), BF16),
                        pltpu.VMEM((ec, tt), BF16),
                        pltpu.VMEM((d, tt), F32)],
        compiler_params=_cp("parallel", "arbitrary"),
        name="peer_experts",
    )(scores, n_t, u, v_t, x, mods)


def _peer(x, g, mods, layer, row_fn, wq_t, keys, u, v_t):
    scores, n_t = _peer_scores(x, g, mods, layer, row_fn, wq_t, keys)
    return _peer_experts(scores, n_t, u, v_t, x, mods, layer, row_fn)


def _dft_tables(l):
    cos, sin = _cos_sin_table(l, l, 2 * l)
    t = jnp.arange(l, dtype=jnp.int32)[None, :]
    first = (jnp.arange(l) == 0)[:, None]
    sin = jnp.where(first, jnp.where(t % 2 == 0, 1.0, -1.0), sin)
    ffwd = jnp.concatenate([cos, sin], axis=0).astype(BF16)
    return ffwd, ffwd.T


def _cos_sin_table(nf, nt, n):
    step = 64
    f = jnp.arange(nf, dtype=jnp.int32)[:, None]
    angle = lambda prod: (2.0 * math.pi) * ((prod % n).astype(F32) / n)
    a = angle(f * (step * jnp.arange(nt // step, dtype=jnp.int32)[None, :]))
    b = angle(f * jnp.arange(step, dtype=jnp.int32)[None, :])
    ca, sa = jnp.cos(a)[:, :, None], jnp.sin(a)[:, :, None]
    cb, sb = jnp.cos(b)[:, None, :], jnp.sin(b)[:, None, :]
    return (ca * cb - sa * sb).reshape(nf, nt), (sa * cb + ca * sb).reshape(nf, nt)


def _fnet_tables(l):
    t2 = jnp.concatenate(_cos_sin_table(l, l, l), axis=1).astype(BF16)
    k = np.arange(FN_GROUP)
    ang64 = 2.0 * np.pi * ((k[:, None] * k[None, :]) % FN_GROUP) / FN_GROUP
    eye = np.eye(FN_W // FN_GROUP)
    scale = 1.0 / math.sqrt(FN_GROUP * l)
    m1 = np.concatenate([np.kron(eye, np.cos(ang64)), -np.kron(eye, np.sin(ang64))], axis=1) * scale
    return jnp.asarray(m1, F32).astype(BF16), t2


def _rope_tables(l):
    rows = l // GRID_W
    row = jnp.repeat(jnp.arange(rows), GRID_W).astype(F32)
    col = jnp.tile(jnp.arange(GRID_W), rows).astype(F32)
    half = QK_DIM // 2
    inv = ROPE_BASE ** (-jnp.arange(0, half, 2, dtype=F32) / half)
    ang = jnp.stack([row[:, None] * inv, col[:, None] * inv], axis=1)
    cos = jnp.repeat(jnp.cos(ang)[:, :, None, :], 2, axis=2)
    sin = jnp.sin(ang)
    sin = jnp.stack([-sin, sin], axis=2)
    rep = lambda a: jnp.tile(a.reshape(l, QK_DIM), (1, 2 * N_HEADS))
    return rep(cos), rep(sin)


def _filter_features(l):
    pos = jnp.arange(l, dtype=F32)
    t = pos / max(l - 1, 1)
    w = 2.0 * math.pi * pos / l
    f = jnp.linspace(1e-4, HY_BANDS - 1, HY_BANDS, dtype=F32)
    feats = jnp.concatenate([t[:, None], jnp.cos(w[:, None] * f), -jnp.sin(w[:, None] * f)], axis=-1)
    feats = jnp.pad(feats, ((0, 0), (0, 64 - HY_EMB)))
    deltas = jnp.abs(jnp.linspace(HY_MIN_DECAY, HY_MAX_DECAY, HY_W, dtype=F32))
    dec = jnp.exp(-t[:, None] * deltas)
    return feats, jnp.tile(dec, (1, HY_ORDER))


def _hyena_filters(l, tabs, ffwd, w1, b1, freq, w2, b2, w3):
    feats, dec = tabs
    taps = _hyena_filter_taps(feats, jnp.pad(w1, ((0, 64 - HY_EMB), (0, 0))), b1[None], freq[None],
                              w2, b2[None], w3, dec)
    kf = _table_matmul(ffwd, taps, F32)
    half = HY_ORDER * HY_W
    kc = kf[:l, :half]
    nyq = kf[l, :half]
    ks = kf[l:, half:]
    n = 2.0 * l
    first = (jnp.arange(l) == 0)[:, None]
    wc = jnp.where(first, 1.0 / n, 2.0 / n)
    ka = kc * wc
    kb = jnp.where(first, 0.0, ks * (2.0 / n))
    kd = jnp.where(first, nyq[None, :] / n, kc * (2.0 / n))
    stack = jnp.stack([ka, kb, kb, kd], axis=0)
    return jnp.moveaxis(stack.reshape(4, l, HY_ORDER, HY_W), 2, 0)


def kernel(x, c, ctx, c_ctx, w_ada, b_ada, g_mix, g_ffn, w_in, hy_conv_w, hy_conv_b, hy_w1, hy_b1, hy_freq, hy_w2, hy_b2, hy_w3, hy_bias, g_q, g_k, lam, g_sub, w_hy, w_fn, w_at, w_out, peer_wq, peer_keys, peer_u, peer_v):
    bsz, seq, d = x.shape
    clen = ctx.shape[1]
    depth = w_ada.shape[0]

    cc = jnp.concatenate([c, c_ctx[None], jnp.zeros((MOD_ROWS - bsz - 1, d), F32)], axis=0)
    mods = _ada_mods(cc, w_ada, b_ada)
    lat_row = lambda bi: bi
    ctx_row = lambda bi: bsz

    rope = _rope_tables(seq)
    tabs = {n: dict(dft=_dft_tables(n), fnet=_fnet_tables(n), feat=_filter_features(n)) for n in (seq, clen)}
    w = N_HEADS * HEAD_W
    lane = np.arange(w)
    bd = jnp.asarray((lane[:, None] // QK_DIM == lane[None, :] // QK_DIM) / QK_DIM, F32).astype(BF16)

    xl, xc = x, ctx
    for l in range(depth):
        last = l == depth - 1
        lam_init = 0.8 - 0.6 * math.exp(-0.3 * l)
        wl = w_in[l]
        w_perm = jnp.concatenate([wl[:, 2560:], wl[:, :2560]], axis=1).astype(BF16)
        gq = jnp.tile(g_q[l].reshape(1, HEAD_W), (1, N_HEADS)) * (QK_DIM ** -0.5)
        gk = jnp.tile(g_k[l].reshape(1, HEAD_W), (1, N_HEADS))
        gsub = g_sub[l][None]
        wts = [a[l].astype(BF16) for a in (w_hy, w_fn, w_at, w_out)]
        filt_args = (hy_w1[l], hy_b1[l], hy_freq[l], hy_w2[l], hy_b2[l], hy_w3[l])

        h_l = _in_projection(xl, g_mix[l][None], mods, l, lat_row, w_perm, 0, P_IN)
        h_c = _in_projection(xc, g_mix[l][None], mods, l, ctx_row, w_perm, COL_K if last else 0, P_IN)
        q_l, k_l = _qk_prep(h_l, gq, gk, bd, rope)
        q_c, k_c = _qk_prep(h_c, gq, gk, bd, None)
        vblk = COL_V // HEAD_W
        att_l = _diff_attention(q_l, [(k_l, h_l, vblk), (k_c, h_c, vblk)], lam[l], gsub, lam_init)
        ffwd, finv = tabs[seq]["dft"]
        kf = _hyena_filters(seq, tabs[seq]["feat"], ffwd, *filt_args)
        hyo_l = _hyena_mix(h_l, hy_conv_w[l], hy_conv_b[l][None], kf, hy_bias[l], ffwd, finv)
        fno_l = _fourier_mix(h_l, *tabs[seq]["fnet"])
        if not last:
            att_c = _diff_attention(q_c, [(k_c, h_c, vblk)], lam[l], gsub, lam_init)
            ffwd_c, finv_c = tabs[clen]["dft"]
            kf_c = _hyena_filters(clen, tabs[clen]["feat"], ffwd_c, *filt_args)
            hyo_c = _hyena_mix(h_c, hy_conv_w[l], hy_conv_b[l][None], kf_c, hy_bias[l], ffwd_c, finv_c)
            fno_c = _fourier_mix(h_c, *tabs[clen]["fnet"])
            xc = _merge(hyo_c, fno_c, att_c, h_c, xc, mods, l, ctx_row, *wts)
        xl = _merge(hyo_l, fno_l, att_l, h_l, xl, mods, l, lat_row, *wts)

        wq_t = peer_wq[l].astype(BF16)
        keys = _split(jnp.swapaxes(peer_keys[l].reshape(2 * PEER_HEADS, PEER_NKEYS, -1), 1, 2))
        u = peer_u[l].astype(BF16)
        v_t = jnp.swapaxes(peer_v[l].reshape(-1, d, d), 1, 2).astype(BF16)
        if not last:
            xc = _peer(xc, g_ffn[l][None], mods, l, ctx_row, wq_t, keys, u, v_t)
        xl = _peer(xl, g_ffn[l][None], mods, l, lat_row, wq_t, keys, u, v_t)
    return xl
```

```python
import functools
import math

import jax
import jax.numpy as jnp
import numpy as np
from jax import lax
from jax.experimental import pallas as pl
from jax.experimental.pallas import tpu as pltpu

F32 = jnp.float32
BF16 = jnp.bfloat16

EPS = 1e-6
GRID_W = 64
ROPE_BASE = 10000.0
N_HEADS = 4
QK_DIM = 64
HEAD_W = 2 * QK_DIM
HY_W = 256
HY_ORDER = 2
HY_EMB = 33
HY_BANDS = (HY_EMB - 1) // 2
HY_MIN_DECAY = math.log(1e-2) / 1.5
HY_MAX_DECAY = math.log(1e-2) / 0.3
FN_GROUP = 64
FN_W = 256
PEER_HEADS = 8
PEER_NKEYS = 128
PEER_TOPK = 16
N_MOD = 6
MOD_ROWS = 16
NEG = -3.0e38
MXU_SLAB = 1024
MXU_COLS = 256
MXU_BURSTS = 1
GATE_VREGS = 8
PEER_TOKENS = 256

VMEM_LIMIT = 56 * 1024 * 1024

COL_GATE = 0
COL_HY = 3072
COL_FN = 3840
COL_Q = 4096
COL_K = 4608
COL_V = 5120
P_IN = 5632


def _cp(*sem):
    return pltpu.CompilerParams(dimension_semantics=sem, vmem_limit_bytes=VMEM_LIMIT)


def _dot(a, b):
    return jnp.dot(a, b, preferred_element_type=F32)


def _dot_nt(a, b):
    return lax.dot_general(a, b, (((1,), (1,)), ((), ())), preferred_element_type=F32)


def _split(a):
    hi = a.astype(BF16)
    lo = (a - hi.astype(F32)).astype(BF16)
    return hi, lo


def _modulated_norm(x, g, shift, scale):
    ms = jnp.mean(x * x, axis=-1, keepdims=True)
    y = x * lax.rsqrt(ms + EPS) * g
    return y * (1.0 + scale) + shift


def _mod_spec(layer, chunk, row_fn):
    def imap(*idx):
        return (layer, row_fn(*idx), chunk, 0, 0)
    return imap


def _ada_kernel(c_ref, w_ref, b_ref, o_ref):
    c = c_ref[...]
    a = c / (1.0 + jnp.exp(-c))
    ah, al = _split(a)
    wh, wl = _split(w_ref[...])
    o_ref[...] = _dot(ah, wh) + _dot(ah, wl) + _dot(al, wh) + b_ref[...]


def _ada_mods(cc, w_ada, b_ada):
    depth, d, n = w_ada.shape
    tn = 512
    out = pl.pallas_call(
        _ada_kernel,
        grid=(depth, n // tn),
        in_specs=[pl.BlockSpec((MOD_ROWS, d), lambda l, j: (0, 0)),
                  pl.BlockSpec((None, d, tn), lambda l, j: (l, 0, j)),
                  pl.BlockSpec((None, 1, tn), lambda l, j: (l, 0, j))],
        out_specs=pl.BlockSpec((None, MOD_ROWS, tn), lambda l, j: (l, 0, j)),
        out_shape=jax.ShapeDtypeStruct((depth, MOD_ROWS, n), F32),
        compiler_params=_cp("parallel", "parallel"),
        name="ada_mods",
    )(cc, w_ada, b_ada.reshape(depth, 1, n))
    return out.reshape(depth, MOD_ROWS, N_MOD, 1, d)


def _inproj_kernel(x_ref, g_ref, sh_ref, sc_ref, w_ref, o_ref, xn_ref):
    @pl.when(pl.program_id(2) == 0)
    def _():
        xn_ref[...] = _modulated_norm(x_ref[...], g_ref[...], sh_ref[...], sc_ref[...]).astype(BF16)
    o_ref[...] = _dot(xn_ref[...], w_ref[...]).astype(o_ref.dtype)


def _in_projection(x, g, mods, layer, row_fn, w, col_lo, col_hi):
    b, l, d = x.shape
    tm = min(l, 1024)
    wide = w.shape[1] // 2
    tn = wide if (col_lo % wide == 0 and (col_hi - col_lo) % wide == 0) else 512
    j0 = col_lo // tn
    nj = (col_hi - col_lo) // tn
    mrow = lambda bi, i, j: row_fn(bi)
    return pl.pallas_call(
        _inproj_kernel,
        grid=(b, l // tm, nj),
        in_specs=[pl.BlockSpec((None, tm, d), lambda bi, i, j: (bi, i, 0)),
                  pl.BlockSpec((1, d), lambda bi, i, j: (0, 0)),
                  pl.BlockSpec((None, None, None, 1, d), _mod_spec(layer, 0, mrow)),
                  pl.BlockSpec((None, None, None, 1, d), _mod_spec(layer, 1, mrow)),
                  pl.BlockSpec((d, tn), lambda bi, i, j: (0, j + j0))],
        out_specs=pl.BlockSpec((None, tm, tn), lambda bi, i, j: (bi, i, j + j0)),
        out_shape=jax.ShapeDtypeStruct((b, l, w.shape[1]), BF16),
        scratch_shapes=[pltpu.VMEM((tm, d), BF16)],
        compiler_params=_cp("parallel", "parallel", "arbitrary"),
        name="in_projection",
    )(x, g, mods, mods, w)


def _qkprep_kernel(*refs, rope):
    if rope:
        q_ref, k_ref, gq_ref, gk_ref, bd_ref, cos_ref, sin_ref, qo_ref, ko_ref = refs
    else:
        q_ref, k_ref, gq_ref, gk_ref, bd_ref, qo_ref, ko_ref = refs
    for src, g_ref, dst in ((q_ref, gq_ref, qo_ref), (k_ref, gk_ref, ko_ref)):
        x = src[...].astype(F32)
        hi, lo = _split(x * x)
        ms = _dot(hi, bd_ref[...]) + _dot(lo, bd_ref[...])
        y = x * lax.rsqrt(ms + EPS) * g_ref[...]
        if rope:
            w = y.shape[1]
            lane = lax.broadcasted_iota(jnp.int32, y.shape, 1)
            first = (lane % (QK_DIM // 2)) < (QK_DIM // 4)
            partner = jnp.where(first, pltpu.roll(y, w - QK_DIM // 4, 1), pltpu.roll(y, QK_DIM // 4, 1))
            y = y * cos_ref[...] + partner * sin_ref[...]
        dst[...] = y.astype(BF16)


def _qk_prep(h, gq, gk, bd, rope_tabs):
    b, l, _ = h.shape
    w = N_HEADS * HEAD_W
    tm = min(l, 512)
    rope = rope_tabs is not None
    in_specs = [pl.BlockSpec((None, tm, w), lambda bi, i: (bi, i, COL_Q // w)),
                pl.BlockSpec((None, tm, w), lambda bi, i: (bi, i, COL_K // w)),
                pl.BlockSpec((1, w), lambda bi, i: (0, 0)),
                pl.BlockSpec((1, w), lambda bi, i: (0, 0)),
                pl.BlockSpec((w, w), lambda bi, i: (0, 0))]
    args = [h, h, gq, gk, bd]
    if rope:
        in_specs += [pl.BlockSpec((tm, w), lambda bi, i: (i, 0))] * 2
        args += list(rope_tabs)
    return pl.pallas_call(
        functools.partial(_qkprep_kernel, rope=rope),
        grid=(b, l // tm),
        in_specs=in_specs,
        out_specs=[pl.BlockSpec((None, tm, w), lambda bi, i: (bi, i, 0))] * 2,
        out_shape=[jax.ShapeDtypeStruct((b, l, w), BF16)] * 2,
        compiler_params=_cp("parallel", "parallel"),
        name="qk_prep",
    )(*args)


def _attn_kernel(*refs, n_src, lam_init):
    q_ref = refs[0]
    kv = refs[1:1 + 2 * n_src]
    lam_ref, gsub_ref, o_ref = refs[1 + 2 * n_src:]
    q = q_ref[...]
    tq = q.shape[0]
    lane = lax.broadcasted_iota(jnp.int32, q.shape, 1)
    zero = jnp.zeros_like(q)
    qq = jnp.concatenate([jnp.where(lane < QK_DIM, q, zero), jnp.where(lane >= QK_DIM, q, zero)], axis=0)
    scores = [_dot_nt(qq, kv[2 * i][...]) for i in range(n_src)]
    m = jnp.max(scores[0], axis=-1, keepdims=True)
    for s in scores[1:]:
        m = jnp.maximum(m, jnp.max(s, axis=-1, keepdims=True))
    z = jnp.zeros_like(m)
    acc = jnp.zeros((2 * tq, HEAD_W), F32)
    for i, s in enumerate(scores):
        e = jnp.exp(s - m)
        z = z + jnp.sum(e, axis=-1, keepdims=True)
        acc = acc + _dot(e.astype(BF16), kv[2 * i + 1][...])
    o2 = acc / z
    lf = lam_ref[...]
    lam_val = (jnp.exp(jnp.sum(lf[0:1] * lf[1:2], axis=-1, keepdims=True))
               - jnp.exp(jnp.sum(lf[2:3] * lf[3:4], axis=-1, keepdims=True)) + lam_init)
    o = o2[:tq] - lam_val * o2[tq:]
    ms = jnp.mean(o * o, axis=-1, keepdims=True)
    o = o * lax.rsqrt(ms + EPS) * gsub_ref[...] * (1.0 - lam_init)
    o_ref[...] = o.astype(BF16)


def _diff_attention(q, sources, lam_l, gsub, lam_init):
    b, lq, w = q.shape
    tq = min(lq, 512)
    in_specs = [pl.BlockSpec((None, tq, HEAD_W), lambda bi, hi, i: (bi, i, hi))]
    args = [q]
    for k, varr, vblk in sources:
        lk = k.shape[1]
        in_specs.append(pl.BlockSpec((None, lk, HEAD_W), lambda bi, hi, i: (bi, 0, hi)))
        in_specs.append(pl.BlockSpec((None, lk, HEAD_W), lambda bi, hi, i, vblk=vblk: (bi, 0, vblk + hi)))
        args += [k, varr]
    in_specs += [pl.BlockSpec(lam_l.shape, lambda bi, hi, i: (0, 0)),
                 pl.BlockSpec((1, HEAD_W), lambda bi, hi, i: (0, 0))]
    args += [lam_l, gsub]
    return pl.pallas_call(
        functools.partial(_attn_kernel, n_src=len(sources), lam_init=lam_init),
        grid=(b, N_HEADS, lq // tq),
        in_specs=in_specs,
        out_specs=pl.BlockSpec((None, tq, HEAD_W), lambda bi, hi, i: (bi, i, hi)),
        out_shape=jax.ShapeDtypeStruct((b, lq, w), BF16),
        compiler_params=_cp("parallel", "parallel", "arbitrary"),
        name="diff_attention",
    )(*args)


def _sconv_kernel(h_ref, w_ref, b_ref, o_ref):
    x = h_ref[...].astype(F32)
    n = x.shape[0]
    row = lax.broadcasted_iota(jnp.int32, x.shape, 0)
    prev = jnp.where(row == 0, 0.0, pltpu.roll(x, 1, 0))
    nxt = jnp.where(row == n - 1, 0.0, pltpu.roll(x, n - 1, 0))
    o_ref[...] = prev * w_ref[0:1, :] + x * w_ref[1:2, :] + nxt * w_ref[2:3, :] + b_ref[...]


def _short_conv(h, w, bias):
    b, l, _ = h.shape
    return pl.pallas_call(
        _sconv_kernel,
        grid=(b, 3),
        in_specs=[pl.BlockSpec((None, l, HY_W), lambda bi, j: (bi, 0, COL_HY // HY_W + j)),
                  pl.BlockSpec((3, HY_W), lambda bi, j: (0, j)),
                  pl.BlockSpec((1, HY_W), lambda bi, j: (0, j))],
        out_specs=pl.BlockSpec((None, None, l, HY_W), lambda bi, j: (bi, j, 0, 0)),
        out_shape=jax.ShapeDtypeStruct((b, 3, l, HY_W), F32),
        compiler_params=_cp("parallel", "parallel"),
        name="short_conv",
    )(h, w, bias)


def _filter_kernel(f_ref, w1_ref, b1_ref, fr_ref, w2_ref, b2_ref, w3_ref, dec_ref, o_ref):
    def mm(a, w_ref_):
        ah, al = _split(a)
        wh, wl = _split(w_ref_[...])
        return _dot(ah, wh) + _dot(ah, wl) + _dot(al, wh)
    fr = fr_ref[...]
    h = jnp.sin(fr * (mm(f_ref[...], w1_ref) + b1_ref[...]))
    h = jnp.sin(fr * (mm(h, w2_ref) + b2_ref[...]))
    h = mm(h, w3_ref)
    half = h.shape[1] // 2
    dec = dec_ref[...]
    hf = h[:, :half] * dec
    hb = h[:, half:] * dec
    row = lax.broadcasted_iota(jnp.int32, hb.shape, 0)
    hb = jnp.where(row == 0, 0.0, hb)
    norm = jnp.sum(jnp.abs(hf) + jnp.abs(hb), axis=0, keepdims=True)
    o_ref[:, :half] = (hf + hb) / norm
    o_ref[:, half:] = (hf - hb) / norm


def _hyena_filter_taps(feats, w1, b1, freq, w2, b2, w3, dec):
    l = feats.shape[0]
    n = w3.shape[1]
    full = lambda a: pl.BlockSpec(a.shape, lambda i: (0,) * a.ndim)
    args = (feats, w1, b1, freq, w2, b2, w3, dec)
    return pl.pallas_call(
        _filter_kernel,
        grid=(1,),
        in_specs=[full(a) for a in args],
        out_specs=pl.BlockSpec((l, n), lambda i: (0, 0)),
        out_shape=jax.ShapeDtypeStruct((l, n), F32),
        compiler_params=_cp("arbitrary"),
        name="hyena_filter_taps",
    )(*args)


def _table_mm_kernel(t_ref, x_ref, o_ref):
    o_ref[...] = _dot(t_ref[...], x_ref[...].astype(BF16)).astype(o_ref.dtype)


def _table_matmul(table, x, out_dtype):
    m, k = table.shape
    n = x.shape[1]
    tm = min(m, 512)
    return pl.pallas_call(
        _table_mm_kernel,
        grid=(m // tm,),
        in_specs=[pl.BlockSpec((tm, k), lambda i: (i, 0)),
                  pl.BlockSpec((k, n), lambda i: (0, 0))],
        out_specs=pl.BlockSpec((tm, n), lambda i: (i, 0)),
        out_shape=jax.ShapeDtypeStruct((m, n), out_dtype),
        compiler_params=_cp("parallel"),
        name="table_matmul",
    )(table, x)


def _dftmul_kernel(fc_ref, fs_ref, z_ref, k_ref, p_ref):
    z = z_ref[...].astype(BF16)
    zc = _dot(fc_ref[...], z)
    zs = _dot(fs_ref[...], z)
    p_ref[0] = (zc * k_ref[0] - zs * k_ref[1]).astype(BF16)
    p_ref[1] = (zc * k_ref[2] + zs * k_ref[3]).astype(BF16)


def _dft_multiply(ffwd, z, z_spec, kf):
    l = ffwd.shape[1]
    b = z.shape[0]
    tf = min(l, 512)
    nf = l // tf
    out = pl.pallas_call(
        _dftmul_kernel,
        grid=(nf, b),
        in_specs=[pl.BlockSpec((tf, l), lambda i, bi: (i, 0)),
                  pl.BlockSpec((tf, l), lambda i, bi: (i + nf, 0)),
                  z_spec,
                  pl.BlockSpec((4, tf, HY_W), lambda i, bi: (0, i, 0))],
        out_specs=pl.BlockSpec((None, 2, tf, HY_W), lambda i, bi: (bi, 0, i, 0)),
        out_shape=jax.ShapeDtypeStruct((b, 2, l, HY_W), BF16),
        compiler_params=_cp("parallel", "arbitrary"),
        name="dft_multiply",
    )(ffwd, ffwd, z, kf)
    return out.reshape(b, 2 * l, HY_W)


def _idft_gate_kernel(fi_ref, p_ref, g_ref, z_ref, b_ref, o_ref):
    conv = _dot(fi_ref[...], p_ref[...])
    o_ref[...] = (g_ref[...] * (conv + b_ref[...] * z_ref[...])).astype(o_ref.dtype)


def _idft_gate(finv, p, gate, gate_spec, z, z_spec, bias, out_dtype):
    l = finv.shape[0]
    b = p.shape[0]
    tt = min(l, 512)
    return pl.pallas_call(
        _idft_gate_kernel,
        grid=(l // tt, b),
        in_specs=[pl.BlockSpec((tt, 2 * l), lambda i, bi: (i, 0)),
                  pl.BlockSpec((None, 2 * l, HY_W), lambda i, bi: (bi, 0, 0)),
                  gate_spec, z_spec,
                  pl.BlockSpec((1, HY_W), lambda i, bi: (0, 0))],
        out_specs=pl.BlockSpec((None, tt, HY_W), lambda i, bi: (bi, i, 0)),
        out_shape=jax.ShapeDtypeStruct((b, l, HY_W), out_dtype),
        compiler_params=_cp("parallel", "arbitrary"),
        name="idft_gate",
    )(finv, p, gate, z, bias)


def _hyena_mix(h, conv_w, conv_b, kf, hy_bias, ffwd, finv):
    b, l, _ = h.shape
    tt = min(l, 512)
    u = _short_conv(h, conv_w, conv_b)
    part = lambda j, rows: pl.BlockSpec((None, None, rows, HY_W),
                                        lambda i, bi, j=j: (bi, j, i if rows != l else 0, 0))
    p = _dft_multiply(ffwd, u, part(0, l), kf[0])
    z1 = _idft_gate(finv, p, u, part(1, tt), u, part(0, tt), hy_bias[0:1], F32)
    p = _dft_multiply(ffwd, z1, pl.BlockSpec((None, l, HY_W), lambda i, bi: (bi, 0, 0)), kf[1])
    return _idft_gate(finv, p, u, part(2, tt), z1,
                      pl.BlockSpec((None, tt, HY_W), lambda i, bi: (bi, i, 0)), hy_bias[1:2], BF16)


def _fn1_kernel(z_ref, m_ref, o_ref):
    r = _dot(z_ref[...], m_ref[...])
    half = r.shape[1] // 2
    o_ref[0] = r[:, :half].astype(BF16)
    o_ref[1] = r[:, half:].astype(BF16)


def _fourier_mix(h, m1, t2):
    b, l, _ = h.shape
    tm = min(l, 512)
    zz = pl.pallas_call(
        _fn1_kernel,
        grid=(b, l // tm),
        in_specs=[pl.BlockSpec((None, tm, FN_W), lambda bi, i: (bi, i, COL_FN // FN_W)),
                  pl.BlockSpec((FN_W, 2 * FN_W), lambda bi, i: (0, 0))],
        out_specs=pl.BlockSpec((None, 2, tm, FN_W), lambda bi, i: (bi, 0, i, 0)),
        out_shape=jax.ShapeDtypeStruct((b, 2, l, FN_W), BF16),
        compiler_params=_cp("parallel", "parallel"),
        name="fnet_channels",
    )(h, m1).reshape(b, 2 * l, FN_W)
    return pl.pallas_call(
        _table_mm_kernel,
        grid=(l // tm, b),
        in_specs=[pl.BlockSpec((tm, 2 * l), lambda i, bi: (i, 0)),
                  pl.BlockSpec((None, 2 * l, FN_W), lambda i, bi: (bi, 0, 0))],
        out_specs=pl.BlockSpec((None, tm, FN_W), lambda i, bi: (bi, i, 0)),
        out_shape=jax.ShapeDtypeStruct((b, l, FN_W), BF16),
        compiler_params=_cp("parallel", "arbitrary"),
        name="fnet_positions",
    )(t2, zz)


def _merge_kernel(hy_ref, fn_ref, at_ref, g_ref, x_ref, gm_ref, why_ref, wfn_ref, wat_ref, wout_ref, o_ref):
    d = x_ref.shape[-1]
    g = 1.0 / (1.0 + jnp.exp(-g_ref[...].astype(F32)))
    y = (g[:, :d] * _dot(hy_ref[...], why_ref[...])
         + g[:, d:2 * d] * _dot(fn_ref[...], wfn_ref[...])
         + g[:, 2 * d:] * _dot(at_ref[...], wat_ref[...]))
    mix = _dot(y.astype(BF16), wout_ref[...])
    o_ref[...] = x_ref[...] + gm_ref[...] * mix


def _merge(hyo, fno, att, h, x, mods, layer, row_fn, w_hy, w_fn, w_at, w_out):
    b, l, d = x.shape
    tm = min(l, 512)
    full = lambda a: pl.BlockSpec(a.shape, lambda bi, i: (0,) * a.ndim)
    tok = lambda wdt, blk=0: pl.BlockSpec((None, tm, wdt), lambda bi, i: (bi, i, blk))
    return pl.pallas_call(
        _merge_kernel,
        grid=(b, l // tm),
        in_specs=[tok(HY_W), tok(FN_W), tok(N_HEADS * HEAD_W), tok(3 * d, COL_GATE), tok(d),
                  pl.BlockSpec((None, None, None, 1, d), _mod_spec(layer, 2, lambda bi, i: row_fn(bi))),
                  full(w_hy), full(w_fn), full(w_at), full(w_out)],
        out_specs=tok(d),
        out_shape=jax.ShapeDtypeStruct((b, l, d), F32),
        compiler_params=_cp("parallel", "parallel"),
        name="merge_residual",
    )(hyo, fno, att, h, x, mods, w_hy, w_fn, w_at, w_out)


def _peer_q_kernel(x_ref, g_ref, sh_ref, sc_ref, wq_ref, kh_ref, kl_ref, s_ref, nt_ref):
    n = _modulated_norm(x_ref[...], g_ref[...], sh_ref[...], sc_ref[...])
    nt_ref[...] = n.T.astype(BF16)
    q = _dot(n.astype(BF16), wq_ref[...])
    dq = kh_ref.shape[1]
    for hp in range(kh_ref.shape[0]):
        qh, ql = _split(q[:, hp * dq:(hp + 1) * dq])
        kh = kh_ref[hp]
        s = _dot(qh, kh) + _dot(ql, kh) + _dot(qh, kl_ref[hp])
        s_ref[hp] = s.T


def _peer_scores(x, g, mods, layer, row_fn, wq, keys_t):
    b, l, d = x.shape
    tt = min(l, 512)
    nt = l // tt
    nhp = 2 * PEER_HEADS
    mrow = lambda bi, i: row_fn(bi)
    full = lambda a: pl.BlockSpec(a.shape, lambda bi, i: (0,) * a.ndim)
    return pl.pallas_call(
        _peer_q_kernel,
        grid=(b, nt),
        in_specs=[pl.BlockSpec((None, tt, d), lambda bi, i: (bi, i, 0)),
                  pl.BlockSpec((1, d), lambda bi, i: (0, 0)),
                  pl.BlockSpec((None, None, None, 1, d), _mod_spec(layer, 3, mrow)),
                  pl.BlockSpec((None, None, None, 1, d), _mod_spec(layer, 4, mrow)),
                  full(wq), full(keys_t[0]), full(keys_t[1])],
        out_specs=[pl.BlockSpec((nhp, PEER_NKEYS, tt), lambda bi, i: (0, 0, bi * nt + i)),
                   pl.BlockSpec((d, tt), lambda bi, i: (0, bi * nt + i))],
        out_shape=[jax.ShapeDtypeStruct((nhp, PEER_NKEYS, b * l), F32),
                   jax.ShapeDtypeStruct((d, b * l), BF16)],
        compiler_params=_cp("parallel", "parallel"),
        name="peer_scores",
    )(x, g, mods, mods, wq, keys_t[0], keys_t[1])


def _top_rows(s, k):
    vals = []
    for r in range(k):
        m = jnp.max(s, axis=0, keepdims=True)
        vals.append(m)
        if r + 1 < k:
            s = jnp.where(s == m, NEG, s)
    return vals


def _merge_sort_pairs(n):
    pairs = []
    p = 1
    while p < n:
        k = p
        while k >= 1:
            for j in range(k % p, n - k, 2 * k):
                for i in range(min(k, n - j - k)):
                    if (i + j) // (2 * p) == (i + j + k) // (2 * p):
                        pairs.append((i + j, i + j + k))
            k //= 2
        p *= 2
    return pairs


def _top_rows_grouped(s, k):
    groups = s.shape[0] // 8
    v = [s[g * 8:(g + 1) * 8, :] for g in range(groups)]
    for i, j in _merge_sort_pairs(groups):
        v[i], v[j] = jnp.maximum(v[i], v[j]), jnp.minimum(v[i], v[j])
    v.append(jnp.full_like(v[0], NEG))
    vals = []
    for r in range(k):
        m = jnp.max(v[0], axis=0, keepdims=True)
        vals.append(m)
        taken = v[0] == m
        for d in range(min(groups, k - 1 - r)):
            v[d] = jnp.where(taken, v[d + 1], v[d])
    return vals


def _peer_expert_kernel(s_ref, nt_ref, u_ref, vt_ref, x_ref, gm_ref, o_ref,
                        tau_ref, l1_ref, l2_ref, l1c_ref, hid0_ref, hid1_ref, p0_ref, p1_ref, acc_ref):
    s = pl.program_id(1)
    nc = pl.num_programs(1) - 2
    k = PEER_TOPK
    tt = nt_ref.shape[1]
    n_i = u_ref.shape[0] // PEER_NKEYS
    piece = GATE_VREGS * 8 * 128 // tt

    def hidden(hid_ref):
        hid_ref[...] = _dot(u_ref[...], nt_ref[...])

    def stages(chunk, hid_w, hid_r, p_w, p_r):
        def hidden_task(slab, cols):
            def run():
                hid_w[slab, cols] = _dot(u_ref[slab, :], nt_ref[:, cols])
            return run

        def project_task(slab, cols):
            def run():
                acc_ref[slab, cols] += _dot(vt_ref[slab, :], p_r[:, cols])
            return run

        tasks = []
        for r0 in range(0, u_ref.shape[0], MXU_SLAB):
            for c0 in range(0, tt, MXU_COLS):
                slab, cols = slice(r0, r0 + MXU_SLAB), slice(c0, min(c0 + MXU_COLS, tt))
                tasks += [hidden_task(slab, cols), project_task(slab, cols)]
        n_jp = PEER_NKEYS // piece
        slots = [[] for _ in range(n_i * n_jp)]
        for t, task in enumerate(tasks):
            slots[(t * MXU_BURSTS // len(tasks)) * (len(slots) // MXU_BURSTS)].append(task)
        first = pl.multiple_of(chunk * n_i, n_i)
        for h in range(PEER_HEADS):
            l1c_ref[h] = l1_ref[h, pl.ds(first, n_i), :]
        for ii in range(n_i):
            l1_row = [jnp.broadcast_to(l1c_ref[h, ii:ii + 1, :], (8, tt))[None] for h in range(PEER_HEADS)]
            for jp in range(n_jp):
                for task in slots[ii * n_jp + jp]:
                    task()
                js = slice(jp * piece, (jp + 1) * piece)
                gate = jnp.zeros((piece // 8, 8, tt), F32)
                for h in range(PEER_HEADS):
                    logw = l1_row[h] + l2_ref[h, js, :].reshape(piece // 8, 8, tt)
                    gate = gate + jnp.where(logw >= tau_ref[h][None], jnp.exp2(logw), 0.0)
                rows = slice(ii * PEER_NKEYS + jp * piece, ii * PEER_NKEYS + (jp + 1) * piece)
                hid = hid_r[rows, :]
                act = hid * (1.0 + lax.erf(hid * (2.0 ** -0.5)))
                p_w[rows, :] = (gate.reshape(piece, tt) * act).astype(BF16)

    @pl.when(s == 0)
    def _():
        acc_ref[...] = jnp.zeros_like(acc_ref)
        p1_ref[...] = jnp.zeros_like(p1_ref)
        hidden(hid0_ref)
        width = min(tt, 256)
        parts = tt // width

        def select(it, carry):
            h = it // parts
            cols = pl.ds(pl.multiple_of((it % parts) * width, width), width)
            s1 = s_ref[2 * h, :, cols]
            s2 = s_ref[2 * h + 1, :, cols]
            a = _top_rows_grouped(s1, k + 1)
            b = _top_rows_grouped(s2, k + 1)
            cand = [a[i] + b[j] for i in range(k + 1) for j in range((k + 1) // (i + 1))]
            pad = (-len(cand)) % 64
            top = _top_rows_grouped(jnp.concatenate(cand + [jnp.full_like(a[0], NEG)] * pad, axis=0), k + 1)
            z = jnp.zeros_like(top[0])
            for t in top[:k]:
                z = z + jnp.exp(t - top[0])
            shift = top[0] + jnp.log(z)
            log2e = 1.0 / math.log(2.0)
            l1_ref[h, :, cols] = s1 * log2e
            l2_ref[h, :, cols] = (s2 - shift) * log2e - 1.0
            tau = (0.5 * (top[k - 1] + top[k]) - shift) * log2e - 1.0
            tau_ref[h, :, cols] = jnp.broadcast_to(tau, (8, width))
            return carry

        lax.fori_loop(0, PEER_HEADS * parts, select, 0)

    @pl.when((s >= 1) & (s <= nc) & (s % 2 == 1))
    def _():
        stages(s - 1, hid1_ref, hid0_ref, p0_ref, p1_ref)

    @pl.when((s >= 1) & (s <= nc) & (s % 2 == 0))
    def _():
        stages(s - 1, hid0_ref, hid1_ref, p1_ref, p0_ref)

    @pl.when(s == nc + 1)
    def _():
        last = p0_ref if (nc - 1) % 2 == 0 else p1_ref
        out_t = acc_ref[...] + _dot(vt_ref[...], last[...])
        o_ref[...] = x_ref[...] + gm_ref[...] * out_t.T


def _peer_experts(scores, n_t, u, v_t, x, mods, layer, row_fn):
    b, l, d = x.shape
    tt = min(l, PEER_TOKENS)
    nt = l // tt
    nc, _, ec = v_t.shape
    assert ec == d
    nhp = scores.shape[0]
    return pl.pallas_call(
        _peer_expert_kernel,
        grid=(b * nt, nc + 2),
        in_specs=[pl.BlockSpec((nhp, PEER_NKEYS, tt), lambda t, s: (0, 0, t)),
                  pl.BlockSpec((d, tt), lambda t, s: (0, t)),
                  pl.BlockSpec((ec, d), lambda t, s: (jnp.minimum(s, nc - 1), 0)),
                  pl.BlockSpec((None, d, ec), lambda t, s: (jnp.clip(s - 2, 0, nc - 1), 0, 0)),
                  pl.BlockSpec((None, tt, d), lambda t, s: (t // nt, t % nt, 0)),
                  pl.BlockSpec((None, None, None, 1, d), _mod_spec(layer, 5, lambda t, s: row_fn(t // nt)))],
        out_specs=pl.BlockSpec((None, tt, d), lambda t, s: (t // nt, t % nt, 0)),
        out_shape=jax.ShapeDtypeStruct((b, l, d), F32),
        scratch_shapes=[pltpu.VMEM((PEER_HEADS, 8, tt), F32),
                        pltpu.VMEM((PEER_HEADS, PEER_NKEYS, tt), F32),
                        pltpu.VMEM((PEER_HEADS, PEER_NKEYS, tt), F32),
                        pltpu.VMEM((PEER_HEADS, d // PEER_NKEYS, tt), F32),
                        pltpu.VMEM((ec, tt), F32),
                        pltpu.VMEM((ec, tt), F32),
                        pltpu.VMEM((ec, tt), BF16),
                        pltpu.VMEM((ec, tt), BF16),
                        pltpu.VMEM((d, tt), F32)],
        compiler_params=_cp("parallel", "arbitrary"),
        name="peer_experts",
    )(scores, n_t, u, v_t, x, mods)


def _peer(x, g, mods, layer, row_fn, wq_t, keys, u, v_t):
    scores, n_t = _peer_scores(x, g, mods, layer, row_fn, wq_t, keys)
    return _peer_experts(scores, n_t, u, v_t, x, mods, layer, row_fn)


def _dft_tables(l):
    cos, sin = _cos_sin_table(l, l, 2 * l)
    t = jnp.arange(l, dtype=jnp.int32)[None, :]
    first = (jnp.arange(l) == 0)[:, None]
    sin = jnp.where(first, jnp.where(t % 2 == 0, 1.0, -1.0), sin)
    ffwd = jnp.concatenate([cos, sin], axis=0).astype(BF16)
    return ffwd, ffwd.T


def _cos_sin_table(nf, nt, n):
    step = 64
    f = jnp.arange(nf, dtype=jnp.int32)[:, None]
    angle = lambda prod: (2.0 * math.pi) * ((prod % n).astype(F32) / n)
    a = angle(f * (step * jnp.arange(nt // step, dtype=jnp.int32)[None, :]))
    b = angle(f * jnp.arange(step, dtype=jnp.int32)[None, :])
    ca, sa = jnp.cos(a)[:, :, None], jnp.sin(a)[:, :, None]
    cb, sb = jnp.cos(b)[:, None, :], jnp.sin(b)[:, None, :]
    return (ca * cb - sa * sb).reshape(nf, nt), (sa * cb + ca * sb).reshape(nf, nt)


def _fnet_tables(l):
    t2 = jnp.concatenate(_cos_sin_table(l, l, l), axis=1).astype(BF16)
    k = np.arange(FN_GROUP)
    ang64 = 2.0 * np.pi * ((k[:, None] * k[None, :]) % FN_GROUP) / FN_GROUP
    eye = np.eye(FN_W // FN_GROUP)
    scale = 1.0 / math.sqrt(FN_GROUP * l)
    m1 = np.concatenate([np.kron(eye, np.cos(ang64)), -np.kron(eye, np.sin(ang64))], axis=1) * scale
    return jnp.asarray(m1, F32).astype(BF16), t2


def _rope_tables(l):
    rows = l // GRID_W
    row = jnp.repeat(jnp.arange(rows), GRID_W).astype(F32)
    col = jnp.tile(jnp.arange(GRID_W), rows).astype(F32)
    half = QK_DIM // 2
    inv = ROPE_BASE ** (-jnp.arange(0, half, 2, dtype=F32) / half)
    ang = jnp.stack([row[:, None] * inv, col[:, None] * inv], axis=1)
    cos = jnp.repeat(jnp.cos(ang)[:, :, None, :], 2, axis=2)
    sin = jnp.sin(ang)
    sin = jnp.stack([-sin, sin], axis=2)
    rep = lambda a: jnp.tile(a.reshape(l, QK_DIM), (1, 2 * N_HEADS))
    return rep(cos), rep(sin)


def _filter_features(l):
    pos = jnp.arange(l, dtype=F32)
    t = pos / max(l - 1, 1)
    w = 2.0 * math.pi * pos / l
    f = jnp.linspace(1e-4, HY_BANDS - 1, HY_BANDS, dtype=F32)
    feats = jnp.concatenate([t[:, None], jnp.cos(w[:, None] * f), -jnp.sin(w[:, None] * f)], axis=-1)
    feats = jnp.pad(feats, ((0, 0), (0, 64 - HY_EMB)))
    deltas = jnp.abs(jnp.linspace(HY_MIN_DECAY, HY_MAX_DECAY, HY_W, dtype=F32))
    dec = jnp.exp(-t[:, None] * deltas)
    return feats, jnp.tile(dec, (1, HY_ORDER))


def _hyena_filters(l, tabs, ffwd, w1, b1, freq, w2, b2, w3):
    feats, dec = tabs
    taps = _hyena_filter_taps(feats, jnp.pad(w1, ((0, 64 - HY_EMB), (0, 0))), b1[None], freq[None],
                              w2, b2[None], w3, dec)
    kf = _table_matmul(ffwd, taps, F32)
    half = HY_ORDER * HY_W
    kc = kf[:l, :half]
    nyq = kf[l, :half]
    ks = kf[l:, half:]
    n = 2.0 * l
    first = (jnp.arange(l) == 0)[:, None]
    wc = jnp.where(first, 1.0 / n, 2.0 / n)
    ka = kc * wc
    kb = jnp.where(first, 0.0, ks * (2.0 / n))
    kd = jnp.where(first, nyq[None, :] / n, kc * (2.0 / n))
    stack = jnp.stack([ka, kb, kb, kd], axis=0)
    return jnp.moveaxis(stack.reshape(4, l, HY_ORDER, HY_W), 2, 0)


def kernel(x, c, ctx, c_ctx, w_ada, b_ada, g_mix, g_ffn, w_in, hy_conv_w, hy_conv_b, hy_w1, hy_b1, hy_freq, hy_w2, hy_b2, hy_w3, hy_bias, g_q, g_k, lam, g_sub, w_hy, w_fn, w_at, w_out, peer_wq, peer_keys, peer_u, peer_v):
    bsz, seq, d = x.shape
    clen = ctx.shape[1]
    depth = w_ada.shape[0]

    cc = jnp.concatenate([c, c_ctx[None], jnp.zeros((MOD_ROWS - bsz - 1, d), F32)], axis=0)
    mods = _ada_mods(cc, w_ada, b_ada)
    lat_row = lambda bi: bi
    ctx_row = lambda bi: bsz

    rope = _rope_tables(seq)
    tabs = {n: dict(dft=_dft_tables(n), fnet=_fnet_tables(n), feat=_filter_features(n)) for n in (seq, clen)}
    w = N_HEADS * HEAD_W
    lane = np.arange(w)
    bd = jnp.asarray((lane[:, None] // QK_DIM == lane[None, :] // QK_DIM) / QK_DIM, F32).astype(BF16)

    xl, xc = x, ctx
    for l in range(depth):
        last = l == depth - 1
        lam_init = 0.8 - 0.6 * math.exp(-0.3 * l)
        wl = w_in[l]
        w_perm = jnp.concatenate([wl[:, 2560:], wl[:, :2560]], axis=1).astype(BF16)
        gq = jnp.tile(g_q[l].reshape(1, HEAD_W), (1, N_HEADS)) * (QK_DIM ** -0.5)
        gk = jnp.tile(g_k[l].reshape(1, HEAD_W), (1, N_HEADS))
        gsub = g_sub[l][None]
        wts = [a[l].astype(BF16) for a in (w_hy, w_fn, w_at, w_out)]
        filt_args = (hy_w1[l], hy_b1[l], hy_freq[l], hy_w2[l], hy_b2[l], hy_w3[l])

        h_l = _in_projection(xl, g_mix[l][None], mods, l, lat_row, w_perm, 0, P_IN)
        h_c = _in_projection(xc, g_mix[l][None], mods, l, ctx_row, w_perm, COL_Q if last else 0, P_IN)
        q_l, k_l = _qk_prep(h_l, gq, gk, bd, rope)
        q_c, k_c = _qk_prep(h_c, gq, gk, bd, None)
        vblk = COL_V // HEAD_W
        k_all = jnp.concatenate([k_l, k_c], axis=1)
        v_all = jnp.concatenate([h_l[:, :, COL_V:], h_c[:, :, COL_V:]], axis=1)
        att_l = _diff_attention(q_l, [(k_all, v_all, 0)], lam[l], gsub, lam_init)
        ffwd, finv = tabs[seq]["dft"]
        kf = _hyena_filters(seq, tabs[seq]["feat"], ffwd, *filt_args)
        hyo_l = _hyena_mix(h_l, hy_conv_w[l], hy_conv_b[l][None], kf, hy_bias[l], ffwd, finv)
        fno_l = _fourier_mix(h_l, *tabs[seq]["fnet"])
        if not last:
            att_c = _diff_attention(q_c, [(k_c, h_c, vblk)], lam[l], gsub, lam_init)
            ffwd_c, finv_c = tabs[clen]["dft"]
            kf_c = _hyena_filters(clen, tabs[clen]["feat"], ffwd_c, *filt_args)
            hyo_c = _hyena_mix(h_c, hy_conv_w[l], hy_conv_b[l][None], kf_c, hy_bias[l], ffwd_c, finv_c)
            fno_c = _fourier_mix(h_c, *tabs[clen]["fnet"])
            xc = _merge(hyo_c, fno_c, att_c, h_c, xc, mods, l, ctx_row, *wts)
        xl = _merge(hyo_l, fno_l, att_l, h_l, xl, mods, l, lat_row, *wts)

        wq_t = peer_wq[l].astype(BF16)
        keys = _split(jnp.swapaxes(peer_keys[l].reshape(2 * PEER_HEADS, PEER_NKEYS, -1), 1, 2))
        u = peer_u[l].astype(BF16)
        v_t = jnp.swapaxes(peer_v[l].reshape(-1, d, d), 1, 2).astype(BF16)
        if not last:
            xc = _peer(xc, g_ffn[l][None], mods, l, ctx_row, wq_t, keys, u, v_t)
        xl = _peer(xl, g_ffn[l][None], mods, l, lat_row, wq_t, keys, u, v_t)
    return xl
```

```python
import functools
import math

import jax
import jax.numpy as jnp
import numpy as np
from jax import lax
from jax.experimental import pallas as pl
from jax.experimental.pallas import tpu as pltpu

F32 = jnp.float32
BF16 = jnp.bfloat16

EPS = 1e-6
GRID_W = 64
ROPE_BASE = 10000.0
N_HEADS = 4
QK_DIM = 64
HEAD_W = 2 * QK_DIM
HY_W = 256
HY_ORDER = 2
HY_EMB = 33
HY_BANDS = (HY_EMB - 1) // 2
HY_MIN_DECAY = math.log(1e-2) / 1.5
HY_MAX_DECAY = math.log(1e-2) / 0.3
FN_GROUP = 64
FN_W = 256
PEER_HEADS = 8
PEER_NKEYS = 128
PEER_TOPK = 16
N_MOD = 6
MOD_ROWS = 16
NEG = -3.0e38
MXU_SLAB = 1024
MXU_COLS = 256
MXU_BURSTS = 1
GATE_VREGS = 8
PEER_TOKENS = 256
PEER_CHUNK = 2048
DFT_ROWS = 1024

VMEM_LIMIT = 56 * 1024 * 1024

COL_GATE = 0
COL_HY = 3072
COL_FN = 3840
COL_Q = 4096
COL_K = 4608
COL_V = 5120
P_IN = 5632


def _cp(*sem):
    return pltpu.CompilerParams(dimension_semantics=sem, vmem_limit_bytes=VMEM_LIMIT)


def _dot(a, b):
    return jnp.dot(a, b, preferred_element_type=F32)


def _dot_nt(a, b):
    return lax.dot_general(a, b, (((1,), (1,)), ((), ())), preferred_element_type=F32)


def _split(a):
    hi = a.astype(BF16)
    lo = (a - hi.astype(F32)).astype(BF16)
    return hi, lo


def _modulated_norm(x, g, shift, scale):
    ms = jnp.mean(x * x, axis=-1, keepdims=True)
    y = x * lax.rsqrt(ms + EPS) * g
    return y * (1.0 + scale) + shift


def _mod_spec(layer, chunk, row_fn):
    def imap(*idx):
        return (layer, row_fn(*idx), chunk, 0, 0)
    return imap


def _ada_kernel(c_ref, w_ref, b_ref, o_ref):
    c = c_ref[...]
    a = c / (1.0 + jnp.exp(-c))
    ah, al = _split(a)
    wh, wl = _split(w_ref[...])
    o_ref[...] = _dot(ah, wh) + _dot(ah, wl) + _dot(al, wh) + b_ref[...]


def _ada_mods(cc, w_ada, b_ada):
    depth, d, n = w_ada.shape
    tn = 512
    out = pl.pallas_call(
        _ada_kernel,
        grid=(depth, n // tn),
        in_specs=[pl.BlockSpec((MOD_ROWS, d), lambda l, j: (0, 0)),
                  pl.BlockSpec((None, d, tn), lambda l, j: (l, 0, j)),
                  pl.BlockSpec((None, 1, tn), lambda l, j: (l, 0, j))],
        out_specs=pl.BlockSpec((None, MOD_ROWS, tn), lambda l, j: (l, 0, j)),
        out_shape=jax.ShapeDtypeStruct((depth, MOD_ROWS, n), F32),
        compiler_params=_cp("parallel", "parallel"),
        name="ada_mods",
    )(cc, w_ada, b_ada.reshape(depth, 1, n))
    return out.reshape(depth, MOD_ROWS, N_MOD, 1, d)


def _inproj_kernel(x_ref, g_ref, sh_ref, sc_ref, w_ref, o_ref, xn_ref):
    @pl.when(pl.program_id(2) == 0)
    def _():
        xn_ref[...] = _modulated_norm(x_ref[...], g_ref[...], sh_ref[...], sc_ref[...]).astype(BF16)
    o_ref[...] = _dot(xn_ref[...], w_ref[...]).astype(o_ref.dtype)


def _in_projection(x, g, mods, layer, row_fn, w, col_lo, col_hi):
    b, l, d = x.shape
    tm = min(l, 1024)
    wide = w.shape[1] // 2
    tn = wide if (col_lo % wide == 0 and (col_hi - col_lo) % wide == 0) else 512
    j0 = col_lo // tn
    nj = (col_hi - col_lo) // tn
    mrow = lambda bi, i, j: row_fn(bi)
    return pl.pallas_call(
        _inproj_kernel,
        grid=(b, l // tm, nj),
        in_specs=[pl.BlockSpec((None, tm, d), lambda bi, i, j: (bi, i, 0)),
                  pl.BlockSpec((1, d), lambda bi, i, j: (0, 0)),
                  pl.BlockSpec((None, None, None, 1, d), _mod_spec(layer, 0, mrow)),
                  pl.BlockSpec((None, None, None, 1, d), _mod_spec(layer, 1, mrow)),
                  pl.BlockSpec((d, tn), lambda bi, i, j: (0, j + j0))],
        out_specs=pl.BlockSpec((None, tm, tn), lambda bi, i, j: (bi, i, j + j0)),
        out_shape=jax.ShapeDtypeStruct((b, l, w.shape[1]), BF16),
        scratch_shapes=[pltpu.VMEM((tm, d), BF16)],
        compiler_params=_cp("parallel", "parallel", "arbitrary"),
        name="in_projection",
    )(x, g, mods, mods, w)


def _qkprep_kernel(*refs, rope):
    if rope:
        q_ref, k_ref, gq_ref, gk_ref, bd_ref, cos_ref, sin_ref, qo_ref, ko_ref = refs
    else:
        q_ref, k_ref, gq_ref, gk_ref, bd_ref, qo_ref, ko_ref = refs
    for src, g_ref, dst in ((q_ref, gq_ref, qo_ref), (k_ref, gk_ref, ko_ref)):
        x = src[...].astype(F32)
        hi, lo = _split(x * x)
        ms = _dot(hi, bd_ref[...]) + _dot(lo, bd_ref[...])
        y = x * lax.rsqrt(ms + EPS) * g_ref[...]
        if rope:
            w = y.shape[1]
            lane = lax.broadcasted_iota(jnp.int32, y.shape, 1)
            first = (lane % (QK_DIM // 2)) < (QK_DIM // 4)
            partner = jnp.where(first, pltpu.roll(y, w - QK_DIM // 4, 1), pltpu.roll(y, QK_DIM // 4, 1))
            y = y * cos_ref[...] + partner * sin_ref[...]
        dst[...] = y.astype(BF16)


def _qk_prep(h, gq, gk, bd, rope_tabs):
    b, l, _ = h.shape
    w = N_HEADS * HEAD_W
    tm = min(l, 512)
    rope = rope_tabs is not None
    in_specs = [pl.BlockSpec((None, tm, w), lambda bi, i: (bi, i, COL_Q // w)),
                pl.BlockSpec((None, tm, w), lambda bi, i: (bi, i, COL_K // w)),
                pl.BlockSpec((1, w), lambda bi, i: (0, 0)),
                pl.BlockSpec((1, w), lambda bi, i: (0, 0)),
                pl.BlockSpec((w, w), lambda bi, i: (0, 0))]
    args = [h, h, gq, gk, bd]
    if rope:
        in_specs += [pl.BlockSpec((tm, w), lambda bi, i: (i, 0))] * 2
        args += list(rope_tabs)
    return pl.pallas_call(
        functools.partial(_qkprep_kernel, rope=rope),
        grid=(b, l // tm),
        in_specs=in_specs,
        out_specs=[pl.BlockSpec((None, tm, w), lambda bi, i: (bi, i, 0))] * 2,
        out_shape=[jax.ShapeDtypeStruct((b, l, w), BF16)] * 2,
        compiler_params=_cp("parallel", "parallel"),
        name="qk_prep",
    )(*args)


def _attn_kernel(*refs, n_src, lam_init):
    q_ref = refs[0]
    kv = refs[1:1 + 2 * n_src]
    lam_ref, gsub_ref, o_ref = refs[1 + 2 * n_src:]
    q = q_ref[...]
    tq = q.shape[0]
    lane = lax.broadcasted_iota(jnp.int32, q.shape, 1)
    zero = jnp.zeros_like(q)
    qq = jnp.concatenate([jnp.where(lane < QK_DIM, q, zero), jnp.where(lane >= QK_DIM, q, zero)], axis=0)
    scores = [_dot_nt(qq, kv[2 * i][...]) for i in range(n_src)]
    m = jnp.max(scores[0], axis=-1, keepdims=True)
    for s in scores[1:]:
        m = jnp.maximum(m, jnp.max(s, axis=-1, keepdims=True))
    z = jnp.zeros_like(m)
    acc = jnp.zeros((2 * tq, HEAD_W), F32)
    for i, s in enumerate(scores):
        e = jnp.exp(s - m)
        z = z + jnp.sum(e, axis=-1, keepdims=True)
        acc = acc + _dot(e.astype(BF16), kv[2 * i + 1][...])
    o2 = acc / z
    lf = lam_ref[...]
    lam_val = (jnp.exp(jnp.sum(lf[0:1] * lf[1:2], axis=-1, keepdims=True))
               - jnp.exp(jnp.sum(lf[2:3] * lf[3:4], axis=-1, keepdims=True)) + lam_init)
    o = o2[:tq] - lam_val * o2[tq:]
    ms = jnp.mean(o * o, axis=-1, keepdims=True)
    o = o * lax.rsqrt(ms + EPS) * gsub_ref[...] * (1.0 - lam_init)
    o_ref[...] = o.astype(BF16)


def _diff_attention(q, sources, lam_l, gsub, lam_init):
    b, lq, w = q.shape
    tq = min(lq, 512)
    in_specs = [pl.BlockSpec((None, tq, HEAD_W), lambda bi, hi, i: (bi, i, hi))]
    args = [q]
    for k, varr, vblk in sources:
        lk = k.shape[1]
        in_specs.append(pl.BlockSpec((None, lk, HEAD_W), lambda bi, hi, i: (bi, 0, hi)))
        in_specs.append(pl.BlockSpec((None, lk, HEAD_W), lambda bi, hi, i, vblk=vblk: (bi, 0, vblk + hi)))
        args += [k, varr]
    in_specs += [pl.BlockSpec(lam_l.shape, lambda bi, hi, i: (0, 0)),
                 pl.BlockSpec((1, HEAD_W), lambda bi, hi, i: (0, 0))]
    args += [lam_l, gsub]
    return pl.pallas_call(
        functools.partial(_attn_kernel, n_src=len(sources), lam_init=lam_init),
        grid=(b, N_HEADS, lq // tq),
        in_specs=in_specs,
        out_specs=pl.BlockSpec((None, tq, HEAD_W), lambda bi, hi, i: (bi, i, hi)),
        out_shape=jax.ShapeDtypeStruct((b, lq, w), BF16),
        compiler_params=_cp("parallel", "parallel", "arbitrary"),
        name="diff_attention",
    )(*args)


def _sconv_kernel(h_ref, w_ref, b_ref, o_ref):
    x = h_ref[...].astype(F32)
    n = x.shape[0]
    row = lax.broadcasted_iota(jnp.int32, x.shape, 0)
    prev = jnp.where(row == 0, 0.0, pltpu.roll(x, 1, 0))
    nxt = jnp.where(row == n - 1, 0.0, pltpu.roll(x, n - 1, 0))
    o_ref[...] = prev * w_ref[0:1, :] + x * w_ref[1:2, :] + nxt * w_ref[2:3, :] + b_ref[...]


def _short_conv(h, w, bias):
    b, l, _ = h.shape
    return pl.pallas_call(
        _sconv_kernel,
        grid=(b, 3),
        in_specs=[pl.BlockSpec((None, l, HY_W), lambda bi, j: (bi, 0, COL_HY // HY_W + j)),
                  pl.BlockSpec((3, HY_W), lambda bi, j: (0, j)),
                  pl.BlockSpec((1, HY_W), lambda bi, j: (0, j))],
        out_specs=pl.BlockSpec((None, None, l, HY_W), lambda bi, j: (bi, j, 0, 0)),
        out_shape=jax.ShapeDtypeStruct((b, 3, l, HY_W), F32),
        compiler_params=_cp("parallel", "parallel"),
        name="short_conv",
    )(h, w, bias)


def _filter_kernel(f_ref, w1_ref, b1_ref, fr_ref, w2_ref, b2_ref, w3_ref, dec_ref, o_ref):
    def mm(a, w_ref_):
        ah, al = _split(a)
        wh, wl = _split(w_ref_[...])
        return _dot(ah, wh) + _dot(ah, wl) + _dot(al, wh)
    fr = fr_ref[...]
    h = jnp.sin(fr * (mm(f_ref[...], w1_ref) + b1_ref[...]))
    h = jnp.sin(fr * (mm(h, w2_ref) + b2_ref[...]))
    h = mm(h, w3_ref)
    half = h.shape[1] // 2
    dec = dec_ref[...]
    hf = h[:, :half] * dec
    hb = h[:, half:] * dec
    row = lax.broadcasted_iota(jnp.int32, hb.shape, 0)
    hb = jnp.where(row == 0, 0.0, hb)
    norm = jnp.sum(jnp.abs(hf) + jnp.abs(hb), axis=0, keepdims=True)
    o_ref[:, :half] = (hf + hb) / norm
    o_ref[:, half:] = (hf - hb) / norm


def _hyena_filter_taps(feats, w1, b1, freq, w2, b2, w3, dec):
    l = feats.shape[0]
    n = w3.shape[1]
    full = lambda a: pl.BlockSpec(a.shape, lambda i: (0,) * a.ndim)
    args = (feats, w1, b1, freq, w2, b2, w3, dec)
    return pl.pallas_call(
        _filter_kernel,
        grid=(1,),
        in_specs=[full(a) for a in args],
        out_specs=pl.BlockSpec((l, n), lambda i: (0, 0)),
        out_shape=jax.ShapeDtypeStruct((l, n), F32),
        compiler_params=_cp("arbitrary"),
        name="hyena_filter_taps",
    )(*args)


def _table_mm_kernel(t_ref, x_ref, o_ref):
    o_ref[...] = _dot(t_ref[...], x_ref[...].astype(BF16)).astype(o_ref.dtype)


def _table_matmul(table, x, out_dtype):
    m, k = table.shape
    n = x.shape[1]
    tm = min(m, 512)
    return pl.pallas_call(
        _table_mm_kernel,
        grid=(m // tm,),
        in_specs=[pl.BlockSpec((tm, k), lambda i: (i, 0)),
                  pl.BlockSpec((k, n), lambda i: (0, 0))],
        out_specs=pl.BlockSpec((tm, n), lambda i: (i, 0)),
        out_shape=jax.ShapeDtypeStruct((m, n), out_dtype),
        compiler_params=_cp("parallel"),
        name="table_matmul",
    )(table, x)


def _dftmul_kernel(fc_ref, fs_ref, z_ref, k_ref, p_ref):
    z = z_ref[...].astype(BF16)
    zc = _dot(fc_ref[...], z)
    zs = _dot(fs_ref[...], z)
    p_ref[0] = (zc * k_ref[0] - zs * k_ref[1]).astype(BF16)
    p_ref[1] = (zc * k_ref[2] + zs * k_ref[3]).astype(BF16)


def _dft_multiply(ffwd, z, z_spec, kf):
    l = ffwd.shape[1]
    b = z.shape[0]
    tf = min(l, DFT_ROWS)
    nf = l // tf
    out = pl.pallas_call(
        _dftmul_kernel,
        grid=(nf, b),
        in_specs=[pl.BlockSpec((tf, l), lambda i, bi: (i, 0)),
                  pl.BlockSpec((tf, l), lambda i, bi: (i + nf, 0)),
                  z_spec,
                  pl.BlockSpec((4, tf, HY_W), lambda i, bi: (0, i, 0))],
        out_specs=pl.BlockSpec((None, 2, tf, HY_W), lambda i, bi: (bi, 0, i, 0)),
        out_shape=jax.ShapeDtypeStruct((b, 2, l, HY_W), BF16),
        compiler_params=_cp("parallel", "arbitrary"),
        name="dft_multiply",
    )(ffwd, ffwd, z, kf)
    return out.reshape(b, 2 * l, HY_W)


def _idft_gate_kernel(fi_ref, p_ref, g_ref, z_ref, b_ref, o_ref):
    conv = _dot(fi_ref[...], p_ref[...])
    o_ref[...] = (g_ref[...] * (conv + b_ref[...] * z_ref[...])).astype(o_ref.dtype)


def _idft_gate(finv, p, gate, gate_spec, z, z_spec, bias, out_dtype):
    l = finv.shape[0]
    b = p.shape[0]
    tt = min(l, DFT_ROWS)
    return pl.pallas_call(
        _idft_gate_kernel,
        grid=(l // tt, b),
        in_specs=[pl.BlockSpec((tt, 2 * l), lambda i, bi: (i, 0)),
                  pl.BlockSpec((None, 2 * l, HY_W), lambda i, bi: (bi, 0, 0)),
                  gate_spec, z_spec,
                  pl.BlockSpec((1, HY_W), lambda i, bi: (0, 0))],
        out_specs=pl.BlockSpec((None, tt, HY_W), lambda i, bi: (bi, i, 0)),
        out_shape=jax.ShapeDtypeStruct((b, l, HY_W), out_dtype),
        compiler_params=_cp("parallel", "arbitrary"),
        name="idft_gate",
    )(finv, p, gate, z, bias)


def _hyena_mix(h, conv_w, conv_b, kf, hy_bias, ffwd, finv):
    b, l, _ = h.shape
    tt = min(l, DFT_ROWS)
    u = _short_conv(h, conv_w, conv_b)
    part = lambda j, rows: pl.BlockSpec((None, None, rows, HY_W),
                                        lambda i, bi, j=j: (bi, j, i if rows != l else 0, 0))
    p = _dft_multiply(ffwd, u, part(0, l), kf[0])
    z1 = _idft_gate(finv, p, u, part(1, tt), u, part(0, tt), hy_bias[0:1], F32)
    p = _dft_multiply(ffwd, z1, pl.BlockSpec((None, l, HY_W), lambda i, bi: (bi, 0, 0)), kf[1])
    return _idft_gate(finv, p, u, part(2, tt), z1,
                      pl.BlockSpec((None, tt, HY_W), lambda i, bi: (bi, i, 0)), hy_bias[1:2], BF16)


def _fn1_kernel(z_ref, m_ref, o_ref):
    r = _dot(z_ref[...], m_ref[...])
    half = r.shape[1] // 2
    o_ref[0] = r[:, :half].astype(BF16)
    o_ref[1] = r[:, half:].astype(BF16)


def _fourier_mix(h, m1, t2):
    b, l, _ = h.shape
    tm = min(l, DFT_ROWS)
    zz = pl.pallas_call(
        _fn1_kernel,
        grid=(b, l // tm),
        in_specs=[pl.BlockSpec((None, tm, FN_W), lambda bi, i: (bi, i, COL_FN // FN_W)),
                  pl.BlockSpec((FN_W, 2 * FN_W), lambda bi, i: (0, 0))],
        out_specs=pl.BlockSpec((None, 2, tm, FN_W), lambda bi, i: (bi, 0, i, 0)),
        out_shape=jax.ShapeDtypeStruct((b, 2, l, FN_W), BF16),
        compiler_params=_cp("parallel", "parallel"),
        name="fnet_channels",
    )(h, m1).reshape(b, 2 * l, FN_W)
    return pl.pallas_call(
        _table_mm_kernel,
        grid=(l // tm, b),
        in_specs=[pl.BlockSpec((tm, 2 * l), lambda i, bi: (i, 0)),
                  pl.BlockSpec((None, 2 * l, FN_W), lambda i, bi: (bi, 0, 0))],
        out_specs=pl.BlockSpec((None, tm, FN_W), lambda i, bi: (bi, i, 0)),
        out_shape=jax.ShapeDtypeStruct((b, l, FN_W), BF16),
        compiler_params=_cp("parallel", "arbitrary"),
        name="fnet_positions",
    )(t2, zz)


def _merge_kernel(hy_ref, fn_ref, at_ref, g_ref, x_ref, gm_ref, why_ref, wfn_ref, wat_ref, wout_ref, o_ref):
    d = x_ref.shape[-1]
    g = 1.0 / (1.0 + jnp.exp(-g_ref[...].astype(F32)))
    y = (g[:, :d] * _dot(hy_ref[...], why_ref[...])
         + g[:, d:2 * d] * _dot(fn_ref[...], wfn_ref[...])
         + g[:, 2 * d:] * _dot(at_ref[...], wat_ref[...]))
    mix = _dot(y.astype(BF16), wout_ref[...])
    o_ref[...] = x_ref[...] + gm_ref[...] * mix


def _merge(hyo, fno, att, h, x, mods, layer, row_fn, w_hy, w_fn, w_at, w_out):
    b, l, d = x.shape
    tm = min(l, 1024)
    full = lambda a: pl.BlockSpec(a.shape, lambda bi, i: (0,) * a.ndim)
    tok = lambda wdt, blk=0: pl.BlockSpec((None, tm, wdt), lambda bi, i: (bi, i, blk))
    return pl.pallas_call(
        _merge_kernel,
        grid=(b, l // tm),
        in_specs=[tok(HY_W), tok(FN_W), tok(N_HEADS * HEAD_W), tok(3 * d, COL_GATE), tok(d),
                  pl.BlockSpec((None, None, None, 1, d), _mod_spec(layer, 2, lambda bi, i: row_fn(bi))),
                  full(w_hy), full(w_fn), full(w_at), full(w_out)],
        out_specs=tok(d),
        out_shape=jax.ShapeDtypeStruct((b, l, d), F32),
        compiler_params=_cp("parallel", "parallel"),
        name="merge_residual",
    )(hyo, fno, att, h, x, mods, w_hy, w_fn, w_at, w_out)


def _peer_q_kernel(x_ref, g_ref, sh_ref, sc_ref, wq_ref, kh_ref, kl_ref, s_ref, nt_ref):
    n = _modulated_norm(x_ref[...], g_ref[...], sh_ref[...], sc_ref[...])
    nt_ref[...] = n.T.astype(BF16)
    q = _dot(n.astype(BF16), wq_ref[...])
    dq = kh_ref.shape[1]
    for hp in range(kh_ref.shape[0]):
        qh, ql = _split(q[:, hp * dq:(hp + 1) * dq])
        kh = kh_ref[hp]
        s = _dot(qh, kh) + _dot(ql, kh) + _dot(qh, kl_ref[hp])
        s_ref[hp] = s.T


def _peer_scores(x, g, mods, layer, row_fn, wq, keys_t):
    b, l, d = x.shape
    tt = min(l, 512)
    nt = l // tt
    nhp = 2 * PEER_HEADS
    mrow = lambda bi, i: row_fn(bi)
    full = lambda a: pl.BlockSpec(a.shape, lambda bi, i: (0,) * a.ndim)
    return pl.pallas_call(
        _peer_q_kernel,
        grid=(b, nt),
        in_specs=[pl.BlockSpec((None, tt, d), lambda bi, i: (bi, i, 0)),
                  pl.BlockSpec((1, d), lambda bi, i: (0, 0)),
                  pl.BlockSpec((None, None, None, 1, d), _mod_spec(layer, 3, mrow)),
                  pl.BlockSpec((None, None, None, 1, d), _mod_spec(layer, 4, mrow)),
                  full(wq), full(keys_t[0]), full(keys_t[1])],
        out_specs=[pl.BlockSpec((nhp, PEER_NKEYS, tt), lambda bi, i: (0, 0, bi * nt + i)),
                   pl.BlockSpec((d, tt), lambda bi, i: (0, bi * nt + i))],
        out_shape=[jax.ShapeDtypeStruct((nhp, PEER_NKEYS, b * l), F32),
                   jax.ShapeDtypeStruct((d, b * l), BF16)],
        compiler_params=_cp("parallel", "parallel"),
        name="peer_scores",
    )(x, g, mods, mods, wq, keys_t[0], keys_t[1])


def _top_rows(s, k):
    vals = []
    for r in range(k):
        m = jnp.max(s, axis=0, keepdims=True)
        vals.append(m)
        if r + 1 < k:
            s = jnp.where(s == m, NEG, s)
    return vals


def _merge_sort_pairs(n):
    pairs = []
    p = 1
    while p < n:
        k = p
        while k >= 1:
            for j in range(k % p, n - k, 2 * k):
                for i in range(min(k, n - j - k)):
                    if (i + j) // (2 * p) == (i + j + k) // (2 * p):
                        pairs.append((i + j, i + j + k))
            k //= 2
        p *= 2
    return pairs


def _top_rows_grouped(s, k):
    groups = s.shape[0] // 8
    v = [s[g * 8:(g + 1) * 8, :] for g in range(groups)]
    for i, j in _merge_sort_pairs(groups):
        v[i], v[j] = jnp.maximum(v[i], v[j]), jnp.minimum(v[i], v[j])
    v.append(jnp.full_like(v[0], NEG))
    vals = []
    for r in range(k):
        m = jnp.max(v[0], axis=0, keepdims=True)
        vals.append(m)
        taken = v[0] == m
        for d in range(min(groups, k - 1 - r)):
            v[d] = jnp.where(taken, v[d + 1], v[d])
    return vals


def _peer_expert_kernel(s_ref, nt_ref, u_ref, vt_ref, x_ref, gm_ref, o_ref,
                        tau_ref, l1_ref, l2_ref, l1c_ref, hid0_ref, hid1_ref, p0_ref, p1_ref, acc_ref):
    s = pl.program_id(1)
    nc = pl.num_programs(1) - 2
    k = PEER_TOPK
    tt = nt_ref.shape[1]
    n_i = u_ref.shape[0] // PEER_NKEYS
    piece = GATE_VREGS * 8 * 128 // tt

    def hidden(hid_ref):
        hid_ref[...] = _dot(u_ref[...], nt_ref[...])

    def stages(chunk, hid_w, hid_r, p_w, p_r):
        def hidden_task(slab, cols):
            def run():
                hid_w[slab, cols] = _dot(u_ref[slab, :], nt_ref[:, cols])
            return run

        def project_task(slab, cols):
            def run():
                acc_ref[slab, cols] += _dot(vt_ref[slab, :], p_r[:, cols])
            return run

        tasks = []
        for c0 in range(0, tt, MXU_COLS):
            cols = slice(c0, min(c0 + MXU_COLS, tt))
            tasks += [hidden_task(slice(r0, r0 + MXU_SLAB), cols) for r0 in range(0, u_ref.shape[0], MXU_SLAB)]
            tasks += [project_task(slice(r0, r0 + MXU_SLAB), cols) for r0 in range(0, vt_ref.shape[0], MXU_SLAB)]
        n_jp = PEER_NKEYS // piece
        slots = [[] for _ in range(n_i * n_jp)]
        for t, task in enumerate(tasks):
            slots[(t * MXU_BURSTS // len(tasks)) * (len(slots) // MXU_BURSTS)].append(task)
        first = pl.multiple_of(chunk * n_i, n_i)
        for h in range(PEER_HEADS):
            l1c_ref[h] = l1_ref[h, pl.ds(first, n_i), :]
        for ii in range(n_i):
            l1_row = [jnp.broadcast_to(l1c_ref[h, ii:ii + 1, :], (8, tt))[None] for h in range(PEER_HEADS)]
            for jp in range(n_jp):
                for task in slots[ii * n_jp + jp]:
                    task()
                js = slice(jp * piece, (jp + 1) * piece)
                gate = jnp.zeros((piece // 8, 8, tt), F32)
                for h in range(PEER_HEADS):
                    logw = l1_row[h] + l2_ref[h, js, :].reshape(piece // 8, 8, tt)
                    gate = gate + jnp.where(logw >= tau_ref[h][None], jnp.exp2(logw), 0.0)
                rows = slice(ii * PEER_NKEYS + jp * piece, ii * PEER_NKEYS + (jp + 1) * piece)
                hid = hid_r[rows, :]
                act = hid * (1.0 + lax.erf(hid * (2.0 ** -0.5)))
                p_w[rows, :] = (gate.reshape(piece, tt) * act).astype(BF16)

    @pl.when(s == 0)
    def _():
        acc_ref[...] = jnp.zeros_like(acc_ref)
        p1_ref[...] = jnp.zeros_like(p1_ref)
        hidden(hid0_ref)
        width = min(tt, 256)
        parts = tt // width

        def select(it, carry):
            h = it // parts
            cols = pl.ds(pl.multiple_of((it % parts) * width, width), width)
            s1 = s_ref[2 * h, :, cols]
            s2 = s_ref[2 * h + 1, :, cols]
            a = _top_rows_grouped(s1, k + 1)
            b = _top_rows_grouped(s2, k + 1)
            cand = [a[i] + b[j] for i in range(k + 1) for j in range((k + 1) // (i + 1))]
            pad = (-len(cand)) % 64
            top = _top_rows_grouped(jnp.concatenate(cand + [jnp.full_like(a[0], NEG)] * pad, axis=0), k + 1)
            z = jnp.zeros_like(top[0])
            for t in top[:k]:
                z = z + jnp.exp(t - top[0])
            shift = top[0] + jnp.log(z)
            log2e = 1.0 / math.log(2.0)
            l1_ref[h, :, cols] = s1 * log2e
            l2_ref[h, :, cols] = (s2 - shift) * log2e - 1.0
            tau = (0.5 * (top[k - 1] + top[k]) - shift) * log2e - 1.0
            tau_ref[h, :, cols] = jnp.broadcast_to(tau, (8, width))
            return carry

        lax.fori_loop(0, PEER_HEADS * parts, select, 0)

    @pl.when((s >= 1) & (s <= nc) & (s % 2 == 1))
    def _():
        stages(s - 1, hid1_ref, hid0_ref, p0_ref, p1_ref)

    @pl.when((s >= 1) & (s <= nc) & (s % 2 == 0))
    def _():
        stages(s - 1, hid0_ref, hid1_ref, p1_ref, p0_ref)

    @pl.when(s == nc + 1)
    def _():
        last = p0_ref if (nc - 1) % 2 == 0 else p1_ref
        out_t = acc_ref[...] + _dot(vt_ref[...], last[...])
        o_ref[...] = x_ref[...] + gm_ref[...] * out_t.T


def _peer_experts(scores, n_t, u, v_t, x, mods, layer, row_fn):
    b, l, d = x.shape
    tt = min(l, PEER_TOKENS)
    nt = l // tt
    nc, _, ec = v_t.shape
    nhp = scores.shape[0]
    return pl.pallas_call(
        _peer_expert_kernel,
        grid=(b * nt, nc + 2),
        in_specs=[pl.BlockSpec((nhp, PEER_NKEYS, tt), lambda t, s: (0, 0, t)),
                  pl.BlockSpec((d, tt), lambda t, s: (0, t)),
                  pl.BlockSpec((ec, d), lambda t, s: (jnp.minimum(s, nc - 1), 0)),
                  pl.BlockSpec((None, d, ec), lambda t, s: (jnp.clip(s - 2, 0, nc - 1), 0, 0)),
                  pl.BlockSpec((None, tt, d), lambda t, s: (t // nt, t % nt, 0)),
                  pl.BlockSpec((None, None, None, 1, d), _mod_spec(layer, 5, lambda t, s: row_fn(t // nt)))],
        out_specs=pl.BlockSpec((None, tt, d), lambda t, s: (t // nt, t % nt, 0)),
        out_shape=jax.ShapeDtypeStruct((b, l, d), F32),
        scratch_shapes=[pltpu.VMEM((PEER_HEADS, 8, tt), F32),
                        pltpu.VMEM((PEER_HEADS, PEER_NKEYS, tt), F32),
                        pltpu.VMEM((PEER_HEADS, PEER_NKEYS, tt), F32),
                        pltpu.VMEM((PEER_HEADS, ec // PEER_NKEYS, tt), F32),
                        pltpu.VMEM((ec, tt), F32),
                        pltpu.VMEM((ec, tt), F32),
                        pltpu.VMEM((ec, tt), BF16),
                        pltpu.VMEM((ec, tt), BF16),
                        pltpu.VMEM((d, tt), F32)],
        compiler_params=_cp("parallel", "arbitrary"),
        name="peer_experts",
    )(scores, n_t, u, v_t, x, mods)


def _peer(x, g, mods, layer, row_fn, wq_t, keys, u, v_t):
    scores, n_t = _peer_scores(x, g, mods, layer, row_fn, wq_t, keys)
    return _peer_experts(scores, n_t, u, v_t, x, mods, layer, row_fn)


def _dft_tables(l):
    cos, sin = _cos_sin_table(l, l, 2 * l)
    t = jnp.arange(l, dtype=jnp.int32)[None, :]
    first = (jnp.arange(l) == 0)[:, None]
    sin = jnp.where(first, jnp.where(t % 2 == 0, 1.0, -1.0), sin)
    ffwd = jnp.concatenate([cos, sin], axis=0).astype(BF16)
    return ffwd, ffwd.T


def _cos_sin_table(nf, nt, n):
    step = 64
    f = jnp.arange(nf, dtype=jnp.int32)[:, None]
    angle = lambda prod: (2.0 * math.pi) * ((prod % n).astype(F32) / n)
    a = angle(f * (step * jnp.arange(nt // step, dtype=jnp.int32)[None, :]))
    b = angle(f * jnp.arange(step, dtype=jnp.int32)[None, :])
    ca, sa = jnp.cos(a)[:, :, None], jnp.sin(a)[:, :, None]
    cb, sb = jnp.cos(b)[:, None, :], jnp.sin(b)[:, None, :]
    return (ca * cb - sa * sb).reshape(nf, nt), (sa * cb + ca * sb).reshape(nf, nt)


def _fnet_tables(l):
    t2 = jnp.concatenate(_cos_sin_table(l, l, l), axis=1).astype(BF16)
    k = np.arange(FN_GROUP)
    ang64 = 2.0 * np.pi * ((k[:, None] * k[None, :]) % FN_GROUP) / FN_GROUP
    eye = np.eye(FN_W // FN_GROUP)
    scale = 1.0 / math.sqrt(FN_GROUP * l)
    m1 = np.concatenate([np.kron(eye, np.cos(ang64)), -np.kron(eye, np.sin(ang64))], axis=1) * scale
    return jnp.asarray(m1, F32).astype(BF16), t2


def _rope_tables(l):
    rows = l // GRID_W
    row = jnp.repeat(jnp.arange(rows), GRID_W).astype(F32)
    col = jnp.tile(jnp.arange(GRID_W), rows).astype(F32)
    half = QK_DIM // 2
    inv = ROPE_BASE ** (-jnp.arange(0, half, 2, dtype=F32) / half)
    ang = jnp.stack([row[:, None] * inv, col[:, None] * inv], axis=1)
    cos = jnp.repeat(jnp.cos(ang)[:, :, None, :], 2, axis=2)
    sin = jnp.sin(ang)
    sin = jnp.stack([-sin, sin], axis=2)
    rep = lambda a: jnp.tile(a.reshape(l, QK_DIM), (1, 2 * N_HEADS))
    return rep(cos), rep(sin)


def _filter_features(l):
    pos = jnp.arange(l, dtype=F32)
    t = pos / max(l - 1, 1)
    w = 2.0 * math.pi * pos / l
    f = jnp.linspace(1e-4, HY_BANDS - 1, HY_BANDS, dtype=F32)
    feats = jnp.concatenate([t[:, None], jnp.cos(w[:, None] * f), -jnp.sin(w[:, None] * f)], axis=-1)
    feats = jnp.pad(feats, ((0, 0), (0, 64 - HY_EMB)))
    deltas = jnp.abs(jnp.linspace(HY_MIN_DECAY, HY_MAX_DECAY, HY_W, dtype=F32))
    dec = jnp.exp(-t[:, None] * deltas)
    return feats, jnp.tile(dec, (1, HY_ORDER))


def _hyena_filters(l, tabs, ffwd, w1, b1, freq, w2, b2, w3):
    feats, dec = tabs
    taps = _hyena_filter_taps(feats, jnp.pad(w1, ((0, 64 - HY_EMB), (0, 0))), b1[None], freq[None],
                              w2, b2[None], w3, dec)
    kf = _table_matmul(ffwd, taps, F32)
    half = HY_ORDER * HY_W
    kc = kf[:l, :half]
    nyq = kf[l, :half]
    ks = kf[l:, half:]
    n = 2.0 * l
    first = (jnp.arange(l) == 0)[:, None]
    wc = jnp.where(first, 1.0 / n, 2.0 / n)
    ka = kc * wc
    kb = jnp.where(first, 0.0, ks * (2.0 / n))
    kd = jnp.where(first, nyq[None, :] / n, kc * (2.0 / n))
    stack = jnp.stack([ka, kb, kb, kd], axis=0)
    return jnp.moveaxis(stack.reshape(4, l, HY_ORDER, HY_W), 2, 0)


def kernel(x, c, ctx, c_ctx, w_ada, b_ada, g_mix, g_ffn, w_in, hy_conv_w, hy_conv_b, hy_w1, hy_b1, hy_freq, hy_w2, hy_b2, hy_w3, hy_bias, g_q, g_k, lam, g_sub, w_hy, w_fn, w_at, w_out, peer_wq, peer_keys, peer_u, peer_v):
    bsz, seq, d = x.shape
    clen = ctx.shape[1]
    depth = w_ada.shape[0]

    cc = jnp.concatenate([c, c_ctx[None], jnp.zeros((MOD_ROWS - bsz - 1, d), F32)], axis=0)
    mods = _ada_mods(cc, w_ada, b_ada)
    lat_row = lambda bi: bi
    ctx_row = lambda bi: bsz

    rope = _rope_tables(seq)
    tabs = {n: dict(dft=_dft_tables(n), fnet=_fnet_tables(n), feat=_filter_features(n)) for n in (seq, clen)}
    w = N_HEADS * HEAD_W
    lane = np.arange(w)
    bd = jnp.asarray((lane[:, None] // QK_DIM == lane[None, :] // QK_DIM) / QK_DIM, F32).astype(BF16)

    xl, xc = x, ctx
    for l in range(depth):
        last = l == depth - 1
        lam_init = 0.8 - 0.6 * math.exp(-0.3 * l)
        wl = w_in[l]
        w_perm = jnp.concatenate([wl[:, 2560:], wl[:, :2560]], axis=1).astype(BF16)
        gq = jnp.tile(g_q[l].reshape(1, HEAD_W), (1, N_HEADS)) * (QK_DIM ** -0.5)
        gk = jnp.tile(g_k[l].reshape(1, HEAD_W), (1, N_HEADS))
        gsub = g_sub[l][None]
        wts = [a[l].astype(BF16) for a in (w_hy, w_fn, w_at, w_out)]
        filt_args = (hy_w1[l], hy_b1[l], hy_freq[l], hy_w2[l], hy_b2[l], hy_w3[l])

        h_l = _in_projection(xl, g_mix[l][None], mods, l, lat_row, w_perm, 0, P_IN)
        h_c = _in_projection(xc, g_mix[l][None], mods, l, ctx_row, w_perm, COL_Q if last else 0, P_IN)
        q_l, k_l = _qk_prep(h_l, gq, gk, bd, rope)
        q_c, k_c = _qk_prep(h_c, gq, gk, bd, None)
        vblk = COL_V // HEAD_W
        k_all = jnp.concatenate([k_l, k_c], axis=1)
        v_all = jnp.concatenate([h_l[:, :, COL_V:], h_c[:, :, COL_V:]], axis=1)
        att_l = _diff_attention(q_l, [(k_all, v_all, 0)], lam[l], gsub, lam_init)
        ffwd, finv = tabs[seq]["dft"]
        kf = _hyena_filters(seq, tabs[seq]["feat"], ffwd, *filt_args)
        hyo_l = _hyena_mix(h_l, hy_conv_w[l], hy_conv_b[l][None], kf, hy_bias[l], ffwd, finv)
        fno_l = _fourier_mix(h_l, *tabs[seq]["fnet"])
        if not last:
            att_c = _diff_attention(q_c, [(k_c, h_c, vblk)], lam[l], gsub, lam_init)
            ffwd_c, finv_c = tabs[clen]["dft"]
            kf_c = _hyena_filters(clen, tabs[clen]["feat"], ffwd_c, *filt_args)
            hyo_c = _hyena_mix(h_c, hy_conv_w[l], hy_conv_b[l][None], kf_c, hy_bias[l], ffwd_c, finv_c)
            fno_c = _fourier_mix(h_c, *tabs[clen]["fnet"])
            xc = _merge(hyo_c, fno_c, att_c, h_c, xc, mods, l, ctx_row, *wts)
        xl = _merge(hyo_l, fno_l, att_l, h_l, xl, mods, l, lat_row, *wts)

        wq_t = peer_wq[l].astype(BF16)
        keys = _split(jnp.swapaxes(peer_keys[l].reshape(2 * PEER_HEADS, PEER_NKEYS, -1), 1, 2))
        u = peer_u[l].astype(BF16)
        v_t = jnp.swapaxes(peer_v[l].reshape(-1, PEER_CHUNK, d), 1, 2).astype(BF16)
        if not last:
            xc = _peer(xc, g_ffn[l][None], mods, l, ctx_row, wq_t, keys, u, v_t)
        xl = _peer(xl, g_ffn[l][None], mods, l, lat_row, wq_t, keys, u, v_t)
    return xl
```

```python
import functools
import math

import jax
import jax.numpy as jnp
import numpy as np
from jax import lax
from jax.experimental import pallas as pl
from jax.experimental.pallas import tpu as pltpu

F32 = jnp.float32
BF16 = jnp.bfloat16

EPS = 1e-6
GRID_W = 64
ROPE_BASE = 10000.0
N_HEADS = 4
QK_DIM = 64
HEAD_W = 2 * QK_DIM
HY_W = 256
HY_ORDER = 2
HY_EMB = 33
HY_BANDS = (HY_EMB - 1) // 2
HY_MIN_DECAY = math.log(1e-2) / 1.5
HY_MAX_DECAY = math.log(1e-2) / 0.3
FN_GROUP = 64
FN_W = 256
PEER_HEADS = 8
PEER_NKEYS = 128
PEER_TOPK = 16
N_MOD = 6
MOD_ROWS = 16
NEG = -3.0e38
MXU_SLAB = 1024
MXU_COLS = 256
MXU_BURSTS = 1
GATE_VREGS = 8
PEER_TOKENS = 256
PEER_CHUNK = 2048
DFT_ROWS = 1024

VMEM_LIMIT = 56 * 1024 * 1024

COL_GATE = 0
COL_HY = 3072
COL_FN = 3840
COL_Q = 4096
COL_K = 4608
COL_V = 5120
P_IN = 5632


def _cp(*sem):
    return pltpu.CompilerParams(dimension_semantics=sem, vmem_limit_bytes=VMEM_LIMIT)


def _dot(a, b):
    return jnp.dot(a, b, preferred_element_type=F32)


def _dot_nt(a, b):
    return lax.dot_general(a, b, (((1,), (1,)), ((), ())), preferred_element_type=F32)


def _split(a):
    hi = a.astype(BF16)
    lo = (a - hi.astype(F32)).astype(BF16)
    return hi, lo


def _modulated_norm(x, g, shift, scale):
    ms = jnp.mean(x * x, axis=-1, keepdims=True)
    y = x * lax.rsqrt(ms + EPS) * g
    return y * (1.0 + scale) + shift


def _mod_spec(layer, chunk, row_fn):
    def imap(*idx):
        return (layer, row_fn(*idx), chunk, 0, 0)
    return imap


def _ada_kernel(c_ref, w_ref, b_ref, o_ref):
    c = c_ref[...]
    a = c / (1.0 + jnp.exp(-c))
    ah, al = _split(a)
    wh, wl = _split(w_ref[...])
    o_ref[...] = _dot(ah, wh) + _dot(ah, wl) + _dot(al, wh) + b_ref[...]


def _ada_mods(cc, w_ada, b_ada):
    depth, d, n = w_ada.shape
    tn = 512
    out = pl.pallas_call(
        _ada_kernel,
        grid=(depth, n // tn),
        in_specs=[pl.BlockSpec((MOD_ROWS, d), lambda l, j: (0, 0)),
                  pl.BlockSpec((None, d, tn), lambda l, j: (l, 0, j)),
                  pl.BlockSpec((None, 1, tn), lambda l, j: (l, 0, j))],
        out_specs=pl.BlockSpec((None, MOD_ROWS, tn), lambda l, j: (l, 0, j)),
        out_shape=jax.ShapeDtypeStruct((depth, MOD_ROWS, n), F32),
        compiler_params=_cp("parallel", "parallel"),
        name="ada_mods",
    )(cc, w_ada, b_ada.reshape(depth, 1, n))
    return out.reshape(depth, MOD_ROWS, N_MOD, 1, d)


def _inproj_kernel(x_ref, g_ref, sh_ref, sc_ref, w_ref, o_ref, xn_ref):
    @pl.when(pl.program_id(2) == 0)
    def _():
        xn_ref[...] = _modulated_norm(x_ref[...], g_ref[...], sh_ref[...], sc_ref[...]).astype(BF16)
    o_ref[...] = _dot(xn_ref[...], w_ref[...]).astype(o_ref.dtype)


def _in_projection(x, g, mods, layer, row_fn, w, col_lo, col_hi):
    b, l, d = x.shape
    tm = min(l, 1024)
    wide = w.shape[1] // 2
    tn = wide if (col_lo % wide == 0 and (col_hi - col_lo) % wide == 0) else 512
    j0 = col_lo // tn
    nj = (col_hi - col_lo) // tn
    mrow = lambda bi, i, j: row_fn(bi)
    return pl.pallas_call(
        _inproj_kernel,
        grid=(b, l // tm, nj),
        in_specs=[pl.BlockSpec((None, tm, d), lambda bi, i, j: (bi, i, 0)),
                  pl.BlockSpec((1, d), lambda bi, i, j: (0, 0)),
                  pl.BlockSpec((None, None, None, 1, d), _mod_spec(layer, 0, mrow)),
                  pl.BlockSpec((None, None, None, 1, d), _mod_spec(layer, 1, mrow)),
                  pl.BlockSpec((d, tn), lambda bi, i, j: (0, j + j0))],
        out_specs=pl.BlockSpec((None, tm, tn), lambda bi, i, j: (bi, i, j + j0)),
        out_shape=jax.ShapeDtypeStruct((b, l, w.shape[1]), BF16),
        scratch_shapes=[pltpu.VMEM((tm, d), BF16)],
        compiler_params=_cp("parallel", "parallel", "arbitrary"),
        name="in_projection",
    )(x, g, mods, mods, w)


def _qkprep_kernel(*refs, rope):
    if rope:
        q_ref, k_ref, gq_ref, gk_ref, bd_ref, cos_ref, sin_ref, qo_ref, ko_ref = refs
    else:
        q_ref, k_ref, gq_ref, gk_ref, bd_ref, qo_ref, ko_ref = refs
    for src, g_ref, dst in ((q_ref, gq_ref, qo_ref), (k_ref, gk_ref, ko_ref)):
        x = src[...].astype(F32)
        hi, lo = _split(x * x)
        ms = _dot(hi, bd_ref[...]) + _dot(lo, bd_ref[...])
        y = x * lax.rsqrt(ms + EPS) * g_ref[...]
        if rope:
            w = y.shape[1]
            lane = lax.broadcasted_iota(jnp.int32, y.shape, 1)
            first = (lane % (QK_DIM // 2)) < (QK_DIM // 4)
            partner = jnp.where(first, pltpu.roll(y, w - QK_DIM // 4, 1), pltpu.roll(y, QK_DIM // 4, 1))
            y = y * cos_ref[...] + partner * sin_ref[...]
        dst[...] = y.astype(BF16)


def _qk_prep(h, gq, gk, bd, rope_tabs):
    b, l, _ = h.shape
    w = N_HEADS * HEAD_W
    tm = min(l, 1024)
    rope = rope_tabs is not None
    in_specs = [pl.BlockSpec((None, tm, w), lambda bi, i: (bi, i, COL_Q // w)),
                pl.BlockSpec((None, tm, w), lambda bi, i: (bi, i, COL_K // w)),
                pl.BlockSpec((1, w), lambda bi, i: (0, 0)),
                pl.BlockSpec((1, w), lambda bi, i: (0, 0)),
                pl.BlockSpec((w, w), lambda bi, i: (0, 0))]
    args = [h, h, gq, gk, bd]
    if rope:
        in_specs += [pl.BlockSpec((tm, w), lambda bi, i: (i, 0))] * 2
        args += list(rope_tabs)
    return pl.pallas_call(
        functools.partial(_qkprep_kernel, rope=rope),
        grid=(b, l // tm),
        in_specs=in_specs,
        out_specs=[pl.BlockSpec((None, tm, w), lambda bi, i: (bi, i, 0))] * 2,
        out_shape=[jax.ShapeDtypeStruct((b, l, w), BF16)] * 2,
        compiler_params=_cp("parallel", "parallel"),
        name="qk_prep",
    )(*args)


def _attn_kernel(*refs, n_src, lam_init):
    q_ref = refs[0]
    kv = refs[1:1 + 2 * n_src]
    lam_ref, gsub_ref, o_ref = refs[1 + 2 * n_src:]
    q = q_ref[...]
    tq = q.shape[0]
    lane = lax.broadcasted_iota(jnp.int32, q.shape, 1)
    zero = jnp.zeros_like(q)
    qq = jnp.concatenate([jnp.where(lane < QK_DIM, q, zero), jnp.where(lane >= QK_DIM, q, zero)], axis=0)
    scores = [_dot_nt(qq, kv[2 * i][...]) for i in range(n_src)]
    m = jnp.max(scores[0], axis=-1, keepdims=True)
    for s in scores[1:]:
        m = jnp.maximum(m, jnp.max(s, axis=-1, keepdims=True))
    z = jnp.zeros_like(m)
    acc = jnp.zeros((2 * tq, HEAD_W), F32)
    for i, s in enumerate(scores):
        e = jnp.exp(s - m)
        z = z + jnp.sum(e, axis=-1, keepdims=True)
        acc = acc + _dot(e.astype(BF16), kv[2 * i + 1][...])
    o2 = acc / z
    lf = lam_ref[...]
    lam_val = (jnp.exp(jnp.sum(lf[0:1] * lf[1:2], axis=-1, keepdims=True))
               - jnp.exp(jnp.sum(lf[2:3] * lf[3:4], axis=-1, keepdims=True)) + lam_init)
    o = o2[:tq] - lam_val * o2[tq:]
    ms = jnp.mean(o * o, axis=-1, keepdims=True)
    o = o * lax.rsqrt(ms + EPS) * gsub_ref[...] * (1.0 - lam_init)
    o_ref[...] = o.astype(BF16)


def _concat_kernel(kl_ref, kc_ref, vl_ref, vc_ref, ko_ref, vo_ref):
    s = kl_ref.shape[0]
    ko_ref[:s, :] = kl_ref[...]
    ko_ref[s:, :] = kc_ref[...]
    vo_ref[:s, :] = vl_ref[...]
    vo_ref[s:, :] = vc_ref[...]


def _concat_kv(k_l, k_c, h_l, h_c):
    b, s, w = k_l.shape
    c = k_c.shape[1]
    vcol = COL_V // w
    rows = lambda n, blk: pl.BlockSpec((None, n, w), lambda bi: (bi, 0, blk))
    return pl.pallas_call(
        _concat_kernel,
        grid=(b,),
        in_specs=[rows(s, 0), rows(c, 0), rows(s, vcol), rows(c, vcol)],
        out_specs=[rows(s + c, 0)] * 2,
        out_shape=[jax.ShapeDtypeStruct((b, s + c, w), BF16)] * 2,
        compiler_params=_cp("parallel"),
        name="concat_kv",
    )(k_l, k_c, h_l, h_c)


def _diff_attention(q, sources, lam_l, gsub, lam_init):
    b, lq, w = q.shape
    tq = min(lq, 512)
    in_specs = [pl.BlockSpec((None, tq, HEAD_W), lambda bi, hi, i: (bi, i, hi))]
    args = [q]
    for k, varr, vblk in sources:
        lk = k.shape[1]
        in_specs.append(pl.BlockSpec((None, lk, HEAD_W), lambda bi, hi, i: (bi, 0, hi)))
        in_specs.append(pl.BlockSpec((None, lk, HEAD_W), lambda bi, hi, i, vblk=vblk: (bi, 0, vblk + hi)))
        args += [k, varr]
    in_specs += [pl.BlockSpec(lam_l.shape, lambda bi, hi, i: (0, 0)),
                 pl.BlockSpec((1, HEAD_W), lambda bi, hi, i: (0, 0))]
    args += [lam_l, gsub]
    return pl.pallas_call(
        functools.partial(_attn_kernel, n_src=len(sources), lam_init=lam_init),
        grid=(b, N_HEADS, lq // tq),
        in_specs=in_specs,
        out_specs=pl.BlockSpec((None, tq, HEAD_W), lambda bi, hi, i: (bi, i, hi)),
        out_shape=jax.ShapeDtypeStruct((b, lq, w), BF16),
        compiler_params=_cp("parallel", "parallel", "arbitrary"),
        name="diff_attention",
    )(*args)


def _sconv_kernel(h_ref, w_ref, b_ref, o_ref):
    x = h_ref[...].astype(F32)
    n = x.shape[0]
    row = lax.broadcasted_iota(jnp.int32, x.shape, 0)
    prev = jnp.where(row == 0, 0.0, pltpu.roll(x, 1, 0))
    nxt = jnp.where(row == n - 1, 0.0, pltpu.roll(x, n - 1, 0))
    o_ref[...] = prev * w_ref[0:1, :] + x * w_ref[1:2, :] + nxt * w_ref[2:3, :] + b_ref[...]


def _short_conv(h, w, bias):
    b, l, _ = h.shape
    return pl.pallas_call(
        _sconv_kernel,
        grid=(b, 3),
        in_specs=[pl.BlockSpec((None, l, HY_W), lambda bi, j: (bi, 0, COL_HY // HY_W + j)),
                  pl.BlockSpec((3, HY_W), lambda bi, j: (0, j)),
                  pl.BlockSpec((1, HY_W), lambda bi, j: (0, j))],
        out_specs=pl.BlockSpec((None, None, l, HY_W), lambda bi, j: (bi, j, 0, 0)),
        out_shape=jax.ShapeDtypeStruct((b, 3, l, HY_W), F32),
        compiler_params=_cp("parallel", "parallel"),
        name="short_conv",
    )(h, w, bias)


def _filter_kernel(f_ref, w1_ref, b1_ref, fr_ref, w2_ref, b2_ref, w3_ref, dec_ref, o_ref):
    def mm(a, w_ref_):
        ah, al = _split(a)
        wh, wl = _split(w_ref_[...])
        return _dot(ah, wh) + _dot(ah, wl) + _dot(al, wh)
    fr = fr_ref[...]
    h = jnp.sin(fr * (mm(f_ref[...], w1_ref) + b1_ref[...]))
    h = jnp.sin(fr * (mm(h, w2_ref) + b2_ref[...]))
    h = mm(h, w3_ref)
    half = h.shape[1] // 2
    dec = dec_ref[...]
    hf = h[:, :half] * dec
    hb = h[:, half:] * dec
    row = lax.broadcasted_iota(jnp.int32, hb.shape, 0)
    hb = jnp.where(row == 0, 0.0, hb)
    norm = jnp.sum(jnp.abs(hf) + jnp.abs(hb), axis=0, keepdims=True)
    o_ref[:, :half] = (hf + hb) / norm
    o_ref[:, half:] = (hf - hb) / norm


def _hyena_filter_taps(feats, w1, b1, freq, w2, b2, w3, dec):
    l = feats.shape[0]
    n = w3.shape[1]
    full = lambda a: pl.BlockSpec(a.shape, lambda i: (0,) * a.ndim)
    args = (feats, w1, b1, freq, w2, b2, w3, dec)
    return pl.pallas_call(
        _filter_kernel,
        grid=(1,),
        in_specs=[full(a) for a in args],
        out_specs=pl.BlockSpec((l, n), lambda i: (0, 0)),
        out_shape=jax.ShapeDtypeStruct((l, n), F32),
        compiler_params=_cp("arbitrary"),
        name="hyena_filter_taps",
    )(*args)


def _table_mm_kernel(t_ref, x_ref, o_ref):
    o_ref[...] = _dot(t_ref[...], x_ref[...].astype(BF16)).astype(o_ref.dtype)


def _table_matmul(table, x, out_dtype):
    m, k = table.shape
    n = x.shape[1]
    tm = min(m, 512)
    return pl.pallas_call(
        _table_mm_kernel,
        grid=(m // tm,),
        in_specs=[pl.BlockSpec((tm, k), lambda i: (i, 0)),
                  pl.BlockSpec((k, n), lambda i: (0, 0))],
        out_specs=pl.BlockSpec((tm, n), lambda i: (i, 0)),
        out_shape=jax.ShapeDtypeStruct((m, n), out_dtype),
        compiler_params=_cp("parallel"),
        name="table_matmul",
    )(table, x)


def _dftmul_kernel(fc_ref, fs_ref, z_ref, k_ref, p_ref):
    z = z_ref[...].astype(BF16)
    zc = _dot(fc_ref[...], z)
    zs = _dot(fs_ref[...], z)
    p_ref[0] = (zc * k_ref[0] - zs * k_ref[1]).astype(BF16)
    p_ref[1] = (zc * k_ref[2] + zs * k_ref[3]).astype(BF16)


def _dft_multiply(ffwd, z, z_spec, kf):
    l = ffwd.shape[1]
    b = z.shape[0]
    tf = min(l, DFT_ROWS)
    nf = l // tf
    out = pl.pallas_call(
        _dftmul_kernel,
        grid=(nf, b),
        in_specs=[pl.BlockSpec((tf, l), lambda i, bi: (i, 0)),
                  pl.BlockSpec((tf, l), lambda i, bi: (i + nf, 0)),
                  z_spec,
                  pl.BlockSpec((4, tf, HY_W), lambda i, bi: (0, i, 0))],
        out_specs=pl.BlockSpec((None, 2, tf, HY_W), lambda i, bi: (bi, 0, i, 0)),
        out_shape=jax.ShapeDtypeStruct((b, 2, l, HY_W), BF16),
        compiler_params=_cp("parallel", "arbitrary"),
        name="dft_multiply",
    )(ffwd, ffwd, z, kf)
    return out.reshape(b, 2 * l, HY_W)


def _idft_gate_kernel(fi_ref, p_ref, g_ref, z_ref, b_ref, o_ref):
    conv = _dot(fi_ref[...], p_ref[...])
    o_ref[...] = (g_ref[...] * (conv + b_ref[...] * z_ref[...])).astype(o_ref.dtype)


def _idft_gate(finv, p, gate, gate_spec, z, z_spec, bias, out_dtype):
    l = finv.shape[0]
    b = p.shape[0]
    tt = min(l, DFT_ROWS)
    return pl.pallas_call(
        _idft_gate_kernel,
        grid=(l // tt, b),
        in_specs=[pl.BlockSpec((tt, 2 * l), lambda i, bi: (i, 0)),
                  pl.BlockSpec((None, 2 * l, HY_W), lambda i, bi: (bi, 0, 0)),
                  gate_spec, z_spec,
                  pl.BlockSpec((1, HY_W), lambda i, bi: (0, 0))],
        out_specs=pl.BlockSpec((None, tt, HY_W), lambda i, bi: (bi, i, 0)),
        out_shape=jax.ShapeDtypeStruct((b, l, HY_W), out_dtype),
        compiler_params=_cp("parallel", "arbitrary"),
        name="idft_gate",
    )(finv, p, gate, z, bias)


def _hyena_mix(h, conv_w, conv_b, kf, hy_bias, ffwd, finv):
    b, l, _ = h.shape
    tt = min(l, DFT_ROWS)
    u = _short_conv(h, conv_w, conv_b)
    part = lambda j, rows: pl.BlockSpec((None, None, rows, HY_W),
                                        lambda i, bi, j=j: (bi, j, i if rows != l else 0, 0))
    p = _dft_multiply(ffwd, u, part(0, l), kf[0])
    z1 = _idft_gate(finv, p, u, part(1, tt), u, part(0, tt), hy_bias[0:1], F32)
    p = _dft_multiply(ffwd, z1, pl.BlockSpec((None, l, HY_W), lambda i, bi: (bi, 0, 0)), kf[1])
    return _idft_gate(finv, p, u, part(2, tt), z1,
                      pl.BlockSpec((None, tt, HY_W), lambda i, bi: (bi, i, 0)), hy_bias[1:2], BF16)


def _fn1_kernel(z_ref, m_ref, o_ref):
    r = _dot(z_ref[...], m_ref[...])
    half = r.shape[1] // 2
    o_ref[0] = r[:, :half].astype(BF16)
    o_ref[1] = r[:, half:].astype(BF16)


def _fourier_mix(h, m1, t2):
    b, l, _ = h.shape
    tm = min(l, DFT_ROWS)
    zz = pl.pallas_call(
        _fn1_kernel,
        grid=(b, l // tm),
        in_specs=[pl.BlockSpec((None, tm, FN_W), lambda bi, i: (bi, i, COL_FN // FN_W)),
                  pl.BlockSpec((FN_W, 2 * FN_W), lambda bi, i: (0, 0))],
        out_specs=pl.BlockSpec((None, 2, tm, FN_W), lambda bi, i: (bi, 0, i, 0)),
        out_shape=jax.ShapeDtypeStruct((b, 2, l, FN_W), BF16),
        compiler_params=_cp("parallel", "parallel"),
        name="fnet_channels",
    )(h, m1).reshape(b, 2 * l, FN_W)
    return pl.pallas_call(
        _table_mm_kernel,
        grid=(l // tm, b),
        in_specs=[pl.BlockSpec((tm, 2 * l), lambda i, bi: (i, 0)),
                  pl.BlockSpec((None, 2 * l, FN_W), lambda i, bi: (bi, 0, 0))],
        out_specs=pl.BlockSpec((None, tm, FN_W), lambda i, bi: (bi, i, 0)),
        out_shape=jax.ShapeDtypeStruct((b, l, FN_W), BF16),
        compiler_params=_cp("parallel", "arbitrary"),
        name="fnet_positions",
    )(t2, zz)


def _merge_kernel(hy_ref, fn_ref, at_ref, g_ref, x_ref, gm_ref, why_ref, wfn_ref, wat_ref, wout_ref, o_ref):
    d = x_ref.shape[-1]
    g = 1.0 / (1.0 + jnp.exp(-g_ref[...].astype(F32)))
    y = (g[:, :d] * _dot(hy_ref[...], why_ref[...])
         + g[:, d:2 * d] * _dot(fn_ref[...], wfn_ref[...])
         + g[:, 2 * d:] * _dot(at_ref[...], wat_ref[...]))
    mix = _dot(y.astype(BF16), wout_ref[...])
    o_ref[...] = x_ref[...] + gm_ref[...] * mix


def _merge(hyo, fno, att, h, x, mods, layer, row_fn, w_hy, w_fn, w_at, w_out):
    b, l, d = x.shape
    tm = min(l, 1024)
    full = lambda a: pl.BlockSpec(a.shape, lambda bi, i: (0,) * a.ndim)
    tok = lambda wdt, blk=0: pl.BlockSpec((None, tm, wdt), lambda bi, i: (bi, i, blk))
    return pl.pallas_call(
        _merge_kernel,
        grid=(b, l // tm),
        in_specs=[tok(HY_W), tok(FN_W), tok(N_HEADS * HEAD_W), tok(3 * d, COL_GATE), tok(d),
                  pl.BlockSpec((None, None, None, 1, d), _mod_spec(layer, 2, lambda bi, i: row_fn(bi))),
                  full(w_hy), full(w_fn), full(w_at), full(w_out)],
        out_specs=tok(d),
        out_shape=jax.ShapeDtypeStruct((b, l, d), F32),
        compiler_params=_cp("parallel", "parallel"),
        name="merge_residual",
    )(hyo, fno, att, h, x, mods, w_hy, w_fn, w_at, w_out)


def _peer_q_kernel(x_ref, g_ref, sh_ref, sc_ref, wq_ref, kh_ref, kl_ref, s_ref, nt_ref):
    n = _modulated_norm(x_ref[...], g_ref[...], sh_ref[...], sc_ref[...])
    nt_ref[...] = n.T.astype(BF16)
    q = _dot(n.astype(BF16), wq_ref[...])
    dq = kh_ref.shape[1]
    for hp in range(kh_ref.shape[0]):
        qh, ql = _split(q[:, hp * dq:(hp + 1) * dq])
        kh = kh_ref[hp]
        s = _dot(qh, kh) + _dot(ql, kh) + _dot(qh, kl_ref[hp])
        s_ref[hp] = s.T


def _peer_scores(x, g, mods, layer, row_fn, wq, keys_t):
    b, l, d = x.shape
    tt = min(l, 512)
    nt = l // tt
    nhp = 2 * PEER_HEADS
    mrow = lambda bi, i: row_fn(bi)
    full = lambda a: pl.BlockSpec(a.shape, lambda bi, i: (0,) * a.ndim)
    return pl.pallas_call(
        _peer_q_kernel,
        grid=(b, nt),
        in_specs=[pl.BlockSpec((None, tt, d), lambda bi, i: (bi, i, 0)),
                  pl.BlockSpec((1, d), lambda bi, i: (0, 0)),
                  pl.BlockSpec((None, None, None, 1, d), _mod_spec(layer, 3, mrow)),
                  pl.BlockSpec((None, None, None, 1, d), _mod_spec(layer, 4, mrow)),
                  full(wq), full(keys_t[0]), full(keys_t[1])],
        out_specs=[pl.BlockSpec((nhp, PEER_NKEYS, tt), lambda bi, i: (0, 0, bi * nt + i)),
                   pl.BlockSpec((d, tt), lambda bi, i: (0, bi * nt + i))],
        out_shape=[jax.ShapeDtypeStruct((nhp, PEER_NKEYS, b * l), F32),
                   jax.ShapeDtypeStruct((d, b * l), BF16)],
        compiler_params=_cp("parallel", "parallel"),
        name="peer_scores",
    )(x, g, mods, mods, wq, keys_t[0], keys_t[1])


def _top_rows(s, k):
    vals = []
    for r in range(k):
        m = jnp.max(s, axis=0, keepdims=True)
        vals.append(m)
        if r + 1 < k:
            s = jnp.where(s == m, NEG, s)
    return vals


def _merge_sort_pairs(n):
    pairs = []
    p = 1
    while p < n:
        k = p
        while k >= 1:
            for j in range(k % p, n - k, 2 * k):
                for i in range(min(k, n - j - k)):
                    if (i + j) // (2 * p) == (i + j + k) // (2 * p):
                        pairs.append((i + j, i + j + k))
            k //= 2
        p *= 2
    return pairs


def _top_rows_grouped(s, k):
    groups = s.shape[0] // 8
    v = [s[g * 8:(g + 1) * 8, :] for g in range(groups)]
    for i, j in _merge_sort_pairs(groups):
        v[i], v[j] = jnp.maximum(v[i], v[j]), jnp.minimum(v[i], v[j])
    v.append(jnp.full_like(v[0], NEG))
    vals = []
    for r in range(k):
        m = jnp.max(v[0], axis=0, keepdims=True)
        vals.append(m)
        taken = v[0] == m
        for d in range(min(groups, k - 1 - r)):
            v[d] = jnp.where(taken, v[d + 1], v[d])
    return vals


def _peer_expert_kernel(s_ref, nt_ref, u_ref, vt_ref, x_ref, gm_ref, o_ref,
                        tau_ref, l1_ref, l2_ref, l1c_ref, hid0_ref, hid1_ref, p0_ref, p1_ref, acc_ref):
    s = pl.program_id(1)
    nc = pl.num_programs(1) - 2
    k = PEER_TOPK
    tt = nt_ref.shape[1]
    n_i = u_ref.shape[0] // PEER_NKEYS
    piece = GATE_VREGS * 8 * 128 // tt

    def hidden(hid_ref):
        hid_ref[...] = _dot(u_ref[...], nt_ref[...])

    def stages(chunk, hid_w, hid_r, p_w, p_r):
        def hidden_task(slab, cols):
            def run():
                hid_w[slab, cols] = _dot(u_ref[slab, :], nt_ref[:, cols])
            return run

        def project_task(slab, cols):
            def run():
                acc_ref[slab, cols] += _dot(vt_ref[slab, :], p_r[:, cols])
            return run

        tasks = []
        for c0 in range(0, tt, MXU_COLS):
            cols = slice(c0, min(c0 + MXU_COLS, tt))
            tasks += [hidden_task(slice(r0, r0 + MXU_SLAB), cols) for r0 in range(0, u_ref.shape[0], MXU_SLAB)]
            tasks += [project_task(slice(r0, r0 + MXU_SLAB), cols) for r0 in range(0, vt_ref.shape[0], MXU_SLAB)]
        n_jp = PEER_NKEYS // piece
        slots = [[] for _ in range(n_i * n_jp)]
        for t, task in enumerate(tasks):
            slots[(t * MXU_BURSTS // len(tasks)) * (len(slots) // MXU_BURSTS)].append(task)
        first = pl.multiple_of(chunk * n_i, n_i)
        for h in range(PEER_HEADS):
            l1c_ref[h] = l1_ref[h, pl.ds(first, n_i), :]
        for ii in range(n_i):
            l1_row = [jnp.broadcast_to(l1c_ref[h, ii:ii + 1, :], (8, tt))[None] for h in range(PEER_HEADS)]
            for jp in range(n_jp):
                for task in slots[ii * n_jp + jp]:
                    task()
                js = slice(jp * piece, (jp + 1) * piece)
                gate = jnp.zeros((piece // 8, 8, tt), F32)
                for h in range(PEER_HEADS):
                    logw = l1_row[h] + l2_ref[h, js, :].reshape(piece // 8, 8, tt)
                    gate = gate + jnp.where(logw >= tau_ref[h][None], jnp.exp2(logw), 0.0)
                rows = slice(ii * PEER_NKEYS + jp * piece, ii * PEER_NKEYS + (jp + 1) * piece)
                hid = hid_r[rows, :]
                act = hid * (1.0 + lax.erf(hid * (2.0 ** -0.5)))
                p_w[rows, :] = (gate.reshape(piece, tt) * act).astype(BF16)

    @pl.when(s == 0)
    def _():
        acc_ref[...] = jnp.zeros_like(acc_ref)
        p1_ref[...] = jnp.zeros_like(p1_ref)
        hidden(hid0_ref)
        width = min(tt, 256)
        parts = tt // width

        def select(it, carry):
            h = it // parts
            cols = pl.ds(pl.multiple_of((it % parts) * width, width), width)
            s1 = s_ref[2 * h, :, cols]
            s2 = s_ref[2 * h + 1, :, cols]
            a = _top_rows_grouped(s1, k + 1)
            b = _top_rows_grouped(s2, k + 1)
            cand = [a[i] + b[j] for i in range(k + 1) for j in range((k + 1) // (i + 1))]
            pad = (-len(cand)) % 64
            top = _top_rows_grouped(jnp.concatenate(cand + [jnp.full_like(a[0], NEG)] * pad, axis=0), k + 1)
            z = jnp.zeros_like(top[0])
            for t in top[:k]:
                z = z + jnp.exp(t - top[0])
            shift = top[0] + jnp.log(z)
            log2e = 1.0 / math.log(2.0)
            l1_ref[h, :, cols] = s1 * log2e
            l2_ref[h, :, cols] = (s2 - shift) * log2e - 1.0
            tau = (0.5 * (top[k - 1] + top[k]) - shift) * log2e - 1.0
            tau_ref[h, :, cols] = jnp.broadcast_to(tau, (8, width))
            return carry

        lax.fori_loop(0, PEER_HEADS * parts, select, 0)

    @pl.when((s >= 1) & (s <= nc) & (s % 2 == 1))
    def _():
        stages(s - 1, hid1_ref, hid0_ref, p0_ref, p1_ref)

    @pl.when((s >= 1) & (s <= nc) & (s % 2 == 0))
    def _():
        stages(s - 1, hid0_ref, hid1_ref, p1_ref, p0_ref)

    @pl.when(s == nc + 1)
    def _():
        last = p0_ref if (nc - 1) % 2 == 0 else p1_ref
        out_t = acc_ref[...] + _dot(vt_ref[...], last[...])
        o_ref[...] = x_ref[...] + gm_ref[...] * out_t.T


def _peer_experts(scores, n_t, u, v_t, x, mods, layer, row_fn):
    b, l, d = x.shape
    tt = min(l, PEER_TOKENS)
    nt = l // tt
    nc, _, ec = v_t.shape
    nhp = scores.shape[0]
    return pl.pallas_call(
        _peer_expert_kernel,
        grid=(b * nt, nc + 2),
        in_specs=[pl.BlockSpec((nhp, PEER_NKEYS, tt), lambda t, s: (0, 0, t)),
                  pl.BlockSpec((d, tt), lambda t, s: (0, t)),
                  pl.BlockSpec((ec, d), lambda t, s: (jnp.minimum(s, nc - 1), 0)),
                  pl.BlockSpec((None, d, ec), lambda t, s: (jnp.clip(s - 2, 0, nc - 1), 0, 0)),
                  pl.BlockSpec((None, tt, d), lambda t, s: (t // nt, t % nt, 0)),
                  pl.BlockSpec((None, None, None, 1, d), _mod_spec(layer, 5, lambda t, s: row_fn(t // nt)))],
        out_specs=pl.BlockSpec((None, tt, d), lambda t, s: (t // nt, t % nt, 0)),
        out_shape=jax.ShapeDtypeStruct((b, l, d), F32),
        scratch_shapes=[pltpu.VMEM((PEER_HEADS, 8, tt), F32),
                        pltpu.VMEM((PEER_HEADS, PEER_NKEYS, tt), F32),
                        pltpu.VMEM((PEER_HEADS, PEER_NKEYS, tt), F32),
                        pltpu.VMEM((PEER_HEADS, ec // PEER_NKEYS, tt), F32),
                        pltpu.VMEM((ec, tt), F32),
                        pltpu.VMEM((ec, tt), F32),
                        pltpu.VMEM((ec, tt), BF16),
                        pltpu.VMEM((ec, tt), BF16),
                        pltpu.VMEM((d, tt), F32)],
        compiler_params=_cp("parallel", "arbitrary"),
        name="peer_experts",
    )(scores, n_t, u, v_t, x, mods)


def _peer(x, g, mods, layer, row_fn, wq_t, keys, u, v_t):
    scores, n_t = _peer_scores(x, g, mods, layer, row_fn, wq_t, keys)
    return _peer_experts(scores, n_t, u, v_t, x, mods, layer, row_fn)


def _dft_tables(l):
    cos, sin = _cos_sin_table(l, l, 2 * l)
    t = jnp.arange(l, dtype=jnp.int32)[None, :]
    first = (jnp.arange(l) == 0)[:, None]
    sin = jnp.where(first, jnp.where(t % 2 == 0, 1.0, -1.0), sin)
    ffwd = jnp.concatenate([cos, sin], axis=0).astype(BF16)
    return ffwd, ffwd.T


def _cos_sin_table(nf, nt, n):
    step = 64
    f = jnp.arange(nf, dtype=jnp.int32)[:, None]
    angle = lambda prod: (2.0 * math.pi) * ((prod % n).astype(F32) / n)
    a = angle(f * (step * jnp.arange(nt // step, dtype=jnp.int32)[None, :]))
    b = angle(f * jnp.arange(step, dtype=jnp.int32)[None, :])
    ca, sa = jnp.cos(a)[:, :, None], jnp.sin(a)[:, :, None]
    cb, sb = jnp.cos(b)[:, None, :], jnp.sin(b)[:, None, :]
    return (ca * cb - sa * sb).reshape(nf, nt), (sa * cb + ca * sb).reshape(nf, nt)


def _fnet_tables(l):
    t2 = jnp.concatenate(_cos_sin_table(l, l, l), axis=1).astype(BF16)
    k = np.arange(FN_GROUP)
    ang64 = 2.0 * np.pi * ((k[:, None] * k[None, :]) % FN_GROUP) / FN_GROUP
    eye = np.eye(FN_W // FN_GROUP)
    scale = 1.0 / math.sqrt(FN_GROUP * l)
    m1 = np.concatenate([np.kron(eye, np.cos(ang64)), -np.kron(eye, np.sin(ang64))], axis=1) * scale
    return jnp.asarray(m1, F32).astype(BF16), t2


def _rope_tables(l):
    rows = l // GRID_W
    row = jnp.repeat(jnp.arange(rows), GRID_W).astype(F32)
    col = jnp.tile(jnp.arange(GRID_W), rows).astype(F32)
    half = QK_DIM // 2
    inv = ROPE_BASE ** (-jnp.arange(0, half, 2, dtype=F32) / half)
    ang = jnp.stack([row[:, None] * inv, col[:, None] * inv], axis=1)
    cos = jnp.repeat(jnp.cos(ang)[:, :, None, :], 2, axis=2)
    sin = jnp.sin(ang)
    sin = jnp.stack([-sin, sin], axis=2)
    rep = lambda a: jnp.tile(a.reshape(l, QK_DIM), (1, 2 * N_HEADS))
    return rep(cos), rep(sin)


def _filter_features(l):
    pos = jnp.arange(l, dtype=F32)
    t = pos / max(l - 1, 1)
    w = 2.0 * math.pi * pos / l
    f = jnp.linspace(1e-4, HY_BANDS - 1, HY_BANDS, dtype=F32)
    feats = jnp.concatenate([t[:, None], jnp.cos(w[:, None] * f), -jnp.sin(w[:, None] * f)], axis=-1)
    feats = jnp.pad(feats, ((0, 0), (0, 64 - HY_EMB)))
    deltas = jnp.abs(jnp.linspace(HY_MIN_DECAY, HY_MAX_DECAY, HY_W, dtype=F32))
    dec = jnp.exp(-t[:, None] * deltas)
    return feats, jnp.tile(dec, (1, HY_ORDER))


def _hyena_filters(l, tabs, ffwd, w1, b1, freq, w2, b2, w3):
    feats, dec = tabs
    taps = _hyena_filter_taps(feats, jnp.pad(w1, ((0, 64 - HY_EMB), (0, 0))), b1[None], freq[None],
                              w2, b2[None], w3, dec)
    kf = _table_matmul(ffwd, taps, F32)
    half = HY_ORDER * HY_W
    kc = kf[:l, :half]
    nyq = kf[l, :half]
    ks = kf[l:, half:]
    n = 2.0 * l
    first = (jnp.arange(l) == 0)[:, None]
    wc = jnp.where(first, 1.0 / n, 2.0 / n)
    ka = kc * wc
    kb = jnp.where(first, 0.0, ks * (2.0 / n))
    kd = jnp.where(first, nyq[None, :] / n, kc * (2.0 / n))
    stack = jnp.stack([ka, kb, kb, kd], axis=0)
    return jnp.moveaxis(stack.reshape(4, l, HY_ORDER, HY_W), 2, 0)


def kernel(x, c, ctx, c_ctx, w_ada, b_ada, g_mix, g_ffn, w_in, hy_conv_w, hy_conv_b, hy_w1, hy_b1, hy_freq, hy_w2, hy_b2, hy_w3, hy_bias, g_q, g_k, lam, g_sub, w_hy, w_fn, w_at, w_out, peer_wq, peer_keys, peer_u, peer_v):
    bsz, seq, d = x.shape
    clen = ctx.shape[1]
    depth = w_ada.shape[0]

    cc = jnp.concatenate([c, c_ctx[None], jnp.zeros((MOD_ROWS - bsz - 1, d), F32)], axis=0)
    mods = _ada_mods(cc, w_ada, b_ada)
    lat_row = lambda bi: bi
    ctx_row = lambda bi: bsz

    rope = _rope_tables(seq)
    tabs = {n: dict(dft=_dft_tables(n), fnet=_fnet_tables(n), feat=_filter_features(n)) for n in (seq, clen)}
    w = N_HEADS * HEAD_W
    lane = np.arange(w)
    bd = jnp.asarray((lane[:, None] // QK_DIM == lane[None, :] // QK_DIM) / QK_DIM, F32).astype(BF16)

    xl, xc = x, ctx
    for l in range(depth):
        last = l == depth - 1
        lam_init = 0.8 - 0.6 * math.exp(-0.3 * l)
        wl = w_in[l]
        w_perm = jnp.concatenate([wl[:, 2560:], wl[:, :2560]], axis=1).astype(BF16)
        gq = jnp.tile(g_q[l].reshape(1, HEAD_W), (1, N_HEADS)) * (QK_DIM ** -0.5)
        gk = jnp.tile(g_k[l].reshape(1, HEAD_W), (1, N_HEADS))
        gsub = g_sub[l][None]
        wts = [a[l].astype(BF16) for a in (w_hy, w_fn, w_at, w_out)]
        filt_args = (hy_w1[l], hy_b1[l], hy_freq[l], hy_w2[l], hy_b2[l], hy_w3[l])

        h_l = _in_projection(xl, g_mix[l][None], mods, l, lat_row, w_perm, 0, P_IN)
        h_c = _in_projection(xc, g_mix[l][None], mods, l, ctx_row, w_perm, COL_Q if last else 0, P_IN)
        q_l, k_l = _qk_prep(h_l, gq, gk, bd, rope)
        q_c, k_c = _qk_prep(h_c, gq, gk, bd, None)
        vblk = COL_V // HEAD_W
        k_all, v_all = _concat_kv(k_l, k_c, h_l, h_c)
        att_l = _diff_attention(q_l, [(k_all, v_all, 0)], lam[l], gsub, lam_init)
        ffwd, finv = tabs[seq]["dft"]
        kf = _hyena_filters(seq, tabs[seq]["feat"], ffwd, *filt_args)
        hyo_l = _hyena_mix(h_l, hy_conv_w[l], hy_conv_b[l][None], kf, hy_bias[l], ffwd, finv)
        fno_l = _fourier_mix(h_l, *tabs[seq]["fnet"])
        if not last:
            att_c = _diff_attention(q_c, [(k_c, h_c, vblk)], lam[l], gsub, lam_init)
            ffwd_c, finv_c = tabs[clen]["dft"]
            kf_c = _hyena_filters(clen, tabs[clen]["feat"], ffwd_c, *filt_args)
            hyo_c = _hyena_mix(h_c, hy_conv_w[l], hy_conv_b[l][None], kf_c, hy_bias[l], ffwd_c, finv_c)
            fno_c = _fourier_mix(h_c, *tabs[clen]["fnet"])
            xc = _merge(hyo_c, fno_c, att_c, h_c, xc, mods, l, ctx_row, *wts)
        xl = _merge(hyo_l, fno_l, att_l, h_l, xl, mods, l, lat_row, *wts)

        wq_t = peer_wq[l].astype(BF16)
        keys = _split(jnp.swapaxes(peer_keys[l].reshape(2 * PEER_HEADS, PEER_NKEYS, -1), 1, 2))
        u = peer_u[l].astype(BF16)
        v_t = jnp.swapaxes(peer_v[l].reshape(-1, PEER_CHUNK, d), 1, 2).astype(BF16)
        if not last:
            xc = _peer(xc, g_ffn[l][None], mods, l, ctx_row, wq_t, keys, u, v_t)
        xl = _peer(xl, g_ffn[l][None], mods, l, lat_row, wq_t, keys, u, v_t)
    return xl
```

```python
import functools
import math

import jax
import jax.numpy as jnp
import numpy as np
from jax import lax
from jax.experimental import pallas as pl
from jax.experimental.pallas import tpu as pltpu

F32 = jnp.float32
BF16 = jnp.bfloat16

EPS = 1e-6
GRID_W = 64
ROPE_BASE = 10000.0
N_HEADS = 4
QK_DIM = 64
HEAD_W = 2 * QK_DIM
HY_W = 256
HY_ORDER = 2
HY_EMB = 33
HY_BANDS = (HY_EMB - 1) // 2
HY_MIN_DECAY = math.log(1e-2) / 1.5
HY_MAX_DECAY = math.log(1e-2) / 0.3
FN_GROUP = 64
FN_W = 256
PEER_HEADS = 8
PEER_NKEYS = 128
PEER_TOPK = 16
N_MOD = 6
MOD_ROWS = 16
NEG = -3.0e38
GATE_VREGS = 8
PEER_TOKENS = 256
PEER_CHUNK = 2048
DFT_ROWS = 1024

VMEM_LIMIT = 56 * 1024 * 1024

COL_GATE = 0
COL_HY = 3072
COL_FN = 3840
COL_Q = 4096
COL_K = 4608
COL_V = 5120
P_IN = 5632


def _cp(*sem):
    return pltpu.CompilerParams(dimension_semantics=sem, vmem_limit_bytes=VMEM_LIMIT)


def _dot(a, b):
    return jnp.dot(a, b, preferred_element_type=F32)


def _dot_nt(a, b):
    return lax.dot_general(a, b, (((1,), (1,)), ((), ())), preferred_element_type=F32)


def _split(a):
    hi = a.astype(BF16)
    lo = (a - hi.astype(F32)).astype(BF16)
    return hi, lo


def _modulated_norm(x, g, shift, scale):
    ms = jnp.mean(x * x, axis=-1, keepdims=True)
    y = x * lax.rsqrt(ms + EPS) * g
    return y * (1.0 + scale) + shift


def _mod_spec(layer, chunk, row_fn):
    def imap(*idx):
        return (layer, row_fn(*idx), chunk, 0, 0)
    return imap


def _ada_kernel(c_ref, w_ref, b_ref, o_ref):
    c = c_ref[...]
    a = c / (1.0 + jnp.exp(-c))
    ah, al = _split(a)
    wh, wl = _split(w_ref[...])
    o_ref[...] = _dot(ah, wh) + _dot(ah, wl) + _dot(al, wh) + b_ref[...]


def _ada_mods(cc, w_ada, b_ada):
    depth, d, n = w_ada.shape
    tn = 512
    out = pl.pallas_call(
        _ada_kernel,
        grid=(depth, n // tn),
        in_specs=[pl.BlockSpec((MOD_ROWS, d), lambda l, j: (0, 0)),
                  pl.BlockSpec((None, d, tn), lambda l, j: (l, 0, j)),
                  pl.BlockSpec((None, 1, tn), lambda l, j: (l, 0, j))],
        out_specs=pl.BlockSpec((None, MOD_ROWS, tn), lambda l, j: (l, 0, j)),
        out_shape=jax.ShapeDtypeStruct((depth, MOD_ROWS, n), F32),
        compiler_params=_cp("parallel", "parallel"),
        name="ada_mods",
    )(cc, w_ada, b_ada.reshape(depth, 1, n))
    return out.reshape(depth, MOD_ROWS, N_MOD, 1, d)


def _inproj_kernel(x_ref, g_ref, sh_ref, sc_ref, w_ref, o_ref, xn_ref):
    @pl.when(pl.program_id(2) == 0)
    def _():
        xn_ref[...] = _modulated_norm(x_ref[...], g_ref[...], sh_ref[...], sc_ref[...]).astype(BF16)
    o_ref[...] = _dot(xn_ref[...], w_ref[...]).astype(o_ref.dtype)


def _in_projection(x, g, mods, layer, row_fn, w, col_lo, col_hi):
    b, l, d = x.shape
    tm = min(l, 1024)
    wide = w.shape[1] // 2
    tn = wide if (col_lo % wide == 0 and (col_hi - col_lo) % wide == 0) else 512
    j0 = col_lo // tn
    nj = (col_hi - col_lo) // tn
    mrow = lambda bi, i, j: row_fn(bi)
    return pl.pallas_call(
        _inproj_kernel,
        grid=(b, l // tm, nj),
        in_specs=[pl.BlockSpec((None, tm, d), lambda bi, i, j: (bi, i, 0)),
                  pl.BlockSpec((1, d), lambda bi, i, j: (0, 0)),
                  pl.BlockSpec((None, None, None, 1, d), _mod_spec(layer, 0, mrow)),
                  pl.BlockSpec((None, None, None, 1, d), _mod_spec(layer, 1, mrow)),
                  pl.BlockSpec((d, tn), lambda bi, i, j: (0, j + j0))],
        out_specs=pl.BlockSpec((None, tm, tn), lambda bi, i, j: (bi, i, j + j0)),
        out_shape=jax.ShapeDtypeStruct((b, l, w.shape[1]), BF16),
        scratch_shapes=[pltpu.VMEM((tm, d), BF16)],
        compiler_params=_cp("parallel", "parallel", "arbitrary"),
        name="in_projection",
    )(x, g, mods, mods, w)


def _qkprep_kernel(*refs, rope):
    if rope:
        q_ref, k_ref, gq_ref, gk_ref, bd_ref, cos_ref, sin_ref, qo_ref, ko_ref = refs
    else:
        q_ref, k_ref, gq_ref, gk_ref, bd_ref, qo_ref, ko_ref = refs
    for src, g_ref, dst in ((q_ref, gq_ref, qo_ref), (k_ref, gk_ref, ko_ref)):
        x = src[...].astype(F32)
        hi, lo = _split(x * x)
        ms = _dot(hi, bd_ref[...]) + _dot(lo, bd_ref[...])
        y = x * lax.rsqrt(ms + EPS) * g_ref[...]
        if rope:
            w = y.shape[1]
            lane = lax.broadcasted_iota(jnp.int32, y.shape, 1)
            first = (lane % (QK_DIM // 2)) < (QK_DIM // 4)
            partner = jnp.where(first, pltpu.roll(y, w - QK_DIM // 4, 1), pltpu.roll(y, QK_DIM // 4, 1))
            y = y * cos_ref[...] + partner * sin_ref[...]
        dst[...] = y.astype(BF16)


def _qk_prep(h, gq, gk, bd, rope_tabs):
    b, l, _ = h.shape
    w = N_HEADS * HEAD_W
    tm = min(l, 1024)
    rope = rope_tabs is not None
    in_specs = [pl.BlockSpec((None, tm, w), lambda bi, i: (bi, i, COL_Q // w)),
                pl.BlockSpec((None, tm, w), lambda bi, i: (bi, i, COL_K // w)),
                pl.BlockSpec((1, w), lambda bi, i: (0, 0)),
                pl.BlockSpec((1, w), lambda bi, i: (0, 0)),
                pl.BlockSpec((w, w), lambda bi, i: (0, 0))]
    args = [h, h, gq, gk, bd]
    if rope:
        in_specs += [pl.BlockSpec((tm, w), lambda bi, i: (i, 0))] * 2
        args += list(rope_tabs)
    return pl.pallas_call(
        functools.partial(_qkprep_kernel, rope=rope),
        grid=(b, l // tm),
        in_specs=in_specs,
        out_specs=[pl.BlockSpec((None, tm, w), lambda bi, i: (bi, i, 0))] * 2,
        out_shape=[jax.ShapeDtypeStruct((b, l, w), BF16)] * 2,
        compiler_params=_cp("parallel", "parallel"),
        name="qk_prep",
    )(*args)


def _attn_kernel(*refs, n_src, lam_init):
    q_ref = refs[0]
    kv = refs[1:1 + 2 * n_src]
    lam_ref, gsub_ref, o_ref = refs[1 + 2 * n_src:]
    q = q_ref[...]
    tq = q.shape[0]
    lane = lax.broadcasted_iota(jnp.int32, q.shape, 1)
    zero = jnp.zeros_like(q)
    qq = jnp.concatenate([jnp.where(lane < QK_DIM, q, zero), jnp.where(lane >= QK_DIM, q, zero)], axis=0)
    scores = [_dot_nt(qq, kv[2 * i][...]) for i in range(n_src)]
    m = jnp.max(scores[0], axis=-1, keepdims=True)
    for s in scores[1:]:
        m = jnp.maximum(m, jnp.max(s, axis=-1, keepdims=True))
    z = jnp.zeros_like(m)
    acc = jnp.zeros((2 * tq, HEAD_W), F32)
    for i, s in enumerate(scores):
        e = jnp.exp2(s - m)
        z = z + jnp.sum(e, axis=-1, keepdims=True)
        acc = acc + _dot(e.astype(BF16), kv[2 * i + 1][...])
    o2 = acc / z
    lf = lam_ref[...]
    lam_val = (jnp.exp(jnp.sum(lf[0:1] * lf[1:2], axis=-1, keepdims=True))
               - jnp.exp(jnp.sum(lf[2:3] * lf[3:4], axis=-1, keepdims=True)) + lam_init)
    o = o2[:tq] - lam_val * o2[tq:]
    ms = jnp.mean(o * o, axis=-1, keepdims=True)
    o = o * lax.rsqrt(ms + EPS) * gsub_ref[...] * (1.0 - lam_init)
    o_ref[...] = o.astype(BF16)


def _concat_kernel(kl_ref, kc_ref, vl_ref, vc_ref, ko_ref, vo_ref):
    s = kl_ref.shape[0]
    ko_ref[:s, :] = kl_ref[...]
    ko_ref[s:, :] = kc_ref[...]
    vo_ref[:s, :] = vl_ref[...]
    vo_ref[s:, :] = vc_ref[...]


def _concat_kv(k_l, k_c, h_l, h_c):
    b, s, w = k_l.shape
    c = k_c.shape[1]
    vcol = COL_V // w
    rows = lambda n, blk: pl.BlockSpec((None, n, w), lambda bi: (bi, 0, blk))
    return pl.pallas_call(
        _concat_kernel,
        grid=(b,),
        in_specs=[rows(s, 0), rows(c, 0), rows(s, vcol), rows(c, vcol)],
        out_specs=[rows(s + c, 0)] * 2,
        out_shape=[jax.ShapeDtypeStruct((b, s + c, w), BF16)] * 2,
        compiler_params=_cp("parallel"),
        name="concat_kv",
    )(k_l, k_c, h_l, h_c)


def _diff_attention(q, sources, lam_l, gsub, lam_init):
    b, lq, w = q.shape
    tq = min(lq, 512)
    in_specs = [pl.BlockSpec((None, tq, HEAD_W), lambda bi, hi, i: (bi, i, hi))]
    args = [q]
    for k, varr, vblk in sources:
        lk = k.shape[1]
        in_specs.append(pl.BlockSpec((None, lk, HEAD_W), lambda bi, hi, i: (bi, 0, hi)))
        in_specs.append(pl.BlockSpec((None, lk, HEAD_W), lambda bi, hi, i, vblk=vblk: (bi, 0, vblk + hi)))
        args += [k, varr]
    in_specs += [pl.BlockSpec(lam_l.shape, lambda bi, hi, i: (0, 0)),
                 pl.BlockSpec((1, HEAD_W), lambda bi, hi, i: (0, 0))]
    args += [lam_l, gsub]
    return pl.pallas_call(
        functools.partial(_attn_kernel, n_src=len(sources), lam_init=lam_init),
        grid=(b, N_HEADS, lq // tq),
        in_specs=in_specs,
        out_specs=pl.BlockSpec((None, tq, HEAD_W), lambda bi, hi, i: (bi, i, hi)),
        out_shape=jax.ShapeDtypeStruct((b, lq, w), BF16),
        compiler_params=_cp("parallel", "parallel", "arbitrary"),
        name="diff_attention",
    )(*args)


def _sconv_kernel(h_ref, w_ref, b_ref, o_ref):
    x = h_ref[...].astype(F32)
    n = x.shape[0]
    row = lax.broadcasted_iota(jnp.int32, x.shape, 0)
    prev = jnp.where(row == 0, 0.0, pltpu.roll(x, 1, 0))
    nxt = jnp.where(row == n - 1, 0.0, pltpu.roll(x, n - 1, 0))
    o_ref[...] = prev * w_ref[0:1, :] + x * w_ref[1:2, :] + nxt * w_ref[2:3, :] + b_ref[...]


def _short_conv(h, w, bias):
    b, l, _ = h.shape
    return pl.pallas_call(
        _sconv_kernel,
        grid=(b, 3),
        in_specs=[pl.BlockSpec((None, l, HY_W), lambda bi, j: (bi, 0, COL_HY // HY_W + j)),
                  pl.BlockSpec((3, HY_W), lambda bi, j: (0, j)),
                  pl.BlockSpec((1, HY_W), lambda bi, j: (0, j))],
        out_specs=pl.BlockSpec((None, None, l, HY_W), lambda bi, j: (bi, j, 0, 0)),
        out_shape=jax.ShapeDtypeStruct((b, 3, l, HY_W), F32),
        compiler_params=_cp("parallel", "parallel"),
        name="short_conv",
    )(h, w, bias)


def _filter_kernel(f_ref, w1_ref, b1_ref, fr_ref, w2_ref, b2_ref, w3_ref, dec_ref, o_ref):
    def mm(a, w_ref_):
        ah, al = _split(a)
        wh, wl = _split(w_ref_[...])
        return _dot(ah, wh) + _dot(ah, wl) + _dot(al, wh)
    fr = fr_ref[...]
    h = jnp.sin(fr * (mm(f_ref[...], w1_ref) + b1_ref[...]))
    h = jnp.sin(fr * (mm(h, w2_ref) + b2_ref[...]))
    h = mm(h, w3_ref)
    half = h.shape[1] // 2
    dec = dec_ref[...]
    hf = h[:, :half] * dec
    hb = h[:, half:] * dec
    row = lax.broadcasted_iota(jnp.int32, hb.shape, 0)
    hb = jnp.where(row == 0, 0.0, hb)
    norm = jnp.sum(jnp.abs(hf) + jnp.abs(hb), axis=0, keepdims=True)
    o_ref[:, :half] = (hf + hb) / norm
    o_ref[:, half:] = (hf - hb) / norm


def _hyena_filter_taps(feats, w1, b1, freq, w2, b2, w3, dec):
    l = feats.shape[0]
    n = w3.shape[1]
    full = lambda a: pl.BlockSpec(a.shape, lambda i: (0,) * a.ndim)
    args = (feats, w1, b1, freq, w2, b2, w3, dec)
    return pl.pallas_call(
        _filter_kernel,
        grid=(1,),
        in_specs=[full(a) for a in args],
        out_specs=pl.BlockSpec((l, n), lambda i: (0, 0)),
        out_shape=jax.ShapeDtypeStruct((l, n), F32),
        compiler_params=_cp("arbitrary"),
        name="hyena_filter_taps",
    )(*args)


def _table_mm_kernel(t_ref, x_ref, o_ref):
    o_ref[...] = _dot(t_ref[...], x_ref[...].astype(BF16)).astype(o_ref.dtype)


def _table_matmul(table, x, out_dtype):
    m, k = table.shape
    n = x.shape[1]
    tm = min(m, 512)
    return pl.pallas_call(
        _table_mm_kernel,
        grid=(m // tm,),
        in_specs=[pl.BlockSpec((tm, k), lambda i: (i, 0)),
                  pl.BlockSpec((k, n), lambda i: (0, 0))],
        out_specs=pl.BlockSpec((tm, n), lambda i: (i, 0)),
        out_shape=jax.ShapeDtypeStruct((m, n), out_dtype),
        compiler_params=_cp("parallel"),
        name="table_matmul",
    )(table, x)


def _dftmul_kernel(fc_ref, fs_ref, z_ref, k_ref, p_ref):
    z = z_ref[...].astype(BF16)
    zc = _dot(fc_ref[...], z)
    zs = _dot(fs_ref[...], z)
    p_ref[0] = (zc * k_ref[0] - zs * k_ref[1]).astype(BF16)
    p_ref[1] = (zc * k_ref[2] + zs * k_ref[3]).astype(BF16)


def _dft_multiply(ffwd, z, z_spec, kf):
    l = ffwd.shape[1]
    b = z.shape[0]
    tf = min(l, DFT_ROWS)
    nf = l // tf
    out = pl.pallas_call(
        _dftmul_kernel,
        grid=(nf, b),
        in_specs=[pl.BlockSpec((tf, l), lambda i, bi: (i, 0)),
                  pl.BlockSpec((tf, l), lambda i, bi: (i + nf, 0)),
                  z_spec,
                  pl.BlockSpec((4, tf, HY_W), lambda i, bi: (0, i, 0))],
        out_specs=pl.BlockSpec((None, 2, tf, HY_W), lambda i, bi: (bi, 0, i, 0)),
        out_shape=jax.ShapeDtypeStruct((b, 2, l, HY_W), BF16),
        compiler_params=_cp("parallel", "arbitrary"),
        name="dft_multiply",
    )(ffwd, ffwd, z, kf)
    return out.reshape(b, 2 * l, HY_W)


def _idft_gate_kernel(fi_ref, p_ref, g_ref, z_ref, b_ref, o_ref):
    conv = _dot(fi_ref[...], p_ref[...])
    o_ref[...] = (g_ref[...] * (conv + b_ref[...] * z_ref[...])).astype(o_ref.dtype)


def _idft_gate(finv, p, gate, gate_spec, z, z_spec, bias, out_dtype):
    l = finv.shape[0]
    b = p.shape[0]
    tt = min(l, DFT_ROWS)
    return pl.pallas_call(
        _idft_gate_kernel,
        grid=(l // tt, b),
        in_specs=[pl.BlockSpec((tt, 2 * l), lambda i, bi: (i, 0)),
                  pl.BlockSpec((None, 2 * l, HY_W), lambda i, bi: (bi, 0, 0)),
                  gate_spec, z_spec,
                  pl.BlockSpec((1, HY_W), lambda i, bi: (0, 0))],
        out_specs=pl.BlockSpec((None, tt, HY_W), lambda i, bi: (bi, i, 0)),
        out_shape=jax.ShapeDtypeStruct((b, l, HY_W), out_dtype),
        compiler_params=_cp("parallel", "arbitrary"),
        name="idft_gate",
    )(finv, p, gate, z, bias)


def _hyena_mix(h, conv_w, conv_b, kf, hy_bias, ffwd, finv):
    b, l, _ = h.shape
    tt = min(l, DFT_ROWS)
    u = _short_conv(h, conv_w, conv_b)
    part = lambda j, rows: pl.BlockSpec((None, None, rows, HY_W),
                                        lambda i, bi, j=j: (bi, j, i if rows != l else 0, 0))
    p = _dft_multiply(ffwd, u, part(0, l), kf[0])
    z1 = _idft_gate(finv, p, u, part(1, tt), u, part(0, tt), hy_bias[0:1], F32)
    p = _dft_multiply(ffwd, z1, pl.BlockSpec((None, l, HY_W), lambda i, bi: (bi, 0, 0)), kf[1])
    return _idft_gate(finv, p, u, part(2, tt), z1,
                      pl.BlockSpec((None, tt, HY_W), lambda i, bi: (bi, i, 0)), hy_bias[1:2], BF16)


def _fn1_kernel(z_ref, m_ref, o_ref):
    r = _dot(z_ref[...], m_ref[...])
    half = r.shape[1] // 2
    o_ref[0] = r[:, :half].astype(BF16)
    o_ref[1] = r[:, half:].astype(BF16)


def _fourier_mix(h, m1, t2):
    b, l, _ = h.shape
    tm = min(l, DFT_ROWS)
    zz = pl.pallas_call(
        _fn1_kernel,
        grid=(b, l // tm),
        in_specs=[pl.BlockSpec((None, tm, FN_W), lambda bi, i: (bi, i, COL_FN // FN_W)),
                  pl.BlockSpec((FN_W, 2 * FN_W), lambda bi, i: (0, 0))],
        out_specs=pl.BlockSpec((None, 2, tm, FN_W), lambda bi, i: (bi, 0, i, 0)),
        out_shape=jax.ShapeDtypeStruct((b, 2, l, FN_W), BF16),
        compiler_params=_cp("parallel", "parallel"),
        name="fnet_channels",
    )(h, m1).reshape(b, 2 * l, FN_W)
    return pl.pallas_call(
        _table_mm_kernel,
        grid=(l // tm, b),
        in_specs=[pl.BlockSpec((tm, 2 * l), lambda i, bi: (i, 0)),
                  pl.BlockSpec((None, 2 * l, FN_W), lambda i, bi: (bi, 0, 0))],
        out_specs=pl.BlockSpec((None, tm, FN_W), lambda i, bi: (bi, i, 0)),
        out_shape=jax.ShapeDtypeStruct((b, l, FN_W), BF16),
        compiler_params=_cp("parallel", "arbitrary"),
        name="fnet_positions",
    )(t2, zz)


def _merge_kernel(hy_ref, fn_ref, at_ref, g_ref, x_ref, gm_ref, why_ref, wfn_ref, wat_ref, wout_ref, o_ref):
    d = x_ref.shape[-1]
    g = 1.0 / (1.0 + jnp.exp(-g_ref[...].astype(F32)))
    y = (g[:, :d] * _dot(hy_ref[...], why_ref[...])
         + g[:, d:2 * d] * _dot(fn_ref[...], wfn_ref[...])
         + g[:, 2 * d:] * _dot(at_ref[...], wat_ref[...]))
    mix = _dot(y.astype(BF16), wout_ref[...])
    o_ref[...] = x_ref[...] + gm_ref[...] * mix


def _merge(hyo, fno, att, h, x, mods, layer, row_fn, w_hy, w_fn, w_at, w_out):
    b, l, d = x.shape
    tm = min(l, 1024)
    full = lambda a: pl.BlockSpec(a.shape, lambda bi, i: (0,) * a.ndim)
    tok = lambda wdt, blk=0: pl.BlockSpec((None, tm, wdt), lambda bi, i: (bi, i, blk))
    return pl.pallas_call(
        _merge_kernel,
        grid=(b, l // tm),
        in_specs=[tok(HY_W), tok(FN_W), tok(N_HEADS * HEAD_W), tok(3 * d, COL_GATE), tok(d),
                  pl.BlockSpec((None, None, None, 1, d), _mod_spec(layer, 2, lambda bi, i: row_fn(bi))),
                  full(w_hy), full(w_fn), full(w_at), full(w_out)],
        out_specs=tok(d),
        out_shape=jax.ShapeDtypeStruct((b, l, d), F32),
        compiler_params=_cp("parallel", "parallel"),
        name="merge_residual",
    )(hyo, fno, att, h, x, mods, w_hy, w_fn, w_at, w_out)


def _peer_q_kernel(x_ref, g_ref, sh_ref, sc_ref, wq_ref, kh_ref, kl_ref, s_ref, nt_ref):
    n = _modulated_norm(x_ref[...], g_ref[...], sh_ref[...], sc_ref[...])
    nt_ref[...] = n.T.astype(BF16)
    q = _dot(n.astype(BF16), wq_ref[...])
    dq = kh_ref.shape[1]
    for hp in range(kh_ref.shape[0]):
        qh, ql = _split(q[:, hp * dq:(hp + 1) * dq])
        kh = kh_ref[hp]
        s = _dot(qh, kh) + _dot(ql, kh) + _dot(qh, kl_ref[hp])
        s_ref[hp] = s.T


def _peer_scores(x, g, mods, layer, row_fn, wq, keys_t):
    b, l, d = x.shape
    tt = min(l, 512)
    nt = l // tt
    nhp = 2 * PEER_HEADS
    mrow = lambda bi, i: row_fn(bi)
    full = lambda a: pl.BlockSpec(a.shape, lambda bi, i: (0,) * a.ndim)
    return pl.pallas_call(
        _peer_q_kernel,
        grid=(b, nt),
        in_specs=[pl.BlockSpec((None, tt, d), lambda bi, i: (bi, i, 0)),
                  pl.BlockSpec((1, d), lambda bi, i: (0, 0)),
                  pl.BlockSpec((None, None, None, 1, d), _mod_spec(layer, 3, mrow)),
                  pl.BlockSpec((None, None, None, 1, d), _mod_spec(layer, 4, mrow)),
                  full(wq), full(keys_t[0]), full(keys_t[1])],
        out_specs=[pl.BlockSpec((nhp, PEER_NKEYS, tt), lambda bi, i: (0, 0, bi * nt + i)),
                   pl.BlockSpec((d, tt), lambda bi, i: (0, bi * nt + i))],
        out_shape=[jax.ShapeDtypeStruct((nhp, PEER_NKEYS, b * l), F32),
                   jax.ShapeDtypeStruct((d, b * l), BF16)],
        compiler_params=_cp("parallel", "parallel"),
        name="peer_scores",
    )(x, g, mods, mods, wq, keys_t[0], keys_t[1])


def _merge_sort_pairs(n):
    pairs = []
    p = 1
    while p < n:
        k = p
        while k >= 1:
            for j in range(k % p, n - k, 2 * k):
                for i in range(min(k, n - j - k)):
                    if (i + j) // (2 * p) == (i + j + k) // (2 * p):
                        pairs.append((i + j, i + j + k))
            k //= 2
        p *= 2
    return pairs


def _top_rows_grouped(s, k):
    groups = s.shape[0] // 8
    v = [s[g * 8:(g + 1) * 8, :] for g in range(groups)]
    for i, j in _merge_sort_pairs(groups):
        v[i], v[j] = jnp.maximum(v[i], v[j]), jnp.minimum(v[i], v[j])
    v.append(jnp.full_like(v[0], NEG))
    vals = []
    for r in range(k):
        m = jnp.max(v[0], axis=0, keepdims=True)
        vals.append(m)
        taken = v[0] == m
        for d in range(min(groups, k - 1 - r)):
            v[d] = jnp.where(taken, v[d + 1], v[d])
    return vals


def _peer_expert_kernel(s_ref, nt_ref, u_ref, vt_ref, x_ref, gm_ref, o_ref,
                        tau_ref, l1_ref, l2_ref, l1c_ref, hid0_ref, hid1_ref, p0_ref, p1_ref, acc_ref):
    s = pl.program_id(1)
    nc = pl.num_programs(1) - 2
    k = PEER_TOPK
    tt = nt_ref.shape[1]
    n_i = u_ref.shape[0] // PEER_NKEYS
    piece = GATE_VREGS * 8 * 128 // tt

    def hidden(hid_ref):
        hid_ref[...] = _dot(u_ref[...], nt_ref[...])

    def stages(chunk, hid_w, hid_r, p_w, p_r):
        first = pl.multiple_of(chunk * n_i, n_i)
        for h in range(PEER_HEADS):
            l1c_ref[h] = l1_ref[h, pl.ds(first, n_i), :]
        hidden(hid_w)
        acc_ref[...] += _dot(vt_ref[...], p_r[...])
        for ii in range(n_i):
            l1_row = [jnp.broadcast_to(l1c_ref[h, ii:ii + 1, :], (8, tt))[None] for h in range(PEER_HEADS)]
            for jp in range(PEER_NKEYS // piece):
                js = slice(jp * piece, (jp + 1) * piece)
                gate = jnp.zeros((piece // 8, 8, tt), F32)
                for h in range(PEER_HEADS):
                    logw = l1_row[h] + l2_ref[h, js, :].reshape(piece // 8, 8, tt)
                    gate = gate + jnp.where(logw >= tau_ref[h][None], jnp.exp2(logw), 0.0)
                rows = slice(ii * PEER_NKEYS + jp * piece, ii * PEER_NKEYS + (jp + 1) * piece)
                hid = hid_r[rows, :]
                act = hid * (1.0 + lax.erf(hid * (2.0 ** -0.5)))
                p_w[rows, :] = (gate.reshape(piece, tt) * act).astype(BF16)

    @pl.when(s == 0)
    def _():
        acc_ref[...] = jnp.zeros_like(acc_ref)
        p1_ref[...] = jnp.zeros_like(p1_ref)
        hidden(hid0_ref)
        width = min(tt, 256)
        parts = tt // width

        def select(it, carry):
            h = it // parts
            cols = pl.ds(pl.multiple_of((it % parts) * width, width), width)
            s1 = s_ref[2 * h, :, cols]
            s2 = s_ref[2 * h + 1, :, cols]
            a = _top_rows_grouped(s1, k + 1)
            b = _top_rows_grouped(s2, k + 1)
            cand = [a[i] + b[j] for i in range(k + 1) for j in range((k + 1) // (i + 1))]
            pad = (-len(cand)) % 64
            top = _top_rows_grouped(jnp.concatenate(cand + [jnp.full_like(a[0], NEG)] * pad, axis=0), k + 1)
            z = jnp.zeros_like(top[0])
            for t in top[:k]:
                z = z + jnp.exp(t - top[0])
            shift = top[0] + jnp.log(z)
            log2e = 1.0 / math.log(2.0)
            l1_ref[h, :, cols] = s1 * log2e
            l2_ref[h, :, cols] = (s2 - shift) * log2e - 1.0
            tau = (0.5 * (top[k - 1] + top[k]) - shift) * log2e - 1.0
            tau_ref[h, :, cols] = jnp.broadcast_to(tau, (8, width))
            return carry

        lax.fori_loop(0, PEER_HEADS * parts, select, 0)

    @pl.when((s >= 1) & (s <= nc) & (s % 2 == 1))
    def _():
        stages(s - 1, hid1_ref, hid0_ref, p0_ref, p1_ref)

    @pl.when((s >= 1) & (s <= nc) & (s % 2 == 0))
    def _():
        stages(s - 1, hid0_ref, hid1_ref, p1_ref, p0_ref)

    @pl.when(s == nc + 1)
    def _():
        last = p0_ref if (nc - 1) % 2 == 0 else p1_ref
        out_t = acc_ref[...] + _dot(vt_ref[...], last[...])
        o_ref[...] = x_ref[...] + gm_ref[...] * out_t.T


def _peer_experts(scores, n_t, u, v_t, x, mods, layer, row_fn):
    b, l, d = x.shape
    tt = min(l, PEER_TOKENS)
    nt = l // tt
    nc, _, ec = v_t.shape
    nhp = scores.shape[0]
    return pl.pallas_call(
        _peer_expert_kernel,
        grid=(b * nt, nc + 2),
        in_specs=[pl.BlockSpec((nhp, PEER_NKEYS, tt), lambda t, s: (0, 0, t)),
                  pl.BlockSpec((d, tt), lambda t, s: (0, t)),
                  pl.BlockSpec((ec, d), lambda t, s: (jnp.minimum(s, nc - 1), 0)),
                  pl.BlockSpec((None, d, ec), lambda t, s: (jnp.clip(s - 2, 0, nc - 1), 0, 0)),
                  pl.BlockSpec((None, tt, d), lambda t, s: (t // nt, t % nt, 0)),
                  pl.BlockSpec((None, None, None, 1, d), _mod_spec(layer, 5, lambda t, s: row_fn(t // nt)))],
        out_specs=pl.BlockSpec((None, tt, d), lambda t, s: (t // nt, t % nt, 0)),
        out_shape=jax.ShapeDtypeStruct((b, l, d), F32),
        scratch_shapes=[pltpu.VMEM((PEER_HEADS, 8, tt), F32),
                        pltpu.VMEM((PEER_HEADS, PEER_NKEYS, tt), F32),
                        pltpu.VMEM((PEER_HEADS, PEER_NKEYS, tt), F32),
                        pltpu.VMEM((PEER_HEADS, ec // PEER_NKEYS, tt), F32),
                        pltpu.VMEM((ec, tt), F32),
                        pltpu.VMEM((ec, tt), F32),
                        pltpu.VMEM((ec, tt), BF16),
                        pltpu.VMEM((ec, tt), BF16),
                        pltpu.VMEM((d, tt), F32)],
        compiler_params=_cp("parallel", "arbitrary"),
        name="peer_experts",
    )(scores, n_t, u, v_t, x, mods)


def _cast_kernel(a_ref, o_ref):
    o_ref[...] = a_ref[...].astype(o_ref.dtype)


def _cast_bf16(a):
    rows, d = a.shape
    return pl.pallas_call(
        _cast_kernel,
        grid=(rows // PEER_CHUNK,),
        in_specs=[pl.BlockSpec((PEER_CHUNK, d), lambda i: (i, 0))],
        out_specs=pl.BlockSpec((PEER_CHUNK, d), lambda i: (i, 0)),
        out_shape=jax.ShapeDtypeStruct((rows, d), BF16),
        compiler_params=_cp("parallel"),
        name="cast_bf16",
    )(a)


def _peer(x, g, mods, layer, row_fn, wq_t, keys, u, v_t):
    scores, n_t = _peer_scores(x, g, mods, layer, row_fn, wq_t, keys)
    return _peer_experts(scores, n_t, u, v_t, x, mods, layer, row_fn)


def _dft_tables(l):
    cos, sin = _cos_sin_table(l, l, 2 * l)
    t = jnp.arange(l, dtype=jnp.int32)[None, :]
    first = (jnp.arange(l) == 0)[:, None]
    sin = jnp.where(first, jnp.where(t % 2 == 0, 1.0, -1.0), sin)
    ffwd = jnp.concatenate([cos, sin], axis=0).astype(BF16)
    return ffwd, ffwd.T


def _cos_sin_table(nf, nt, n):
    step = 64
    f = jnp.arange(nf, dtype=jnp.int32)[:, None]
    angle = lambda prod: (2.0 * math.pi) * ((prod % n).astype(F32) / n)
    a = angle(f * (step * jnp.arange(nt // step, dtype=jnp.int32)[None, :]))
    b = angle(f * jnp.arange(step, dtype=jnp.int32)[None, :])
    ca, sa = jnp.cos(a)[:, :, None], jnp.sin(a)[:, :, None]
    cb, sb = jnp.cos(b)[:, None, :], jnp.sin(b)[:, None, :]
    return (ca * cb - sa * sb).reshape(nf, nt), (sa * cb + ca * sb).reshape(nf, nt)


def _fnet_tables(l):
    t2 = jnp.concatenate(_cos_sin_table(l, l, l), axis=1).astype(BF16)
    k = np.arange(FN_GROUP)
    ang64 = 2.0 * np.pi * ((k[:, None] * k[None, :]) % FN_GROUP) / FN_GROUP
    eye = np.eye(FN_W // FN_GROUP)
    scale = 1.0 / math.sqrt(FN_GROUP * l)
    m1 = np.concatenate([np.kron(eye, np.cos(ang64)), -np.kron(eye, np.sin(ang64))], axis=1) * scale
    return jnp.asarray(m1, F32).astype(BF16), t2


def _rope_tables(l):
    rows = l // GRID_W
    row = jnp.repeat(jnp.arange(rows), GRID_W).astype(F32)
    col = jnp.tile(jnp.arange(GRID_W), rows).astype(F32)
    half = QK_DIM // 2
    inv = ROPE_BASE ** (-jnp.arange(0, half, 2, dtype=F32) / half)
    ang = jnp.stack([row[:, None] * inv, col[:, None] * inv], axis=1)
    cos = jnp.repeat(jnp.cos(ang)[:, :, None, :], 2, axis=2)
    sin = jnp.sin(ang)
    sin = jnp.stack([-sin, sin], axis=2)
    rep = lambda a: jnp.tile(a.reshape(l, QK_DIM), (1, 2 * N_HEADS))
    return rep(cos), rep(sin)


def _filter_features(l):
    pos = jnp.arange(l, dtype=F32)
    t = pos / max(l - 1, 1)
    w = 2.0 * math.pi * pos / l
    f = jnp.linspace(1e-4, HY_BANDS - 1, HY_BANDS, dtype=F32)
    feats = jnp.concatenate([t[:, None], jnp.cos(w[:, None] * f), -jnp.sin(w[:, None] * f)], axis=-1)
    feats = jnp.pad(feats, ((0, 0), (0, 64 - HY_EMB)))
    deltas = jnp.abs(jnp.linspace(HY_MIN_DECAY, HY_MAX_DECAY, HY_W, dtype=F32))
    dec = jnp.exp(-t[:, None] * deltas)
    return feats, jnp.tile(dec, (1, HY_ORDER))


def _hyena_filters(l, tabs, ffwd, w1, b1, freq, w2, b2, w3):
    feats, dec = tabs
    taps = _hyena_filter_taps(feats, jnp.pad(w1, ((0, 64 - HY_EMB), (0, 0))), b1[None], freq[None],
                              w2, b2[None], w3, dec)
    kf = _table_matmul(ffwd, taps, F32)
    half = HY_ORDER * HY_W
    kc = kf[:l, :half]
    nyq = kf[l, :half]
    ks = kf[l:, half:]
    n = 2.0 * l
    first = (jnp.arange(l) == 0)[:, None]
    wc = jnp.where(first, 1.0 / n, 2.0 / n)
    ka = kc * wc
    kb = jnp.where(first, 0.0, ks * (2.0 / n))
    kd = jnp.where(first, nyq[None, :] / n, kc * (2.0 / n))
    stack = jnp.stack([ka, kb, kb, kd], axis=0)
    return jnp.moveaxis(stack.reshape(4, l, HY_ORDER, HY_W), 2, 0)


def kernel(x, c, ctx, c_ctx, w_ada, b_ada, g_mix, g_ffn, w_in, hy_conv_w, hy_conv_b, hy_w1, hy_b1, hy_freq, hy_w2, hy_b2, hy_w3, hy_bias, g_q, g_k, lam, g_sub, w_hy, w_fn, w_at, w_out, peer_wq, peer_keys, peer_u, peer_v):
    bsz, seq, d = x.shape
    clen = ctx.shape[1]
    depth = w_ada.shape[0]

    cc = jnp.concatenate([c, c_ctx[None], jnp.zeros((MOD_ROWS - bsz - 1, d), F32)], axis=0)
    mods = _ada_mods(cc, w_ada, b_ada)
    lat_row = lambda bi: bi
    ctx_row = lambda bi: bsz

    rope = _rope_tables(seq)
    tabs = {n: dict(dft=_dft_tables(n), fnet=_fnet_tables(n), feat=_filter_features(n)) for n in (seq, clen)}
    w = N_HEADS * HEAD_W
    lane = np.arange(w)
    bd = jnp.asarray((lane[:, None] // QK_DIM == lane[None, :] // QK_DIM) / QK_DIM, F32).astype(BF16)

    xl, xc = x, ctx
    for l in range(depth):
        last = l == depth - 1
        lam_init = 0.8 - 0.6 * math.exp(-0.3 * l)
        wl = w_in[l]
        w_perm = jnp.concatenate([wl[:, 2560:], wl[:, :2560]], axis=1).astype(BF16)
        gq = jnp.tile(g_q[l].reshape(1, HEAD_W), (1, N_HEADS)) * (QK_DIM ** -0.5 / math.log(2.0))
        gk = jnp.tile(g_k[l].reshape(1, HEAD_W), (1, N_HEADS))
        gsub = g_sub[l][None]
        wts = [a[l].astype(BF16) for a in (w_hy, w_fn, w_at, w_out)]
        filt_args = (hy_w1[l], hy_b1[l], hy_freq[l], hy_w2[l], hy_b2[l], hy_w3[l])

        h_l = _in_projection(xl, g_mix[l][None], mods, l, lat_row, w_perm, 0, P_IN)
        h_c = _in_projection(xc, g_mix[l][None], mods, l, ctx_row, w_perm, COL_Q if last else 0, P_IN)
        q_l, k_l = _qk_prep(h_l, gq, gk, bd, rope)
        q_c, k_c = _qk_prep(h_c, gq, gk, bd, None)
        vblk = COL_V // HEAD_W
        k_all, v_all = _concat_kv(k_l, k_c, h_l, h_c)
        att_l = _diff_attention(q_l, [(k_all, v_all, 0)], lam[l], gsub, lam_init)
        ffwd, finv = tabs[seq]["dft"]
        kf = _hyena_filters(seq, tabs[seq]["feat"], ffwd, *filt_args)
        hyo_l = _hyena_mix(h_l, hy_conv_w[l], hy_conv_b[l][None], kf, hy_bias[l], ffwd, finv)
        fno_l = _fourier_mix(h_l, *tabs[seq]["fnet"])
        if not last:
            att_c = _diff_attention(q_c, [(k_c, h_c, vblk)], lam[l], gsub, lam_init)
            ffwd_c, finv_c = tabs[clen]["dft"]
            kf_c = _hyena_filters(clen, tabs[clen]["feat"], ffwd_c, *filt_args)
            hyo_c = _hyena_mix(h_c, hy_conv_w[l], hy_conv_b[l][None], kf_c, hy_bias[l], ffwd_c, finv_c)
            fno_c = _fourier_mix(h_c, *tabs[clen]["fnet"])
            xc = _merge(hyo_c, fno_c, att_c, h_c, xc, mods, l, ctx_row, *wts)
        xl = _merge(hyo_l, fno_l, att_l, h_l, xl, mods, l, lat_row, *wts)

        wq_t = peer_wq[l].astype(BF16)
        keys = _split(jnp.swapaxes(peer_keys[l].reshape(2 * PEER_HEADS, PEER_NKEYS, -1), 1, 2))
        u = _cast_bf16(peer_u[l])
        v_t = jnp.swapaxes(peer_v[l].reshape(-1, PEER_CHUNK, d), 1, 2).astype(BF16)
        if not last:
            xc = _peer(xc, g_ffn[l][None], mods, l, ctx_row, wq_t, keys, u, v_t)
        xl = _peer(xl, g_ffn[l][None], mods, l, lat_row, wq_t, keys, u, v_t)
    return xl
```

```python
import functools
import math

import jax
import jax.numpy as jnp
import numpy as np
from jax import lax
from jax.experimental import pallas as pl
from jax.experimental.pallas import tpu as pltpu

F32 = jnp.float32
BF16 = jnp.bfloat16

EPS = 1e-6
GRID_W = 64
ROPE_BASE = 10000.0
N_HEADS = 4
QK_DIM = 64
HEAD_W = 2 * QK_DIM
HY_W = 256
HY_ORDER = 2
HY_EMB = 33
HY_BANDS = (HY_EMB - 1) // 2
HY_MIN_DECAY = math.log(1e-2) / 1.5
HY_MAX_DECAY = math.log(1e-2) / 0.3
FN_GROUP = 64
FN_W = 256
PEER_HEADS = 8
PEER_NKEYS = 128
PEER_TOPK = 16
N_MOD = 6
MOD_ROWS = 16
NEG = -3.0e38
GATE_VREGS = 8
PEER_TOKENS = 256
PEER_CHUNK = 2048
DFT_ROWS = 1024

VMEM_LIMIT = 56 * 1024 * 1024

COL_GATE = 0
COL_HY = 3072
COL_FN = 3840
COL_Q = 4096
COL_K = 4608
COL_V = 5120
P_IN = 5632


def _cp(*sem):
    return pltpu.CompilerParams(dimension_semantics=sem, vmem_limit_bytes=VMEM_LIMIT)


def _dot(a, b):
    return jnp.dot(a, b, preferred_element_type=F32)


def _dot_nt(a, b):
    return lax.dot_general(a, b, (((1,), (1,)), ((), ())), preferred_element_type=F32)


def _split(a):
    hi = a.astype(BF16)
    lo = (a - hi.astype(F32)).astype(BF16)
    return hi, lo


def _modulated_norm(x, g, shift, scale):
    ms = jnp.mean(x * x, axis=-1, keepdims=True)
    y = x * lax.rsqrt(ms + EPS) * g
    return y * (1.0 + scale) + shift


def _mod_spec(layer, chunk, row_fn):
    def imap(*idx):
        return (layer, row_fn(*idx), chunk, 0, 0)
    return imap


def _ada_kernel(c_ref, w_ref, b_ref, o_ref):
    c = c_ref[...]
    a = c / (1.0 + jnp.exp(-c))
    ah, al = _split(a)
    wh, wl = _split(w_ref[...])
    o_ref[...] = _dot(ah, wh) + _dot(ah, wl) + _dot(al, wh) + b_ref[...]


def _ada_mods(cc, w_ada, b_ada):
    depth, d, n = w_ada.shape
    tn = 512
    out = pl.pallas_call(
        _ada_kernel,
        grid=(depth, n // tn),
        in_specs=[pl.BlockSpec((MOD_ROWS, d), lambda l, j: (0, 0)),
                  pl.BlockSpec((None, d, tn), lambda l, j: (l, 0, j)),
                  pl.BlockSpec((None, 1, tn), lambda l, j: (l, 0, j))],
        out_specs=pl.BlockSpec((None, MOD_ROWS, tn), lambda l, j: (l, 0, j)),
        out_shape=jax.ShapeDtypeStruct((depth, MOD_ROWS, n), F32),
        compiler_params=_cp("parallel", "parallel"),
        name="ada_mods",
    )(cc, w_ada, b_ada.reshape(depth, 1, n))
    return out.reshape(depth, MOD_ROWS, N_MOD, 1, d)


def _inproj_kernel(x_ref, g_ref, sh_ref, sc_ref, w_ref, o_ref, xn_ref):
    @pl.when(pl.program_id(2) == 0)
    def _():
        xn_ref[...] = _modulated_norm(x_ref[...], g_ref[...], sh_ref[...], sc_ref[...]).astype(BF16)
    o_ref[...] = _dot(xn_ref[...], w_ref[...]).astype(o_ref.dtype)


def _in_projection(x, g, mods, layer, row_fn, w, col_lo, col_hi):
    b, l, d = x.shape
    tm = min(l, 1024)
    wide = w.shape[1] // 2
    tn = wide if (col_lo % wide == 0 and (col_hi - col_lo) % wide == 0) else 512
    j0 = col_lo // tn
    nj = (col_hi - col_lo) // tn
    mrow = lambda bi, i, j: row_fn(bi)
    return pl.pallas_call(
        _inproj_kernel,
        grid=(b, l // tm, nj),
        in_specs=[pl.BlockSpec((None, tm, d), lambda bi, i, j: (bi, i, 0)),
                  pl.BlockSpec((1, d), lambda bi, i, j: (0, 0)),
                  pl.BlockSpec((None, None, None, 1, d), _mod_spec(layer, 0, mrow)),
                  pl.BlockSpec((None, None, None, 1, d), _mod_spec(layer, 1, mrow)),
                  pl.BlockSpec((d, tn), lambda bi, i, j: (0, j + j0))],
        out_specs=pl.BlockSpec((None, tm, tn), lambda bi, i, j: (bi, i, j + j0)),
        out_shape=jax.ShapeDtypeStruct((b, l, w.shape[1]), BF16),
        scratch_shapes=[pltpu.VMEM((tm, d), BF16)],
        compiler_params=_cp("parallel", "parallel", "arbitrary"),
        name="in_projection",
    )(x, g, mods, mods, w)


def _qkprep_kernel(*refs, rope):
    if rope:
        q_ref, k_ref, gq_ref, gk_ref, bd_ref, cos_ref, sin_ref, qo_ref, ko_ref = refs
    else:
        q_ref, k_ref, gq_ref, gk_ref, bd_ref, qo_ref, ko_ref = refs
    for src, g_ref, dst in ((q_ref, gq_ref, qo_ref), (k_ref, gk_ref, ko_ref)):
        x = src[...].astype(F32)
        hi, lo = _split(x * x)
        ms = _dot(hi, bd_ref[...]) + _dot(lo, bd_ref[...])
        y = x * lax.rsqrt(ms + EPS) * g_ref[...]
        if rope:
            w = y.shape[1]
            lane = lax.broadcasted_iota(jnp.int32, y.shape, 1)
            first = (lane % (QK_DIM // 2)) < (QK_DIM // 4)
            partner = jnp.where(first, pltpu.roll(y, w - QK_DIM // 4, 1), pltpu.roll(y, QK_DIM // 4, 1))
            y = y * cos_ref[...] + partner * sin_ref[...]
        dst[...] = y.astype(BF16)


def _qk_prep(h, gq, gk, bd, rope_tabs):
    b, l, _ = h.shape
    w = N_HEADS * HEAD_W
    tm = min(l, 1024)
    rope = rope_tabs is not None
    in_specs = [pl.BlockSpec((None, tm, w), lambda bi, i: (bi, i, COL_Q // w)),
                pl.BlockSpec((None, tm, w), lambda bi, i: (bi, i, COL_K // w)),
                pl.BlockSpec((1, w), lambda bi, i: (0, 0)),
                pl.BlockSpec((1, w), lambda bi, i: (0, 0)),
                pl.BlockSpec((w, w), lambda bi, i: (0, 0))]
    args = [h, h, gq, gk, bd]
    if rope:
        in_specs += [pl.BlockSpec((tm, w), lambda bi, i: (i, 0))] * 2
        args += list(rope_tabs)
    return pl.pallas_call(
        functools.partial(_qkprep_kernel, rope=rope),
        grid=(b, l // tm),
        in_specs=in_specs,
        out_specs=[pl.BlockSpec((None, tm, w), lambda bi, i: (bi, i, 0))] * 2,
        out_shape=[jax.ShapeDtypeStruct((b, l, w), BF16)] * 2,
        compiler_params=_cp("parallel", "parallel"),
        name="qk_prep",
    )(*args)


def _attn_kernel(*refs, n_src, lam_init):
    q_ref = refs[0]
    kv = refs[1:1 + 2 * n_src]
    lam_ref, gsub_ref, o_ref = refs[1 + 2 * n_src:]
    q = q_ref[...]
    tq = q.shape[0]
    lane = lax.broadcasted_iota(jnp.int32, q.shape, 1)
    zero = jnp.zeros_like(q)
    qq = jnp.concatenate([jnp.where(lane < QK_DIM, q, zero), jnp.where(lane >= QK_DIM, q, zero)], axis=0)
    scores = [_dot_nt(qq, kv[2 * i][...]) for i in range(n_src)]
    m = jnp.max(scores[0], axis=-1, keepdims=True)
    for s in scores[1:]:
        m = jnp.maximum(m, jnp.max(s, axis=-1, keepdims=True))
    z = jnp.zeros_like(m)
    acc = jnp.zeros((2 * tq, HEAD_W), F32)
    for i, s in enumerate(scores):
        e = jnp.exp2(s - m)
        z = z + jnp.sum(e, axis=-1, keepdims=True)
        acc = acc + _dot(e.astype(BF16), kv[2 * i + 1][...])
    o2 = acc / z
    lf = lam_ref[...]
    lam_val = (jnp.exp(jnp.sum(lf[0:1] * lf[1:2], axis=-1, keepdims=True))
               - jnp.exp(jnp.sum(lf[2:3] * lf[3:4], axis=-1, keepdims=True)) + lam_init)
    o = o2[:tq] - lam_val * o2[tq:]
    ms = jnp.mean(o * o, axis=-1, keepdims=True)
    o = o * lax.rsqrt(ms + EPS) * gsub_ref[...] * (1.0 - lam_init)
    o_ref[...] = o.astype(BF16)


def _concat_kernel(kl_ref, kc_ref, vl_ref, vc_ref, ko_ref, vo_ref):
    s = kl_ref.shape[0]
    ko_ref[:s, :] = kl_ref[...]
    ko_ref[s:, :] = kc_ref[...]
    vo_ref[:s, :] = vl_ref[...]
    vo_ref[s:, :] = vc_ref[...]


def _concat_kv(k_l, k_c, h_l, h_c):
    b, s, w = k_l.shape
    c = k_c.shape[1]
    vcol = COL_V // w
    rows = lambda n, blk: pl.BlockSpec((None, n, w), lambda bi: (bi, 0, blk))
    return pl.pallas_call(
        _concat_kernel,
        grid=(b,),
        in_specs=[rows(s, 0), rows(c, 0), rows(s, vcol), rows(c, vcol)],
        out_specs=[rows(s + c, 0)] * 2,
        out_shape=[jax.ShapeDtypeStruct((b, s + c, w), BF16)] * 2,
        compiler_params=_cp("parallel"),
        name="concat_kv",
    )(k_l, k_c, h_l, h_c)


def _diff_attention(q, sources, lam_l, gsub, lam_init):
    b, lq, w = q.shape
    tq = min(lq, 512)
    in_specs = [pl.BlockSpec((None, tq, HEAD_W), lambda bi, hi, i: (bi, i, hi))]
    args = [q]
    for k, varr, vblk in sources:
        lk = k.shape[1]
        in_specs.append(pl.BlockSpec((None, lk, HEAD_W), lambda bi, hi, i: (bi, 0, hi)))
        in_specs.append(pl.BlockSpec((None, lk, HEAD_W), lambda bi, hi, i, vblk=vblk: (bi, 0, vblk + hi)))
        args += [k, varr]
    in_specs += [pl.BlockSpec(lam_l.shape, lambda bi, hi, i: (0, 0)),
                 pl.BlockSpec((1, HEAD_W), lambda bi, hi, i: (0, 0))]
    args += [lam_l, gsub]
    return pl.pallas_call(
        functools.partial(_attn_kernel, n_src=len(sources), lam_init=lam_init),
        grid=(b, N_HEADS, lq // tq),
        in_specs=in_specs,
        out_specs=pl.BlockSpec((None, tq, HEAD_W), lambda bi, hi, i: (bi, i, hi)),
        out_shape=jax.ShapeDtypeStruct((b, lq, w), BF16),
        compiler_params=_cp("parallel", "parallel", "arbitrary"),
        name="diff_attention",
    )(*args)


def _sconv_kernel(h_ref, w_ref, b_ref, o_ref):
    x = h_ref[...].astype(F32)
    n = x.shape[0]
    row = lax.broadcasted_iota(jnp.int32, x.shape, 0)
    prev = jnp.where(row == 0, 0.0, pltpu.roll(x, 1, 0))
    nxt = jnp.where(row == n - 1, 0.0, pltpu.roll(x, n - 1, 0))
    o_ref[...] = prev * w_ref[0:1, :] + x * w_ref[1:2, :] + nxt * w_ref[2:3, :] + b_ref[...]


def _short_conv(h, w, bias):
    b, l, _ = h.shape
    return pl.pallas_call(
        _sconv_kernel,
        grid=(b, 3),
        in_specs=[pl.BlockSpec((None, l, HY_W), lambda bi, j: (bi, 0, COL_HY // HY_W + j)),
                  pl.BlockSpec((3, HY_W), lambda bi, j: (0, j)),
                  pl.BlockSpec((1, HY_W), lambda bi, j: (0, j))],
        out_specs=pl.BlockSpec((None, None, l, HY_W), lambda bi, j: (bi, j, 0, 0)),
        out_shape=jax.ShapeDtypeStruct((b, 3, l, HY_W), F32),
        compiler_params=_cp("parallel", "parallel"),
        name="short_conv",
    )(h, w, bias)


def _filter_kernel(f_ref, w1_ref, b1_ref, fr_ref, w2_ref, b2_ref, w3_ref, dec_ref, o_ref):
    def mm(a, w_ref_):
        ah, al = _split(a)
        wh, wl = _split(w_ref_[...])
        return _dot(ah, wh) + _dot(ah, wl) + _dot(al, wh)
    fr = fr_ref[...]
    h = jnp.sin(fr * (mm(f_ref[...], w1_ref) + b1_ref[...]))
    h = jnp.sin(fr * (mm(h, w2_ref) + b2_ref[...]))
    h = mm(h, w3_ref)
    half = h.shape[1] // 2
    dec = dec_ref[...]
    hf = h[:, :half] * dec
    hb = h[:, half:] * dec
    row = lax.broadcasted_iota(jnp.int32, hb.shape, 0)
    hb = jnp.where(row == 0, 0.0, hb)
    norm = jnp.sum(jnp.abs(hf) + jnp.abs(hb), axis=0, keepdims=True)
    o_ref[:, :half] = (hf + hb) / norm
    o_ref[:, half:] = (hf - hb) / norm


def _hyena_filter_taps(feats, w1, b1, freq, w2, b2, w3, dec):
    l = feats.shape[0]
    n = w3.shape[1]
    full = lambda a: pl.BlockSpec(a.shape, lambda i: (0,) * a.ndim)
    args = (feats, w1, b1, freq, w2, b2, w3, dec)
    return pl.pallas_call(
        _filter_kernel,
        grid=(1,),
        in_specs=[full(a) for a in args],
        out_specs=pl.BlockSpec((l, n), lambda i: (0, 0)),
        out_shape=jax.ShapeDtypeStruct((l, n), F32),
        compiler_params=_cp("arbitrary"),
        name="hyena_filter_taps",
    )(*args)


def _table_mm_kernel(t_ref, x_ref, o_ref):
    o_ref[...] = _dot(t_ref[...], x_ref[...].astype(BF16)).astype(o_ref.dtype)


def _table_matmul(table, x, out_dtype):
    m, k = table.shape
    n = x.shape[1]
    tm = min(m, 512)
    return pl.pallas_call(
        _table_mm_kernel,
        grid=(m // tm,),
        in_specs=[pl.BlockSpec((tm, k), lambda i: (i, 0)),
                  pl.BlockSpec((k, n), lambda i: (0, 0))],
        out_specs=pl.BlockSpec((tm, n), lambda i: (i, 0)),
        out_shape=jax.ShapeDtypeStruct((m, n), out_dtype),
        compiler_params=_cp("parallel"),
        name="table_matmul",
    )(table, x)


def _dftmul_kernel(fc_ref, fs_ref, z_ref, k_ref, p_ref):
    z = z_ref[...].astype(BF16)
    zc = _dot(fc_ref[...], z)
    zs = _dot(fs_ref[...], z)
    p_ref[0] = (zc * k_ref[0] - zs * k_ref[1]).astype(BF16)
    p_ref[1] = (zc * k_ref[2] + zs * k_ref[3]).astype(BF16)


def _dft_multiply(ffwd, z, z_spec, kf):
    l = ffwd.shape[1]
    b = z.shape[0]
    tf = min(l, DFT_ROWS)
    nf = l // tf
    out = pl.pallas_call(
        _dftmul_kernel,
        grid=(nf, b),
        in_specs=[pl.BlockSpec((tf, l), lambda i, bi: (i, 0)),
                  pl.BlockSpec((tf, l), lambda i, bi: (i + nf, 0)),
                  z_spec,
                  pl.BlockSpec((4, tf, HY_W), lambda i, bi: (0, i, 0))],
        out_specs=pl.BlockSpec((None, 2, tf, HY_W), lambda i, bi: (bi, 0, i, 0)),
        out_shape=jax.ShapeDtypeStruct((b, 2, l, HY_W), BF16),
        compiler_params=_cp("parallel", "arbitrary"),
        name="dft_multiply",
    )(ffwd, ffwd, z, kf)
    return out.reshape(b, 2 * l, HY_W)


def _idft_gate_kernel(fi_ref, p_ref, g_ref, z_ref, b_ref, o_ref):
    conv = _dot(fi_ref[...], p_ref[...])
    o_ref[...] = (g_ref[...] * (conv + b_ref[...] * z_ref[...])).astype(o_ref.dtype)


def _idft_gate(finv, p, gate, gate_spec, z, z_spec, bias, out_dtype):
    l = finv.shape[0]
    b = p.shape[0]
    tt = min(l, DFT_ROWS)
    return pl.pallas_call(
        _idft_gate_kernel,
        grid=(l // tt, b),
        in_specs=[pl.BlockSpec((tt, 2 * l), lambda i, bi: (i, 0)),
                  pl.BlockSpec((None, 2 * l, HY_W), lambda i, bi: (bi, 0, 0)),
                  gate_spec, z_spec,
                  pl.BlockSpec((1, HY_W), lambda i, bi: (0, 0))],
        out_specs=pl.BlockSpec((None, tt, HY_W), lambda i, bi: (bi, i, 0)),
        out_shape=jax.ShapeDtypeStruct((b, l, HY_W), out_dtype),
        compiler_params=_cp("parallel", "arbitrary"),
        name="idft_gate",
    )(finv, p, gate, z, bias)


def _hyena_mix(h, conv_w, conv_b, kf, hy_bias, ffwd, finv):
    b, l, _ = h.shape
    tt = min(l, DFT_ROWS)
    u = _short_conv(h, conv_w, conv_b)
    part = lambda j, rows: pl.BlockSpec((None, None, rows, HY_W),
                                        lambda i, bi, j=j: (bi, j, i if rows != l else 0, 0))
    p = _dft_multiply(ffwd, u, part(0, l), kf[0])
    z1 = _idft_gate(finv, p, u, part(1, tt), u, part(0, tt), hy_bias[0:1], F32)
    p = _dft_multiply(ffwd, z1, pl.BlockSpec((None, l, HY_W), lambda i, bi: (bi, 0, 0)), kf[1])
    return _idft_gate(finv, p, u, part(2, tt), z1,
                      pl.BlockSpec((None, tt, HY_W), lambda i, bi: (bi, i, 0)), hy_bias[1:2], BF16)


def _fn1_kernel(z_ref, m_ref, o_ref):
    r = _dot(z_ref[...], m_ref[...])
    half = r.shape[1] // 2
    o_ref[0] = r[:, :half].astype(BF16)
    o_ref[1] = r[:, half:].astype(BF16)


def _fourier_mix(h, m1, t2):
    b, l, _ = h.shape
    tm = min(l, DFT_ROWS)
    zz = pl.pallas_call(
        _fn1_kernel,
        grid=(b, l // tm),
        in_specs=[pl.BlockSpec((None, tm, FN_W), lambda bi, i: (bi, i, COL_FN // FN_W)),
                  pl.BlockSpec((FN_W, 2 * FN_W), lambda bi, i: (0, 0))],
        out_specs=pl.BlockSpec((None, 2, tm, FN_W), lambda bi, i: (bi, 0, i, 0)),
        out_shape=jax.ShapeDtypeStruct((b, 2, l, FN_W), BF16),
        compiler_params=_cp("parallel", "parallel"),
        name="fnet_channels",
    )(h, m1).reshape(b, 2 * l, FN_W)
    return pl.pallas_call(
        _table_mm_kernel,
        grid=(l // tm, b),
        in_specs=[pl.BlockSpec((tm, 2 * l), lambda i, bi: (i, 0)),
                  pl.BlockSpec((None, 2 * l, FN_W), lambda i, bi: (bi, 0, 0))],
        out_specs=pl.BlockSpec((None, tm, FN_W), lambda i, bi: (bi, i, 0)),
        out_shape=jax.ShapeDtypeStruct((b, l, FN_W), BF16),
        compiler_params=_cp("parallel", "arbitrary"),
        name="fnet_positions",
    )(t2, zz)


def _merge_kernel(hy_ref, fn_ref, at_ref, g_ref, x_ref, gm_ref, why_ref, wfn_ref, wat_ref, wout_ref, o_ref):
    d = x_ref.shape[-1]
    g = 1.0 / (1.0 + jnp.exp(-g_ref[...].astype(F32)))
    y = (g[:, :d] * _dot(hy_ref[...], why_ref[...])
         + g[:, d:2 * d] * _dot(fn_ref[...], wfn_ref[...])
         + g[:, 2 * d:] * _dot(at_ref[...], wat_ref[...]))
    mix = _dot(y.astype(BF16), wout_ref[...])
    o_ref[...] = x_ref[...] + gm_ref[...] * mix


def _merge(hyo, fno, att, h, x, mods, layer, row_fn, w_hy, w_fn, w_at, w_out):
    b, l, d = x.shape
    tm = min(l, 1024)
    full = lambda a: pl.BlockSpec(a.shape, lambda bi, i: (0,) * a.ndim)
    tok = lambda wdt, blk=0: pl.BlockSpec((None, tm, wdt), lambda bi, i: (bi, i, blk))
    return pl.pallas_call(
        _merge_kernel,
        grid=(b, l // tm),
        in_specs=[tok(HY_W), tok(FN_W), tok(N_HEADS * HEAD_W), tok(3 * d, COL_GATE), tok(d),
                  pl.BlockSpec((None, None, None, 1, d), _mod_spec(layer, 2, lambda bi, i: row_fn(bi))),
                  full(w_hy), full(w_fn), full(w_at), full(w_out)],
        out_specs=tok(d),
        out_shape=jax.ShapeDtypeStruct((b, l, d), F32),
        compiler_params=_cp("parallel", "parallel"),
        name="merge_residual",
    )(hyo, fno, att, h, x, mods, w_hy, w_fn, w_at, w_out)


def _peer_q_kernel(x_ref, g_ref, sh_ref, sc_ref, wq_ref, kh_ref, kl_ref, s_ref, nt_ref):
    n = _modulated_norm(x_ref[...], g_ref[...], sh_ref[...], sc_ref[...])
    nt_ref[...] = n.T.astype(BF16)
    q = _dot(n.astype(BF16), wq_ref[...])
    dq = kh_ref.shape[1]
    for hp in range(kh_ref.shape[0]):
        qh, ql = _split(q[:, hp * dq:(hp + 1) * dq])
        kh = kh_ref[hp]
        s = _dot(qh, kh) + _dot(ql, kh) + _dot(qh, kl_ref[hp])
        s_ref[hp] = s.T


def _peer_scores(x, g, mods, layer, row_fn, wq, keys_t):
    b, l, d = x.shape
    tt = min(l, 512)
    nt = l // tt
    nhp = 2 * PEER_HEADS
    mrow = lambda bi, i: row_fn(bi)
    full = lambda a: pl.BlockSpec(a.shape, lambda bi, i: (0,) * a.ndim)
    return pl.pallas_call(
        _peer_q_kernel,
        grid=(b, nt),
        in_specs=[pl.BlockSpec((None, tt, d), lambda bi, i: (bi, i, 0)),
                  pl.BlockSpec((1, d), lambda bi, i: (0, 0)),
                  pl.BlockSpec((None, None, None, 1, d), _mod_spec(layer, 3, mrow)),
                  pl.BlockSpec((None, None, None, 1, d), _mod_spec(layer, 4, mrow)),
                  full(wq), full(keys_t[0]), full(keys_t[1])],
        out_specs=[pl.BlockSpec((nhp, PEER_NKEYS, tt), lambda bi, i: (0, 0, bi * nt + i)),
                   pl.BlockSpec((d, tt), lambda bi, i: (0, bi * nt + i))],
        out_shape=[jax.ShapeDtypeStruct((nhp, PEER_NKEYS, b * l), F32),
                   jax.ShapeDtypeStruct((d, b * l), BF16)],
        compiler_params=_cp("parallel", "parallel"),
        name="peer_scores",
    )(x, g, mods, mods, wq, keys_t[0], keys_t[1])


def _merge_sort_pairs(n):
    pairs = []
    p = 1
    while p < n:
        k = p
        while k >= 1:
            for j in range(k % p, n - k, 2 * k):
                for i in range(min(k, n - j - k)):
                    if (i + j) // (2 * p) == (i + j + k) // (2 * p):
                        pairs.append((i + j, i + j + k))
            k //= 2
        p *= 2
    return pairs


def _top_rows_grouped(s, k):
    groups = s.shape[0] // 8
    v = [s[g * 8:(g + 1) * 8, :] for g in range(groups)]
    for i, j in _merge_sort_pairs(groups):
        v[i], v[j] = jnp.maximum(v[i], v[j]), jnp.minimum(v[i], v[j])
    v.append(jnp.full_like(v[0], NEG))
    vals = []
    for r in range(k):
        m = jnp.max(v[0], axis=0, keepdims=True)
        vals.append(m)
        taken = v[0] == m
        for d in range(min(groups, k - 1 - r)):
            v[d] = jnp.where(taken, v[d + 1], v[d])
    return vals


def _peer_expert_kernel(s_ref, nt_ref, u_ref, vt_ref, x_ref, gm_ref, o_ref,
                        tau_ref, l1_ref, l2_ref, l1c_ref, hid0_ref, hid1_ref, p0_ref, p1_ref, acc_ref):
    s = pl.program_id(1)
    nc = pl.num_programs(1) - 2
    k = PEER_TOPK
    tt = nt_ref.shape[1]
    n_i = u_ref.shape[0] // PEER_NKEYS
    piece = GATE_VREGS * 8 * 128 // tt

    def hidden(hid_ref):
        hid_ref[...] = _dot(u_ref[...], nt_ref[...])

    def stages(chunk, hid_w, hid_r, p_w, p_r):
        first = pl.multiple_of(chunk * n_i, n_i)
        for h in range(PEER_HEADS):
            l1c_ref[h] = l1_ref[h, pl.ds(first, n_i), :]
        hidden(hid_w)
        acc_ref[...] += _dot(vt_ref[...], p_r[...])
        for ii in range(n_i):
            l1_row = [jnp.broadcast_to(l1c_ref[h, ii:ii + 1, :], (8, tt))[None] for h in range(PEER_HEADS)]
            for jp in range(PEER_NKEYS // piece):
                js = slice(jp * piece, (jp + 1) * piece)
                gate = jnp.zeros((piece // 8, 8, tt), F32)
                for h in range(PEER_HEADS):
                    logw = l1_row[h] + l2_ref[h, js, :].reshape(piece // 8, 8, tt)
                    gate = gate + jnp.where(logw >= tau_ref[h][None], jnp.exp2(logw), 0.0)
                rows = slice(ii * PEER_NKEYS + jp * piece, ii * PEER_NKEYS + (jp + 1) * piece)
                hid = hid_r[rows, :]
                act = hid * (1.0 + lax.erf(hid * (2.0 ** -0.5)))
                p_w[rows, :] = (gate.reshape(piece, tt) * act).astype(BF16)

    @pl.when(s == 0)
    def _():
        acc_ref[...] = jnp.zeros_like(acc_ref)
        p1_ref[...] = jnp.zeros_like(p1_ref)
        hidden(hid0_ref)
        width = min(tt, 256)
        parts = tt // width

        def select(it, carry):
            h = it // parts
            cols = pl.ds(pl.multiple_of((it % parts) * width, width), width)
            s1 = s_ref[2 * h, :, cols]
            s2 = s_ref[2 * h + 1, :, cols]
            a = _top_rows_grouped(s1, k + 1)
            b = _top_rows_grouped(s2, k + 1)
            cand = [a[i] + b[j] for i in range(k + 1) for j in range((k + 1) // (i + 1))]
            pad = (-len(cand)) % 64
            top = _top_rows_grouped(jnp.concatenate(cand + [jnp.full_like(a[0], NEG)] * pad, axis=0), k + 1)
            z = jnp.zeros_like(top[0])
            for t in top[:k]:
                z = z + jnp.exp(t - top[0])
            shift = top[0] + jnp.log(z)
            log2e = 1.0 / math.log(2.0)
            l1_ref[h, :, cols] = s1 * log2e
            l2_ref[h, :, cols] = (s2 - shift) * log2e - 1.0
            tau = (0.5 * (top[k - 1] + top[k]) - shift) * log2e - 1.0
            tau_ref[h, :, cols] = jnp.broadcast_to(tau, (8, width))
            return carry

        lax.fori_loop(0, PEER_HEADS * parts, select, 0)

    @pl.when((s >= 1) & (s <= nc) & (s % 2 == 1))
    def _():
        stages(s - 1, hid1_ref, hid0_ref, p0_ref, p1_ref)

    @pl.when((s >= 1) & (s <= nc) & (s % 2 == 0))
    def _():
        stages(s - 1, hid0_ref, hid1_ref, p1_ref, p0_ref)

    @pl.when(s == nc + 1)
    def _():
        last = p0_ref if (nc - 1) % 2 == 0 else p1_ref
        out_t = acc_ref[...] + _dot(vt_ref[...], last[...])
        o_ref[...] = x_ref[...] + gm_ref[...] * out_t.T


def _peer_experts(scores, n_t, u, v_t, x, mods, layer, row_fn):
    b, l, d = x.shape
    tt = min(l, PEER_TOKENS)
    nt = l // tt
    nc, _, ec = v_t.shape
    nhp = scores.shape[0]
    return pl.pallas_call(
        _peer_expert_kernel,
        grid=(b * nt, nc + 2),
        in_specs=[pl.BlockSpec((nhp, PEER_NKEYS, tt), lambda t, s: (0, 0, t)),
                  pl.BlockSpec((d, tt), lambda t, s: (0, t)),
                  pl.BlockSpec((ec, d), lambda t, s: (jnp.minimum(s, nc - 1), 0)),
                  pl.BlockSpec((None, d, ec), lambda t, s: (jnp.clip(s - 2, 0, nc - 1), 0, 0)),
                  pl.BlockSpec((None, tt, d), lambda t, s: (t // nt, t % nt, 0)),
                  pl.BlockSpec((None, None, None, 1, d), _mod_spec(layer, 5, lambda t, s: row_fn(t // nt)))],
        out_specs=pl.BlockSpec((None, tt, d), lambda t, s: (t // nt, t % nt, 0)),
        out_shape=jax.ShapeDtypeStruct((b, l, d), F32),
        scratch_shapes=[pltpu.VMEM((PEER_HEADS, 8, tt), F32),
                        pltpu.VMEM((PEER_HEADS, PEER_NKEYS, tt), F32),
                        pltpu.VMEM((PEER_HEADS, PEER_NKEYS, tt), F32),
                        pltpu.VMEM((PEER_HEADS, ec // PEER_NKEYS, tt), F32),
                        pltpu.VMEM((ec, tt), F32),
                        pltpu.VMEM((ec, tt), F32),
                        pltpu.VMEM((ec, tt), BF16),
                        pltpu.VMEM((ec, tt), BF16),
                        pltpu.VMEM((d, tt), F32)],
        compiler_params=_cp("parallel", "arbitrary"),
        name="peer_experts",
    )(scores, n_t, u, v_t, x, mods)


def _cast_kernel(a_ref, o_ref):
    o_ref[...] = a_ref[...].astype(o_ref.dtype)


def _cast_bf16(a, layer):
    _, rows, d = a.shape
    return pl.pallas_call(
        _cast_kernel,
        grid=(rows // PEER_CHUNK,),
        in_specs=[pl.BlockSpec((None, PEER_CHUNK, d), lambda i: (layer, i, 0))],
        out_specs=pl.BlockSpec((PEER_CHUNK, d), lambda i: (i, 0)),
        out_shape=jax.ShapeDtypeStruct((rows, d), BF16),
        compiler_params=_cp("parallel"),
        name="cast_bf16",
    )(a)


def _transpose_cast_kernel(a_ref, o_ref):
    o_ref[...] = a_ref[...].T.astype(o_ref.dtype)


def _chunk_transpose_bf16(a, layer):
    _, rows, d = a.shape
    nc = rows // PEER_CHUNK
    return pl.pallas_call(
        _transpose_cast_kernel,
        grid=(nc,),
        in_specs=[pl.BlockSpec((None, PEER_CHUNK, d), lambda i: (layer, i, 0))],
        out_specs=pl.BlockSpec((None, d, PEER_CHUNK), lambda i: (i, 0, 0)),
        out_shape=jax.ShapeDtypeStruct((nc, d, PEER_CHUNK), BF16),
        compiler_params=_cp("parallel"),
        name="chunk_transpose_bf16",
    )(a)


def _permute_in_weights(w, layer):
    _, d, n = w.shape
    tn = 512
    nb = n // tn
    shift = (n - 3 * d) // tn
    return pl.pallas_call(
        _cast_kernel,
        grid=(nb,),
        in_specs=[pl.BlockSpec((None, d, tn), lambda j: (layer, 0, (j + shift) % nb))],
        out_specs=pl.BlockSpec((d, tn), lambda j: (0, j)),
        out_shape=jax.ShapeDtypeStruct((d, n), BF16),
        compiler_params=_cp("parallel"),
        name="permute_in_weights",
    )(w)


def _peer(x, g, mods, layer, row_fn, wq_t, keys, u, v_t):
    scores, n_t = _peer_scores(x, g, mods, layer, row_fn, wq_t, keys)
    return _peer_experts(scores, n_t, u, v_t, x, mods, layer, row_fn)


def _dft_tables(l):
    cos, sin = _cos_sin_table(l, l, 2 * l)
    t = jnp.arange(l, dtype=jnp.int32)[None, :]
    first = (jnp.arange(l) == 0)[:, None]
    sin = jnp.where(first, jnp.where(t % 2 == 0, 1.0, -1.0), sin)
    ffwd = jnp.concatenate([cos, sin], axis=0).astype(BF16)
    return ffwd, ffwd.T


def _cos_sin_table(nf, nt, n):
    step = 64
    f = jnp.arange(nf, dtype=jnp.int32)[:, None]
    angle = lambda prod: (2.0 * math.pi) * ((prod % n).astype(F32) / n)
    a = angle(f * (step * jnp.arange(nt // step, dtype=jnp.int32)[None, :]))
    b = angle(f * jnp.arange(step, dtype=jnp.int32)[None, :])
    ca, sa = jnp.cos(a)[:, :, None], jnp.sin(a)[:, :, None]
    cb, sb = jnp.cos(b)[:, None, :], jnp.sin(b)[:, None, :]
    return (ca * cb - sa * sb).reshape(nf, nt), (sa * cb + ca * sb).reshape(nf, nt)


def _fnet_tables(l):
    t2 = jnp.concatenate(_cos_sin_table(l, l, l), axis=1).astype(BF16)
    k = np.arange(FN_GROUP)
    ang64 = 2.0 * np.pi * ((k[:, None] * k[None, :]) % FN_GROUP) / FN_GROUP
    eye = np.eye(FN_W // FN_GROUP)
    scale = 1.0 / math.sqrt(FN_GROUP * l)
    m1 = np.concatenate([np.kron(eye, np.cos(ang64)), -np.kron(eye, np.sin(ang64))], axis=1) * scale
    return jnp.asarray(m1, F32).astype(BF16), t2


def _rope_tables(l):
    rows = l // GRID_W
    row = jnp.repeat(jnp.arange(rows), GRID_W).astype(F32)
    col = jnp.tile(jnp.arange(GRID_W), rows).astype(F32)
    half = QK_DIM // 2
    inv = ROPE_BASE ** (-jnp.arange(0, half, 2, dtype=F32) / half)
    ang = jnp.stack([row[:, None] * inv, col[:, None] * inv], axis=1)
    cos = jnp.repeat(jnp.cos(ang)[:, :, None, :], 2, axis=2)
    sin = jnp.sin(ang)
    sin = jnp.stack([-sin, sin], axis=2)
    rep = lambda a: jnp.tile(a.reshape(l, QK_DIM), (1, 2 * N_HEADS))
    return rep(cos), rep(sin)


def _filter_features(l):
    pos = jnp.arange(l, dtype=F32)
    t = pos / max(l - 1, 1)
    w = 2.0 * math.pi * pos / l
    f = jnp.linspace(1e-4, HY_BANDS - 1, HY_BANDS, dtype=F32)
    feats = jnp.concatenate([t[:, None], jnp.cos(w[:, None] * f), -jnp.sin(w[:, None] * f)], axis=-1)
    feats = jnp.pad(feats, ((0, 0), (0, 64 - HY_EMB)))
    deltas = jnp.abs(jnp.linspace(HY_MIN_DECAY, HY_MAX_DECAY, HY_W, dtype=F32))
    dec = jnp.exp(-t[:, None] * deltas)
    return feats, jnp.tile(dec, (1, HY_ORDER))


def _hyena_filters(l, tabs, ffwd, w1, b1, freq, w2, b2, w3):
    feats, dec = tabs
    taps = _hyena_filter_taps(feats, jnp.pad(w1, ((0, 64 - HY_EMB), (0, 0))), b1[None], freq[None],
                              w2, b2[None], w3, dec)
    kf = _table_matmul(ffwd, taps, F32)
    half = HY_ORDER * HY_W
    kc = kf[:l, :half]
    nyq = kf[l, :half]
    ks = kf[l:, half:]
    n = 2.0 * l
    first = (jnp.arange(l) == 0)[:, None]
    wc = jnp.where(first, 1.0 / n, 2.0 / n)
    ka = kc * wc
    kb = jnp.where(first, 0.0, ks * (2.0 / n))
    kd = jnp.where(first, nyq[None, :] / n, kc * (2.0 / n))
    stack = jnp.stack([ka, kb, kb, kd], axis=0)
    return jnp.moveaxis(stack.reshape(4, l, HY_ORDER, HY_W), 2, 0)


def kernel(x, c, ctx, c_ctx, w_ada, b_ada, g_mix, g_ffn, w_in, hy_conv_w, hy_conv_b, hy_w1, hy_b1, hy_freq, hy_w2, hy_b2, hy_w3, hy_bias, g_q, g_k, lam, g_sub, w_hy, w_fn, w_at, w_out, peer_wq, peer_keys, peer_u, peer_v):
    bsz, seq, d = x.shape
    clen = ctx.shape[1]
    depth = w_ada.shape[0]

    cc = jnp.concatenate([c, c_ctx[None], jnp.zeros((MOD_ROWS - bsz - 1, d), F32)], axis=0)
    mods = _ada_mods(cc, w_ada, b_ada)
    lat_row = lambda bi: bi
    ctx_row = lambda bi: bsz

    rope = _rope_tables(seq)
    tabs = {n: dict(dft=_dft_tables(n), fnet=_fnet_tables(n), feat=_filter_features(n)) for n in (seq, clen)}
    w = N_HEADS * HEAD_W
    lane = np.arange(w)
    bd = jnp.asarray((lane[:, None] // QK_DIM == lane[None, :] // QK_DIM) / QK_DIM, F32).astype(BF16)

    xl, xc = x, ctx
    for l in range(depth):
        last = l == depth - 1
        lam_init = 0.8 - 0.6 * math.exp(-0.3 * l)
        w_perm = _permute_in_weights(w_in, l)
        gq = jnp.tile(g_q[l].reshape(1, HEAD_W), (1, N_HEADS)) * (QK_DIM ** -0.5 / math.log(2.0))
        gk = jnp.tile(g_k[l].reshape(1, HEAD_W), (1, N_HEADS))
        gsub = g_sub[l][None]
        wts = [a[l].astype(BF16) for a in (w_hy, w_fn, w_at, w_out)]
        filt_args = (hy_w1[l], hy_b1[l], hy_freq[l], hy_w2[l], hy_b2[l], hy_w3[l])

        h_l = _in_projection(xl, g_mix[l][None], mods, l, lat_row, w_perm, 0, P_IN)
        h_c = _in_projection(xc, g_mix[l][None], mods, l, ctx_row, w_perm, COL_Q if last else 0, P_IN)
        q_l, k_l = _qk_prep(h_l, gq, gk, bd, rope)
        q_c, k_c = _qk_prep(h_c, gq, gk, bd, None)
        vblk = COL_V // HEAD_W
        k_all, v_all = _concat_kv(k_l, k_c, h_l, h_c)
        att_l = _diff_attention(q_l, [(k_all, v_all, 0)], lam[l], gsub, lam_init)
        ffwd, finv = tabs[seq]["dft"]
        kf = _hyena_filters(seq, tabs[seq]["feat"], ffwd, *filt_args)
        hyo_l = _hyena_mix(h_l, hy_conv_w[l], hy_conv_b[l][None], kf, hy_bias[l], ffwd, finv)
        fno_l = _fourier_mix(h_l, *tabs[seq]["fnet"])
        if not last:
            att_c = _diff_attention(q_c, [(k_c, h_c, vblk)], lam[l], gsub, lam_init)
            ffwd_c, finv_c = tabs[clen]["dft"]
            kf_c = _hyena_filters(clen, tabs[clen]["feat"], ffwd_c, *filt_args)
            hyo_c = _hyena_mix(h_c, hy_conv_w[l], hy_conv_b[l][None], kf_c, hy_bias[l], ffwd_c, finv_c)
            fno_c = _fourier_mix(h_c, *tabs[clen]["fnet"])
            xc = _merge(hyo_c, fno_c, att_c, h_c, xc, mods, l, ctx_row, *wts)
        xl = _merge(hyo_l, fno_l, att_l, h_l, xl, mods, l, lat_row, *wts)

        wq_t = peer_wq[l].astype(BF16)
        keys = _split(jnp.swapaxes(peer_keys[l].reshape(2 * PEER_HEADS, PEER_NKEYS, -1), 1, 2))
        u = _cast_bf16(peer_u, l)
        v_t = _chunk_transpose_bf16(peer_v, l)
        if not last:
            xc = _peer(xc, g_ffn[l][None], mods, l, ctx_row, wq_t, keys, u, v_t)
        xl = _peer(xl, g_ffn[l][None], mods, l, lat_row, wq_t, keys, u, v_t)
    return xl
```

```python
import functools
import math

import jax
import jax.numpy as jnp
import numpy as np
from jax import lax
from jax.experimental import pallas as pl
from jax.experimental.pallas import tpu as pltpu

F32 = jnp.float32
BF16 = jnp.bfloat16

EPS = 1e-6
GRID_W = 64
ROPE_BASE = 10000.0
N_HEADS = 4
QK_DIM = 64
HEAD_W = 2 * QK_DIM
HY_W = 256
HY_ORDER = 2
HY_EMB = 33
HY_BANDS = (HY_EMB - 1) // 2
HY_MIN_DECAY = math.log(1e-2) / 1.5
HY_MAX_DECAY = math.log(1e-2) / 0.3
FN_GROUP = 64
FN_W = 256
PEER_HEADS = 8
PEER_NKEYS = 128
PEER_TOPK = 16
N_MOD = 6
MOD_ROWS = 16
NEG = -3.0e38
GATE_VREGS = 8
PEER_TOKENS = 256
PEER_CHUNK = 2048
DFT_ROWS = 1024

VMEM_LIMIT = 56 * 1024 * 1024

COL_GATE = 0
COL_HY = 3072
COL_FN = 3840
COL_Q = 4096
COL_K = 4608
COL_V = 5120
P_IN = 5632


def _cp(*sem):
    return pltpu.CompilerParams(dimension_semantics=sem, vmem_limit_bytes=VMEM_LIMIT)


def _dot(a, b):
    return jnp.dot(a, b, preferred_element_type=F32)


def _dot_nt(a, b):
    return lax.dot_general(a, b, (((1,), (1,)), ((), ())), preferred_element_type=F32)


def _split(a):
    hi = a.astype(BF16)
    lo = (a - hi.astype(F32)).astype(BF16)
    return hi, lo


def _modulated_norm(x, g, shift, scale):
    ms = jnp.mean(x * x, axis=-1, keepdims=True)
    y = x * lax.rsqrt(ms + EPS) * g
    return y * (1.0 + scale) + shift


def _mod_spec(layer, chunk, row_fn):
    def imap(*idx):
        return (layer, row_fn(*idx), chunk, 0, 0)
    return imap


def _ada_kernel(c_ref, w_ref, b_ref, o_ref):
    c = c_ref[...]
    a = c / (1.0 + jnp.exp(-c))
    ah, al = _split(a)
    wh, wl = _split(w_ref[...])
    o_ref[...] = _dot(ah, wh) + _dot(ah, wl) + _dot(al, wh) + b_ref[...]


def _ada_mods(cc, w_ada, b_ada):
    depth, d, n = w_ada.shape
    tn = 512
    out = pl.pallas_call(
        _ada_kernel,
        grid=(depth, n // tn),
        in_specs=[pl.BlockSpec((MOD_ROWS, d), lambda l, j: (0, 0)),
                  pl.BlockSpec((None, d, tn), lambda l, j: (l, 0, j)),
                  pl.BlockSpec((None, 1, tn), lambda l, j: (l, 0, j))],
        out_specs=pl.BlockSpec((None, MOD_ROWS, tn), lambda l, j: (l, 0, j)),
        out_shape=jax.ShapeDtypeStruct((depth, MOD_ROWS, n), F32),
        compiler_params=_cp("parallel", "parallel"),
        name="ada_mods",
    )(cc, w_ada, b_ada.reshape(depth, 1, n))
    return out.reshape(depth, MOD_ROWS, N_MOD, 1, d)


def _inproj_kernel(x_ref, g_ref, sh_ref, sc_ref, w_ref, o_ref, xn_ref):
    @pl.when(pl.program_id(2) == 0)
    def _():
        xn_ref[...] = _modulated_norm(x_ref[...], g_ref[...], sh_ref[...], sc_ref[...]).astype(BF16)
    o_ref[...] = _dot(xn_ref[...], w_ref[...]).astype(o_ref.dtype)


def _in_projection(x, g, mods, layer, row_fn, w, col_lo, col_hi):
    b, l, d = x.shape
    tm = min(l, 1024)
    wide = w.shape[1] // 2
    tn = wide if (col_lo % wide == 0 and (col_hi - col_lo) % wide == 0) else 512
    j0 = col_lo // tn
    nj = (col_hi - col_lo) // tn
    mrow = lambda bi, i, j: row_fn(bi)
    return pl.pallas_call(
        _inproj_kernel,
        grid=(b, l // tm, nj),
        in_specs=[pl.BlockSpec((None, tm, d), lambda bi, i, j: (bi, i, 0)),
                  pl.BlockSpec((1, d), lambda bi, i, j: (0, 0)),
                  pl.BlockSpec((None, None, None, 1, d), _mod_spec(layer, 0, mrow)),
                  pl.BlockSpec((None, None, None, 1, d), _mod_spec(layer, 1, mrow)),
                  pl.BlockSpec((d, tn), lambda bi, i, j: (0, j + j0))],
        out_specs=pl.BlockSpec((None, tm, tn), lambda bi, i, j: (bi, i, j + j0)),
        out_shape=jax.ShapeDtypeStruct((b, l, w.shape[1]), BF16),
        scratch_shapes=[pltpu.VMEM((tm, d), BF16)],
        compiler_params=_cp("parallel", "parallel", "arbitrary"),
        name="in_projection",
    )(x, g, mods, mods, w)


def _qkprep_kernel(*refs, rope):
    if rope:
        q_ref, k_ref, gq_ref, gk_ref, bd_ref, cos_ref, sin_ref, qo_ref, ko_ref = refs
    else:
        q_ref, k_ref, gq_ref, gk_ref, bd_ref, qo_ref, ko_ref = refs
    for src, g_ref, dst in ((q_ref, gq_ref, qo_ref), (k_ref, gk_ref, ko_ref)):
        x = src[...].astype(F32)
        hi, lo = _split(x * x)
        ms = _dot(hi, bd_ref[...]) + _dot(lo, bd_ref[...])
        y = x * lax.rsqrt(ms + EPS) * g_ref[...]
        if rope:
            w = y.shape[1]
            lane = lax.broadcasted_iota(jnp.int32, y.shape, 1)
            first = (lane % (QK_DIM // 2)) < (QK_DIM // 4)
            partner = jnp.where(first, pltpu.roll(y, w - QK_DIM // 4, 1), pltpu.roll(y, QK_DIM // 4, 1))
            y = y * cos_ref[...] + partner * sin_ref[...]
        dst[...] = y.astype(BF16)


def _qk_prep(h, gq, gk, bd, rope_tabs):
    b, l, _ = h.shape
    w = N_HEADS * HEAD_W
    tm = min(l, 1024)
    rope = rope_tabs is not None
    in_specs = [pl.BlockSpec((None, tm, w), lambda bi, i: (bi, i, COL_Q // w)),
                pl.BlockSpec((None, tm, w), lambda bi, i: (bi, i, COL_K // w)),
                pl.BlockSpec((1, w), lambda bi, i: (0, 0)),
                pl.BlockSpec((1, w), lambda bi, i: (0, 0)),
                pl.BlockSpec((w, w), lambda bi, i: (0, 0))]
    args = [h, h, gq, gk, bd]
    if rope:
        in_specs += [pl.BlockSpec((tm, w), lambda bi, i: (i, 0))] * 2
        args += list(rope_tabs)
    return pl.pallas_call(
        functools.partial(_qkprep_kernel, rope=rope),
        grid=(b, l // tm),
        in_specs=in_specs,
        out_specs=[pl.BlockSpec((None, tm, w), lambda bi, i: (bi, i, 0))] * 2,
        out_shape=[jax.ShapeDtypeStruct((b, l, w), BF16)] * 2,
        compiler_params=_cp("parallel", "parallel"),
        name="qk_prep",
    )(*args)


def _attn_kernel(q_ref, k_ref, v_ref, lam_ref, gsub_ref, o_ref, *, lam_init):
    q = q_ref[...]
    tq = q.shape[0]
    lane = lax.broadcasted_iota(jnp.int32, q.shape, 1)
    zero = jnp.zeros_like(q)
    qq = jnp.concatenate([jnp.where(lane < QK_DIM, q, zero), jnp.where(lane >= QK_DIM, q, zero)], axis=0)
    s = _dot_nt(qq, k_ref[...])
    m = jnp.max(s, axis=-1, keepdims=True)
    e = jnp.exp2(s - m)
    acc = _dot(e.astype(BF16), v_ref[...])
    o2 = acc[:, :HEAD_W] / acc[:, HEAD_W:HEAD_W + 1]
    lf = lam_ref[...]
    lam_val = (jnp.exp(jnp.sum(lf[0:1] * lf[1:2], axis=-1, keepdims=True))
               - jnp.exp(jnp.sum(lf[2:3] * lf[3:4], axis=-1, keepdims=True)) + lam_init)
    o = o2[:tq] - lam_val * o2[tq:]
    ms = jnp.mean(o * o, axis=-1, keepdims=True)
    o = o * lax.rsqrt(ms + EPS) * gsub_ref[...] * (1.0 - lam_init)
    o_ref[...] = o.astype(BF16)


def _concat_kernel(*refs, n_streams):
    k_refs, v_refs = refs[:n_streams], refs[n_streams:2 * n_streams]
    ko_ref, vo_ref = refs[2 * n_streams:]
    r0 = 0
    for k_ref, v_ref in zip(k_refs, v_refs):
        n = k_ref.shape[0]
        ko_ref[r0:r0 + n, :] = k_ref[...]
        ones = jnp.ones((n, HEAD_W), vo_ref.dtype)
        for h in range(N_HEADS):
            vo_ref[r0:r0 + n, 2 * h * HEAD_W:(2 * h + 1) * HEAD_W] = v_ref[:, h * HEAD_W:(h + 1) * HEAD_W]
            vo_ref[r0:r0 + n, (2 * h + 1) * HEAD_W:(2 * h + 2) * HEAD_W] = ones
        r0 += n


def _concat_kv(streams):
    b, _, w = streams[0][0].shape
    total = sum(k.shape[1] for k, _ in streams)
    vcol = COL_V // w
    rows = lambda n, width, blk: pl.BlockSpec((None, n, width), lambda bi: (bi, 0, blk))
    return pl.pallas_call(
        functools.partial(_concat_kernel, n_streams=len(streams)),
        grid=(b,),
        in_specs=[rows(k.shape[1], w, 0) for k, _ in streams] + [rows(k.shape[1], w, vcol) for k, _ in streams],
        out_specs=[rows(total, w, 0), rows(total, 2 * w, 0)],
        out_shape=[jax.ShapeDtypeStruct((b, total, w), BF16), jax.ShapeDtypeStruct((b, total, 2 * w), BF16)],
        compiler_params=_cp("parallel"),
        name="concat_kv",
    )(*[k for k, _ in streams], *[h for _, h in streams])


def _diff_attention(q, k, v, lam_l, gsub, lam_init):
    b, lq, w = q.shape
    lk = k.shape[1]
    tq = min(lq, 512)
    in_specs = [pl.BlockSpec((None, tq, HEAD_W), lambda bi, hi, i: (bi, i, hi)),
                pl.BlockSpec((None, lk, HEAD_W), lambda bi, hi, i: (bi, 0, hi)),
                pl.BlockSpec((None, lk, 2 * HEAD_W), lambda bi, hi, i: (bi, 0, hi)),
                pl.BlockSpec(lam_l.shape, lambda bi, hi, i: (0, 0)),
                pl.BlockSpec((1, HEAD_W), lambda bi, hi, i: (0, 0))]
    args = [q, k, v, lam_l, gsub]
    return pl.pallas_call(
        functools.partial(_attn_kernel, lam_init=lam_init),
        grid=(b, N_HEADS, lq // tq),
        in_specs=in_specs,
        out_specs=pl.BlockSpec((None, tq, HEAD_W), lambda bi, hi, i: (bi, i, hi)),
        out_shape=jax.ShapeDtypeStruct((b, lq, w), BF16),
        compiler_params=_cp("parallel", "parallel", "arbitrary"),
        name="diff_attention",
    )(*args)


def _sconv_kernel(h_ref, w_ref, b_ref, o_ref):
    x = h_ref[...].astype(F32)
    n = x.shape[0]
    row = lax.broadcasted_iota(jnp.int32, x.shape, 0)
    prev = jnp.where(row == 0, 0.0, pltpu.roll(x, 1, 0))
    nxt = jnp.where(row == n - 1, 0.0, pltpu.roll(x, n - 1, 0))
    o_ref[...] = prev * w_ref[0:1, :] + x * w_ref[1:2, :] + nxt * w_ref[2:3, :] + b_ref[...]


def _short_conv(h, w, bias):
    b, l, _ = h.shape
    return pl.pallas_call(
        _sconv_kernel,
        grid=(b, 3),
        in_specs=[pl.BlockSpec((None, l, HY_W), lambda bi, j: (bi, 0, COL_HY // HY_W + j)),
                  pl.BlockSpec((3, HY_W), lambda bi, j: (0, j)),
                  pl.BlockSpec((1, HY_W), lambda bi, j: (0, j))],
        out_specs=pl.BlockSpec((None, None, l, HY_W), lambda bi, j: (bi, j, 0, 0)),
        out_shape=jax.ShapeDtypeStruct((b, 3, l, HY_W), F32),
        compiler_params=_cp("parallel", "parallel"),
        name="short_conv",
    )(h, w, bias)


def _filter_kernel(f_ref, w1_ref, b1_ref, fr_ref, w2_ref, b2_ref, w3_ref, dec_ref, o_ref):
    def mm(a, w_ref_):
        ah, al = _split(a)
        wh, wl = _split(w_ref_[...])
        return _dot(ah, wh) + _dot(ah, wl) + _dot(al, wh)
    fr = fr_ref[...]
    h = jnp.sin(fr * (mm(f_ref[...], w1_ref) + b1_ref[...]))
    h = jnp.sin(fr * (mm(h, w2_ref) + b2_ref[...]))
    h = mm(h, w3_ref)
    half = h.shape[1] // 2
    dec = dec_ref[...]
    hf = h[:, :half] * dec
    hb = h[:, half:] * dec
    row = lax.broadcasted_iota(jnp.int32, hb.shape, 0)
    hb = jnp.where(row == 0, 0.0, hb)
    norm = jnp.sum(jnp.abs(hf) + jnp.abs(hb), axis=0, keepdims=True)
    o_ref[:, :half] = (hf + hb) / norm
    o_ref[:, half:] = (hf - hb) / norm


def _hyena_filter_taps(feats, w1, b1, freq, w2, b2, w3, dec):
    l = feats.shape[0]
    n = w3.shape[1]
    full = lambda a: pl.BlockSpec(a.shape, lambda i: (0,) * a.ndim)
    args = (feats, w1, b1, freq, w2, b2, w3, dec)
    return pl.pallas_call(
        _filter_kernel,
        grid=(1,),
        in_specs=[full(a) for a in args],
        out_specs=pl.BlockSpec((l, n), lambda i: (0, 0)),
        out_shape=jax.ShapeDtypeStruct((l, n), F32),
        compiler_params=_cp("arbitrary"),
        name="hyena_filter_taps",
    )(*args)


def _table_mm_kernel(t_ref, x_ref, o_ref):
    o_ref[...] = _dot(t_ref[...], x_ref[...].astype(BF16)).astype(o_ref.dtype)


def _table_matmul(table, x, out_dtype):
    m, k = table.shape
    n = x.shape[1]
    tm = min(m, 512)
    return pl.pallas_call(
        _table_mm_kernel,
        grid=(m // tm,),
        in_specs=[pl.BlockSpec((tm, k), lambda i: (i, 0)),
                  pl.BlockSpec((k, n), lambda i: (0, 0))],
        out_specs=pl.BlockSpec((tm, n), lambda i: (i, 0)),
        out_shape=jax.ShapeDtypeStruct((m, n), out_dtype),
        compiler_params=_cp("parallel"),
        name="table_matmul",
    )(table, x)


def _dftmul_kernel(fc_ref, fs_ref, z_ref, k_ref, p_ref):
    z = z_ref[...].astype(BF16)
    zc = _dot(fc_ref[...], z)
    zs = _dot(fs_ref[...], z)
    p_ref[0] = (zc * k_ref[0] - zs * k_ref[1]).astype(BF16)
    p_ref[1] = (zc * k_ref[2] + zs * k_ref[3]).astype(BF16)


def _dft_multiply(ffwd, z, z_spec, kf):
    l = ffwd.shape[1]
    b = z.shape[0]
    tf = min(l, DFT_ROWS)
    nf = l // tf
    out = pl.pallas_call(
        _dftmul_kernel,
        grid=(nf, b),
        in_specs=[pl.BlockSpec((tf, l), lambda i, bi: (i, 0)),
                  pl.BlockSpec((tf, l), lambda i, bi: (i + nf, 0)),
                  z_spec,
                  pl.BlockSpec((4, tf, HY_W), lambda i, bi: (0, i, 0))],
        out_specs=pl.BlockSpec((None, 2, tf, HY_W), lambda i, bi: (bi, 0, i, 0)),
        out_shape=jax.ShapeDtypeStruct((b, 2, l, HY_W), BF16),
        compiler_params=_cp("parallel", "arbitrary"),
        name="dft_multiply",
    )(ffwd, ffwd, z, kf)
    return out.reshape(b, 2 * l, HY_W)


def _idft_gate_kernel(fi_ref, p_ref, g_ref, z_ref, b_ref, o_ref):
    conv = _dot(fi_ref[...], p_ref[...])
    o_ref[...] = (g_ref[...] * (conv + b_ref[...] * z_ref[...])).astype(o_ref.dtype)


def _idft_gate(finv, p, gate, gate_spec, z, z_spec, bias, out_dtype):
    l = finv.shape[0]
    b = p.shape[0]
    tt = min(l, DFT_ROWS)
    return pl.pallas_call(
        _idft_gate_kernel,
        grid=(l // tt, b),
        in_specs=[pl.BlockSpec((tt, 2 * l), lambda i, bi: (i, 0)),
                  pl.BlockSpec((None, 2 * l, HY_W), lambda i, bi: (bi, 0, 0)),
                  gate_spec, z_spec,
                  pl.BlockSpec((1, HY_W), lambda i, bi: (0, 0))],
        out_specs=pl.BlockSpec((None, tt, HY_W), lambda i, bi: (bi, i, 0)),
        out_shape=jax.ShapeDtypeStruct((b, l, HY_W), out_dtype),
        compiler_params=_cp("parallel", "arbitrary"),
        name="idft_gate",
    )(finv, p, gate, z, bias)


def _hyena_mix(h, conv_w, conv_b, kf, hy_bias, ffwd, finv):
    b, l, _ = h.shape
    tt = min(l, DFT_ROWS)
    u = _short_conv(h, conv_w, conv_b)
    part = lambda j, rows: pl.BlockSpec((None, None, rows, HY_W),
                                        lambda i, bi, j=j: (bi, j, i if rows != l else 0, 0))
    p = _dft_multiply(ffwd, u, part(0, l), kf[0])
    z1 = _idft_gate(finv, p, u, part(1, tt), u, part(0, tt), hy_bias[0:1], F32)
    p = _dft_multiply(ffwd, z1, pl.BlockSpec((None, l, HY_W), lambda i, bi: (bi, 0, 0)), kf[1])
    return _idft_gate(finv, p, u, part(2, tt), z1,
                      pl.BlockSpec((None, tt, HY_W), lambda i, bi: (bi, i, 0)), hy_bias[1:2], BF16)


def _fn1_kernel(z_ref, m_ref, o_ref):
    r = _dot(z_ref[...], m_ref[...])
    half = r.shape[1] // 2
    o_ref[0] = r[:, :half].astype(BF16)
    o_ref[1] = r[:, half:].astype(BF16)


def _fourier_mix(h, m1, t2):
    b, l, _ = h.shape
    tm = min(l, DFT_ROWS)
    zz = pl.pallas_call(
        _fn1_kernel,
        grid=(b, l // tm),
        in_specs=[pl.BlockSpec((None, tm, FN_W), lambda bi, i: (bi, i, COL_FN // FN_W)),
                  pl.BlockSpec((FN_W, 2 * FN_W), lambda bi, i: (0, 0))],
        out_specs=pl.BlockSpec((None, 2, tm, FN_W), lambda bi, i: (bi, 0, i, 0)),
        out_shape=jax.ShapeDtypeStruct((b, 2, l, FN_W), BF16),
        compiler_params=_cp("parallel", "parallel"),
        name="fnet_channels",
    )(h, m1).reshape(b, 2 * l, FN_W)
    return pl.pallas_call(
        _table_mm_kernel,
        grid=(l // tm, b),
        in_specs=[pl.BlockSpec((tm, 2 * l), lambda i, bi: (i, 0)),
                  pl.BlockSpec((None, 2 * l, FN_W), lambda i, bi: (bi, 0, 0))],
        out_specs=pl.BlockSpec((None, tm, FN_W), lambda i, bi: (bi, i, 0)),
        out_shape=jax.ShapeDtypeStruct((b, l, FN_W), BF16),
        compiler_params=_cp("parallel", "arbitrary"),
        name="fnet_positions",
    )(t2, zz)


def _merge_kernel(hy_ref, fn_ref, at_ref, g_ref, x_ref, gm_ref, why_ref, wfn_ref, wat_ref, wout_ref, o_ref):
    d = x_ref.shape[-1]
    g = 1.0 / (1.0 + jnp.exp(-g_ref[...].astype(F32)))
    y = (g[:, :d] * _dot(hy_ref[...], why_ref[...])
         + g[:, d:2 * d] * _dot(fn_ref[...], wfn_ref[...])
         + g[:, 2 * d:] * _dot(at_ref[...], wat_ref[...]))
    mix = _dot(y.astype(BF16), wout_ref[...])
    o_ref[...] = x_ref[...] + gm_ref[...] * mix


def _merge(hyo, fno, att, h, x, mods, layer, row_fn, w_hy, w_fn, w_at, w_out):
    b, l, d = x.shape
    tm = min(l, 1024)
    full = lambda a: pl.BlockSpec(a.shape, lambda bi, i: (0,) * a.ndim)
    tok = lambda wdt, blk=0: pl.BlockSpec((None, tm, wdt), lambda bi, i: (bi, i, blk))
    return pl.pallas_call(
        _merge_kernel,
        grid=(b, l // tm),
        in_specs=[tok(HY_W), tok(FN_W), tok(N_HEADS * HEAD_W), tok(3 * d, COL_GATE), tok(d),
                  pl.BlockSpec((None, None, None, 1, d), _mod_spec(layer, 2, lambda bi, i: row_fn(bi))),
                  full(w_hy), full(w_fn), full(w_at), full(w_out)],
        out_specs=tok(d),
        out_shape=jax.ShapeDtypeStruct((b, l, d), F32),
        compiler_params=_cp("parallel", "parallel"),
        name="merge_residual",
    )(hyo, fno, att, h, x, mods, w_hy, w_fn, w_at, w_out)


def _peer_q_kernel(x_ref, g_ref, sh_ref, sc_ref, wq_ref, kh_ref, kl_ref, s_ref, nt_ref):
    n = _modulated_norm(x_ref[...], g_ref[...], sh_ref[...], sc_ref[...])
    nt_ref[...] = n.T.astype(BF16)
    q = _dot(n.astype(BF16), wq_ref[...])
    dq = kh_ref.shape[1]
    for hp in range(kh_ref.shape[0]):
        qh, ql = _split(q[:, hp * dq:(hp + 1) * dq])
        kh = kh_ref[hp]
        s = _dot(qh, kh) + _dot(ql, kh) + _dot(qh, kl_ref[hp])
        s_ref[hp] = s.T


def _peer_scores(x, g, mods, layer, row_fn, wq, keys_t):
    b, l, d = x.shape
    tt = min(l, 512)
    nt = l // tt
    nhp = 2 * PEER_HEADS
    mrow = lambda bi, i: row_fn(bi)
    full = lambda a: pl.BlockSpec(a.shape, lambda bi, i: (0,) * a.ndim)
    return pl.pallas_call(
        _peer_q_kernel,
        grid=(b, nt),
        in_specs=[pl.BlockSpec((None, tt, d), lambda bi, i: (bi, i, 0)),
                  pl.BlockSpec((1, d), lambda bi, i: (0, 0)),
                  pl.BlockSpec((None, None, None, 1, d), _mod_spec(layer, 3, mrow)),
                  pl.BlockSpec((None, None, None, 1, d), _mod_spec(layer, 4, mrow)),
                  full(wq), full(keys_t[0]), full(keys_t[1])],
        out_specs=[pl.BlockSpec((nhp, PEER_NKEYS, tt), lambda bi, i: (0, 0, bi * nt + i)),
                   pl.BlockSpec((d, tt), lambda bi, i: (0, bi * nt + i))],
        out_shape=[jax.ShapeDtypeStruct((nhp, PEER_NKEYS, b * l), F32),
                   jax.ShapeDtypeStruct((d, b * l), BF16)],
        compiler_params=_cp("parallel", "parallel"),
        name="peer_scores",
    )(x, g, mods, mods, wq, keys_t[0], keys_t[1])


def _merge_sort_pairs(n):
    pairs = []
    p = 1
    while p < n:
        k = p
        while k >= 1:
            for j in range(k % p, n - k, 2 * k):
                for i in range(min(k, n - j - k)):
                    if (i + j) // (2 * p) == (i + j + k) // (2 * p):
                        pairs.append((i + j, i + j + k))
            k //= 2
        p *= 2
    return pairs


def _top_rows_grouped(s, k):
    groups = s.shape[0] // 8
    v = [s[g * 8:(g + 1) * 8, :] for g in range(groups)]
    for i, j in _merge_sort_pairs(groups):
        v[i], v[j] = jnp.maximum(v[i], v[j]), jnp.minimum(v[i], v[j])
    v.append(jnp.full_like(v[0], NEG))
    vals = []
    for r in range(k):
        m = jnp.max(v[0], axis=0, keepdims=True)
        vals.append(m)
        taken = v[0] == m
        for d in range(min(groups, k - 1 - r)):
            v[d] = jnp.where(taken, v[d + 1], v[d])
    return vals


def _peer_expert_kernel(s_ref, nt_ref, u_ref, vt_ref, x_ref, gm_ref, o_ref,
                        tau_ref, l1_ref, l2_ref, l1c_ref, hid0_ref, hid1_ref, p0_ref, p1_ref, acc_ref):
    s = pl.program_id(1)
    nc = pl.num_programs(1) - 2
    k = PEER_TOPK
    tt = nt_ref.shape[1]
    n_i = u_ref.shape[0] // PEER_NKEYS
    piece = GATE_VREGS * 8 * 128 // tt

    def hidden(hid_ref):
        hid_ref[...] = _dot(u_ref[...], nt_ref[...])

    def stages(chunk, hid_w, hid_r, p_w, p_r):
        first = pl.multiple_of(chunk * n_i, n_i)
        for h in range(PEER_HEADS):
            l1c_ref[h] = l1_ref[h, pl.ds(first, n_i), :]
        hidden(hid_w)
        acc_ref[...] += _dot(vt_ref[...], p_r[...])
        for ii in range(n_i):
            l1_row = [jnp.broadcast_to(l1c_ref[h, ii:ii + 1, :], (8, tt))[None] for h in range(PEER_HEADS)]
            for jp in range(PEER_NKEYS // piece):
                js = slice(jp * piece, (jp + 1) * piece)
                gate = jnp.zeros((piece // 8, 8, tt), F32)
                for h in range(PEER_HEADS):
                    logw = l1_row[h] + l2_ref[h, js, :].reshape(piece // 8, 8, tt)
                    gate = gate + jnp.where(logw >= tau_ref[h][None], jnp.exp2(logw), 0.0)
                rows = slice(ii * PEER_NKEYS + jp * piece, ii * PEER_NKEYS + (jp + 1) * piece)
                hid = hid_r[rows, :]
                act = hid * (1.0 + lax.erf(hid * (2.0 ** -0.5)))
                p_w[rows, :] = (gate.reshape(piece, tt) * act).astype(BF16)

    @pl.when(s == 0)
    def _():
        acc_ref[...] = jnp.zeros_like(acc_ref)
        p1_ref[...] = jnp.zeros_like(p1_ref)
        hidden(hid0_ref)
        width = min(tt, 256)
        parts = tt // width

        def select(it, carry):
            h = it // parts
            cols = pl.ds(pl.multiple_of((it % parts) * width, width), width)
            s1 = s_ref[2 * h, :, cols]
            s2 = s_ref[2 * h + 1, :, cols]
            a = _top_rows_grouped(s1, k + 1)
            b = _top_rows_grouped(s2, k + 1)
            cand = [a[i] + b[j] for i in range(k + 1) for j in range((k + 1) // (i + 1))]
            pad = (-len(cand)) % 64
            top = _top_rows_grouped(jnp.concatenate(cand + [jnp.full_like(a[0], NEG)] * pad, axis=0), k + 1)
            z = jnp.zeros_like(top[0])
            for t in top[:k]:
                z = z + jnp.exp(t - top[0])
            shift = top[0] + jnp.log(z)
            log2e = 1.0 / math.log(2.0)
            l1_ref[h, :, cols] = s1 * log2e
            l2_ref[h, :, cols] = (s2 - shift) * log2e - 1.0
            tau = (0.5 * (top[k - 1] + top[k]) - shift) * log2e - 1.0
            tau_ref[h, :, cols] = jnp.broadcast_to(tau, (8, width))
            return carry

        lax.fori_loop(0, PEER_HEADS * parts, select, 0)

    @pl.when((s >= 1) & (s <= nc) & (s % 2 == 1))
    def _():
        stages(s - 1, hid1_ref, hid0_ref, p0_ref, p1_ref)

    @pl.when((s >= 1) & (s <= nc) & (s % 2 == 0))
    def _():
        stages(s - 1, hid0_ref, hid1_ref, p1_ref, p0_ref)

    @pl.when(s == nc + 1)
    def _():
        last = p0_ref if (nc - 1) % 2 == 0 else p1_ref
        out_t = acc_ref[...] + _dot(vt_ref[...], last[...])
        o_ref[...] = x_ref[...] + gm_ref[...] * out_t.T


def _peer_experts(scores, n_t, u, v_t, x, mods, layer, row_fn):
    b, l, d = x.shape
    tt = min(l, PEER_TOKENS)
    nt = l // tt
    nc, _, ec = v_t.shape
    nhp = scores.shape[0]
    return pl.pallas_call(
        _peer_expert_kernel,
        grid=(b * nt, nc + 2),
        in_specs=[pl.BlockSpec((nhp, PEER_NKEYS, tt), lambda t, s: (0, 0, t)),
                  pl.BlockSpec((d, tt), lambda t, s: (0, t)),
                  pl.BlockSpec((ec, d), lambda t, s: (jnp.minimum(s, nc - 1), 0)),
                  pl.BlockSpec((None, d, ec), lambda t, s: (jnp.clip(s - 2, 0, nc - 1), 0, 0)),
                  pl.BlockSpec((None, tt, d), lambda t, s: (t // nt, t % nt, 0)),
                  pl.BlockSpec((None, None, None, 1, d), _mod_spec(layer, 5, lambda t, s: row_fn(t // nt)))],
        out_specs=pl.BlockSpec((None, tt, d), lambda t, s: (t // nt, t % nt, 0)),
        out_shape=jax.ShapeDtypeStruct((b, l, d), F32),
        scratch_shapes=[pltpu.VMEM((PEER_HEADS, 8, tt), F32),
                        pltpu.VMEM((PEER_HEADS, PEER_NKEYS, tt), F32),
                        pltpu.VMEM((PEER_HEADS, PEER_NKEYS, tt), F32),
                        pltpu.VMEM((PEER_HEADS, ec // PEER_NKEYS, tt), F32),
                        pltpu.VMEM((ec, tt), F32),
                        pltpu.VMEM((ec, tt), F32),
                        pltpu.VMEM((ec, tt), BF16),
                        pltpu.VMEM((ec, tt), BF16),
                        pltpu.VMEM((d, tt), F32)],
        compiler_params=_cp("parallel", "arbitrary"),
        name="peer_experts",
    )(scores, n_t, u, v_t, x, mods)


def _cast_kernel(a_ref, o_ref):
    o_ref[...] = a_ref[...].astype(o_ref.dtype)


def _cast_bf16(a, layer):
    _, rows, d = a.shape
    return pl.pallas_call(
        _cast_kernel,
        grid=(rows // PEER_CHUNK,),
        in_specs=[pl.BlockSpec((None, PEER_CHUNK, d), lambda i: (layer, i, 0))],
        out_specs=pl.BlockSpec((PEER_CHUNK, d), lambda i: (i, 0)),
        out_shape=jax.ShapeDtypeStruct((rows, d), BF16),
        compiler_params=_cp("parallel"),
        name="cast_bf16",
    )(a)


def _transpose_cast_kernel(a_ref, o_ref):
    o_ref[...] = a_ref[...].T.astype(o_ref.dtype)


def _chunk_transpose_bf16(a, layer):
    _, rows, d = a.shape
    nc = rows // PEER_CHUNK
    return pl.pallas_call(
        _transpose_cast_kernel,
        grid=(nc,),
        in_specs=[pl.BlockSpec((None, PEER_CHUNK, d), lambda i: (layer, i, 0))],
        out_specs=pl.BlockSpec((None, d, PEER_CHUNK), lambda i: (i, 0, 0)),
        out_shape=jax.ShapeDtypeStruct((nc, d, PEER_CHUNK), BF16),
        compiler_params=_cp("parallel"),
        name="chunk_transpose_bf16",
    )(a)


def _permute_in_weights(w, layer):
    _, d, n = w.shape
    tn = 512
    nb = n // tn
    shift = (n - 3 * d) // tn
    return pl.pallas_call(
        _cast_kernel,
        grid=(nb,),
        in_specs=[pl.BlockSpec((None, d, tn), lambda j: (layer, 0, (j + shift) % nb))],
        out_specs=pl.BlockSpec((d, tn), lambda j: (0, j)),
        out_shape=jax.ShapeDtypeStruct((d, n), BF16),
        compiler_params=_cp("parallel"),
        name="permute_in_weights",
    )(w)


def _peer(x, g, mods, layer, row_fn, wq_t, keys, u, v_t):
    scores, n_t = _peer_scores(x, g, mods, layer, row_fn, wq_t, keys)
    return _peer_experts(scores, n_t, u, v_t, x, mods, layer, row_fn)


def _dft_tables(l):
    cos, sin = _cos_sin_table(l, l, 2 * l)
    t = jnp.arange(l, dtype=jnp.int32)[None, :]
    first = (jnp.arange(l) == 0)[:, None]
    sin = jnp.where(first, jnp.where(t % 2 == 0, 1.0, -1.0), sin)
    ffwd = jnp.concatenate([cos, sin], axis=0).astype(BF16)
    return ffwd, ffwd.T


def _cos_sin_table(nf, nt, n):
    step = 64
    f = jnp.arange(nf, dtype=jnp.int32)[:, None]
    angle = lambda prod: (2.0 * math.pi) * ((prod % n).astype(F32) / n)
    a = angle(f * (step * jnp.arange(nt // step, dtype=jnp.int32)[None, :]))
    b = angle(f * jnp.arange(step, dtype=jnp.int32)[None, :])
    ca, sa = jnp.cos(a)[:, :, None], jnp.sin(a)[:, :, None]
    cb, sb = jnp.cos(b)[:, None, :], jnp.sin(b)[:, None, :]
    return (ca * cb - sa * sb).reshape(nf, nt), (sa * cb + ca * sb).reshape(nf, nt)


def _fnet_tables(l):
    t2 = jnp.concatenate(_cos_sin_table(l, l, l), axis=1).astype(BF16)
    k = np.arange(FN_GROUP)
    ang64 = 2.0 * np.pi * ((k[:, None] * k[None, :]) % FN_GROUP) / FN_GROUP
    eye = np.eye(FN_W // FN_GROUP)
    scale = 1.0 / math.sqrt(FN_GROUP * l)
    m1 = np.concatenate([np.kron(eye, np.cos(ang64)), -np.kron(eye, np.sin(ang64))], axis=1) * scale
    return jnp.asarray(m1, F32).astype(BF16), t2


def _rope_tables(l):
    rows = l // GRID_W
    row = jnp.repeat(jnp.arange(rows), GRID_W).astype(F32)
    col = jnp.tile(jnp.arange(GRID_W), rows).astype(F32)
    half = QK_DIM // 2
    inv = ROPE_BASE ** (-jnp.arange(0, half, 2, dtype=F32) / half)
    ang = jnp.stack([row[:, None] * inv, col[:, None] * inv], axis=1)
    cos = jnp.repeat(jnp.cos(ang)[:, :, None, :], 2, axis=2)
    sin = jnp.sin(ang)
    sin = jnp.stack([-sin, sin], axis=2)
    rep = lambda a: jnp.tile(a.reshape(l, QK_DIM), (1, 2 * N_HEADS))
    return rep(cos), rep(sin)


def _filter_features(l):
    pos = jnp.arange(l, dtype=F32)
    t = pos / max(l - 1, 1)
    w = 2.0 * math.pi * pos / l
    f = jnp.linspace(1e-4, HY_BANDS - 1, HY_BANDS, dtype=F32)
    feats = jnp.concatenate([t[:, None], jnp.cos(w[:, None] * f), -jnp.sin(w[:, None] * f)], axis=-1)
    feats = jnp.pad(feats, ((0, 0), (0, 64 - HY_EMB)))
    deltas = jnp.abs(jnp.linspace(HY_MIN_DECAY, HY_MAX_DECAY, HY_W, dtype=F32))
    dec = jnp.exp(-t[:, None] * deltas)
    return feats, jnp.tile(dec, (1, HY_ORDER))


def _hyena_filters(l, tabs, ffwd, w1, b1, freq, w2, b2, w3):
    feats, dec = tabs
    taps = _hyena_filter_taps(feats, jnp.pad(w1, ((0, 64 - HY_EMB), (0, 0))), b1[None], freq[None],
                              w2, b2[None], w3, dec)
    kf = _table_matmul(ffwd, taps, F32)
    half = HY_ORDER * HY_W
    kc = kf[:l, :half]
    nyq = kf[l, :half]
    ks = kf[l:, half:]
    n = 2.0 * l
    first = (jnp.arange(l) == 0)[:, None]
    wc = jnp.where(first, 1.0 / n, 2.0 / n)
    ka = kc * wc
    kb = jnp.where(first, 0.0, ks * (2.0 / n))
    kd = jnp.where(first, nyq[None, :] / n, kc * (2.0 / n))
    stack = jnp.stack([ka, kb, kb, kd], axis=0)
    return jnp.moveaxis(stack.reshape(4, l, HY_ORDER, HY_W), 2, 0)


def kernel(x, c, ctx, c_ctx, w_ada, b_ada, g_mix, g_ffn, w_in, hy_conv_w, hy_conv_b, hy_w1, hy_b1, hy_freq, hy_w2, hy_b2, hy_w3, hy_bias, g_q, g_k, lam, g_sub, w_hy, w_fn, w_at, w_out, peer_wq, peer_keys, peer_u, peer_v):
    bsz, seq, d = x.shape
    clen = ctx.shape[1]
    depth = w_ada.shape[0]

    cc = jnp.concatenate([c, c_ctx[None], jnp.zeros((MOD_ROWS - bsz - 1, d), F32)], axis=0)
    mods = _ada_mods(cc, w_ada, b_ada)
    lat_row = lambda bi: bi
    ctx_row = lambda bi: bsz

    rope = _rope_tables(seq)
    tabs = {n: dict(dft=_dft_tables(n), fnet=_fnet_tables(n), feat=_filter_features(n)) for n in (seq, clen)}
    w = N_HEADS * HEAD_W
    lane = np.arange(w)
    bd = jnp.asarray((lane[:, None] // QK_DIM == lane[None, :] // QK_DIM) / QK_DIM, F32).astype(BF16)

    xl, xc = x, ctx
    for l in range(depth):
        last = l == depth - 1
        lam_init = 0.8 - 0.6 * math.exp(-0.3 * l)
        w_perm = _permute_in_weights(w_in, l)
        gq = jnp.tile(g_q[l].reshape(1, HEAD_W), (1, N_HEADS)) * (QK_DIM ** -0.5 / math.log(2.0))
        gk = jnp.tile(g_k[l].reshape(1, HEAD_W), (1, N_HEADS))
        gsub = g_sub[l][None]
        wts = [a[l].astype(BF16) for a in (w_hy, w_fn, w_at, w_out)]
        filt_args = (hy_w1[l], hy_b1[l], hy_freq[l], hy_w2[l], hy_b2[l], hy_w3[l])

        h_l = _in_projection(xl, g_mix[l][None], mods, l, lat_row, w_perm, 0, P_IN)
        h_c = _in_projection(xc, g_mix[l][None], mods, l, ctx_row, w_perm, COL_Q if last else 0, P_IN)
        q_l, k_l = _qk_prep(h_l, gq, gk, bd, rope)
        q_c, k_c = _qk_prep(h_c, gq, gk, bd, None)
        k_all, v_all = _concat_kv([(k_l, h_l), (k_c, h_c)])
        att_l = _diff_attention(q_l, k_all, v_all, lam[l], gsub, lam_init)
        ffwd, finv = tabs[seq]["dft"]
        kf = _hyena_filters(seq, tabs[seq]["feat"], ffwd, *filt_args)
        hyo_l = _hyena_mix(h_l, hy_conv_w[l], hy_conv_b[l][None], kf, hy_bias[l], ffwd, finv)
        fno_l = _fourier_mix(h_l, *tabs[seq]["fnet"])
        if not last:
            att_c = _diff_attention(q_c, *_concat_kv([(k_c, h_c)]), lam[l], gsub, lam_init)
            ffwd_c, finv_c = tabs[clen]["dft"]
            kf_c = _hyena_filters(clen, tabs[clen]["feat"], ffwd_c, *filt_args)
            hyo_c = _hyena_mix(h_c, hy_conv_w[l], hy_conv_b[l][None], kf_c, hy_bias[l], ffwd_c, finv_c)
            fno_c = _fourier_mix(h_c, *tabs[clen]["fnet"])
            xc = _merge(hyo_c, fno_c, att_c, h_c, xc, mods, l, ctx_row, *wts)
        xl = _merge(hyo_l, fno_l, att_l, h_l, xl, mods, l, lat_row, *wts)

        wq_t = peer_wq[l].astype(BF16)
        keys = _split(jnp.swapaxes(peer_keys[l].reshape(2 * PEER_HEADS, PEER_NKEYS, -1), 1, 2))
        u = _cast_bf16(peer_u, l)
        v_t = _chunk_transpose_bf16(peer_v, l)
        if not last:
            xc = _peer(xc, g_ffn[l][None], mods, l, ctx_row, wq_t, keys, u, v_t)
        xl = _peer(xl, g_ffn[l][None], mods, l, lat_row, wq_t, keys, u, v_t)
    return xl
```

```python
import functools
import math

import jax
import jax.numpy as jnp
import numpy as np
from jax import lax
from jax.experimental import pallas as pl
from jax.experimental.pallas import tpu as pltpu

F32 = jnp.float32
BF16 = jnp.bfloat16

EPS = 1e-6
GRID_W = 64
ROPE_BASE = 10000.0
N_HEADS = 4
QK_DIM = 64
HEAD_W = 2 * QK_DIM
HY_W = 256
HY_ORDER = 2
HY_EMB = 33
HY_BANDS = (HY_EMB - 1) // 2
HY_MIN_DECAY = math.log(1e-2) / 1.5
HY_MAX_DECAY = math.log(1e-2) / 0.3
FN_GROUP = 64
FN_W = 256
PEER_HEADS = 8
PEER_NKEYS = 128
PEER_TOPK = 16
N_MOD = 6
MOD_ROWS = 16
NEG = -3.0e38
GATE_VREGS = 8
PEER_TOKENS = 256
PEER_CHUNK = 2048
DFT_ROWS = 1024

VMEM_LIMIT = 56 * 1024 * 1024

COL_GATE = 0
COL_HY = 3072
COL_FN = 3840
COL_Q = 4096
COL_K = 4608
COL_V = 5120
P_IN = 5632


def _cp(*sem):
    return pltpu.CompilerParams(dimension_semantics=sem, vmem_limit_bytes=VMEM_LIMIT)


def _dot(a, b):
    return jnp.dot(a, b, preferred_element_type=F32)


def _dot_nt(a, b):
    return lax.dot_general(a, b, (((1,), (1,)), ((), ())), preferred_element_type=F32)


def _split(a):
    hi = a.astype(BF16)
    lo = (a - hi.astype(F32)).astype(BF16)
    return hi, lo


def _modulated_norm(x, g, shift, scale):
    ms = jnp.mean(x * x, axis=-1, keepdims=True)
    y = x * lax.rsqrt(ms + EPS) * g
    return y * (1.0 + scale) + shift


def _mod_spec(layer, chunk, row_fn):
    def imap(*idx):
        return (layer, row_fn(*idx), chunk, 0, 0)
    return imap


def _ada_kernel(c_ref, w_ref, b_ref, o_ref):
    c = c_ref[...]
    a = c / (1.0 + jnp.exp(-c))
    ah, al = _split(a)
    wh, wl = _split(w_ref[...])
    o_ref[...] = _dot(ah, wh) + _dot(ah, wl) + _dot(al, wh) + b_ref[...]


def _ada_mods(cc, w_ada, b_ada):
    depth, d, n = w_ada.shape
    tn = 512
    out = pl.pallas_call(
        _ada_kernel,
        grid=(depth, n // tn),
        in_specs=[pl.BlockSpec((MOD_ROWS, d), lambda l, j: (0, 0)),
                  pl.BlockSpec((None, d, tn), lambda l, j: (l, 0, j)),
                  pl.BlockSpec((None, 1, tn), lambda l, j: (l, 0, j))],
        out_specs=pl.BlockSpec((None, MOD_ROWS, tn), lambda l, j: (l, 0, j)),
        out_shape=jax.ShapeDtypeStruct((depth, MOD_ROWS, n), F32),
        compiler_params=_cp("parallel", "parallel"),
        name="ada_mods",
    )(cc, w_ada, b_ada.reshape(depth, 1, n))
    return out.reshape(depth, MOD_ROWS, N_MOD, 1, d)


def _inproj_kernel(x_ref, g_ref, sh_ref, sc_ref, w_ref, o_ref, xn_ref):
    @pl.when(pl.program_id(2) == 0)
    def _():
        xn_ref[...] = _modulated_norm(x_ref[...], g_ref[...], sh_ref[...], sc_ref[...]).astype(BF16)
    o_ref[...] = _dot(xn_ref[...], w_ref[...]).astype(o_ref.dtype)


def _in_projection(x, g, mods, layer, row_fn, w, col_lo, col_hi):
    b, l, d = x.shape
    tm = min(l, 1024)
    wide = w.shape[1] // 2
    tn = wide if (col_lo % wide == 0 and (col_hi - col_lo) % wide == 0) else 512
    j0 = col_lo // tn
    nj = (col_hi - col_lo) // tn
    mrow = lambda bi, i, j: row_fn(bi)
    return pl.pallas_call(
        _inproj_kernel,
        grid=(b, l // tm, nj),
        in_specs=[pl.BlockSpec((None, tm, d), lambda bi, i, j: (bi, i, 0)),
                  pl.BlockSpec((1, d), lambda bi, i, j: (0, 0)),
                  pl.BlockSpec((None, None, None, 1, d), _mod_spec(layer, 0, mrow)),
                  pl.BlockSpec((None, None, None, 1, d), _mod_spec(layer, 1, mrow)),
                  pl.BlockSpec((d, tn), lambda bi, i, j: (0, j + j0))],
        out_specs=pl.BlockSpec((None, tm, tn), lambda bi, i, j: (bi, i, j + j0)),
        out_shape=jax.ShapeDtypeStruct((b, l, w.shape[1]), BF16),
        scratch_shapes=[pltpu.VMEM((tm, d), BF16)],
        compiler_params=_cp("parallel", "parallel", "arbitrary"),
        name="in_projection",
    )(x, g, mods, mods, w)


def _qkprep_kernel(*refs, rope):
    if rope:
        q_ref, k_ref, gq_ref, gk_ref, bd_ref, cos_ref, sin_ref, qo_ref, ko_ref = refs
    else:
        q_ref, k_ref, gq_ref, gk_ref, bd_ref, qo_ref, ko_ref = refs
    for src, g_ref, dst in ((q_ref, gq_ref, qo_ref), (k_ref, gk_ref, ko_ref)):
        x = src[...].astype(F32)
        ms = _dot((x * x).astype(BF16), bd_ref[...])
        y = x * lax.rsqrt(ms + EPS) * g_ref[...]
        if rope:
            w = y.shape[1]
            lane = lax.broadcasted_iota(jnp.int32, y.shape, 1)
            first = (lane % (QK_DIM // 2)) < (QK_DIM // 4)
            partner = jnp.where(first, pltpu.roll(y, w - QK_DIM // 4, 1), pltpu.roll(y, QK_DIM // 4, 1))
            y = y * cos_ref[...] + partner * sin_ref[...]
        dst[...] = y.astype(BF16)


def _qk_prep(h, gq, gk, bd, rope_tabs):
    b, l, _ = h.shape
    w = N_HEADS * HEAD_W
    tm = min(l, 1024)
    rope = rope_tabs is not None
    in_specs = [pl.BlockSpec((None, tm, w), lambda bi, i: (bi, i, COL_Q // w)),
                pl.BlockSpec((None, tm, w), lambda bi, i: (bi, i, COL_K // w)),
                pl.BlockSpec((1, w), lambda bi, i: (0, 0)),
                pl.BlockSpec((1, w), lambda bi, i: (0, 0)),
                pl.BlockSpec((w, w), lambda bi, i: (0, 0))]
    args = [h, h, gq, gk, bd]
    if rope:
        in_specs += [pl.BlockSpec((tm, w), lambda bi, i: (i, 0))] * 2
        args += list(rope_tabs)
    return pl.pallas_call(
        functools.partial(_qkprep_kernel, rope=rope),
        grid=(b, l // tm),
        in_specs=in_specs,
        out_specs=[pl.BlockSpec((None, tm, w), lambda bi, i: (bi, i, 0))] * 2,
        out_shape=[jax.ShapeDtypeStruct((b, l, w), BF16)] * 2,
        compiler_params=_cp("parallel", "parallel"),
        name="qk_prep",
    )(*args)


def _attn_kernel(q_ref, k_ref, v_ref, lam_ref, gsub_ref, o_ref, *, lam_init):
    q = q_ref[...]
    tq = q.shape[0]
    lane = lax.broadcasted_iota(jnp.int32, q.shape, 1)
    zero = jnp.zeros_like(q)
    qq = jnp.concatenate([jnp.where(lane < QK_DIM, q, zero), jnp.where(lane >= QK_DIM, q, zero)], axis=0)
    s = _dot_nt(qq, k_ref[...])
    m = jnp.max(s, axis=-1, keepdims=True)
    e = jnp.exp2(s - m)
    acc = _dot(e.astype(BF16), v_ref[...])
    o2 = acc[:, :HEAD_W] / acc[:, HEAD_W:HEAD_W + 1]
    lf = lam_ref[...]
    lam_val = (jnp.exp(jnp.sum(lf[0:1] * lf[1:2], axis=-1, keepdims=True))
               - jnp.exp(jnp.sum(lf[2:3] * lf[3:4], axis=-1, keepdims=True)) + lam_init)
    o = o2[:tq] - lam_val * o2[tq:]
    ms = jnp.mean(o * o, axis=-1, keepdims=True)
    o = o * lax.rsqrt(ms + EPS) * gsub_ref[...] * (1.0 - lam_init)
    o_ref[...] = o.astype(BF16)


def _concat_kernel(*refs, n_streams):
    k_refs, v_refs = refs[:n_streams], refs[n_streams:2 * n_streams]
    ko_ref, vo_ref = refs[2 * n_streams:]
    r0 = 0
    for k_ref, v_ref in zip(k_refs, v_refs):
        n = k_ref.shape[0]
        ko_ref[r0:r0 + n, :] = k_ref[...]
        ones = jnp.ones((n, HEAD_W), vo_ref.dtype)
        for h in range(N_HEADS):
            vo_ref[r0:r0 + n, 2 * h * HEAD_W:(2 * h + 1) * HEAD_W] = v_ref[:, h * HEAD_W:(h + 1) * HEAD_W]
            vo_ref[r0:r0 + n, (2 * h + 1) * HEAD_W:(2 * h + 2) * HEAD_W] = ones
        r0 += n


def _concat_kv(streams):
    b, _, w = streams[0][0].shape
    total = sum(k.shape[1] for k, _ in streams)
    vcol = COL_V // w
    rows = lambda n, width, blk: pl.BlockSpec((None, n, width), lambda bi: (bi, 0, blk))
    return pl.pallas_call(
        functools.partial(_concat_kernel, n_streams=len(streams)),
        grid=(b,),
        in_specs=[rows(k.shape[1], w, 0) for k, _ in streams] + [rows(k.shape[1], w, vcol) for k, _ in streams],
        out_specs=[rows(total, w, 0), rows(total, 2 * w, 0)],
        out_shape=[jax.ShapeDtypeStruct((b, total, w), BF16), jax.ShapeDtypeStruct((b, total, 2 * w), BF16)],
        compiler_params=_cp("parallel"),
        name="concat_kv",
    )(*[k for k, _ in streams], *[h for _, h in streams])


def _diff_attention(q, k, v, lam_l, gsub, lam_init):
    b, lq, w = q.shape
    lk = k.shape[1]
    tq = min(lq, 512)
    in_specs = [pl.BlockSpec((None, tq, HEAD_W), lambda bi, hi, i: (bi, i, hi)),
                pl.BlockSpec((None, lk, HEAD_W), lambda bi, hi, i: (bi, 0, hi)),
                pl.BlockSpec((None, lk, 2 * HEAD_W), lambda bi, hi, i: (bi, 0, hi)),
                pl.BlockSpec(lam_l.shape, lambda bi, hi, i: (0, 0)),
                pl.BlockSpec((1, HEAD_W), lambda bi, hi, i: (0, 0))]
    args = [q, k, v, lam_l, gsub]
    return pl.pallas_call(
        functools.partial(_attn_kernel, lam_init=lam_init),
        grid=(b, N_HEADS, lq // tq),
        in_specs=in_specs,
        out_specs=pl.BlockSpec((None, tq, HEAD_W), lambda bi, hi, i: (bi, i, hi)),
        out_shape=jax.ShapeDtypeStruct((b, lq, w), BF16),
        compiler_params=_cp("parallel", "parallel", "arbitrary"),
        name="diff_attention",
    )(*args)


def _sconv_kernel(h_ref, w_ref, b_ref, o_ref):
    x = h_ref[...].astype(F32)
    n = x.shape[0]
    row = lax.broadcasted_iota(jnp.int32, x.shape, 0)
    prev = jnp.where(row == 0, 0.0, pltpu.roll(x, 1, 0))
    nxt = jnp.where(row == n - 1, 0.0, pltpu.roll(x, n - 1, 0))
    o_ref[...] = prev * w_ref[0:1, :] + x * w_ref[1:2, :] + nxt * w_ref[2:3, :] + b_ref[...]


def _short_conv(h, w, bias):
    b, l, _ = h.shape
    return pl.pallas_call(
        _sconv_kernel,
        grid=(b, 3),
        in_specs=[pl.BlockSpec((None, l, HY_W), lambda bi, j: (bi, 0, COL_HY // HY_W + j)),
                  pl.BlockSpec((3, HY_W), lambda bi, j: (0, j)),
                  pl.BlockSpec((1, HY_W), lambda bi, j: (0, j))],
        out_specs=pl.BlockSpec((None, None, l, HY_W), lambda bi, j: (bi, j, 0, 0)),
        out_shape=jax.ShapeDtypeStruct((b, 3, l, HY_W), F32),
        compiler_params=_cp("parallel", "parallel"),
        name="short_conv",
    )(h, w, bias)


def _filter_kernel(f_ref, w1_ref, b1_ref, fr_ref, w2_ref, b2_ref, w3_ref, dec_ref, o_ref):
    def mm(a, w_ref_):
        ah, al = _split(a)
        wh, wl = _split(w_ref_[...])
        return _dot(ah, wh) + _dot(ah, wl) + _dot(al, wh)
    fr = fr_ref[...]
    h = jnp.sin(fr * (mm(f_ref[...], w1_ref) + b1_ref[...]))
    h = jnp.sin(fr * (mm(h, w2_ref) + b2_ref[...]))
    h = mm(h, w3_ref)
    half = h.shape[1] // 2
    dec = dec_ref[...]
    hf = h[:, :half] * dec
    hb = h[:, half:] * dec
    row = lax.broadcasted_iota(jnp.int32, hb.shape, 0)
    hb = jnp.where(row == 0, 0.0, hb)
    norm = jnp.sum(jnp.abs(hf) + jnp.abs(hb), axis=0, keepdims=True)
    o_ref[:, :half] = (hf + hb) / norm
    o_ref[:, half:] = (hf - hb) / norm


def _hyena_filter_taps(feats, w1, b1, freq, w2, b2, w3, dec):
    l = feats.shape[0]
    n = w3.shape[1]
    full = lambda a: pl.BlockSpec(a.shape, lambda i: (0,) * a.ndim)
    args = (feats, w1, b1, freq, w2, b2, w3, dec)
    return pl.pallas_call(
        _filter_kernel,
        grid=(1,),
        in_specs=[full(a) for a in args],
        out_specs=pl.BlockSpec((l, n), lambda i: (0, 0)),
        out_shape=jax.ShapeDtypeStruct((l, n), F32),
        compiler_params=_cp("arbitrary"),
        name="hyena_filter_taps",
    )(*args)


def _table_mm_kernel(t_ref, x_ref, o_ref):
    o_ref[...] = _dot(t_ref[...], x_ref[...].astype(BF16)).astype(o_ref.dtype)


def _table_matmul(table, x, out_dtype):
    m, k = table.shape
    n = x.shape[1]
    tm = min(m, 512)
    return pl.pallas_call(
        _table_mm_kernel,
        grid=(m // tm,),
        in_specs=[pl.BlockSpec((tm, k), lambda i: (i, 0)),
                  pl.BlockSpec((k, n), lambda i: (0, 0))],
        out_specs=pl.BlockSpec((tm, n), lambda i: (i, 0)),
        out_shape=jax.ShapeDtypeStruct((m, n), out_dtype),
        compiler_params=_cp("parallel"),
        name="table_matmul",
    )(table, x)


def _dftmul_kernel(fc_ref, fs_ref, z_ref, k_ref, p_ref):
    z = z_ref[...].astype(BF16)
    zc = _dot(fc_ref[...], z)
    zs = _dot(fs_ref[...], z)
    p_ref[0] = (zc * k_ref[0] - zs * k_ref[1]).astype(BF16)
    p_ref[1] = (zc * k_ref[2] + zs * k_ref[3]).astype(BF16)


def _dft_multiply(ffwd, z, z_spec, kf):
    l = ffwd.shape[1]
    b = z.shape[0]
    tf = min(l, DFT_ROWS)
    nf = l // tf
    out = pl.pallas_call(
        _dftmul_kernel,
        grid=(nf, b),
        in_specs=[pl.BlockSpec((tf, l), lambda i, bi: (i, 0)),
                  pl.BlockSpec((tf, l), lambda i, bi: (i + nf, 0)),
                  z_spec,
                  pl.BlockSpec((4, tf, HY_W), lambda i, bi: (0, i, 0))],
        out_specs=pl.BlockSpec((None, 2, tf, HY_W), lambda i, bi: (bi, 0, i, 0)),
        out_shape=jax.ShapeDtypeStruct((b, 2, l, HY_W), BF16),
        compiler_params=_cp("parallel", "arbitrary"),
        name="dft_multiply",
    )(ffwd, ffwd, z, kf)
    return out.reshape(b, 2 * l, HY_W)


def _idft_gate_kernel(fi_ref, p_ref, g_ref, z_ref, b_ref, o_ref):
    conv = _dot(fi_ref[...], p_ref[...])
    o_ref[...] = (g_ref[...] * (conv + b_ref[...] * z_ref[...])).astype(o_ref.dtype)


def _idft_gate(finv, p, gate, gate_spec, z, z_spec, bias, out_dtype):
    l = finv.shape[0]
    b = p.shape[0]
    tt = min(l, DFT_ROWS)
    return pl.pallas_call(
        _idft_gate_kernel,
        grid=(l // tt, b),
        in_specs=[pl.BlockSpec((tt, 2 * l), lambda i, bi: (i, 0)),
                  pl.BlockSpec((None, 2 * l, HY_W), lambda i, bi: (bi, 0, 0)),
                  gate_spec, z_spec,
                  pl.BlockSpec((1, HY_W), lambda i, bi: (0, 0))],
        out_specs=pl.BlockSpec((None, tt, HY_W), lambda i, bi: (bi, i, 0)),
        out_shape=jax.ShapeDtypeStruct((b, l, HY_W), out_dtype),
        compiler_params=_cp("parallel", "arbitrary"),
        name="idft_gate",
    )(finv, p, gate, z, bias)


def _hyena_mix(h, conv_w, conv_b, kf, hy_bias, ffwd, finv):
    b, l, _ = h.shape
    tt = min(l, DFT_ROWS)
    u = _short_conv(h, conv_w, conv_b)
    part = lambda j, rows: pl.BlockSpec((None, None, rows, HY_W),
                                        lambda i, bi, j=j: (bi, j, i if rows != l else 0, 0))
    p = _dft_multiply(ffwd, u, part(0, l), kf[0])
    z1 = _idft_gate(finv, p, u, part(1, tt), u, part(0, tt), hy_bias[0:1], F32)
    p = _dft_multiply(ffwd, z1, pl.BlockSpec((None, l, HY_W), lambda i, bi: (bi, 0, 0)), kf[1])
    return _idft_gate(finv, p, u, part(2, tt), z1,
                      pl.BlockSpec((None, tt, HY_W), lambda i, bi: (bi, i, 0)), hy_bias[1:2], BF16)


def _fn1_kernel(z_ref, m_ref, o_ref):
    r = _dot(z_ref[...], m_ref[...])
    half = r.shape[1] // 2
    o_ref[0] = r[:, :half].astype(BF16)
    o_ref[1] = r[:, half:].astype(BF16)


def _fourier_mix(h, m1, t2):
    b, l, _ = h.shape
    tm = min(l, DFT_ROWS)
    zz = pl.pallas_call(
        _fn1_kernel,
        grid=(b, l // tm),
        in_specs=[pl.BlockSpec((None, tm, FN_W), lambda bi, i: (bi, i, COL_FN // FN_W)),
                  pl.BlockSpec((FN_W, 2 * FN_W), lambda bi, i: (0, 0))],
        out_specs=pl.BlockSpec((None, 2, tm, FN_W), lambda bi, i: (bi, 0, i, 0)),
        out_shape=jax.ShapeDtypeStruct((b, 2, l, FN_W), BF16),
        compiler_params=_cp("parallel", "parallel"),
        name="fnet_channels",
    )(h, m1).reshape(b, 2 * l, FN_W)
    return pl.pallas_call(
        _table_mm_kernel,
        grid=(l // tm, b),
        in_specs=[pl.BlockSpec((tm, 2 * l), lambda i, bi: (i, 0)),
                  pl.BlockSpec((None, 2 * l, FN_W), lambda i, bi: (bi, 0, 0))],
        out_specs=pl.BlockSpec((None, tm, FN_W), lambda i, bi: (bi, i, 0)),
        out_shape=jax.ShapeDtypeStruct((b, l, FN_W), BF16),
        compiler_params=_cp("parallel", "arbitrary"),
        name="fnet_positions",
    )(t2, zz)


def _merge_kernel(hy_ref, fn_ref, at_ref, g_ref, x_ref, gm_ref, why_ref, wfn_ref, wat_ref, wout_ref, o_ref):
    d = x_ref.shape[-1]
    g = 1.0 / (1.0 + jnp.exp(-g_ref[...].astype(F32)))
    y = (g[:, :d] * _dot(hy_ref[...], why_ref[...])
         + g[:, d:2 * d] * _dot(fn_ref[...], wfn_ref[...])
         + g[:, 2 * d:] * _dot(at_ref[...], wat_ref[...]))
    mix = _dot(y.astype(BF16), wout_ref[...])
    o_ref[...] = x_ref[...] + gm_ref[...] * mix


def _merge(hyo, fno, att, h, x, mods, layer, row_fn, w_hy, w_fn, w_at, w_out):
    b, l, d = x.shape
    tm = min(l, 1024)
    full = lambda a: pl.BlockSpec(a.shape, lambda bi, i: (0,) * a.ndim)
    tok = lambda wdt, blk=0: pl.BlockSpec((None, tm, wdt), lambda bi, i: (bi, i, blk))
    return pl.pallas_call(
        _merge_kernel,
        grid=(b, l // tm),
        in_specs=[tok(HY_W), tok(FN_W), tok(N_HEADS * HEAD_W), tok(3 * d, COL_GATE), tok(d),
                  pl.BlockSpec((None, None, None, 1, d), _mod_spec(layer, 2, lambda bi, i: row_fn(bi))),
                  full(w_hy), full(w_fn), full(w_at), full(w_out)],
        out_specs=tok(d),
        out_shape=jax.ShapeDtypeStruct((b, l, d), F32),
        compiler_params=_cp("parallel", "parallel"),
        name="merge_residual",
    )(hyo, fno, att, h, x, mods, w_hy, w_fn, w_at, w_out)


def _peer_q_kernel(x_ref, g_ref, sh_ref, sc_ref, wq_ref, kh_ref, kl_ref, s_ref, nt_ref):
    n = _modulated_norm(x_ref[...], g_ref[...], sh_ref[...], sc_ref[...])
    nt_ref[...] = n.T.astype(BF16)
    q = _dot(n.astype(BF16), wq_ref[...])
    dq = kh_ref.shape[1]
    for hp in range(kh_ref.shape[0]):
        qh, ql = _split(q[:, hp * dq:(hp + 1) * dq])
        kh = kh_ref[hp]
        s = _dot(qh, kh) + _dot(ql, kh) + _dot(qh, kl_ref[hp])
        s_ref[hp] = s.T


def _peer_scores(x, g, mods, layer, row_fn, wq, keys_t):
    b, l, d = x.shape
    tt = min(l, 512)
    nt = l // tt
    nhp = 2 * PEER_HEADS
    mrow = lambda bi, i: row_fn(bi)
    full = lambda a: pl.BlockSpec(a.shape, lambda bi, i: (0,) * a.ndim)
    return pl.pallas_call(
        _peer_q_kernel,
        grid=(b, nt),
        in_specs=[pl.BlockSpec((None, tt, d), lambda bi, i: (bi, i, 0)),
                  pl.BlockSpec((1, d), lambda bi, i: (0, 0)),
                  pl.BlockSpec((None, None, None, 1, d), _mod_spec(layer, 3, mrow)),
                  pl.BlockSpec((None, None, None, 1, d), _mod_spec(layer, 4, mrow)),
                  full(wq), full(keys_t[0]), full(keys_t[1])],
        out_specs=[pl.BlockSpec((nhp, PEER_NKEYS, tt), lambda bi, i: (0, 0, bi * nt + i)),
                   pl.BlockSpec((d, tt), lambda bi, i: (0, bi * nt + i))],
        out_shape=[jax.ShapeDtypeStruct((nhp, PEER_NKEYS, b * l), F32),
                   jax.ShapeDtypeStruct((d, b * l), BF16)],
        compiler_params=_cp("parallel", "parallel"),
        name="peer_scores",
    )(x, g, mods, mods, wq, keys_t[0], keys_t[1])


def _merge_sort_pairs(n):
    pairs = []
    p = 1
    while p < n:
        k = p
        while k >= 1:
            for j in range(k % p, n - k, 2 * k):
                for i in range(min(k, n - j - k)):
                    if (i + j) // (2 * p) == (i + j + k) // (2 * p):
                        pairs.append((i + j, i + j + k))
            k //= 2
        p *= 2
    return pairs


def _top_rows_grouped(s, k):
    groups = s.shape[0] // 8
    v = [s[g * 8:(g + 1) * 8, :] for g in range(groups)]
    for i, j in _merge_sort_pairs(groups):
        v[i], v[j] = jnp.maximum(v[i], v[j]), jnp.minimum(v[i], v[j])
    v.append(jnp.full_like(v[0], NEG))
    vals = []
    for r in range(k):
        m = jnp.max(v[0], axis=0, keepdims=True)
        vals.append(m)
        taken = v[0] == m
        for d in range(min(groups, k - 1 - r)):
            v[d] = jnp.where(taken, v[d + 1], v[d])
    return vals


def _peer_expert_kernel(s_ref, nt_ref, u_ref, vt_ref, x_ref, gm_ref, o_ref,
                        tau_ref, l1_ref, l2_ref, l1c_ref, hid0_ref, hid1_ref, p0_ref, p1_ref, acc_ref):
    s = pl.program_id(1)
    nc = pl.num_programs(1) - 2
    k = PEER_TOPK
    tt = nt_ref.shape[1]
    n_i = u_ref.shape[0] // PEER_NKEYS
    piece = GATE_VREGS * 8 * 128 // tt

    def hidden(hid_ref):
        hid_ref[...] = _dot(u_ref[...], nt_ref[...])

    def stages(chunk, hid_w, hid_r, p_w, p_r):
        first = pl.multiple_of(chunk * n_i, n_i)
        for h in range(PEER_HEADS):
            l1c_ref[h] = l1_ref[h, pl.ds(first, n_i), :]
        hidden(hid_w)
        acc_ref[...] += _dot(vt_ref[...], p_r[...])
        for ii in range(n_i):
            l1_row = [jnp.broadcast_to(l1c_ref[h, ii:ii + 1, :], (8, tt))[None] for h in range(PEER_HEADS)]
            for jp in range(PEER_NKEYS // piece):
                js = slice(jp * piece, (jp + 1) * piece)
                gate = jnp.zeros((piece // 8, 8, tt), F32)
                for h in range(PEER_HEADS):
                    logw = l1_row[h] + l2_ref[h, js, :].reshape(piece // 8, 8, tt)
                    gate = gate + jnp.where(logw >= tau_ref[h][None], jnp.exp2(logw), 0.0)
                rows = slice(ii * PEER_NKEYS + jp * piece, ii * PEER_NKEYS + (jp + 1) * piece)
                hid = hid_r[rows, :]
                act = hid * (1.0 + lax.erf(hid * (2.0 ** -0.5)))
                p_w[rows, :] = (gate.reshape(piece, tt) * act).astype(BF16)

    @pl.when(s == 0)
    def _():
        acc_ref[...] = jnp.zeros_like(acc_ref)
        p1_ref[...] = jnp.zeros_like(p1_ref)
        hidden(hid0_ref)
        width = min(tt, 256)
        parts = tt // width

        def select(it, carry):
            h = it // parts
            cols = pl.ds(pl.multiple_of((it % parts) * width, width), width)
            s1 = s_ref[2 * h, :, cols]
            s2 = s_ref[2 * h + 1, :, cols]
            a = _top_rows_grouped(s1, k + 1)
            b = _top_rows_grouped(s2, k + 1)
            cand = [a[i] + b[j] for i in range(k + 1) for j in range((k + 1) // (i + 1))]
            pad = (-len(cand)) % 64
            top = _top_rows_grouped(jnp.concatenate(cand + [jnp.full_like(a[0], NEG)] * pad, axis=0), k + 1)
            z = jnp.zeros_like(top[0])
            for t in top[:k]:
                z = z + jnp.exp(t - top[0])
            shift = top[0] + jnp.log(z)
            log2e = 1.0 / math.log(2.0)
            l1_ref[h, :, cols] = s1 * log2e
            l2_ref[h, :, cols] = (s2 - shift) * log2e - 1.0
            tau = (0.5 * (top[k - 1] + top[k]) - shift) * log2e - 1.0
            tau_ref[h, :, cols] = jnp.broadcast_to(tau, (8, width))
            return carry

        lax.fori_loop(0, PEER_HEADS * parts, select, 0)

    @pl.when((s >= 1) & (s <= nc) & (s % 2 == 1))
    def _():
        stages(s - 1, hid1_ref, hid0_ref, p0_ref, p1_ref)

    @pl.when((s >= 1) & (s <= nc) & (s % 2 == 0))
    def _():
        stages(s - 1, hid0_ref, hid1_ref, p1_ref, p0_ref)

    @pl.when(s == nc + 1)
    def _():
        last = p0_ref if (nc - 1) % 2 == 0 else p1_ref
        out_t = acc_ref[...] + _dot(vt_ref[...], last[...])
        o_ref[...] = x_ref[...] + gm_ref[...] * out_t.T


def _peer_experts(scores, n_t, u, v_t, x, mods, layer, row_fn):
    b, l, d = x.shape
    tt = min(l, PEER_TOKENS)
    nt = l // tt
    nc, _, ec = v_t.shape
    nhp = scores.shape[0]
    return pl.pallas_call(
        _peer_expert_kernel,
        grid=(b * nt, nc + 2),
        in_specs=[pl.BlockSpec((nhp, PEER_NKEYS, tt), lambda t, s: (0, 0, t)),
                  pl.BlockSpec((d, tt), lambda t, s: (0, t)),
                  pl.BlockSpec((ec, d), lambda t, s: (jnp.minimum(s, nc - 1), 0)),
                  pl.BlockSpec((None, d, ec), lambda t, s: (jnp.clip(s - 2, 0, nc - 1), 0, 0)),
                  pl.BlockSpec((None, tt, d), lambda t, s: (t // nt, t % nt, 0)),
                  pl.BlockSpec((None, None, None, 1, d), _mod_spec(layer, 5, lambda t, s: row_fn(t // nt)))],
        out_specs=pl.BlockSpec((None, tt, d), lambda t, s: (t // nt, t % nt, 0)),
        out_shape=jax.ShapeDtypeStruct((b, l, d), F32),
        scratch_shapes=[pltpu.VMEM((PEER_HEADS, 8, tt), F32),
                        pltpu.VMEM((PEER_HEADS, PEER_NKEYS, tt), F32),
                        pltpu.VMEM((PEER_HEADS, PEER_NKEYS, tt), F32),
                        pltpu.VMEM((PEER_HEADS, ec // PEER_NKEYS, tt), F32),
                        pltpu.VMEM((ec, tt), F32),
                        pltpu.VMEM((ec, tt), F32),
                        pltpu.VMEM((ec, tt), BF16),
                        pltpu.VMEM((ec, tt), BF16),
                        pltpu.VMEM((d, tt), F32)],
        compiler_params=_cp("parallel", "arbitrary"),
        name="peer_experts",
    )(scores, n_t, u, v_t, x, mods)


def _cast_kernel(a_ref, o_ref):
    o_ref[...] = a_ref[...].astype(o_ref.dtype)


def _cast_bf16(a, layer):
    _, rows, d = a.shape
    return pl.pallas_call(
        _cast_kernel,
        grid=(rows // PEER_CHUNK,),
        in_specs=[pl.BlockSpec((None, PEER_CHUNK, d), lambda i: (layer, i, 0))],
        out_specs=pl.BlockSpec((PEER_CHUNK, d), lambda i: (i, 0)),
        out_shape=jax.ShapeDtypeStruct((rows, d), BF16),
        compiler_params=_cp("parallel"),
        name="cast_bf16",
    )(a)


def _transpose_cast_kernel(a_ref, o_ref):
    o_ref[...] = a_ref[...].T.astype(o_ref.dtype)


def _chunk_transpose_bf16(a, layer):
    _, rows, d = a.shape
    nc = rows // PEER_CHUNK
    return pl.pallas_call(
        _transpose_cast_kernel,
        grid=(nc,),
        in_specs=[pl.BlockSpec((None, PEER_CHUNK, d), lambda i: (layer, i, 0))],
        out_specs=pl.BlockSpec((None, d, PEER_CHUNK), lambda i: (i, 0, 0)),
        out_shape=jax.ShapeDtypeStruct((nc, d, PEER_CHUNK), BF16),
        compiler_params=_cp("parallel"),
        name="chunk_transpose_bf16",
    )(a)


def _permute_in_weights(w, layer):
    _, d, n = w.shape
    tn = 512
    nb = n // tn
    shift = (n - 3 * d) // tn
    return pl.pallas_call(
        _cast_kernel,
        grid=(nb,),
        in_specs=[pl.BlockSpec((None, d, tn), lambda j: (layer, 0, (j + shift) % nb))],
        out_specs=pl.BlockSpec((d, tn), lambda j: (0, j)),
        out_shape=jax.ShapeDtypeStruct((d, n), BF16),
        compiler_params=_cp("parallel"),
        name="permute_in_weights",
    )(w)


def _peer(x, g, mods, layer, row_fn, wq_t, keys, u, v_t):
    scores, n_t = _peer_scores(x, g, mods, layer, row_fn, wq_t, keys)
    return _peer_experts(scores, n_t, u, v_t, x, mods, layer, row_fn)


def _dft_tables(l):
    cos, sin = _cos_sin_table(l, l, 2 * l)
    t = jnp.arange(l, dtype=jnp.int32)[None, :]
    first = (jnp.arange(l) == 0)[:, None]
    sin = jnp.where(first, jnp.where(t % 2 == 0, 1.0, -1.0), sin)
    ffwd = jnp.concatenate([cos, sin], axis=0).astype(BF16)
    return ffwd, ffwd.T


def _cos_sin_table(nf, nt, n):
    step = 64
    f = jnp.arange(nf, dtype=jnp.int32)[:, None]
    angle = lambda prod: (2.0 * math.pi) * ((prod % n).astype(F32) / n)
    a = angle(f * (step * jnp.arange(nt // step, dtype=jnp.int32)[None, :]))
    b = angle(f * jnp.arange(step, dtype=jnp.int32)[None, :])
    ca, sa = jnp.cos(a)[:, :, None], jnp.sin(a)[:, :, None]
    cb, sb = jnp.cos(b)[:, None, :], jnp.sin(b)[:, None, :]
    return (ca * cb - sa * sb).reshape(nf, nt), (sa * cb + ca * sb).reshape(nf, nt)


def _fnet_tables(l):
    t2 = jnp.concatenate(_cos_sin_table(l, l, l), axis=1).astype(BF16)
    k = np.arange(FN_GROUP)
    ang64 = 2.0 * np.pi * ((k[:, None] * k[None, :]) % FN_GROUP) / FN_GROUP
    eye = np.eye(FN_W // FN_GROUP)
    scale = 1.0 / math.sqrt(FN_GROUP * l)
    m1 = np.concatenate([np.kron(eye, np.cos(ang64)), -np.kron(eye, np.sin(ang64))], axis=1) * scale
    return jnp.asarray(m1, F32).astype(BF16), t2


def _rope_tables(l):
    rows = l // GRID_W
    row = jnp.repeat(jnp.arange(rows), GRID_W).astype(F32)
    col = jnp.tile(jnp.arange(GRID_W), rows).astype(F32)
    half = QK_DIM // 2
    inv = ROPE_BASE ** (-jnp.arange(0, half, 2, dtype=F32) / half)
    ang = jnp.stack([row[:, None] * inv, col[:, None] * inv], axis=1)
    cos = jnp.repeat(jnp.cos(ang)[:, :, None, :], 2, axis=2)
    sin = jnp.sin(ang)
    sin = jnp.stack([-sin, sin], axis=2)
    rep = lambda a: jnp.tile(a.reshape(l, QK_DIM), (1, 2 * N_HEADS))
    return rep(cos), rep(sin)


def _filter_features(l):
    pos = jnp.arange(l, dtype=F32)
    t = pos / max(l - 1, 1)
    w = 2.0 * math.pi * pos / l
    f = jnp.linspace(1e-4, HY_BANDS - 1, HY_BANDS, dtype=F32)
    feats = jnp.concatenate([t[:, None], jnp.cos(w[:, None] * f), -jnp.sin(w[:, None] * f)], axis=-1)
    feats = jnp.pad(feats, ((0, 0), (0, 64 - HY_EMB)))
    deltas = jnp.abs(jnp.linspace(HY_MIN_DECAY, HY_MAX_DECAY, HY_W, dtype=F32))
    dec = jnp.exp(-t[:, None] * deltas)
    return feats, jnp.tile(dec, (1, HY_ORDER))


def _hyena_filters(l, tabs, ffwd, w1, b1, freq, w2, b2, w3):
    feats, dec = tabs
    taps = _hyena_filter_taps(feats, jnp.pad(w1, ((0, 64 - HY_EMB), (0, 0))), b1[None], freq[None],
                              w2, b2[None], w3, dec)
    kf = _table_matmul(ffwd, taps, F32)
    half = HY_ORDER * HY_W
    kc = kf[:l, :half]
    nyq = kf[l, :half]
    ks = kf[l:, half:]
    n = 2.0 * l
    first = (jnp.arange(l) == 0)[:, None]
    wc = jnp.where(first, 1.0 / n, 2.0 / n)
    ka = kc * wc
    kb = jnp.where(first, 0.0, ks * (2.0 / n))
    kd = jnp.where(first, nyq[None, :] / n, kc * (2.0 / n))
    stack = jnp.stack([ka, kb, kb, kd], axis=0)
    return jnp.moveaxis(stack.reshape(4, l, HY_ORDER, HY_W), 2, 0)


def kernel(x, c, ctx, c_ctx, w_ada, b_ada, g_mix, g_ffn, w_in, hy_conv_w, hy_conv_b, hy_w1, hy_b1, hy_freq, hy_w2, hy_b2, hy_w3, hy_bias, g_q, g_k, lam, g_sub, w_hy, w_fn, w_at, w_out, peer_wq, peer_keys, peer_u, peer_v):
    bsz, seq, d = x.shape
    clen = ctx.shape[1]
    depth = w_ada.shape[0]

    cc = jnp.concatenate([c, c_ctx[None], jnp.zeros((MOD_ROWS - bsz - 1, d), F32)], axis=0)
    mods = _ada_mods(cc, w_ada, b_ada)
    lat_row = lambda bi: bi
    ctx_row = lambda bi: bsz

    rope = _rope_tables(seq)
    tabs = {n: dict(dft=_dft_tables(n), fnet=_fnet_tables(n), feat=_filter_features(n)) for n in (seq, clen)}
    w = N_HEADS * HEAD_W
    lane = np.arange(w)
    bd = jnp.asarray((lane[:, None] // QK_DIM == lane[None, :] // QK_DIM) / QK_DIM, F32).astype(BF16)

    xl, xc = x, ctx
    for l in range(depth):
        last = l == depth - 1
        lam_init = 0.8 - 0.6 * math.exp(-0.3 * l)
        w_perm = _permute_in_weights(w_in, l)
        gq = jnp.tile(g_q[l].reshape(1, HEAD_W), (1, N_HEADS)) * (QK_DIM ** -0.5 / math.log(2.0))
        gk = jnp.tile(g_k[l].reshape(1, HEAD_W), (1, N_HEADS))
        gsub = g_sub[l][None]
        wts = [a[l].astype(BF16) for a in (w_hy, w_fn, w_at, w_out)]
        filt_args = (hy_w1[l], hy_b1[l], hy_freq[l], hy_w2[l], hy_b2[l], hy_w3[l])

        h_l = _in_projection(xl, g_mix[l][None], mods, l, lat_row, w_perm, 0, P_IN)
        h_c = _in_projection(xc, g_mix[l][None], mods, l, ctx_row, w_perm, COL_Q if last else 0, P_IN)
        q_l, k_l = _qk_prep(h_l, gq, gk, bd, rope)
        q_c, k_c = _qk_prep(h_c, gq, gk, bd, None)
        k_all, v_all = _concat_kv([(k_l, h_l), (k_c, h_c)])
        att_l = _diff_attention(q_l, k_all, v_all, lam[l], gsub, lam_init)
        ffwd, finv = tabs[seq]["dft"]
        kf = _hyena_filters(seq, tabs[seq]["feat"], ffwd, *filt_args)
        hyo_l = _hyena_mix(h_l, hy_conv_w[l], hy_conv_b[l][None], kf, hy_bias[l], ffwd, finv)
        fno_l = _fourier_mix(h_l, *tabs[seq]["fnet"])
        if not last:
            att_c = _diff_attention(q_c, *_concat_kv([(k_c, h_c)]), lam[l], gsub, lam_init)
            ffwd_c, finv_c = tabs[clen]["dft"]
            kf_c = _hyena_filters(clen, tabs[clen]["feat"], ffwd_c, *filt_args)
            hyo_c = _hyena_mix(h_c, hy_conv_w[l], hy_conv_b[l][None], kf_c, hy_bias[l], ffwd_c, finv_c)
            fno_c = _fourier_mix(h_c, *tabs[clen]["fnet"])
            xc = _merge(hyo_c, fno_c, att_c, h_c, xc, mods, l, ctx_row, *wts)
        xl = _merge(hyo_l, fno_l, att_l, h_l, xl, mods, l, lat_row, *wts)

        wq_t = peer_wq[l].astype(BF16)
        keys = _split(jnp.swapaxes(peer_keys[l].reshape(2 * PEER_HEADS, PEER_NKEYS, -1), 1, 2))
        u = _cast_bf16(peer_u, l)
        v_t = _chunk_transpose_bf16(peer_v, l)
        if not last:
            xc = _peer(xc, g_ffn[l][None], mods, l, ctx_row, wq_t, keys, u, v_t)
        xl = _peer(xl, g_ffn[l][None], mods, l, lat_row, wq_t, keys, u, v_t)
    return xl
```

```python
import functools
import math

import jax
import jax.numpy as jnp
import numpy as np
from jax import lax
from jax.experimental import pallas as pl
from jax.experimental.pallas import tpu as pltpu

F32 = jnp.float32
BF16 = jnp.bfloat16

EPS = 1e-6
GRID_W = 64
ROPE_BASE = 10000.0
N_HEADS = 4
QK_DIM = 64
HEAD_W = 2 * QK_DIM
HY_W = 256
HY_ORDER = 2
HY_EMB = 33
HY_BANDS = (HY_EMB - 1) // 2
HY_MIN_DECAY = math.log(1e-2) / 1.5
HY_MAX_DECAY = math.log(1e-2) / 0.3
FN_GROUP = 64
FN_W = 256
PEER_HEADS = 8
PEER_NKEYS = 128
PEER_TOPK = 16
N_MOD = 6
MOD_ROWS = 16
NEG = -3.0e38
GATE_VREGS = 8
PEER_TOKENS = 256
PEER_CHUNK = 2048
DFT_ROWS = 1024

VMEM_LIMIT = 56 * 1024 * 1024

COL_GATE = 0
COL_HY = 3072
COL_FN = 3840
COL_Q = 4096
COL_K = 4608
COL_V = 5120
P_IN = 5632


def _cp(*sem):
    return pltpu.CompilerParams(dimension_semantics=sem, vmem_limit_bytes=VMEM_LIMIT)


def _dot(a, b):
    return jnp.dot(a, b, preferred_element_type=F32)


def _dot_nt(a, b):
    return lax.dot_general(a, b, (((1,), (1,)), ((), ())), preferred_element_type=F32)


def _split(a):
    hi = a.astype(BF16)
    lo = (a - hi.astype(F32)).astype(BF16)
    return hi, lo


def _modulated_norm(x, g, shift, scale):
    ms = jnp.mean(x * x, axis=-1, keepdims=True)
    y = x * lax.rsqrt(ms + EPS) * g
    return y * (1.0 + scale) + shift


def _mod_spec(layer, chunk, row_fn):
    def imap(*idx):
        return (layer, row_fn(*idx), chunk, 0, 0)
    return imap


def _ada_kernel(c_ref, w_ref, b_ref, o_ref):
    c = c_ref[...]
    a = c / (1.0 + jnp.exp(-c))
    ah, al = _split(a)
    wh, wl = _split(w_ref[...])
    o_ref[...] = _dot(ah, wh) + _dot(ah, wl) + _dot(al, wh) + b_ref[...]


def _ada_mods(cc, w_ada, b_ada):
    depth, d, n = w_ada.shape
    tn = 512
    out = pl.pallas_call(
        _ada_kernel,
        grid=(depth, n // tn),
        in_specs=[pl.BlockSpec((MOD_ROWS, d), lambda l, j: (0, 0)),
                  pl.BlockSpec((None, d, tn), lambda l, j: (l, 0, j)),
                  pl.BlockSpec((None, 1, tn), lambda l, j: (l, 0, j))],
        out_specs=pl.BlockSpec((None, MOD_ROWS, tn), lambda l, j: (l, 0, j)),
        out_shape=jax.ShapeDtypeStruct((depth, MOD_ROWS, n), F32),
        compiler_params=_cp("parallel", "parallel"),
        name="ada_mods",
    )(cc, w_ada, b_ada.reshape(depth, 1, n))
    return out.reshape(depth, MOD_ROWS, N_MOD, 1, d)


def _inproj_kernel(x_ref, g_ref, sh_ref, sc_ref, w_ref, o_ref, xn_ref):
    @pl.when(pl.program_id(2) == 0)
    def _():
        xn_ref[...] = _modulated_norm(x_ref[...], g_ref[...], sh_ref[...], sc_ref[...]).astype(BF16)
    o_ref[...] = _dot(xn_ref[...], w_ref[...]).astype(o_ref.dtype)


def _in_projection(x, g, mods, layer, row_fn, w, col_lo, col_hi):
    b, l, d = x.shape
    tm = min(l, 1024)
    wide = w.shape[1] // 2
    tn = wide if (col_lo % wide == 0 and (col_hi - col_lo) % wide == 0) else 512
    j0 = col_lo // tn
    nj = (col_hi - col_lo) // tn
    mrow = lambda bi, i, j: row_fn(bi)
    return pl.pallas_call(
        _inproj_kernel,
        grid=(b, l // tm, nj),
        in_specs=[pl.BlockSpec((None, tm, d), lambda bi, i, j: (bi, i, 0)),
                  pl.BlockSpec((1, d), lambda bi, i, j: (0, 0)),
                  pl.BlockSpec((None, None, None, 1, d), _mod_spec(layer, 0, mrow)),
                  pl.BlockSpec((None, None, None, 1, d), _mod_spec(layer, 1, mrow)),
                  pl.BlockSpec((d, tn), lambda bi, i, j: (0, j + j0))],
        out_specs=pl.BlockSpec((None, tm, tn), lambda bi, i, j: (bi, i, j + j0)),
        out_shape=jax.ShapeDtypeStruct((b, l, w.shape[1]), BF16),
        scratch_shapes=[pltpu.VMEM((tm, d), BF16)],
        compiler_params=_cp("parallel", "parallel", "arbitrary"),
        name="in_projection",
    )(x, g, mods, mods, w)


def _qkprep_kernel(*refs, rope):
    if rope:
        q_ref, k_ref, gq_ref, gk_ref, bd_ref, cos_ref, sin_ref, qo_ref, ko_ref = refs
    else:
        q_ref, k_ref, gq_ref, gk_ref, bd_ref, qo_ref, ko_ref = refs
    for src, g_ref, dst in ((q_ref, gq_ref, qo_ref), (k_ref, gk_ref, ko_ref)):
        x = src[...].astype(F32)
        hi, lo = _split(x * x)
        ms = _dot(hi, bd_ref[...]) + _dot(lo, bd_ref[...])
        y = x * lax.rsqrt(ms + EPS) * g_ref[...]
        if rope:
            w = y.shape[1]
            lane = lax.broadcasted_iota(jnp.int32, y.shape, 1)
            first = (lane % (QK_DIM // 2)) < (QK_DIM // 4)
            partner = jnp.where(first, pltpu.roll(y, w - QK_DIM // 4, 1), pltpu.roll(y, QK_DIM // 4, 1))
            y = y * cos_ref[...] + partner * sin_ref[...]
        dst[...] = y.astype(BF16)


def _qk_prep(h, gq, gk, bd, rope_tabs):
    b, l, _ = h.shape
    w = N_HEADS * HEAD_W
    tm = min(l, 1024)
    rope = rope_tabs is not None
    in_specs = [pl.BlockSpec((None, tm, w), lambda bi, i: (bi, i, COL_Q // w)),
                pl.BlockSpec((None, tm, w), lambda bi, i: (bi, i, COL_K // w)),
                pl.BlockSpec((1, w), lambda bi, i: (0, 0)),
                pl.BlockSpec((1, w), lambda bi, i: (0, 0)),
                pl.BlockSpec((w, w), lambda bi, i: (0, 0))]
    args = [h, h, gq, gk, bd]
    if rope:
        in_specs += [pl.BlockSpec((tm, w), lambda bi, i: (i, 0))] * 2
        args += list(rope_tabs)
    return pl.pallas_call(
        functools.partial(_qkprep_kernel, rope=rope),
        grid=(b, l // tm),
        in_specs=in_specs,
        out_specs=[pl.BlockSpec((None, tm, w), lambda bi, i: (bi, i, 0))] * 2,
        out_shape=[jax.ShapeDtypeStruct((b, l, w), BF16)] * 2,
        compiler_params=_cp("parallel", "parallel"),
        name="qk_prep",
    )(*args)


def _attn_kernel(q_ref, k_ref, v_ref, lam_ref, gsub_ref, o_ref, *, lam_init):
    q = q_ref[...]
    tq = q.shape[0]
    lane = lax.broadcasted_iota(jnp.int32, q.shape, 1)
    zero = jnp.zeros_like(q)

    def attend(q_map):
        s = _dot_nt(q_map, k_ref[...])
        e = jnp.exp2(s - jnp.max(s, axis=-1, keepdims=True))
        acc = _dot(e.astype(BF16), v_ref[...])
        return acc[:, :HEAD_W] / acc[:, HEAD_W:HEAD_W + 1]

    lf = lam_ref[...]
    lam_val = (jnp.exp(jnp.sum(lf[0:1] * lf[1:2], axis=-1, keepdims=True))
               - jnp.exp(jnp.sum(lf[2:3] * lf[3:4], axis=-1, keepdims=True)) + lam_init)
    o = attend(jnp.where(lane < QK_DIM, q, zero)) - lam_val * attend(jnp.where(lane >= QK_DIM, q, zero))
    ms = jnp.mean(o * o, axis=-1, keepdims=True)
    o = o * lax.rsqrt(ms + EPS) * gsub_ref[...] * (1.0 - lam_init)
    o_ref[...] = o.astype(BF16)


def _concat_kernel(*refs, n_streams):
    k_refs, v_refs = refs[:n_streams], refs[n_streams:2 * n_streams]
    ko_ref, vo_ref = refs[2 * n_streams:]
    r0 = 0
    for k_ref, v_ref in zip(k_refs, v_refs):
        n = k_ref.shape[0]
        ko_ref[r0:r0 + n, :] = k_ref[...]
        ones = jnp.ones((n, HEAD_W), vo_ref.dtype)
        for h in range(N_HEADS):
            vo_ref[r0:r0 + n, 2 * h * HEAD_W:(2 * h + 1) * HEAD_W] = v_ref[:, h * HEAD_W:(h + 1) * HEAD_W]
            vo_ref[r0:r0 + n, (2 * h + 1) * HEAD_W:(2 * h + 2) * HEAD_W] = ones
        r0 += n


def _concat_kv(streams):
    b, _, w = streams[0][0].shape
    total = sum(k.shape[1] for k, _ in streams)
    vcol = COL_V // w
    rows = lambda n, width, blk: pl.BlockSpec((None, n, width), lambda bi: (bi, 0, blk))
    return pl.pallas_call(
        functools.partial(_concat_kernel, n_streams=len(streams)),
        grid=(b,),
        in_specs=[rows(k.shape[1], w, 0) for k, _ in streams] + [rows(k.shape[1], w, vcol) for k, _ in streams],
        out_specs=[rows(total, w, 0), rows(total, 2 * w, 0)],
        out_shape=[jax.ShapeDtypeStruct((b, total, w), BF16), jax.ShapeDtypeStruct((b, total, 2 * w), BF16)],
        compiler_params=_cp("parallel"),
        name="concat_kv",
    )(*[k for k, _ in streams], *[h for _, h in streams])


def _diff_attention(q, k, v, lam_l, gsub, lam_init):
    b, lq, w = q.shape
    lk = k.shape[1]
    tq = min(lq, 512)
    in_specs = [pl.BlockSpec((None, tq, HEAD_W), lambda bi, hi, i: (bi, i, hi)),
                pl.BlockSpec((None, lk, HEAD_W), lambda bi, hi, i: (bi, 0, hi)),
                pl.BlockSpec((None, lk, 2 * HEAD_W), lambda bi, hi, i: (bi, 0, hi)),
                pl.BlockSpec(lam_l.shape, lambda bi, hi, i: (0, 0)),
                pl.BlockSpec((1, HEAD_W), lambda bi, hi, i: (0, 0))]
    args = [q, k, v, lam_l, gsub]
    return pl.pallas_call(
        functools.partial(_attn_kernel, lam_init=lam_init),
        grid=(b, N_HEADS, lq // tq),
        in_specs=in_specs,
        out_specs=pl.BlockSpec((None, tq, HEAD_W), lambda bi, hi, i: (bi, i, hi)),
        out_shape=jax.ShapeDtypeStruct((b, lq, w), BF16),
        compiler_params=_cp("parallel", "parallel", "arbitrary"),
        name="diff_attention",
    )(*args)


def _sconv_kernel(h_ref, w_ref, b_ref, o_ref):
    x = h_ref[...].astype(F32)
    n = x.shape[0]
    row = lax.broadcasted_iota(jnp.int32, x.shape, 0)
    prev = jnp.where(row == 0, 0.0, pltpu.roll(x, 1, 0))
    nxt = jnp.where(row == n - 1, 0.0, pltpu.roll(x, n - 1, 0))
    o_ref[...] = prev * w_ref[0:1, :] + x * w_ref[1:2, :] + nxt * w_ref[2:3, :] + b_ref[...]


def _short_conv(h, w, bias):
    b, l, _ = h.shape
    return pl.pallas_call(
        _sconv_kernel,
        grid=(b, 3),
        in_specs=[pl.BlockSpec((None, l, HY_W), lambda bi, j: (bi, 0, COL_HY // HY_W + j)),
                  pl.BlockSpec((3, HY_W), lambda bi, j: (0, j)),
                  pl.BlockSpec((1, HY_W), lambda bi, j: (0, j))],
        out_specs=pl.BlockSpec((None, None, l, HY_W), lambda bi, j: (bi, j, 0, 0)),
        out_shape=jax.ShapeDtypeStruct((b, 3, l, HY_W), F32),
        compiler_params=_cp("parallel", "parallel"),
        name="short_conv",
    )(h, w, bias)


def _filter_kernel(f_ref, w1_ref, b1_ref, fr_ref, w2_ref, b2_ref, w3_ref, dec_ref, o_ref):
    def mm(a, w_ref_):
        ah, al = _split(a)
        wh, wl = _split(w_ref_[...])
        return _dot(ah, wh) + _dot(ah, wl) + _dot(al, wh)
    fr = fr_ref[...]
    h = jnp.sin(fr * (mm(f_ref[...], w1_ref) + b1_ref[...]))
    h = jnp.sin(fr * (mm(h, w2_ref) + b2_ref[...]))
    h = mm(h, w3_ref)
    half = h.shape[1] // 2
    dec = dec_ref[...]
    hf = h[:, :half] * dec
    hb = h[:, half:] * dec
    row = lax.broadcasted_iota(jnp.int32, hb.shape, 0)
    hb = jnp.where(row == 0, 0.0, hb)
    norm = jnp.sum(jnp.abs(hf) + jnp.abs(hb), axis=0, keepdims=True)
    o_ref[:, :half] = (hf + hb) / norm
    o_ref[:, half:] = (hf - hb) / norm


def _hyena_filter_taps(feats, w1, b1, freq, w2, b2, w3, dec):
    l = feats.shape[0]
    n = w3.shape[1]
    full = lambda a: pl.BlockSpec(a.shape, lambda i: (0,) * a.ndim)
    args = (feats, w1, b1, freq, w2, b2, w3, dec)
    return pl.pallas_call(
        _filter_kernel,
        grid=(1,),
        in_specs=[full(a) for a in args],
        out_specs=pl.BlockSpec((l, n), lambda i: (0, 0)),
        out_shape=jax.ShapeDtypeStruct((l, n), F32),
        compiler_params=_cp("arbitrary"),
        name="hyena_filter_taps",
    )(*args)


def _table_mm_kernel(t_ref, x_ref, o_ref):
    o_ref[...] = _dot(t_ref[...], x_ref[...].astype(BF16)).astype(o_ref.dtype)


def _table_matmul(table, x, out_dtype):
    m, k = table.shape
    n = x.shape[1]
    tm = min(m, 512)
    return pl.pallas_call(
        _table_mm_kernel,
        grid=(m // tm,),
        in_specs=[pl.BlockSpec((tm, k), lambda i: (i, 0)),
                  pl.BlockSpec((k, n), lambda i: (0, 0))],
        out_specs=pl.BlockSpec((tm, n), lambda i: (i, 0)),
        out_shape=jax.ShapeDtypeStruct((m, n), out_dtype),
        compiler_params=_cp("parallel"),
        name="table_matmul",
    )(table, x)


def _dftmul_kernel(fc_ref, fs_ref, z_ref, k_ref, p_ref):
    z = z_ref[...].astype(BF16)
    zc = _dot(fc_ref[...], z)
    zs = _dot(fs_ref[...], z)
    p_ref[0] = (zc * k_ref[0] - zs * k_ref[1]).astype(BF16)
    p_ref[1] = (zc * k_ref[2] + zs * k_ref[3]).astype(BF16)


def _dft_multiply(ffwd, z, z_spec, kf):
    l = ffwd.shape[1]
    b = z.shape[0]
    tf = min(l, DFT_ROWS)
    nf = l // tf
    out = pl.pallas_call(
        _dftmul_kernel,
        grid=(nf, b),
        in_specs=[pl.BlockSpec((tf, l), lambda i, bi: (i, 0)),
                  pl.BlockSpec((tf, l), lambda i, bi: (i + nf, 0)),
                  z_spec,
                  pl.BlockSpec((4, tf, HY_W), lambda i, bi: (0, i, 0))],
        out_specs=pl.BlockSpec((None, 2, tf, HY_W), lambda i, bi: (bi, 0, i, 0)),
        out_shape=jax.ShapeDtypeStruct((b, 2, l, HY_W), BF16),
        compiler_params=_cp("parallel", "arbitrary"),
        name="dft_multiply",
    )(ffwd, ffwd, z, kf)
    return out.reshape(b, 2 * l, HY_W)


def _idft_gate_kernel(fi_ref, p_ref, g_ref, z_ref, b_ref, o_ref):
    conv = _dot(fi_ref[...], p_ref[...])
    o_ref[...] = (g_ref[...] * (conv + b_ref[...] * z_ref[...])).astype(o_ref.dtype)


def _idft_gate(finv, p, gate, gate_spec, z, z_spec, bias, out_dtype):
    l = finv.shape[0]
    b = p.shape[0]
    tt = min(l, DFT_ROWS)
    return pl.pallas_call(
        _idft_gate_kernel,
        grid=(l // tt, b),
        in_specs=[pl.BlockSpec((tt, 2 * l), lambda i, bi: (i, 0)),
                  pl.BlockSpec((None, 2 * l, HY_W), lambda i, bi: (bi, 0, 0)),
                  gate_spec, z_spec,
                  pl.BlockSpec((1, HY_W), lambda i, bi: (0, 0))],
        out_specs=pl.BlockSpec((None, tt, HY_W), lambda i, bi: (bi, i, 0)),
        out_shape=jax.ShapeDtypeStruct((b, l, HY_W), out_dtype),
        compiler_params=_cp("parallel", "arbitrary"),
        name="idft_gate",
    )(finv, p, gate, z, bias)


def _hyena_mix(h, conv_w, conv_b, kf, hy_bias, ffwd, finv):
    b, l, _ = h.shape
    tt = min(l, DFT_ROWS)
    u = _short_conv(h, conv_w, conv_b)
    part = lambda j, rows: pl.BlockSpec((None, None, rows, HY_W),
                                        lambda i, bi, j=j: (bi, j, i if rows != l else 0, 0))
    p = _dft_multiply(ffwd, u, part(0, l), kf[0])
    z1 = _idft_gate(finv, p, u, part(1, tt), u, part(0, tt), hy_bias[0:1], F32)
    p = _dft_multiply(ffwd, z1, pl.BlockSpec((None, l, HY_W), lambda i, bi: (bi, 0, 0)), kf[1])
    return _idft_gate(finv, p, u, part(2, tt), z1,
                      pl.BlockSpec((None, tt, HY_W), lambda i, bi: (bi, i, 0)), hy_bias[1:2], BF16)


def _fn1_kernel(z_ref, m_ref, o_ref):
    r = _dot(z_ref[...], m_ref[...])
    half = r.shape[1] // 2
    o_ref[0] = r[:, :half].astype(BF16)
    o_ref[1] = r[:, half:].astype(BF16)


def _fourier_mix(h, m1, t2):
    b, l, _ = h.shape
    tm = min(l, DFT_ROWS)
    zz = pl.pallas_call(
        _fn1_kernel,
        grid=(b, l // tm),
        in_specs=[pl.BlockSpec((None, tm, FN_W), lambda bi, i: (bi, i, COL_FN // FN_W)),
                  pl.BlockSpec((FN_W, 2 * FN_W), lambda bi, i: (0, 0))],
        out_specs=pl.BlockSpec((None, 2, tm, FN_W), lambda bi, i: (bi, 0, i, 0)),
        out_shape=jax.ShapeDtypeStruct((b, 2, l, FN_W), BF16),
        compiler_params=_cp("parallel", "parallel"),
        name="fnet_channels",
    )(h, m1).reshape(b, 2 * l, FN_W)
    return pl.pallas_call(
        _table_mm_kernel,
        grid=(l // tm, b),
        in_specs=[pl.BlockSpec((tm, 2 * l), lambda i, bi: (i, 0)),
                  pl.BlockSpec((None, 2 * l, FN_W), lambda i, bi: (bi, 0, 0))],
        out_specs=pl.BlockSpec((None, tm, FN_W), lambda i, bi: (bi, i, 0)),
        out_shape=jax.ShapeDtypeStruct((b, l, FN_W), BF16),
        compiler_params=_cp("parallel", "arbitrary"),
        name="fnet_positions",
    )(t2, zz)


def _merge_kernel(hy_ref, fn_ref, at_ref, g_ref, x_ref, gm_ref, why_ref, wfn_ref, wat_ref, wout_ref, o_ref):
    d = x_ref.shape[-1]
    g = 1.0 / (1.0 + jnp.exp(-g_ref[...].astype(F32)))
    y = (g[:, :d] * _dot(hy_ref[...], why_ref[...])
         + g[:, d:2 * d] * _dot(fn_ref[...], wfn_ref[...])
         + g[:, 2 * d:] * _dot(at_ref[...], wat_ref[...]))
    mix = _dot(y.astype(BF16), wout_ref[...])
    o_ref[...] = x_ref[...] + gm_ref[...] * mix


def _merge(hyo, fno, att, h, x, mods, layer, row_fn, w_hy, w_fn, w_at, w_out):
    b, l, d = x.shape
    tm = min(l, 1024)
    full = lambda a: pl.BlockSpec(a.shape, lambda bi, i: (0,) * a.ndim)
    tok = lambda wdt, blk=0: pl.BlockSpec((None, tm, wdt), lambda bi, i: (bi, i, blk))
    return pl.pallas_call(
        _merge_kernel,
        grid=(b, l // tm),
        in_specs=[tok(HY_W), tok(FN_W), tok(N_HEADS * HEAD_W), tok(3 * d, COL_GATE), tok(d),
                  pl.BlockSpec((None, None, None, 1, d), _mod_spec(layer, 2, lambda bi, i: row_fn(bi))),
                  full(w_hy), full(w_fn), full(w_at), full(w_out)],
        out_specs=tok(d),
        out_shape=jax.ShapeDtypeStruct((b, l, d), F32),
        compiler_params=_cp("parallel", "parallel"),
        name="merge_residual",
    )(hyo, fno, att, h, x, mods, w_hy, w_fn, w_at, w_out)


def _peer_q_kernel(x_ref, g_ref, sh_ref, sc_ref, wq_ref, kh_ref, kl_ref, s_ref, nt_ref):
    n = _modulated_norm(x_ref[...], g_ref[...], sh_ref[...], sc_ref[...])
    nt_ref[...] = n.T.astype(BF16)
    q = _dot(n.astype(BF16), wq_ref[...])
    dq = kh_ref.shape[1]
    for hp in range(kh_ref.shape[0]):
        qh, ql = _split(q[:, hp * dq:(hp + 1) * dq])
        kh = kh_ref[hp]
        s = _dot(qh, kh) + _dot(ql, kh) + _dot(qh, kl_ref[hp])
        s_ref[hp] = s.T


def _peer_scores(x, g, mods, layer, row_fn, wq, keys_t):
    b, l, d = x.shape
    tt = min(l, 512)
    nt = l // tt
    nhp = 2 * PEER_HEADS
    mrow = lambda bi, i: row_fn(bi)
    full = lambda a: pl.BlockSpec(a.shape, lambda bi, i: (0,) * a.ndim)
    return pl.pallas_call(
        _peer_q_kernel,
        grid=(b, nt),
        in_specs=[pl.BlockSpec((None, tt, d), lambda bi, i: (bi, i, 0)),
                  pl.BlockSpec((1, d), lambda bi, i: (0, 0)),
                  pl.BlockSpec((None, None, None, 1, d), _mod_spec(layer, 3, mrow)),
                  pl.BlockSpec((None, None, None, 1, d), _mod_spec(layer, 4, mrow)),
                  full(wq), full(keys_t[0]), full(keys_t[1])],
        out_specs=[pl.BlockSpec((nhp, PEER_NKEYS, tt), lambda bi, i: (0, 0, bi * nt + i)),
                   pl.BlockSpec((d, tt), lambda bi, i: (0, bi * nt + i))],
        out_shape=[jax.ShapeDtypeStruct((nhp, PEER_NKEYS, b * l), F32),
                   jax.ShapeDtypeStruct((d, b * l), BF16)],
        compiler_params=_cp("parallel", "parallel"),
        name="peer_scores",
    )(x, g, mods, mods, wq, keys_t[0], keys_t[1])


def _merge_sort_pairs(n):
    pairs = []
    p = 1
    while p < n:
        k = p
        while k >= 1:
            for j in range(k % p, n - k, 2 * k):
                for i in range(min(k, n - j - k)):
                    if (i + j) // (2 * p) == (i + j + k) // (2 * p):
                        pairs.append((i + j, i + j + k))
            k //= 2
        p *= 2
    return pairs


def _top_rows_grouped(s, k):
    groups = s.shape[0] // 8
    v = [s[g * 8:(g + 1) * 8, :] for g in range(groups)]
    for i, j in _merge_sort_pairs(groups):
        v[i], v[j] = jnp.maximum(v[i], v[j]), jnp.minimum(v[i], v[j])
    v.append(jnp.full_like(v[0], NEG))
    vals = []
    for r in range(k):
        m = jnp.max(v[0], axis=0, keepdims=True)
        vals.append(m)
        taken = v[0] == m
        for d in range(min(groups, k - 1 - r)):
            v[d] = jnp.where(taken, v[d + 1], v[d])
    return vals


def _peer_expert_kernel(s_ref, nt_ref, u_ref, vt_ref, x_ref, gm_ref, o_ref,
                        tau_ref, l1_ref, l2_ref, l1c_ref, hid0_ref, hid1_ref, p0_ref, p1_ref, acc_ref):
    s = pl.program_id(1)
    nc = pl.num_programs(1) - 2
    k = PEER_TOPK
    tt = nt_ref.shape[1]
    n_i = u_ref.shape[0] // PEER_NKEYS
    piece = GATE_VREGS * 8 * 128 // tt

    def hidden(hid_ref):
        hid_ref[...] = _dot(u_ref[...], nt_ref[...])

    def stages(chunk, hid_w, hid_r, p_w, p_r):
        first = pl.multiple_of(chunk * n_i, n_i)
        for h in range(PEER_HEADS):
            l1c_ref[h] = l1_ref[h, pl.ds(first, n_i), :]
        hidden(hid_w)
        acc_ref[...] += _dot(vt_ref[...], p_r[...])
        for ii in range(n_i):
            l1_row = [jnp.broadcast_to(l1c_ref[h, ii:ii + 1, :], (8, tt))[None] for h in range(PEER_HEADS)]
            for jp in range(PEER_NKEYS // piece):
                js = slice(jp * piece, (jp + 1) * piece)
                gate = jnp.zeros((piece // 8, 8, tt), F32)
                for h in range(PEER_HEADS):
                    logw = l1_row[h] + l2_ref[h, js, :].reshape(piece // 8, 8, tt)
                    gate = gate + jnp.where(logw >= tau_ref[h][None], jnp.exp2(logw), 0.0)
                rows = slice(ii * PEER_NKEYS + jp * piece, ii * PEER_NKEYS + (jp + 1) * piece)
                hid = hid_r[rows, :]
                act = hid * (1.0 + lax.erf(hid * (2.0 ** -0.5)))
                p_w[rows, :] = (gate.reshape(piece, tt) * act).astype(BF16)

    @pl.when(s == 0)
    def _():
        acc_ref[...] = jnp.zeros_like(acc_ref)
        p1_ref[...] = jnp.zeros_like(p1_ref)
        hidden(hid0_ref)
        width = min(tt, 256)
        parts = tt // width

        def select(it, carry):
            h = it // parts
            cols = pl.ds(pl.multiple_of((it % parts) * width, width), width)
            s1 = s_ref[2 * h, :, cols]
            s2 = s_ref[2 * h + 1, :, cols]
            a = _top_rows_grouped(s1, k + 1)
            b = _top_rows_grouped(s2, k + 1)
            cand = [a[i] + b[j] for i in range(k + 1) for j in range((k + 1) // (i + 1))]
            pad = (-len(cand)) % 64
            top = _top_rows_grouped(jnp.concatenate(cand + [jnp.full_like(a[0], NEG)] * pad, axis=0), k + 1)
            z = jnp.zeros_like(top[0])
            for t in top[:k]:
                z = z + jnp.exp(t - top[0])
            shift = top[0] + jnp.log(z)
            log2e = 1.0 / math.log(2.0)
            l1_ref[h, :, cols] = s1 * log2e
            l2_ref[h, :, cols] = (s2 - shift) * log2e - 1.0
            tau = (0.5 * (top[k - 1] + top[k]) - shift) * log2e - 1.0
            tau_ref[h, :, cols] = jnp.broadcast_to(tau, (8, width))
            return carry

        lax.fori_loop(0, PEER_HEADS * parts, select, 0)

    @pl.when((s >= 1) & (s <= nc) & (s % 2 == 1))
    def _():
        stages(s - 1, hid1_ref, hid0_ref, p0_ref, p1_ref)

    @pl.when((s >= 1) & (s <= nc) & (s % 2 == 0))
    def _():
        stages(s - 1, hid0_ref, hid1_ref, p1_ref, p0_ref)

    @pl.when(s == nc + 1)
    def _():
        last = p0_ref if (nc - 1) % 2 == 0 else p1_ref
        out_t = acc_ref[...] + _dot(vt_ref[...], last[...])
        o_ref[...] = x_ref[...] + gm_ref[...] * out_t.T


def _peer_experts(scores, n_t, u, v_t, x, mods, layer, row_fn):
    b, l, d = x.shape
    tt = min(l, PEER_TOKENS)
    nt = l // tt
    nc, _, ec = v_t.shape
    nhp = scores.shape[0]
    return pl.pallas_call(
        _peer_expert_kernel,
        grid=(b * nt, nc + 2),
        in_specs=[pl.BlockSpec((nhp, PEER_NKEYS, tt), lambda t, s: (0, 0, t)),
                  pl.BlockSpec((d, tt), lambda t, s: (0, t)),
                  pl.BlockSpec((ec, d), lambda t, s: (jnp.minimum(s, nc - 1), 0)),
                  pl.BlockSpec((None, d, ec), lambda t, s: (jnp.clip(s - 2, 0, nc - 1), 0, 0)),
                  pl.BlockSpec((None, tt, d), lambda t, s: (t // nt, t % nt, 0)),
                  pl.BlockSpec((None, None, None, 1, d), _mod_spec(layer, 5, lambda t, s: row_fn(t // nt)))],
        out_specs=pl.BlockSpec((None, tt, d), lambda t, s: (t // nt, t % nt, 0)),
        out_shape=jax.ShapeDtypeStruct((b, l, d), F32),
        scratch_shapes=[pltpu.VMEM((PEER_HEADS, 8, tt), F32),
                        pltpu.VMEM((PEER_HEADS, PEER_NKEYS, tt), F32),
                        pltpu.VMEM((PEER_HEADS, PEER_NKEYS, tt), F32),
                        pltpu.VMEM((PEER_HEADS, ec // PEER_NKEYS, tt), F32),
                        pltpu.VMEM((ec, tt), F32),
                        pltpu.VMEM((ec, tt), F32),
                        pltpu.VMEM((ec, tt), BF16),
                        pltpu.VMEM((ec, tt), BF16),
                        pltpu.VMEM((d, tt), F32)],
        compiler_params=_cp("parallel", "arbitrary"),
        name="peer_experts",
    )(scores, n_t, u, v_t, x, mods)


def _cast_kernel(a_ref, o_ref):
    o_ref[...] = a_ref[...].astype(o_ref.dtype)


def _cast_bf16(a, layer):
    _, rows, d = a.shape
    return pl.pallas_call(
        _cast_kernel,
        grid=(rows // PEER_CHUNK,),
        in_specs=[pl.BlockSpec((None, PEER_CHUNK, d), lambda i: (layer, i, 0))],
        out_specs=pl.BlockSpec((PEER_CHUNK, d), lambda i: (i, 0)),
        out_shape=jax.ShapeDtypeStruct((rows, d), BF16),
        compiler_params=_cp("parallel"),
        name="cast_bf16",
    )(a)


def _transpose_cast_kernel(a_ref, o_ref):
    o_ref[...] = a_ref[...].T.astype(o_ref.dtype)


def _chunk_transpose_bf16(a, layer):
    _, rows, d = a.shape
    nc = rows // PEER_CHUNK
    return pl.pallas_call(
        _transpose_cast_kernel,
        grid=(nc,),
        in_specs=[pl.BlockSpec((None, PEER_CHUNK, d), lambda i: (layer, i, 0))],
        out_specs=pl.BlockSpec((None, d, PEER_CHUNK), lambda i: (i, 0, 0)),
        out_shape=jax.ShapeDtypeStruct((nc, d, PEER_CHUNK), BF16),
        compiler_params=_cp("parallel"),
        name="chunk_transpose_bf16",
    )(a)


def _permute_in_weights(w, layer):
    _, d, n = w.shape
    tn = 512
    nb = n // tn
    shift = (n - 3 * d) // tn
    return pl.pallas_call(
        _cast_kernel,
        grid=(nb,),
        in_specs=[pl.BlockSpec((None, d, tn), lambda j: (layer, 0, (j + shift) % nb))],
        out_specs=pl.BlockSpec((d, tn), lambda j: (0, j)),
        out_shape=jax.ShapeDtypeStruct((d, n), BF16),
        compiler_params=_cp("parallel"),
        name="permute_in_weights",
    )(w)


def _peer(x, g, mods, layer, row_fn, wq_t, keys, u, v_t):
    scores, n_t = _peer_scores(x, g, mods, layer, row_fn, wq_t, keys)
    return _peer_experts(scores, n_t, u, v_t, x, mods, layer, row_fn)


def _dft_tables(l):
    cos, sin = _cos_sin_table(l, l, 2 * l)
    t = jnp.arange(l, dtype=jnp.int32)[None, :]
    first = (jnp.arange(l) == 0)[:, None]
    sin = jnp.where(first, jnp.where(t % 2 == 0, 1.0, -1.0), sin)
    ffwd = jnp.concatenate([cos, sin], axis=0).astype(BF16)
    return ffwd, ffwd.T


def _cos_sin_table(nf, nt, n):
    step = 64
    f = jnp.arange(nf, dtype=jnp.int32)[:, None]
    angle = lambda prod: (2.0 * math.pi) * ((prod % n).astype(F32) / n)
    a = angle(f * (step * jnp.arange(nt // step, dtype=jnp.int32)[None, :]))
    b = angle(f * jnp.arange(step, dtype=jnp.int32)[None, :])
    ca, sa = jnp.cos(a)[:, :, None], jnp.sin(a)[:, :, None]
    cb, sb = jnp.cos(b)[:, None, :], jnp.sin(b)[:, None, :]
    return (ca * cb - sa * sb).reshape(nf, nt), (sa * cb + ca * sb).reshape(nf, nt)


def _fnet_tables(l):
    t2 = jnp.concatenate(_cos_sin_table(l, l, l), axis=1).astype(BF16)
    k = np.arange(FN_GROUP)
    ang64 = 2.0 * np.pi * ((k[:, None] * k[None, :]) % FN_GROUP) / FN_GROUP
    eye = np.eye(FN_W // FN_GROUP)
    scale = 1.0 / math.sqrt(FN_GROUP * l)
    m1 = np.concatenate([np.kron(eye, np.cos(ang64)), -np.kron(eye, np.sin(ang64))], axis=1) * scale
    return jnp.asarray(m1, F32).astype(BF16), t2


def _rope_tables(l):
    rows = l // GRID_W
    row = jnp.repeat(jnp.arange(rows), GRID_W).astype(F32)
    col = jnp.tile(jnp.arange(GRID_W), rows).astype(F32)
    half = QK_DIM // 2
    inv = ROPE_BASE ** (-jnp.arange(0, half, 2, dtype=F32) / half)
    ang = jnp.stack([row[:, None] * inv, col[:, None] * inv], axis=1)
    cos = jnp.repeat(jnp.cos(ang)[:, :, None, :], 2, axis=2)
    sin = jnp.sin(ang)
    sin = jnp.stack([-sin, sin], axis=2)
    rep = lambda a: jnp.tile(a.reshape(l, QK_DIM), (1, 2 * N_HEADS))
    return rep(cos), rep(sin)


def _filter_features(l):
    pos = jnp.arange(l, dtype=F32)
    t = pos / max(l - 1, 1)
    w = 2.0 * math.pi * pos / l
    f = jnp.linspace(1e-4, HY_BANDS - 1, HY_BANDS, dtype=F32)
    feats = jnp.concatenate([t[:, None], jnp.cos(w[:, None] * f), -jnp.sin(w[:, None] * f)], axis=-1)
    feats = jnp.pad(feats, ((0, 0), (0, 64 - HY_EMB)))
    deltas = jnp.abs(jnp.linspace(HY_MIN_DECAY, HY_MAX_DECAY, HY_W, dtype=F32))
    dec = jnp.exp(-t[:, None] * deltas)
    return feats, jnp.tile(dec, (1, HY_ORDER))


def _hyena_filters(l, tabs, ffwd, w1, b1, freq, w2, b2, w3):
    feats, dec = tabs
    taps = _hyena_filter_taps(feats, jnp.pad(w1, ((0, 64 - HY_EMB), (0, 0))), b1[None], freq[None],
                              w2, b2[None], w3, dec)
    kf = _table_matmul(ffwd, taps, F32)
    half = HY_ORDER * HY_W
    kc = kf[:l, :half]
    nyq = kf[l, :half]
    ks = kf[l:, half:]
    n = 2.0 * l
    first = (jnp.arange(l) == 0)[:, None]
    wc = jnp.where(first, 1.0 / n, 2.0 / n)
    ka = kc * wc
    kb = jnp.where(first, 0.0, ks * (2.0 / n))
    kd = jnp.where(first, nyq[None, :] / n, kc * (2.0 / n))
    stack = jnp.stack([ka, kb, kb, kd], axis=0)
    return jnp.moveaxis(stack.reshape(4, l, HY_ORDER, HY_W), 2, 0)


def kernel(x, c, ctx, c_ctx, w_ada, b_ada, g_mix, g_ffn, w_in, hy_conv_w, hy_conv_b, hy_w1, hy_b1, hy_freq, hy_w2, hy_b2, hy_w3, hy_bias, g_q, g_k, lam, g_sub, w_hy, w_fn, w_at, w_out, peer_wq, peer_keys, peer_u, peer_v):
    bsz, seq, d = x.shape
    clen = ctx.shape[1]
    depth = w_ada.shape[0]

    cc = jnp.concatenate([c, c_ctx[None], jnp.zeros((MOD_ROWS - bsz - 1, d), F32)], axis=0)
    mods = _ada_mods(cc, w_ada, b_ada)
    lat_row = lambda bi: bi
    ctx_row = lambda bi: bsz

    rope = _rope_tables(seq)
    tabs = {n: dict(dft=_dft_tables(n), fnet=_fnet_tables(n), feat=_filter_features(n)) for n in (seq, clen)}
    w = N_HEADS * HEAD_W
    lane = np.arange(w)
    bd = jnp.asarray((lane[:, None] // QK_DIM == lane[None, :] // QK_DIM) / QK_DIM, F32).astype(BF16)

    xl, xc = x, ctx
    for l in range(depth):
        last = l == depth - 1
        lam_init = 0.8 - 0.6 * math.exp(-0.3 * l)
        w_perm = _permute_in_weights(w_in, l)
        gq = jnp.tile(g_q[l].reshape(1, HEAD_W), (1, N_HEADS)) * (QK_DIM ** -0.5 / math.log(2.0))
        gk = jnp.tile(g_k[l].reshape(1, HEAD_W), (1, N_HEADS))
        gsub = g_sub[l][None]
        wts = [a[l].astype(BF16) for a in (w_hy, w_fn, w_at, w_out)]
        filt_args = (hy_w1[l], hy_b1[l], hy_freq[l], hy_w2[l], hy_b2[l], hy_w3[l])

        h_l = _in_projection(xl, g_mix[l][None], mods, l, lat_row, w_perm, 0, P_IN)
        h_c = _in_projection(xc, g_mix[l][None], mods, l, ctx_row, w_perm, COL_Q if last else 0, P_IN)
        q_l, k_l = _qk_prep(h_l, gq, gk, bd, rope)
        q_c, k_c = _qk_prep(h_c, gq, gk, bd, None)
        k_all, v_all = _concat_kv([(k_l, h_l), (k_c, h_c)])
        att_l = _diff_attention(q_l, k_all, v_all, lam[l], gsub, lam_init)
        ffwd, finv = tabs[seq]["dft"]
        kf = _hyena_filters(seq, tabs[seq]["feat"], ffwd, *filt_args)
        hyo_l = _hyena_mix(h_l, hy_conv_w[l], hy_conv_b[l][None], kf, hy_bias[l], ffwd, finv)
        fno_l = _fourier_mix(h_l, *tabs[seq]["fnet"])
        if not last:
            att_c = _diff_attention(q_c, *_concat_kv([(k_c, h_c)]), lam[l], gsub, lam_init)
            ffwd_c, finv_c = tabs[clen]["dft"]
            kf_c = _hyena_filters(clen, tabs[clen]["feat"], ffwd_c, *filt_args)
            hyo_c = _hyena_mix(h_c, hy_conv_w[l], hy_conv_b[l][None], kf_c, hy_bias[l], ffwd_c, finv_c)
            fno_c = _fourier_mix(h_c, *tabs[clen]["fnet"])
            xc = _merge(hyo_c, fno_c, att_c, h_c, xc, mods, l, ctx_row, *wts)
        xl = _merge(hyo_l, fno_l, att_l, h_l, xl, mods, l, lat_row, *wts)

        wq_t = peer_wq[l].astype(BF16)
        keys = _split(jnp.swapaxes(peer_keys[l].reshape(2 * PEER_HEADS, PEER_NKEYS, -1), 1, 2))
        u = _cast_bf16(peer_u, l)
        v_t = _chunk_transpose_bf16(peer_v, l)
        if not last:
            xc = _peer(xc, g_ffn[l][None], mods, l, ctx_row, wq_t, keys, u, v_t)
        xl = _peer(xl, g_ffn[l][None], mods, l, lat_row, wq_t, keys, u, v_t)
    return xl
```
